```python
import math
import jax
import jax.numpy as jnp
from jax import lax
import numpy as np

D_MODEL = 2048
BATCH = 4
SEQ = 2048
DEPTH = 4
DEC_BATCH = 128
DEC_SEQ = 1
PAST_LEN = 16384
PAGE_SIZE = 128

D_MIX = D_MODEL
GROUP = D_MIX // 4
D_FF = 4 * D_MODEL
CHUNK = 64
ML_H = 4
ML_DV = GROUP // ML_H
ML_DK = ML_DV // 2
GLA_H = 4
GLA_DV = GROUP // GLA_H
GLA_DK = GLA_DV // 2
GLA_RANK = 16
GLA_TAU = 16.0
RW_N = 64
RW_H = GROUP // RW_N
RW_W_RANK = 32
RW_A_RANK = 32
RW_G_RANK = 64
RW_DECAY_SCALE = 0.606531
RW_GN_EPS = 64e-5
DN_H = 4
DN_D = GROUP // DN_H
DN_CONV = 4
DN_QKV = 3 * GROUP

ML_COLS = (ML_H * ML_DK, ML_H * ML_DK, GROUP, GROUP, ML_H, ML_H)
GLA_COLS = (GLA_H * GLA_DK, GLA_H * GLA_DK, GROUP, GROUP, GLA_RANK)
RW_COLS = (GROUP, GROUP, GROUP, RW_W_RANK, RW_A_RANK, RW_G_RANK)
DN_COLS = (DN_QKV, GROUP, DN_H, DN_H)
N_ML = sum(ML_COLS)
N_GLA = sum(GLA_COLS)
N_RW = sum(RW_COLS)
N_DN = sum(DN_COLS)
N_IN = N_ML + N_GLA + N_RW + N_DN

ALPHA = (2 * DEPTH) ** 0.25
BETA_INIT = (8 * DEPTH) ** -0.25
LN_EPS = 1e-5
NORM_EPS = 1e-6

kernel_name = 'hybrid_mlstm_gla_rwkv7_gdn_decoder_step'


def _split(x, widths):
    parts, off = [], 0
    for w in widths:
        parts.append(x[..., off:off + w])
        off += w
    return parts


def _heads(x, h):
    b, t, c = x.shape
    return x.reshape(b, t, h, c // h).transpose(0, 2, 1, 3)


def _to_chunks(x, size):
    b, h, t = x.shape[:3]
    x = x.reshape((b, h, t // size, size) + x.shape[3:])
    return jnp.moveaxis(x, 2, 0)


def _from_chunks(o):
    n, b, h, size, d = o.shape
    return o.transpose(1, 0, 3, 2, 4).reshape(b, n * size, h * d)


def _l2norm(x):
    return x * lax.rsqrt(jnp.sum(x * x, -1, keepdims=True) + NORM_EPS)


def _layernorm(x, g, b):
    xf = x.astype(jnp.float32)
    mu = jnp.mean(xf, -1, keepdims=True)
    var = jnp.mean(jnp.square(xf - mu), -1, keepdims=True)
    return ((xf - mu) * lax.rsqrt(var + LN_EPS) * g + b).astype(x.dtype)


def _head_norm(o, h, g):
    b, t, c = o.shape
    oh = o.reshape(b, t, h, c // h)
    oh = oh * lax.rsqrt(jnp.mean(oh * oh, -1, keepdims=True) + NORM_EPS) * g
    return oh.reshape(b, t, c)


def _mlstm(cols, gate_b, norm_g, c0, n0, m0):
    t = cols.shape[1]
    q, k, v, o, ig, fg = _split(cols, ML_COLS)
    q = _heads(q, ML_H)
    k = _heads(k, ML_H) * ML_DK ** -0.5
    v = _heads(v, ML_H)
    i_pre = (ig + gate_b[:ML_H]).transpose(0, 2, 1)
    log_f = jax.nn.log_sigmoid(fg + gate_b[ML_H:]).transpose(0, 2, 1)
    size = math.gcd(t, CHUNK)
    causal = jnp.tril(jnp.ones((size, size), bool))

    def chunk_step(carry, inp):
        c, n, m = carry
        qc, kc, vc, ic, fc = inp
        bcum = jnp.cumsum(fc, -1)
        d = jnp.where(causal, bcum[..., :, None] - bcum[..., None, :] + ic[..., None, :], -jnp.inf)
        inter = bcum + m[..., None]
        m_t = jnp.maximum(inter, jnp.max(d, -1))
        w_inter = jnp.exp(inter - m_t)
        s = jnp.einsum('bhld,bhsd->bhls', qc, kc) * jnp.exp(d - m_t[..., None])
        num = w_inter[..., None] * jnp.einsum('bhld,bhdv->bhlv', qc, c) + jnp.einsum('bhls,bhsv->bhlv', s, vc)
        den = w_inter * jnp.einsum('bhld,bhd->bhl', qc, n) + jnp.sum(s, -1)
        h = num / jnp.maximum(jnp.abs(den), jnp.exp(-m_t))[..., None]
        m_new = m_t[..., -1]
        f_state = jnp.exp(bcum[..., -1] + m - m_new)
        kw = kc * jnp.exp(bcum[..., -1:] - bcum + ic - m_new[..., None])[..., None]
        c_new = f_state[..., None, None] * c + jnp.einsum('bhsd,bhsv->bhdv', kw, vc)
        n_new = f_state[..., None] * n + jnp.sum(kw, -2)
        return (c_new, n_new, m_new), h

    (c1, n1, m1), h = lax.scan(chunk_step, (c0, n0, m0),
                               (_to_chunks(q, size), _to_chunks(k, size), _to_chunks(v, size),
                                _to_chunks(i_pre, size), _to_chunks(log_f, size)))
    h = _head_norm(_from_chunks(h), ML_H, norm_g.reshape(ML_H, ML_DV)) * jax.nn.sigmoid(o)
    return h, (c1, n1, m1)


def _gla(cols, gate_up, gate_b, norm_g, s0):
    t = cols.shape[1]
    q, k, v, og, lr = _split(cols, GLA_COLS)
    q = _heads(q, GLA_H) * GLA_DK ** -0.5
    k = _heads(k, GLA_H)
    v = _heads(v, GLA_H)
    log_a = _heads(jax.nn.log_sigmoid(lr @ gate_up + gate_b) / GLA_TAU, GLA_H)
    size = math.gcd(t, CHUNK)
    causal = jnp.tril(jnp.ones((size, size), bool))

    def chunk_step(s, inp):
        qc, kc, vc, ac = inp
        bcum = jnp.cumsum(ac, -2)
        ref = bcum[..., size // 2:size // 2 + 1, :]
        a = jnp.einsum('bhld,bhsd->bhls', qc * jnp.exp(bcum - ref), kc * jnp.exp(ref - bcum))
        a = jnp.where(causal, a, 0.0)
        o = jnp.einsum('bhld,bhdv->bhlv', qc * jnp.exp(bcum), s) + jnp.einsum('bhls,bhsv->bhlv', a, vc)
        last = bcum[..., -1:, :]
        s_new = jnp.exp(last[..., 0, :])[..., None] * s + jnp.einsum('bhsd,bhsv->bhdv', kc * jnp.exp(last - bcum), vc)
        return s_new, o

    s1, o = lax.scan(chunk_step, s0, (_to_chunks(q, size), _to_chunks(k, size),
                                      _to_chunks(v, size), _to_chunks(log_a, size)))
    o = _head_norm(_from_chunks(o), GLA_H, norm_g.reshape(GLA_H, GLA_DV)) * jax.nn.silu(og)
    return o, s1


def _rwkv7(cols, shift0, mu, w_up, w0, a_up, a0, g_up, k_k, k_a, r_k, norm_g, norm_b, s0):
    b, t, _ = cols.shape
    prev = jnp.concatenate([shift0[:, None, :], cols[:, :-1]], 1)
    xs = cols + (prev - cols) * mu
    r, k, v, wd, ad, gd = _split(xs, RW_COLS)
    log_w = -RW_DECAY_SCALE * jax.nn.sigmoid(w0 + jnp.tanh(wd) @ w_up)
    a = jax.nn.sigmoid(a0 + ad @ a_up)
    g = jax.nn.sigmoid(gd) @ g_up
    hv = lambda z: z.reshape(b, t, RW_H, RW_N)
    kk = _l2norm(hv(k * k_k))
    k = k * (1.0 + (a - 1.0) * k_a)
    r_h, k_h, v_h = hv(r), hv(k), hv(v)
    seq = tuple(jnp.moveaxis(z, 1, 0) for z in (r_h, k_h, v_h, hv(log_w), hv(a), kk))

    def step(s, inp):
        rt, kt, vt, lwt, at, kkt = inp
        sk = jnp.einsum('bhk,bhkv->bhv', kkt, s)
        s = jnp.exp(lwt)[..., None] * s - (kkt * at)[..., None] * sk[..., None, :] + kt[..., None] * vt[..., None, :]
        return s, jnp.einsum('bhk,bhkv->bhv', rt, s)

    s1, o = lax.scan(step, s0, seq)
    o = jnp.moveaxis(o, 0, 1)
    mu_o = jnp.mean(o, -1, keepdims=True)
    var_o = jnp.mean(jnp.square(o - mu_o), -1, keepdims=True)
    o = (o - mu_o) * lax.rsqrt(var_o + RW_GN_EPS) * norm_g.reshape(RW_H, RW_N) + norm_b.reshape(RW_H, RW_N)
    bonus = jnp.sum(r_h * k_h * r_k.reshape(RW_H, RW_N), -1, keepdims=True) * v_h
    out = (o + bonus).reshape(b, t, GROUP) * g
    return out, (s1, cols[:, -1])


def _gdn(cols, buf0, conv_w, a_log, dt_bias, norm_g, s0):
    t = cols.shape[1]
    qkv_raw, z, beta_pre, a_pre = _split(cols, DN_COLS)
    xp = jnp.concatenate([buf0, qkv_raw], 1)
    conv = xp[:, 0:t] * conv_w[0]
    for w in range(1, DN_CONV):
        conv = conv + xp[:, w:w + t] * conv_w[w]
    q, k, v = _split(jax.nn.silu(conv), (GROUP, GROUP, GROUP))
    q = _l2norm(_heads(q, DN_H)) * DN_D ** -0.5
    k = _l2norm(_heads(k, DN_H))
    v = _heads(v, DN_H)
    beta = jax.nn.sigmoid(beta_pre).transpose(0, 2, 1)
    g = (-jnp.exp(a_log) * jax.nn.softplus(a_pre + dt_bias)).transpose(0, 2, 1)
    size = math.gcd(t, CHUNK)
    causal = jnp.tril(jnp.ones((size, size), bool))
    strict = jnp.tril(jnp.ones((size, size), bool), -1)

    def chunk_step(s, inp):
        qc, kc, vc, gc, bc = inp
        gcum = jnp.cumsum(gc, -1)
        decay = jnp.exp(jnp.where(causal, gcum[..., :, None] - gcum[..., None, :], -jnp.inf))
        kb = kc * bc[..., None]
        a_low = jnp.where(strict, jnp.einsum('bhld,bhsd->bhls', kb, kc) * decay, 0.0)
        rhs = jnp.concatenate([vc * bc[..., None], kb * jnp.exp(gcum)[..., None]], -1)
        sol = lax.linalg.triangular_solve(a_low, rhs, left_side=True, lower=True, unit_diagonal=True)
        u, wy = sol[..., :DN_D], sol[..., DN_D:]
        v_new = u - jnp.einsum('bhld,bhdv->bhlv', wy, s)
        o = (jnp.einsum('bhld,bhdv->bhlv', qc * jnp.exp(gcum)[..., None], s)
             + jnp.einsum('bhls,bhsv->bhlv', jnp.einsum('bhld,bhsd->bhls', qc, kc) * decay, v_new))
        glast = gcum[..., -1:]
        s_new = jnp.exp(glast)[..., None] * s + jnp.einsum('bhsd,bhsv->bhdv', kc * jnp.exp(glast - gcum)[..., None], v_new)
        return s_new, o

    s1, o = lax.scan(chunk_step, s0, (_to_chunks(q, size), _to_chunks(k, size), _to_chunks(v, size),
                                      _to_chunks(g, size), _to_chunks(beta, size)))
    o = _head_norm(_from_chunks(o), DN_H, norm_g) * jax.nn.silu(z)
    return o, (s1, xp[:, -(DN_CONV - 1):])


def _fresh_state(b):
    f32 = jnp.float32
    return (jnp.zeros((b, ML_H, ML_DK, ML_DV), f32), jnp.zeros((b, ML_H, ML_DK), f32),
            jnp.zeros((b, ML_H), f32), jnp.zeros((b, GLA_H, GLA_DK, GLA_DV), f32),
            jnp.zeros((b, RW_H, RW_N, RW_N), f32), jnp.zeros((b, N_RW), f32),
            jnp.zeros((b, DN_H, DN_D, DN_D), f32), jnp.zeros((b, DN_CONV - 1, DN_QKV), f32))


def _layer(x, state, lw):
    (w_in, w_out, w_up, w_down, ln1_g, ln1_b, ln2_g, ln2_b,
     mlstm_gate_b, mlstm_norm_g, gla_gate_up, gla_gate_b, gla_norm_g,
     rwkv_mu, rwkv_w_up, rwkv_w0, rwkv_a_up, rwkv_a0, rwkv_g_up, rwkv_k_k, rwkv_k_a, rwkv_r_k,
     rwkv_norm_g, rwkv_norm_b, dn_conv_w, dn_a_log, dn_dt_bias, dn_norm_g) = lw
    c0, n0, m0, s_gla0, s_rw0, shift0, s_dn0, buf0 = [s.astype(jnp.float32) for s in state]
    cols = (x @ w_in).astype(jnp.float32)
    c_ml, c_gla, c_rw, c_dn = _split(cols, (N_ML, N_GLA, N_RW, N_DN))
    h_ml, (c1, n1, m1) = _mlstm(c_ml, mlstm_gate_b, mlstm_norm_g, c0, n0, m0)
    h_gla, s_gla1 = _gla(c_gla, gla_gate_up, gla_gate_b, gla_norm_g, s_gla0)
    h_rw, (s_rw1, shift1) = _rwkv7(c_rw, shift0, rwkv_mu, rwkv_w_up, rwkv_w0, rwkv_a_up, rwkv_a0,
                                   rwkv_g_up, rwkv_k_k, rwkv_k_a, rwkv_r_k, rwkv_norm_g, rwkv_norm_b, s_rw0)
    h_dn, (s_dn1, buf1) = _gdn(c_dn, buf0, dn_conv_w, dn_a_log, dn_dt_bias, dn_norm_g, s_dn0)
    mix = jnp.concatenate([h_ml, h_gla, h_rw, h_dn], -1).astype(x.dtype)
    x = _layernorm(ALPHA * x + mix @ w_out, ln1_g, ln1_b)
    ff = jnp.square(jax.nn.relu(x @ w_up))
    x = _layernorm(ALPHA * x + ff @ w_down, ln2_g, ln2_b)
    return x, (c1, n1, m1, s_gla1, s_rw1, shift1, s_dn1, buf1)


def setup_inputs(seed: int = 0) -> dict:
    key = jax.random.key(seed)
    keys = iter(jax.random.split(key, 48))

    def nrm(shape, scale):
        return scale * jax.random.normal(next(keys), shape, jnp.float32)

    def unif(shape, lo, hi):
        return jax.random.uniform(next(keys), shape, jnp.float32, lo, hi)

    L = DEPTH
    return {
        'x_prompt': nrm((BATCH, SEQ, D_MODEL), 1.0),
        'x_sample': nrm((DEC_BATCH, DEC_SEQ, D_MODEL), 1.0),
        'state_mlstm_c': nrm((L, DEC_BATCH, ML_H, ML_DK, ML_DV), 0.5),
        'state_mlstm_n': nrm((L, DEC_BATCH, ML_H, ML_DK), 0.5),
        'state_mlstm_m': nrm((L, DEC_BATCH, ML_H), 0.5),
        'state_gla': nrm((L, DEC_BATCH, GLA_H, GLA_DK, GLA_DV), 0.1),
        'state_rwkv': nrm((L, DEC_BATCH, RW_H, RW_N, RW_N), 0.1),
        'state_rwkv_shift': nrm((L, DEC_BATCH, N_RW), 1.0),
        'state_dn': nrm((L, DEC_BATCH, DN_H, DN_D, DN_D), 0.1),
        'state_dn_conv': nrm((L, DEC_BATCH, DN_CONV - 1, DN_QKV), 1.0),
        'w_in': nrm((L, D_MODEL, N_IN), D_MODEL ** -0.5),
        'w_out': nrm((L, D_MIX, D_MODEL), BETA_INIT * D_MIX ** -0.5),
        'w_up': nrm((L, D_MODEL, D_FF), D_MODEL ** -0.5),
        'w_down': nrm((L, D_FF, D_MODEL), BETA_INIT * D_FF ** -0.5),
        'ln1_g': 1.0 + nrm((L, D_MODEL), 0.02),
        'ln1_b': nrm((L, D_MODEL), 0.02),
        'ln2_g': 1.0 + nrm((L, D_MODEL), 0.02),
        'ln2_b': nrm((L, D_MODEL), 0.02),
        'mlstm_gate_b': jnp.concatenate([nrm((L, ML_H), 0.1), 3.0 + nrm((L, ML_H), 0.5)], -1),
        'mlstm_norm_g': 1.0 + nrm((L, GROUP), 0.02),
        'gla_gate_up': nrm((L, GLA_RANK, GLA_H * GLA_DK), GLA_RANK ** -0.5),
        'gla_gate_b': nrm((L, GLA_H * GLA_DK), 0.1),
        'gla_norm_g': 1.0 + nrm((L, GROUP), 0.02),
        'rwkv_mu': unif((L, N_RW), 0.0, 1.0),
        'rwkv_w_up': nrm((L, RW_W_RANK, GROUP), RW_W_RANK ** -0.5),
        'rwkv_w0': nrm((L, GROUP), 0.5),
        'rwkv_a_up': nrm((L, RW_A_RANK, GROUP), RW_A_RANK ** -0.5),
        'rwkv_a0': nrm((L, GROUP), 0.1),
        'rwkv_g_up': nrm((L, RW_G_RANK, GROUP), RW_G_RANK ** -0.5),
        'rwkv_k_k': 0.85 + nrm((L, GROUP), 0.02),
        'rwkv_k_a': 1.0 + nrm((L, GROUP), 0.02),
        'rwkv_r_k': nrm((L, GROUP), 0.1),
        'rwkv_norm_g': 1.0 + nrm((L, GROUP), 0.02),
        'rwkv_norm_b': nrm((L, GROUP), 0.02),
        'dn_conv_w': nrm((L, DN_CONV, DN_QKV), DN_CONV ** -0.5),
        'dn_a_log': jnp.log(unif((L, DN_H), 1.0, 16.0)),
        'dn_dt_bias': nrm((L, DN_H), 0.1),
        'dn_norm_g': 1.0 + nrm((L, DN_D), 0.02),
    }


def reference(x_prompt, x_sample, state_mlstm_c, state_mlstm_n, state_mlstm_m, state_gla, state_rwkv,
              state_rwkv_shift, state_dn, state_dn_conv, w_in, w_out, w_up, w_down, ln1_g, ln1_b,
              ln2_g, ln2_b, mlstm_gate_b, mlstm_norm_g, gla_gate_up, gla_gate_b, gla_norm_g, rwkv_mu,
              rwkv_w_up, rwkv_w0, rwkv_a_up, rwkv_a0, rwkv_g_up, rwkv_k_k, rwkv_k_a, rwkv_r_k,
              rwkv_norm_g, rwkv_norm_b, dn_conv_w, dn_a_log, dn_dt_bias, dn_norm_g):
    y_prompt, y_sample = x_prompt, x_sample
    new_prompt, new_sample = [], []
    for l in range(DEPTH):
        lw = (w_in[l], w_out[l], w_up[l], w_down[l], ln1_g[l], ln1_b[l], ln2_g[l], ln2_b[l],
              mlstm_gate_b[l], mlstm_norm_g[l], gla_gate_up[l], gla_gate_b[l], gla_norm_g[l],
              rwkv_mu[l], rwkv_w_up[l], rwkv_w0[l], rwkv_a_up[l], rwkv_a0[l], rwkv_g_up[l],
              rwkv_k_k[l], rwkv_k_a[l], rwkv_r_k[l], rwkv_norm_g[l], rwkv_norm_b[l],
              dn_conv_w[l], dn_a_log[l], dn_dt_bias[l], dn_norm_g[l])
        y_prompt, sp = _layer(y_prompt, _fresh_state(x_prompt.shape[0]), lw)
        y_sample, ss = _layer(y_sample, (state_mlstm_c[l], state_mlstm_n[l], state_mlstm_m[l], state_gla[l],
                                         state_rwkv[l], state_rwkv_shift[l], state_dn[l], state_dn_conv[l]), lw)
        new_prompt.append(sp)
        new_sample.append(ss)
    (mlstm_c_p, mlstm_n_p, mlstm_m_p, gla_p, rwkv_p, rwkv_shift_p, dn_p, dn_conv_p) = [
        jnp.stack(z) for z in zip(*new_prompt)]
    (mlstm_c_s, mlstm_n_s, mlstm_m_s, gla_s, rwkv_s, rwkv_shift_s, dn_s, dn_conv_s) = [
        jnp.stack(z) for z in zip(*new_sample)]
    return (y_prompt, y_sample, mlstm_c_p, mlstm_c_s, mlstm_n_p, mlstm_n_s, mlstm_m_p, mlstm_m_s,
            gla_p, gla_s, rwkv_p, rwkv_s, rwkv_shift_p, rwkv_shift_s, dn_p, dn_s, dn_conv_p, dn_conv_s)
```

```python
import functools

import jax
import jax.numpy as jnp
from jax import lax
from jax.experimental import pallas as pl
from jax.experimental.pallas import tpu as pltpu

f32 = jnp.float32
bf16 = jnp.bfloat16
_HI = lax.Precision.HIGHEST

D_MODEL = 2048
DEPTH = 4
GROUP = 512
D_FF = 4 * D_MODEL
CHUNK = 64
ML_H, ML_DK, ML_DV = 4, 64, 128
GLA_H, GLA_DK, GLA_DV, GLA_RANK, GLA_TAU = 4, 64, 128, 16, 16.0
RW_H, RW_N = 8, 64
RW_DECAY_SCALE = 0.606531
RW_GN_EPS = 64e-5
DN_H, DN_D, DN_CONV, DN_QKV = 4, 128, 4, 1536
N_ML, N_GLA, N_RW, N_DN = 1544, 1552, 1664, 2056
W_ML, W_GLA, W_RW, W_DN = 1664, 1664, 1664, 2176
ALPHA = (2 * DEPTH) ** 0.25
LN_EPS = 1e-5
NORM_EPS = 1e-6
LANES = 128
DEC_BLOCK = 8
VMEM_LIMIT = 56 * 1024 * 1024


def _ein(a, b):
    return jnp.dot(a.astype(bf16), b.astype(bf16), preferred_element_type=f32)


def _ein_nt(a, b):
    return lax.dot_general(a.astype(bf16), b.astype(bf16), (((1,), (1,)), ((), ())), preferred_element_type=f32)


def _ein_tn(a, b):
    return lax.dot_general(a.astype(bf16), b.astype(bf16), (((0,), (0,)), ((), ())), preferred_element_type=f32)


def _exact(a, b):
    return jnp.dot(a, b, precision=_HI, preferred_element_type=f32)


def _exact_nt(a, b):
    return lax.dot_general(a, b, (((1,), (1,)), ((), ())), precision=_HI, preferred_element_type=f32)


def _log_sigmoid(x):
    return jnp.minimum(x, 0.0) - jnp.log1p(jnp.exp(-jnp.abs(x)))


def _softplus(x):
    return jnp.maximum(x, 0.0) + jnp.log1p(jnp.exp(-jnp.abs(x)))


def _silu(x):
    return x * jax.nn.sigmoid(x)


def _iota(shape, axis):
    return lax.broadcasted_iota(jnp.int32, shape, axis)


def _col(row):
    n = row.shape[1]
    eye = _iota((n, n), 0) == _iota((n, n), 1)
    return jnp.sum(jnp.where(eye, row, 0.0), axis=1, keepdims=True)


def _tri_masks(n):
    r, c = _iota((n, n), 0), _iota((n, n), 1)
    return c <= r, c < r


def _unit_lower_inverse(a_strict):
    n = a_strict.shape[0]
    x = -a_strict
    p = jnp.where(_iota((n, n), 0) == _iota((n, n), 1), 1.0, 0.0) + x
    steps = max(n.bit_length() - 2, 0)
    for _ in range(steps):
        x = _exact(x, x)
        p = p + _exact(p, x)
    return p


def _rows_of(y):
    e8 = jnp.where(_iota((8, LANES), 0) == _iota((8, LANES), 1), 1.0, 0.0)
    return _exact_nt(e8, y)


def _cparams(sem):
    return pltpu.CompilerParams(dimension_semantics=sem, vmem_limit_bytes=VMEM_LIMIT)


def _proj_body(x_ref, w_ref, o_ref):
    o_ref[...] = jnp.dot(x_ref[...], w_ref[...], preferred_element_type=f32)


def _proj(xb, w, tm):
    m, k = xb.shape
    n = w.shape[1]
    return pl.pallas_call(
        _proj_body, grid=(m // tm,),
        in_specs=[pl.BlockSpec((tm, k), lambda i: (i, 0)), pl.BlockSpec((k, n), lambda i: (0, 0))],
        out_specs=pl.BlockSpec((tm, n), lambda i: (i, 0)),
        out_shape=jax.ShapeDtypeStruct((m, n), f32),
        compiler_params=_cparams(("parallel",)), name="proj")(xb, w)


def _layernorm(y, g, b):
    mu = jnp.mean(y, -1, keepdims=True)
    yc = y - mu
    var = jnp.mean(yc * yc, -1, keepdims=True)
    return yc * lax.rsqrt(var + LN_EPS) * g + b


def _outproj_ln_body(h0, h1, h2, h3, w_ref, x_ref, g_ref, b_ref, o_ref, ob_ref):
    acc = ALPHA * x_ref[...]
    for j, h in enumerate((h0, h1, h2, h3)):
        acc = acc + jnp.dot(h[...], w_ref[j * GROUP:(j + 1) * GROUP, :], preferred_element_type=f32)
    y = _layernorm(acc, g_ref[...], b_ref[...])
    o_ref[...] = y
    ob_ref[...] = y.astype(bf16)


def _outproj_ln(hs, w, x, g, b, tm):
    m = x.shape[0]
    hspec = pl.BlockSpec((tm, GROUP), lambda i: (i, 0))
    xspec = pl.BlockSpec((tm, D_MODEL), lambda i: (i, 0))
    vspec = pl.BlockSpec((1, D_MODEL), lambda i: (0, 0))
    return pl.pallas_call(
        _outproj_ln_body, grid=(m // tm,),
        in_specs=[hspec] * 4 + [pl.BlockSpec((D_MODEL, D_MODEL), lambda i: (0, 0)), xspec, vspec, vspec],
        out_specs=[xspec, xspec],
        out_shape=[jax.ShapeDtypeStruct((m, D_MODEL), f32), jax.ShapeDtypeStruct((m, D_MODEL), bf16)],
        compiler_params=_cparams(("parallel",)), name="outproj_ln")(*hs, w, x, g, b)


def _ffn_ln_body(x_ref, xb_ref, wu_ref, wd_ref, g_ref, b_ref, o_ref, ob_ref, acc_ref):
    f = pl.program_id(1)

    @pl.when(f == 0)
    def _():
        acc_ref[...] = ALPHA * x_ref[...]

    h = jnp.maximum(jnp.dot(xb_ref[...], wu_ref[...], preferred_element_type=f32), 0.0)
    acc_ref[...] += jnp.dot((h * h).astype(bf16), wd_ref[...], preferred_element_type=f32)

    @pl.when(f == pl.num_programs(1) - 1)
    def _():
        y = _layernorm(acc_ref[...], g_ref[...], b_ref[...])
        o_ref[...] = y
        ob_ref[...] = y.astype(bf16)


def _ffn_ln(x, xb, wu, wd, g, b, tm, tf):
    m = x.shape[0]
    xspec = pl.BlockSpec((tm, D_MODEL), lambda i, f: (i, 0))
    vspec = pl.BlockSpec((1, D_MODEL), lambda i, f: (0, 0))
    return pl.pallas_call(
        _ffn_ln_body, grid=(m // tm, D_FF // tf),
        in_specs=[xspec, xspec, pl.BlockSpec((D_MODEL, tf), lambda i, f: (0, f)),
                  pl.BlockSpec((tf, D_MODEL), lambda i, f: (f, 0)), vspec, vspec],
        out_specs=[xspec, xspec],
        out_shape=[jax.ShapeDtypeStruct((m, D_MODEL), f32), jax.ShapeDtypeStruct((m, D_MODEL), bf16)],
        scratch_shapes=[pltpu.VMEM((tm, D_MODEL), f32)],
        compiler_params=_cparams(("parallel", "arbitrary")), name="ffn_ln")(x, xb, wu, wd, g, b)


def _mlstm_prompt_body(cols_ref, gb_ref, ng_ref, h_ref, c_ref, n_ref, m_ref):
    @pl.when(pl.program_id(1) == 0)
    def _():
        c_ref[...] = jnp.zeros_like(c_ref)
        n_ref[...] = jnp.zeros_like(n_ref)
        m_ref[...] = jnp.zeros_like(m_ref)

    L = CHUNK
    causal, _ = _tri_masks(L)
    tril = jnp.where(causal, 1.0, 0.0)
    g = cols_ref[:, 1536:1664] + gb_ref[...]
    bc = _exact(tril, _log_sigmoid(g))
    y = jnp.where(_iota((L, LANES), 1) < ML_H, g, bc)
    yt = _rows_of(y)
    for h in range(ML_H):
        q = cols_ref[:, h * ML_DK:(h + 1) * ML_DK]
        k = cols_ref[:, 256 + h * ML_DK:256 + (h + 1) * ML_DK] * ML_DK ** -0.5
        v = cols_ref[:, 512 + h * ML_DV:512 + (h + 1) * ML_DV]
        og = cols_ref[:, 1024 + h * ML_DV:1024 + (h + 1) * ML_DV]
        i_col, b_col = y[:, h:h + 1], y[:, ML_H + h:ML_H + h + 1]
        i_row, b_row = yt[h:h + 1, :], yt[ML_H + h:ML_H + h + 1, :]
        m = m_ref[0, 0:1, h:h + 1]
        d = jnp.where(causal, b_col - b_row + i_row, -jnp.inf)
        inter = b_col + m
        m_t = jnp.maximum(inter, jnp.max(d, axis=-1, keepdims=True))
        w_inter = jnp.exp(inter - m_t)
        s = _ein_nt(q, k) * jnp.exp(d - m_t)
        c = c_ref[0, h]
        n = n_ref[0, 0:1, h * ML_DK:(h + 1) * ML_DK]
        num = w_inter * _ein(q, c) + _ein(s, v)
        den = w_inter * jnp.sum(q * n, -1, keepdims=True) + jnp.sum(s, -1, keepdims=True)
        hh = num / jnp.maximum(jnp.abs(den), jnp.exp(-m_t))
        m_new = m_t[L - 1:L, :]
        b_last = b_col[L - 1:L, :]
        f_state = jnp.exp(b_last + m - m_new)
        kw = k * jnp.exp(b_last - b_col + i_col - m_new)
        c_ref[0, h] = f_state * c + _ein_tn(kw, v)
        n_ref[0, 0:1, h * ML_DK:(h + 1) * ML_DK] = f_state * n + jnp.sum(kw, axis=0, keepdims=True)
        m_ref[0, 0:1, h:h + 1] = m_new
        hn = hh * lax.rsqrt(jnp.mean(hh * hh, -1, keepdims=True) + NORM_EPS) * ng_ref[:, h * ML_DV:(h + 1) * ML_DV]
        h_ref[:, h * ML_DV:(h + 1) * ML_DV] = (hn * jax.nn.sigmoid(og)).astype(h_ref.dtype)


def _chunk_grid_specs(bp, t, width):
    nc = t // CHUNK
    cols = pl.BlockSpec((CHUNK, width), lambda b, c: (b * nc + c, 0))
    out = pl.BlockSpec((CHUNK, GROUP), lambda b, c: (b * nc + c, 0))
    return (bp, nc), cols, out


def _const_spec(shape):
    return pl.BlockSpec(shape, lambda b, c: (0,) * len(shape))


def _mlstm_prompt(cols, gb, ng, bp, t):
    grid, cspec, ospec = _chunk_grid_specs(bp, t, W_ML)
    return pl.pallas_call(
        _mlstm_prompt_body, grid=grid,
        in_specs=[cspec, _const_spec((1, LANES)), _const_spec((1, GROUP))],
        out_specs=[ospec, pl.BlockSpec((1, ML_H, ML_DK, ML_DV), lambda b, c: (b, 0, 0, 0)),
                   pl.BlockSpec((1, 1, ML_H * ML_DK), lambda b, c: (b, 0, 0)),
                   pl.BlockSpec((1, 1, LANES), lambda b, c: (b, 0, 0))],
        out_shape=[jax.ShapeDtypeStruct((bp * t, GROUP), bf16), jax.ShapeDtypeStruct((bp, ML_H, ML_DK, ML_DV), f32),
                   jax.ShapeDtypeStruct((bp, 1, ML_H * ML_DK), f32), jax.ShapeDtypeStruct((bp, 1, LANES), f32)],
        compiler_params=_cparams(("parallel", "arbitrary")), name="mlstm_prompt")(cols, gb, ng)


def _gla_prompt_body(cols_ref, gup_ref, gb_ref, ng_ref, h_ref, s_ref):
    @pl.when(pl.program_id(1) == 0)
    def _():
        s_ref[...] = jnp.zeros_like(s_ref)

    L = CHUNK
    causal, _ = _tri_masks(L)
    tril = jnp.where(causal, 1.0, 0.0)
    la = _log_sigmoid(_ein(cols_ref[:, 1536:1664], gup_ref[...]) + gb_ref[...]) / GLA_TAU
    bc = _exact(tril, la)
    mid = bc[L // 2:L // 2 + 1, :]
    last = bc[L - 1:L, :]
    e_q_mid, e_k_mid = jnp.exp(bc - mid), jnp.exp(mid - bc)
    e_q, e_k_last, e_last = jnp.exp(bc), jnp.exp(last - bc), jnp.exp(last)
    for h in range(GLA_H):
        sl = slice(h * GLA_DK, (h + 1) * GLA_DK)
        q = cols_ref[:, h * GLA_DK:(h + 1) * GLA_DK] * GLA_DK ** -0.5
        k = cols_ref[:, 256 + h * GLA_DK:256 + (h + 1) * GLA_DK]
        v = cols_ref[:, 512 + h * GLA_DV:512 + (h + 1) * GLA_DV]
        og = cols_ref[:, 1024 + h * GLA_DV:1024 + (h + 1) * GLA_DV]
        a = jnp.where(causal, _ein_nt(q * e_q_mid[:, sl], k * e_k_mid[:, sl]), 0.0)
        s = s_ref[0, h]
        o = _ein(q * e_q[:, sl], s) + _ein(a, v)
        s_ref[0, h] = _col(e_last[:, sl]) * s + _ein_tn(k * e_k_last[:, sl], v)
        on = o * lax.rsqrt(jnp.mean(o * o, -1, keepdims=True) + NORM_EPS) * ng_ref[:, h * GLA_DV:(h + 1) * GLA_DV]
        h_ref[:, h * GLA_DV:(h + 1) * GLA_DV] = (on * _silu(og)).astype(h_ref.dtype)


def _gla_prompt(cols, gup, gb, ng, bp, t):
    grid, cspec, ospec = _chunk_grid_specs(bp, t, W_GLA)
    return pl.pallas_call(
        _gla_prompt_body, grid=grid,
        in_specs=[cspec, _const_spec((LANES, GLA_H * GLA_DK)), _const_spec((1, GLA_H * GLA_DK)), _const_spec((1, GROUP))],
        out_specs=[ospec, pl.BlockSpec((1, GLA_H, GLA_DK, GLA_DV), lambda b, c: (b, 0, 0, 0))],
        out_shape=[jax.ShapeDtypeStruct((bp * t, GROUP), bf16), jax.ShapeDtypeStruct((bp, GLA_H, GLA_DK, GLA_DV), f32)],
        compiler_params=_cparams(("parallel", "arbitrary")), name="gla_prompt")(cols, gup, gb, ng)


def _rwkv_pre(x, prev, mu, wup, aup, gup, vec):
    xs = x + (prev - x) * mu
    r, k, v, lb = xs[:, 0:512], xs[:, 512:1024], xs[:, 1024:1536], xs[:, 1536:1664]
    w0, a0, k_k, k_a = vec[0:1, :], vec[1:2, :], vec[2:3, :], vec[3:4, :]
    lw = -RW_DECAY_SCALE * jax.nn.sigmoid(w0 + _ein(jnp.tanh(lb), wup))
    a = jax.nn.sigmoid(a0 + _ein(lb, aup))
    g = _ein(jax.nn.sigmoid(lb), gup)
    kk_raw = k * k_k
    k2 = k * (1.0 + (a - 1.0) * k_a)
    return r, k2, v, lw, a, g, kk_raw


def _rwkv_post(o, r, k2, v, g, vec, sl):
    r_k, n_g, n_b = vec[4:5, sl], vec[5:6, sl], vec[6:7, sl]
    mu_o = jnp.mean(o, -1, keepdims=True)
    oc = o - mu_o
    var_o = jnp.mean(oc * oc, -1, keepdims=True)
    on = oc * lax.rsqrt(var_o + RW_GN_EPS) * n_g + n_b
    bonus = jnp.sum(r[:, sl] * k2[:, sl] * r_k, -1, keepdims=True) * v[:, sl]
    return (on + bonus) * g[:, sl]


def _rwkv_prompt_body(cols_ref, mu_ref, wup_ref, aup_ref, gup_ref, vec_ref, h_ref, s_ref, prev_scr):
    @pl.when(pl.program_id(1) == 0)
    def _():
        s_ref[...] = jnp.zeros_like(s_ref)
        prev_scr[...] = jnp.zeros_like(prev_scr)

    L = CHUNK
    causal, strict = _tri_masks(L)
    tril = jnp.where(causal, 1.0, 0.0)
    x = cols_ref[...]
    prev = jnp.where(_iota(x.shape, 0) == 0, prev_scr[0:1, :], pltpu.roll(x, 1, 0))
    prev_scr[0:1, :] = x[L - 1:L, :]
    vec = vec_ref[...]
    r, k2, v, lw, a, g, kk_raw = _rwkv_pre(x, prev, mu_ref[...], wup_ref[...], aup_ref[...], gup_ref[...], vec)
    cum = _exact(tril, lw)
    cum_prev = cum - lw
    mid = cum[L // 2:L // 2 + 1, :]
    last = cum[L - 1:L, :]
    e_prev_mid, e_mid_cum, e_cum_mid = jnp.exp(cum_prev - mid), jnp.exp(mid - cum), jnp.exp(cum - mid)
    e_prev, e_cum, e_last_cum, e_last = jnp.exp(cum_prev), jnp.exp(cum), jnp.exp(last - cum), jnp.exp(last)
    outs = []
    for h in range(RW_H):
        sl = slice(h * RW_N, (h + 1) * RW_N)
        kk = kk_raw[:, sl]
        kk = kk * lax.rsqrt(jnp.sum(kk * kk, -1, keepdims=True) + NORM_EPS)
        b = kk * a[:, sl]
        kt, rt = kk * e_prev_mid[:, sl], r[:, sl] * e_cum_mid[:, sl]
        bh, kh = b * e_mid_cum[:, sl], k2[:, sl] * e_mid_cum[:, sl]
        a_bb = jnp.where(strict, _ein_nt(kt, bh), 0.0)
        a_bk = jnp.where(strict, _ein_nt(kt, kh), 0.0)
        a_rb = jnp.where(causal, _ein_nt(rt, bh), 0.0)
        a_rk = jnp.where(causal, _ein_nt(rt, kh), 0.0)
        s = s_ref[0, h]
        vh = v[:, sl]
        u = -_exact(_unit_lower_inverse(a_bb), _ein(kk * e_prev[:, sl], s) + _ein(a_bk, vh))
        o = _ein(r[:, sl] * e_cum[:, sl], s) + _ein(a_rb, u) + _ein(a_rk, vh)
        s_ref[0, h] = (_col(e_last[:, sl]) * s + _ein_tn(b * e_last_cum[:, sl], u)
                       + _ein_tn(k2[:, sl] * e_last_cum[:, sl], vh))
        outs.append(_rwkv_post(o, r, k2, v, g, vec, sl))
    h_ref[...] = jnp.concatenate(outs, axis=-1).astype(h_ref.dtype)


def _rwkv_prompt(cols, mu, wup, aup, gup, vec, bp, t):
    grid, cspec, ospec = _chunk_grid_specs(bp, t, W_RW)
    return pl.pallas_call(
        _rwkv_prompt_body, grid=grid,
        in_specs=[cspec, _const_spec((1, W_RW)), _const_spec((LANES, GROUP)), _const_spec((LANES, GROUP)),
                  _const_spec((LANES, GROUP)), _const_spec((8, GROUP))],
        out_specs=[ospec, pl.BlockSpec((1, RW_H, RW_N, RW_N), lambda b, c: (b, 0, 0, 0))],
        out_shape=[jax.ShapeDtypeStruct((bp * t, GROUP), bf16), jax.ShapeDtypeStruct((bp, RW_H, RW_N, RW_N), f32)],
        scratch_shapes=[pltpu.VMEM((8, W_RW), f32)],
        compiler_params=_cparams(("parallel", "arbitrary")), name="rwkv_prompt")(cols, mu, wup, aup, gup, vec)


def _gdn_gates(gb, alog, dtb):
    return jax.nn.sigmoid(gb), -jnp.exp(alog) * _softplus(gb + dtb)


def _l2norm(x):
    return x * lax.rsqrt(jnp.sum(x * x, -1, keepdims=True) + NORM_EPS)


def _gdn_prompt_body(cols_ref, cw_ref, alog_ref, dtb_ref, ng_ref, h_ref, s_ref, xp_scr):
    @pl.when(pl.program_id(1) == 0)
    def _():
        s_ref[...] = jnp.zeros_like(s_ref)
        xp_scr[0:8, :] = jnp.zeros((8, DN_QKV), f32)

    L = CHUNK
    causal, strict = _tri_masks(L)
    tril = jnp.where(causal, 1.0, 0.0)
    raw = cols_ref[:, 0:DN_QKV]
    xp_scr[8:8 + L, :] = raw
    conv = xp_scr[5:5 + L, :] * cw_ref[0:1, :]
    for w in range(1, DN_CONV):
        conv = conv + xp_scr[5 + w:5 + w + L, :] * cw_ref[w:w + 1, :]
    xp_scr[0:8, :] = raw[L - 8:L, :]
    act = _silu(conv)
    beta_all, g_all = _gdn_gates(cols_ref[:, 2048:2176], alog_ref[...], dtb_ref[...])
    gc = _exact(tril, g_all)
    gt = _rows_of(gc)
    for h in range(DN_H):
        q = _l2norm(act[:, h * DN_D:(h + 1) * DN_D]) * DN_D ** -0.5
        k = _l2norm(act[:, GROUP + h * DN_D:GROUP + (h + 1) * DN_D])
        v = act[:, 2 * GROUP + h * DN_D:2 * GROUP + (h + 1) * DN_D]
        z = cols_ref[:, DN_QKV + h * DN_D:DN_QKV + (h + 1) * DN_D]
        beta = beta_all[:, h:h + 1]
        g_col, g_row = gc[:, DN_H + h:DN_H + h + 1], gt[DN_H + h:DN_H + h + 1, :]
        decay = jnp.exp(jnp.where(causal, g_col - g_row, -jnp.inf))
        kb = k * beta
        t_inv = _unit_lower_inverse(jnp.where(strict, _ein_nt(kb, k) * decay, 0.0))
        eg = jnp.exp(g_col)
        u, wy = _exact(t_inv, v * beta), _exact(t_inv, kb * eg)
        s = s_ref[0, h]
        v_new = u - _ein(wy, s)
        o = _ein(q * eg, s) + _ein(_ein_nt(q, k) * decay, v_new)
        g_last = g_col[L - 1:L, :]
        s_ref[0, h] = jnp.exp(g_last) * s + _ein_tn(k * jnp.exp(g_last - g_col), v_new)
        on = o * lax.rsqrt(jnp.mean(o * o, -1, keepdims=True) + NORM_EPS) * ng_ref[...]
        h_ref[:, h * DN_D:(h + 1) * DN_D] = (on * _silu(z)).astype(h_ref.dtype)


def _gdn_prompt(cols, cw, alog, dtb, ng, bp, t):
    grid, cspec, ospec = _chunk_grid_specs(bp, t, W_DN)
    return pl.pallas_call(
        _gdn_prompt_body, grid=grid,
        in_specs=[cspec, _const_spec((8, DN_QKV)), _const_spec((1, LANES)), _const_spec((1, LANES)), _const_spec((1, DN_D))],
        out_specs=[ospec, pl.BlockSpec((1, DN_H, DN_D, DN_D), lambda b, c: (b, 0, 0, 0))],
        out_shape=[jax.ShapeDtypeStruct((bp * t, GROUP), bf16), jax.ShapeDtypeStruct((bp, DN_H, DN_D, DN_D), f32)],
        scratch_shapes=[pltpu.VMEM((8 + CHUNK, DN_QKV), f32)],
        compiler_params=_cparams(("parallel", "arbitrary")), name="gdn_prompt")(cols, cw, alog, dtb, ng)


def _mlstm_step_body(cols_ref, gb_ref, ng_ref, c_ref, n_ref, m_ref, h_ref, co_ref, no_ref, mo_ref,
                     q_scr, kw_scr, v_scr, f_scr, qc_scr):
    nb = DEC_BLOCK
    g = cols_ref[:, 1536:1664] + gb_ref[...]
    lf = _log_sigmoid(g)
    hs = []
    for h in range(ML_H):
        q = cols_ref[:, h * ML_DK:(h + 1) * ML_DK]
        k = cols_ref[:, 256 + h * ML_DK:256 + (h + 1) * ML_DK] * ML_DK ** -0.5
        v = cols_ref[:, 512 + h * ML_DV:512 + (h + 1) * ML_DV]
        i_pre, f_log = g[:, h:h + 1], lf[:, ML_H + h:ML_H + h + 1]
        m = m_ref[:, h:h + 1]
        inter = f_log + m
        m_t = jnp.maximum(inter, i_pre)
        w_inter = jnp.exp(inter - m_t)
        kw = k * jnp.exp(i_pre - m_t)
        q_scr[h], kw_scr[h], v_scr[h] = q, kw, v
        f_scr[h] = jnp.broadcast_to(w_inter, (nb, LANES))
        hs.append((q, k, v, i_pre, m_t, w_inter, kw))

    def row(b, carry):
        for h in range(ML_H):
            c = c_ref[b, h]
            q_col = _col(q_scr[h, pl.ds(b, 1), :])
            kw_col = _col(kw_scr[h, pl.ds(b, 1), :])
            qc_scr[h, pl.ds(b, 1), :] = jnp.sum(q_col * c, axis=0, keepdims=True)
            co_ref[b, h] = f_scr[h, pl.ds(b, 1), :] * c + kw_col * v_scr[h, pl.ds(b, 1), :]
        return carry

    lax.fori_loop(0, nb, row, 0)
    for h, (q, k, v, i_pre, m_t, w_inter, kw) in enumerate(hs):
        og = cols_ref[:, 1024 + h * ML_DV:1024 + (h + 1) * ML_DV]
        n = n_ref[:, h * ML_DK:(h + 1) * ML_DK]
        s = jnp.sum(q * k, -1, keepdims=True) * jnp.exp(i_pre - m_t)
        num = w_inter * qc_scr[h] + s * v
        den = w_inter * jnp.sum(q * n, -1, keepdims=True) + s
        hh = num / jnp.maximum(jnp.abs(den), jnp.exp(-m_t))
        no_ref[:, h * ML_DK:(h + 1) * ML_DK] = w_inter * n + kw
        mo_ref[:, h:h + 1] = m_t
        hn = hh * lax.rsqrt(jnp.mean(hh * hh, -1, keepdims=True) + NORM_EPS) * ng_ref[:, h * ML_DV:(h + 1) * ML_DV]
        h_ref[:, h * ML_DV:(h + 1) * ML_DV] = (hn * jax.nn.sigmoid(og)).astype(h_ref.dtype)


def _row_spec(width):
    return pl.BlockSpec((DEC_BLOCK, width), lambda i: (i, 0))


def _state_spec(h, dk, dv):
    return pl.BlockSpec((DEC_BLOCK, h, dk, dv), lambda i: (i, 0, 0, 0))


def _vec_spec(shape):
    return pl.BlockSpec(shape, lambda i: (0,) * len(shape))


def _mlstm_step(cols, gb, ng, c0, n0, m0):
    bs = cols.shape[0]
    nb = DEC_BLOCK
    return pl.pallas_call(
        _mlstm_step_body, grid=(bs // nb,),
        in_specs=[_row_spec(W_ML), _vec_spec((1, LANES)), _vec_spec((1, GROUP)), _state_spec(ML_H, ML_DK, ML_DV),
                  _row_spec(ML_H * ML_DK), _row_spec(ML_H)],
        out_specs=[_row_spec(GROUP), _state_spec(ML_H, ML_DK, ML_DV), _row_spec(ML_H * ML_DK), _row_spec(ML_H)],
        out_shape=[jax.ShapeDtypeStruct((bs, GROUP), bf16), jax.ShapeDtypeStruct(c0.shape, f32),
                   jax.ShapeDtypeStruct(n0.shape, f32), jax.ShapeDtypeStruct(m0.shape, f32)],
        scratch_shapes=[pltpu.VMEM((ML_H, nb, ML_DK), f32), pltpu.VMEM((ML_H, nb, ML_DK), f32),
                        pltpu.VMEM((ML_H, nb, ML_DV), f32), pltpu.VMEM((ML_H, nb, LANES), f32),
                        pltpu.VMEM((ML_H, nb, ML_DV), f32)],
        compiler_params=_cparams(("parallel",)), name="mlstm_step")(cols, gb, ng, c0, n0, m0)


def _gla_step_body(cols_ref, gup_ref, gb_ref, ng_ref, s_ref, h_ref, so_ref, qe_scr, ea_scr, k_scr, v_scr, qs_scr):
    nb = DEC_BLOCK
    la = _log_sigmoid(_ein(cols_ref[:, 1536:1664], gup_ref[...]) + gb_ref[...]) / GLA_TAU
    ea = jnp.exp(la)
    hs = []
    for h in range(GLA_H):
        q = cols_ref[:, h * GLA_DK:(h + 1) * GLA_DK] * GLA_DK ** -0.5
        k = cols_ref[:, 256 + h * GLA_DK:256 + (h + 1) * GLA_DK]
        v = cols_ref[:, 512 + h * GLA_DV:512 + (h + 1) * GLA_DV]
        ea_h = ea[:, h * GLA_DK:(h + 1) * GLA_DK]
        qe_scr[h], ea_scr[h], k_scr[h], v_scr[h] = q * ea_h, ea_h, k, v
        hs.append((q, k, v))

    def row(b, carry):
        for h in range(GLA_H):
            s = s_ref[b, h]
            qe_col = _col(qe_scr[h, pl.ds(b, 1), :])
            ea_col = _col(ea_scr[h, pl.ds(b, 1), :])
            k_col = _col(k_scr[h, pl.ds(b, 1), :])
            qs_scr[h, pl.ds(b, 1), :] = jnp.sum(qe_col * s, axis=0, keepdims=True)
            so_ref[b, h] = ea_col * s + k_col * v_scr[h, pl.ds(b, 1), :]
        return carry

    lax.fori_loop(0, nb, row, 0)
    for h, (q, k, v) in enumerate(hs):
        og = cols_ref[:, 1024 + h * GLA_DV:1024 + (h + 1) * GLA_DV]
        o = qs_scr[h] + jnp.sum(q * k, -1, keepdims=True) * v
        on = o * lax.rsqrt(jnp.mean(o * o, -1, keepdims=True) + NORM_EPS) * ng_ref[:, h * GLA_DV:(h + 1) * GLA_DV]
        h_ref[:, h * GLA_DV:(h + 1) * GLA_DV] = (on * _silu(og)).astype(h_ref.dtype)


def _gla_step(cols, gup, gb, ng, s0):
    bs = cols.shape[0]
    nb = DEC_BLOCK
    return pl.pallas_call(
        _gla_step_body, grid=(bs // nb,),
        in_specs=[_row_spec(W_GLA), _vec_spec((LANES, GLA_H * GLA_DK)), _vec_spec((1, GLA_H * GLA_DK)),
                  _vec_spec((1, GROUP)), _state_spec(GLA_H, GLA_DK, GLA_DV)],
        out_specs=[_row_spec(GROUP), _state_spec(GLA_H, GLA_DK, GLA_DV)],
        out_shape=[jax.ShapeDtypeStruct((bs, GROUP), bf16), jax.ShapeDtypeStruct(s0.shape, f32)],
        scratch_shapes=[pltpu.VMEM((GLA_H, nb, GLA_DK), f32)] * 3 + [pltpu.VMEM((GLA_H, nb, GLA_DV), f32)] * 2,
        compiler_params=_cparams(("parallel",)), name="gla_step")(cols, gup, gb, ng, s0)


def _rwkv_step_body(cols_ref, shift_ref, mu_ref, wup_ref, aup_ref, gup_ref, vec_ref, s_ref, h_ref, so_ref,
                    kk_scr, w_scr, b_scr, k_scr, r_scr, v_scr, o_scr):
    nb = DEC_BLOCK
    vec = vec_ref[...]
    r, k2, v, lw, a, g, kk_raw = _rwkv_pre(cols_ref[...], shift_ref[...], mu_ref[...], wup_ref[...], aup_ref[...],
                                           gup_ref[...], vec)
    w = jnp.exp(lw)
    for h in range(RW_H):
        sl = slice(h * RW_N, (h + 1) * RW_N)
        kk = _l2norm(kk_raw[:, sl])
        kk_scr[h], w_scr[h], b_scr[h], k_scr[h], r_scr[h], v_scr[h] = kk, w[:, sl], kk * a[:, sl], k2[:, sl], r[:, sl], v[:, sl]

    def row(b, carry):
        for h in range(RW_H):
            s = s_ref[b, h]
            kk_col = _col(kk_scr[h, pl.ds(b, 1), :])
            sk = jnp.sum(kk_col * s, axis=0, keepdims=True)
            s_new = (_col(w_scr[h, pl.ds(b, 1), :]) * s - _col(b_scr[h, pl.ds(b, 1), :]) * sk
                     + _col(k_scr[h, pl.ds(b, 1), :]) * v_scr[h, pl.ds(b, 1), :])
            so_ref[b, h] = s_new
            o_scr[h, pl.ds(b, 1), :] = jnp.sum(_col(r_scr[h, pl.ds(b, 1), :]) * s_new, axis=0, keepdims=True)
        return carry

    lax.fori_loop(0, nb, row, 0)
    outs = [_rwkv_post(o_scr[h], r, k2, v, g, vec, slice(h * RW_N, (h + 1) * RW_N)) for h in range(RW_H)]
    h_ref[...] = jnp.concatenate(outs, axis=-1).astype(h_ref.dtype)


def _rwkv_step(cols, shift0, mu, wup, aup, gup, vec, s0):
    bs = cols.shape[0]
    nb = DEC_BLOCK
    return pl.pallas_call(
        _rwkv_step_body, grid=(bs // nb,),
        in_specs=[_row_spec(W_RW), _row_spec(W_RW), _vec_spec((1, W_RW)), _vec_spec((LANES, GROUP)),
                  _vec_spec((LANES, GROUP)), _vec_spec((LANES, GROUP)), _vec_spec((8, GROUP)), _state_spec(RW_H, RW_N, RW_N)],
        out_specs=[_row_spec(GROUP), _state_spec(RW_H, RW_N, RW_N)],
        out_shape=[jax.ShapeDtypeStruct((bs, GROUP), bf16), jax.ShapeDtypeStruct(s0.shape, f32)],
        scratch_shapes=[pltpu.VMEM((RW_H, nb, RW_N), f32)] * 7,
        compiler_params=_cparams(("parallel",)), name="rwkv_step")(cols, shift0, mu, wup, aup, gup, vec, s0)


def _gdn_step_body(cols_ref, buf_ref, cw_ref, alog_ref, dtb_ref, ng_ref, s_ref, h_ref, so_ref,
                   q_scr, k_scr, v_scr, sc_scr, ks_scr, qs_scr, vn_scr):
    nb = DEC_BLOCK
    conv = cols_ref[:, 0:DN_QKV] * cw_ref[DN_CONV - 1:DN_CONV, :]
    for w in range(DN_CONV - 1):
        conv = conv + buf_ref[:, w * DN_QKV:(w + 1) * DN_QKV] * cw_ref[w:w + 1, :]
    act = _silu(conv)
    beta_all, g_all = _gdn_gates(cols_ref[:, 2048:2176], alog_ref[...], dtb_ref[...])
    eg_all = jnp.exp(g_all)
    hs = []
    for h in range(DN_H):
        q = _l2norm(act[:, h * DN_D:(h + 1) * DN_D]) * DN_D ** -0.5
        k = _l2norm(act[:, GROUP + h * DN_D:GROUP + (h + 1) * DN_D])
        v = act[:, 2 * GROUP + h * DN_D:2 * GROUP + (h + 1) * DN_D]
        beta, eg = beta_all[:, h:h + 1], eg_all[:, DN_H + h:DN_H + h + 1]
        q_scr[h], k_scr[h], v_scr[h] = q, k, v
        sc_scr[h] = jnp.where(_iota((nb, LANES), 1) == 0, beta, eg)
        hs.append((q, k, eg))

    def row(b, carry):
        for h in range(DN_H):
            s = s_ref[b, h]
            k_col = _col(k_scr[h, pl.ds(b, 1), :])
            q_col = _col(q_scr[h, pl.ds(b, 1), :])
            sc = sc_scr[h, pl.ds(b, 1), :]
            beta, eg = sc[:, 0:1], sc[:, 1:2]
            ks = jnp.sum(k_col * s, axis=0, keepdims=True)
            v_new = beta * (v_scr[h, pl.ds(b, 1), :] - eg * ks)
            qs_scr[h, pl.ds(b, 1), :] = jnp.sum(q_col * s, axis=0, keepdims=True)
            vn_scr[h, pl.ds(b, 1), :] = v_new
            so_ref[b, h] = eg * s + k_col * v_new
        return carry

    lax.fori_loop(0, nb, row, 0)
    for h, (q, k, eg) in enumerate(hs):
        z = cols_ref[:, DN_QKV + h * DN_D:DN_QKV + (h + 1) * DN_D]
        o = eg * qs_scr[h] + jnp.sum(q * k, -1, keepdims=True) * vn_scr[h]
        on = o * lax.rsqrt(jnp.mean(o * o, -1, keepdims=True) + NORM_EPS) * ng_ref[...]
        h_ref[:, h * DN_D:(h + 1) * DN_D] = (on * _silu(z)).astype(h_ref.dtype)


def _gdn_step(cols, buf, cw, alog, dtb, ng, s0):
    bs = cols.shape[0]
    nb = DEC_BLOCK
    return pl.pallas_call(
        _gdn_step_body, grid=(bs // nb,),
        in_specs=[_row_spec(W_DN), _row_spec((DN_CONV - 1) * DN_QKV), _vec_spec((8, DN_QKV)), _vec_spec((1, LANES)),
                  _vec_spec((1, LANES)), _vec_spec((1, DN_D)), _state_spec(DN_H, DN_D, DN_D)],
        out_specs=[_row_spec(GROUP), _state_spec(DN_H, DN_D, DN_D)],
        out_shape=[jax.ShapeDtypeStruct((bs, GROUP), bf16), jax.ShapeDtypeStruct(s0.shape, f32)],
        scratch_shapes=[pltpu.VMEM((DN_H, nb, DN_D), f32)] * 3 + [pltpu.VMEM((DN_H, nb, LANES), f32)]
        + [pltpu.VMEM((DN_H, nb, DN_D), f32)] * 3,
        compiler_params=_cparams(("parallel",)), name="gdn_step")(cols, buf, cw, alog, dtb, ng, s0)


def _pad_cols(w, width):
    return jnp.pad(w, ((0, 0), (0, width - w.shape[1])))


def _pad_rows(w, start, total):
    return jnp.pad(w, ((start, total - start - w.shape[0]), (0, 0)))


def _lane_row(v, start):
    return jnp.pad(v, (start, LANES - start - v.shape[0]))[None, :]


def kernel(x_prompt, x_sample, state_mlstm_c, state_mlstm_n, state_mlstm_m, state_gla, state_rwkv, state_rwkv_shift, state_dn, state_dn_conv, w_in, w_out, w_up, w_down, ln1_g, ln1_b, ln2_g, ln2_b, mlstm_gate_b, mlstm_norm_g, gla_gate_up, gla_gate_b, gla_norm_g, rwkv_mu, rwkv_w_up, rwkv_w0, rwkv_a_up, rwkv_a0, rwkv_g_up, rwkv_k_k, rwkv_k_a, rwkv_r_k, rwkv_norm_g, rwkv_norm_b, dn_conv_w, dn_a_log, dn_dt_bias, dn_norm_g):
    bp, t, _ = x_prompt.shape
    bs = x_sample.shape[0]
    xp = x_prompt.reshape(bp * t, D_MODEL)
    xs = x_sample.reshape(bs, D_MODEL)
    xpb, xsb = xp.astype(bf16), xs.astype(bf16)
    tm_p, tm_s, tf = 512, bs, 512
    outs_p, outs_s = [], []
    for l in range(DEPTH):
        o1, o2, o3 = N_ML, N_ML + N_GLA, N_ML + N_GLA + N_RW
        w_secs = [_pad_cols(w_in[l][:, 0:o1], W_ML).astype(bf16), _pad_cols(w_in[l][:, o1:o2], W_GLA).astype(bf16),
                  w_in[l][:, o2:o3].astype(bf16), _pad_cols(w_in[l][:, o3:], W_DN).astype(bf16)]
        wo, wu, wd = w_out[l].astype(bf16), w_up[l].astype(bf16), w_down[l].astype(bf16)
        g1, b1, g2, b2 = ln1_g[l][None], ln1_b[l][None], ln2_g[l][None], ln2_b[l][None]
        ml_gb, ml_ng = _lane_row(mlstm_gate_b[l], 0), mlstm_norm_g[l][None]
        gla_gup, gla_gb, gla_ng = _pad_rows(gla_gate_up[l], 0, LANES), gla_gate_b[l][None], gla_norm_g[l][None]
        rw_mu = rwkv_mu[l][None]
        rw_wup, rw_aup, rw_gup = (_pad_rows(rwkv_w_up[l], 0, LANES), _pad_rows(rwkv_a_up[l], 32, LANES),
                                  _pad_rows(rwkv_g_up[l], 64, LANES))
        rw_vec = jnp.stack([rwkv_w0[l], rwkv_a0[l], rwkv_k_k[l], rwkv_k_a[l], rwkv_r_k[l], rwkv_norm_g[l],
                            rwkv_norm_b[l], jnp.zeros_like(rwkv_w0[l])])
        dn_cw = _pad_rows(dn_conv_w[l], 0, 8)
        dn_alog, dn_dtb, dn_ng = _lane_row(dn_a_log[l], DN_H), _lane_row(dn_dt_bias[l], DN_H), dn_norm_g[l][None]

        c_ml, c_gla, c_rw, c_dn = [_proj(xpb, w, tm_p) for w in w_secs]
        h_ml, mc, mn, mm = _mlstm_prompt(c_ml, ml_gb, ml_ng, bp, t)
        h_gla, gs = _gla_prompt(c_gla, gla_gup, gla_gb, gla_ng, bp, t)
        h_rw, rs = _rwkv_prompt(c_rw, rw_mu, rw_wup, rw_aup, rw_gup, rw_vec, bp, t)
        h_dn, ds = _gdn_prompt(c_dn, dn_cw, dn_alog, dn_dtb, dn_ng, bp, t)
        x1, x1b = _outproj_ln((h_ml, h_gla, h_rw, h_dn), wo, xp, g1, b1, tm_p)
        xp, xpb = _ffn_ln(x1, x1b, wu, wd, g2, b2, tm_p, tf)
        outs_p.append((mc, mn.reshape(bp, ML_H, ML_DK), mm[:, 0, :ML_H], gs, rs, c_rw.reshape(bp, t, W_RW)[:, -1],
                       ds, c_dn.reshape(bp, t, W_DN)[:, t - (DN_CONV - 1):, :DN_QKV]))

        c_ml, c_gla, c_rw, c_dn = [_proj(xsb, w, tm_s) for w in w_secs]
        h_ml, mc, mn, mm = _mlstm_step(c_ml, ml_gb, ml_ng, state_mlstm_c[l],
                                       state_mlstm_n[l].reshape(bs, ML_H * ML_DK), state_mlstm_m[l])
        h_gla, gs = _gla_step(c_gla, gla_gup, gla_gb, gla_ng, state_gla[l])
        h_rw, rs = _rwkv_step(c_rw, state_rwkv_shift[l], rw_mu, rw_wup, rw_aup, rw_gup, rw_vec, state_rwkv[l])
        h_dn, ds = _gdn_step(c_dn, state_dn_conv[l].reshape(bs, (DN_CONV - 1) * DN_QKV), dn_cw, dn_alog, dn_dtb,
                             dn_ng, state_dn[l])
        x1, x1b = _outproj_ln((h_ml, h_gla, h_rw, h_dn), wo, xs, g1, b1, tm_s)
        xs, xsb = _ffn_ln(x1, x1b, wu, wd, g2, b2, tm_s, tf)
        outs_s.append((mc, mn.reshape(bs, ML_H, ML_DK), mm, gs, rs, c_rw,
                       ds, jnp.concatenate([state_dn_conv[l][:, 1:], c_dn[:, None, :DN_QKV]], axis=1)))

    sp = [jnp.stack(z) for z in zip(*outs_p)]
    ss = [jnp.stack(z) for z in zip(*outs_s)]
    res = [xp.reshape(bp, t, D_MODEL), xs.reshape(bs, 1, D_MODEL)]
    for a, b in zip(sp, ss):
        res += [a, b]
    return tuple(res)
```

```python
import functools

import jax
import jax.numpy as jnp
from jax import lax
from jax.experimental import pallas as pl
from jax.experimental.pallas import tpu as pltpu

f32 = jnp.float32
bf16 = jnp.bfloat16
_HI = lax.Precision.HIGHEST

D_MODEL = 2048
DEPTH = 4
GROUP = 512
D_FF = 4 * D_MODEL
CHUNK = 64
ML_H, ML_DK, ML_DV = 4, 64, 128
GLA_H, GLA_DK, GLA_DV, GLA_RANK, GLA_TAU = 4, 64, 128, 16, 16.0
RW_H, RW_N = 8, 64
RW_DECAY_SCALE = 0.606531
RW_GN_EPS = 64e-5
DN_H, DN_D, DN_CONV, DN_QKV = 4, 128, 4, 1536
N_ML, N_GLA, N_RW, N_DN = 1544, 1552, 1664, 2056
W_ML, W_GLA, W_RW, W_DN = 1664, 1664, 1664, 2176
ALPHA = (2 * DEPTH) ** 0.25
LN_EPS = 1e-5
NORM_EPS = 1e-6
LANES = 128
DEC_BLOCK = 8
VMEM_LIMIT = 56 * 1024 * 1024


def _ein(a, b):
    return jnp.dot(a.astype(bf16), b.astype(bf16), preferred_element_type=f32)


def _ein_nt(a, b):
    return lax.dot_general(a.astype(bf16), b.astype(bf16), (((1,), (1,)), ((), ())), preferred_element_type=f32)


def _ein_tn(a, b):
    return lax.dot_general(a.astype(bf16), b.astype(bf16), (((0,), (0,)), ((), ())), preferred_element_type=f32)


def _exact(a, b):
    return jnp.dot(a, b, precision=_HI, preferred_element_type=f32)


def _exact_nt(a, b):
    return lax.dot_general(a, b, (((1,), (1,)), ((), ())), precision=_HI, preferred_element_type=f32)


def _log_sigmoid(x):
    return jnp.minimum(x, 0.0) - jnp.log1p(jnp.exp(-jnp.abs(x)))


def _softplus(x):
    return jnp.maximum(x, 0.0) + jnp.log1p(jnp.exp(-jnp.abs(x)))


def _silu(x):
    return x * jax.nn.sigmoid(x)


def _iota(shape, axis):
    return lax.broadcasted_iota(jnp.int32, shape, axis)


def _col(row):
    n = row.shape[1]
    eye = _iota((n, n), 0) == _iota((n, n), 1)
    return jnp.sum(jnp.where(eye, row, 0.0), axis=1, keepdims=True)


def _tri_masks(n):
    r, c = _iota((n, n), 0), _iota((n, n), 1)
    return c <= r, c < r


def _split2(a):
    hi = a.astype(bf16)
    return hi, (a - hi.astype(f32)).astype(bf16)


def _dot3(a, b):
    ah, al = _split2(a)
    bh, bl = _split2(b)
    d = functools.partial(jnp.dot, preferred_element_type=f32)
    return d(ah, bh) + d(ah, bl) + d(al, bh)


def _unit_lower_inverses(a_list):
    n = a_list[0].shape[0]
    eye = jnp.where(_iota((n, n), 0) == _iota((n, n), 1), 1.0, 0.0)
    xs = [-a for a in a_list]
    ps = [eye + x for x in xs]
    for _ in range(max(n.bit_length() - 2, 0)):
        xs = [_dot3(x, x) for x in xs]
        ps = [p + _dot3(p, x) for p, x in zip(ps, xs)]
    return ps


def _rows_of(y):
    e8 = jnp.where(_iota((8, LANES), 0) == _iota((8, LANES), 1), 1.0, 0.0)
    return _exact_nt(e8, y)


def _cparams(sem):
    return pltpu.CompilerParams(dimension_semantics=sem, vmem_limit_bytes=VMEM_LIMIT)


def _proj_body(x_ref, w_ref, o_ref):
    o_ref[...] = jnp.dot(x_ref[...], w_ref[...], preferred_element_type=f32)


def _proj(xb, w, tm):
    m, k = xb.shape
    n = w.shape[1]
    return pl.pallas_call(
        _proj_body, grid=(m // tm,),
        in_specs=[pl.BlockSpec((tm, k), lambda i: (i, 0)), pl.BlockSpec((k, n), lambda i: (0, 0))],
        out_specs=pl.BlockSpec((tm, n), lambda i: (i, 0)),
        out_shape=jax.ShapeDtypeStruct((m, n), f32),
        compiler_params=_cparams(("parallel",)), name="proj")(xb, w)


def _layernorm(y, g, b):
    mu = jnp.mean(y, -1, keepdims=True)
    yc = y - mu
    var = jnp.mean(yc * yc, -1, keepdims=True)
    return yc * lax.rsqrt(var + LN_EPS) * g + b


def _outproj_ln_body(h0, h1, h2, h3, w_ref, x_ref, g_ref, b_ref, o_ref, ob_ref):
    acc = ALPHA * x_ref[...]
    for j, h in enumerate((h0, h1, h2, h3)):
        acc = acc + jnp.dot(h[...], w_ref[j * GROUP:(j + 1) * GROUP, :], preferred_element_type=f32)
    y = _layernorm(acc, g_ref[...], b_ref[...])
    o_ref[...] = y
    ob_ref[...] = y.astype(bf16)


def _outproj_ln(hs, w, x, g, b, tm):
    m = x.shape[0]
    hspec = pl.BlockSpec((tm, GROUP), lambda i: (i, 0))
    xspec = pl.BlockSpec((tm, D_MODEL), lambda i: (i, 0))
    vspec = pl.BlockSpec((1, D_MODEL), lambda i: (0, 0))
    return pl.pallas_call(
        _outproj_ln_body, grid=(m // tm,),
        in_specs=[hspec] * 4 + [pl.BlockSpec((D_MODEL, D_MODEL), lambda i: (0, 0)), xspec, vspec, vspec],
        out_specs=[xspec, xspec],
        out_shape=[jax.ShapeDtypeStruct((m, D_MODEL), f32), jax.ShapeDtypeStruct((m, D_MODEL), bf16)],
        compiler_params=_cparams(("parallel",)), name="outproj_ln")(*hs, w, x, g, b)


def _ffn_ln_body(x_ref, xb_ref, wu_ref, wd_ref, g_ref, b_ref, o_ref, ob_ref, acc_ref):
    f = pl.program_id(1)

    @pl.when(f == 0)
    def _():
        acc_ref[...] = ALPHA * x_ref[...]

    h = jnp.maximum(jnp.dot(xb_ref[...], wu_ref[...], preferred_element_type=f32), 0.0)
    acc_ref[...] += jnp.dot((h * h).astype(bf16), wd_ref[...], preferred_element_type=f32)

    @pl.when(f == pl.num_programs(1) - 1)
    def _():
        y = _layernorm(acc_ref[...], g_ref[...], b_ref[...])
        o_ref[...] = y
        ob_ref[...] = y.astype(bf16)


def _ffn_ln(x, xb, wu, wd, g, b, tm, tf):
    m = x.shape[0]
    xspec = pl.BlockSpec((tm, D_MODEL), lambda i, f: (i, 0))
    vspec = pl.BlockSpec((1, D_MODEL), lambda i, f: (0, 0))
    return pl.pallas_call(
        _ffn_ln_body, grid=(m // tm, D_FF // tf),
        in_specs=[xspec, xspec, pl.BlockSpec((D_MODEL, tf), lambda i, f: (0, f)),
                  pl.BlockSpec((tf, D_MODEL), lambda i, f: (f, 0)), vspec, vspec],
        out_specs=[xspec, xspec],
        out_shape=[jax.ShapeDtypeStruct((m, D_MODEL), f32), jax.ShapeDtypeStruct((m, D_MODEL), bf16)],
        scratch_shapes=[pltpu.VMEM((tm, D_MODEL), f32)],
        compiler_params=_cparams(("parallel", "arbitrary")), name="ffn_ln")(x, xb, wu, wd, g, b)


def _mlstm_prompt_body(cols_ref, gb_ref, ng_ref, h_ref, c_ref, n_ref, m_ref):
    @pl.when(pl.program_id(1) == 0)
    def _():
        c_ref[...] = jnp.zeros_like(c_ref)
        n_ref[...] = jnp.zeros_like(n_ref)
        m_ref[...] = jnp.zeros_like(m_ref)

    L = CHUNK
    causal, _ = _tri_masks(L)
    tril = jnp.where(causal, 1.0, 0.0)
    g = cols_ref[:, 1536:1664] + gb_ref[...]
    bc = _exact(tril, _log_sigmoid(g))
    y = jnp.where(_iota((L, LANES), 1) < ML_H, g, bc)
    yt = _rows_of(y)
    hs = range(ML_H)
    q = [cols_ref[:, h * ML_DK:(h + 1) * ML_DK] for h in hs]
    k = [cols_ref[:, 256 + h * ML_DK:256 + (h + 1) * ML_DK] * ML_DK ** -0.5 for h in hs]
    v = [cols_ref[:, 512 + h * ML_DV:512 + (h + 1) * ML_DV] for h in hs]
    c = [c_ref[0, h] for h in hs]
    n = [n_ref[0, 0:1, h * ML_DK:(h + 1) * ML_DK] for h in hs]
    m = [m_ref[0, 0:1, h:h + 1] for h in hs]
    qk = [_ein_nt(q[h], k[h]) for h in hs]
    qc = [_ein(q[h], c[h]) for h in hs]
    i_col = [y[:, h:h + 1] for h in hs]
    b_col = [y[:, ML_H + h:ML_H + h + 1] for h in hs]
    d = [jnp.where(causal, b_col[h] - yt[ML_H + h:ML_H + h + 1, :] + yt[h:h + 1, :], -jnp.inf) for h in hs]
    inter = [b_col[h] + m[h] for h in hs]
    m_t = [jnp.maximum(inter[h], jnp.max(d[h], axis=-1, keepdims=True)) for h in hs]
    w_inter = [jnp.exp(inter[h] - m_t[h]) for h in hs]
    s = [qk[h] * jnp.exp(d[h] - m_t[h]) for h in hs]
    m_new = [m_t[h][L - 1:L, :] for h in hs]
    b_last = [b_col[h][L - 1:L, :] for h in hs]
    kw = [k[h] * jnp.exp(b_last[h] - b_col[h] + i_col[h] - m_new[h]) for h in hs]
    sv = [_ein(s[h], v[h]) for h in hs]
    kv = [_ein_tn(kw[h], v[h]) for h in hs]
    for h in hs:
        f_state = jnp.exp(b_last[h] + m[h] - m_new[h])
        c_ref[0, h] = f_state * c[h] + kv[h]
        n_ref[0, 0:1, h * ML_DK:(h + 1) * ML_DK] = f_state * n[h] + jnp.sum(kw[h], axis=0, keepdims=True)
        m_ref[0, 0:1, h:h + 1] = m_new[h]
    for h in hs:
        og = cols_ref[:, 1024 + h * ML_DV:1024 + (h + 1) * ML_DV]
        num = w_inter[h] * qc[h] + sv[h]
        den = w_inter[h] * jnp.sum(q[h] * n[h], -1, keepdims=True) + jnp.sum(s[h], -1, keepdims=True)
        hh = num / jnp.maximum(jnp.abs(den), jnp.exp(-m_t[h]))
        hn = hh * lax.rsqrt(jnp.mean(hh * hh, -1, keepdims=True) + NORM_EPS) * ng_ref[:, h * ML_DV:(h + 1) * ML_DV]
        h_ref[:, h * ML_DV:(h + 1) * ML_DV] = (hn * jax.nn.sigmoid(og)).astype(h_ref.dtype)


def _chunk_grid_specs(bp, t, width):
    nc = t // CHUNK
    cols = pl.BlockSpec((CHUNK, width), lambda b, c: (b * nc + c, 0))
    out = pl.BlockSpec((CHUNK, GROUP), lambda b, c: (b * nc + c, 0))
    return (bp, nc), cols, out


def _const_spec(shape):
    return pl.BlockSpec(shape, lambda b, c: (0,) * len(shape))


def _mlstm_prompt(cols, gb, ng, bp, t):
    grid, cspec, ospec = _chunk_grid_specs(bp, t, W_ML)
    return pl.pallas_call(
        _mlstm_prompt_body, grid=grid,
        in_specs=[cspec, _const_spec((1, LANES)), _const_spec((1, GROUP))],
        out_specs=[ospec, pl.BlockSpec((1, ML_H, ML_DK, ML_DV), lambda b, c: (b, 0, 0, 0)),
                   pl.BlockSpec((1, 1, ML_H * ML_DK), lambda b, c: (b, 0, 0)),
                   pl.BlockSpec((1, 1, LANES), lambda b, c: (b, 0, 0))],
        out_shape=[jax.ShapeDtypeStruct((bp * t, GROUP), bf16), jax.ShapeDtypeStruct((bp, ML_H, ML_DK, ML_DV), f32),
                   jax.ShapeDtypeStruct((bp, 1, ML_H * ML_DK), f32), jax.ShapeDtypeStruct((bp, 1, LANES), f32)],
        compiler_params=_cparams(("parallel", "arbitrary")), name="mlstm_prompt")(cols, gb, ng)


def _gla_prompt_body(cols_ref, gup_ref, gb_ref, ng_ref, h_ref, s_ref):
    @pl.when(pl.program_id(1) == 0)
    def _():
        s_ref[...] = jnp.zeros_like(s_ref)

    L = CHUNK
    causal, _ = _tri_masks(L)
    tril = jnp.where(causal, 1.0, 0.0)
    la = _log_sigmoid(_ein(cols_ref[:, 1536:1664], gup_ref[...]) + gb_ref[...]) / GLA_TAU
    bc = _exact(tril, la)
    mid = bc[L // 2:L // 2 + 1, :]
    last = bc[L - 1:L, :]
    e_q_mid, e_k_mid = jnp.exp(bc - mid), jnp.exp(mid - bc)
    e_q, e_k_last, e_last = jnp.exp(bc), jnp.exp(last - bc), jnp.exp(last)
    hs = range(GLA_H)
    sl = [slice(h * GLA_DK, (h + 1) * GLA_DK) for h in hs]
    q = [cols_ref[:, h * GLA_DK:(h + 1) * GLA_DK] * GLA_DK ** -0.5 for h in hs]
    k = [cols_ref[:, 256 + h * GLA_DK:256 + (h + 1) * GLA_DK] for h in hs]
    v = [cols_ref[:, 512 + h * GLA_DV:512 + (h + 1) * GLA_DV] for h in hs]
    s = [s_ref[0, h] for h in hs]
    a = [jnp.where(causal, _ein_nt(q[h] * e_q_mid[:, sl[h]], k[h] * e_k_mid[:, sl[h]]), 0.0) for h in hs]
    o0 = [_ein(q[h] * e_q[:, sl[h]], s[h]) for h in hs]
    kv = [_ein_tn(k[h] * e_k_last[:, sl[h]], v[h]) for h in hs]
    o = [o0[h] + _ein(a[h], v[h]) for h in hs]
    for h in hs:
        s_ref[0, h] = _col(e_last[:, sl[h]]) * s[h] + kv[h]
    for h in hs:
        og = cols_ref[:, 1024 + h * GLA_DV:1024 + (h + 1) * GLA_DV]
        on = o[h] * lax.rsqrt(jnp.mean(o[h] * o[h], -1, keepdims=True) + NORM_EPS) * ng_ref[:, h * GLA_DV:(h + 1) * GLA_DV]
        h_ref[:, h * GLA_DV:(h + 1) * GLA_DV] = (on * _silu(og)).astype(h_ref.dtype)


def _gla_prompt(cols, gup, gb, ng, bp, t):
    grid, cspec, ospec = _chunk_grid_specs(bp, t, W_GLA)
    return pl.pallas_call(
        _gla_prompt_body, grid=grid,
        in_specs=[cspec, _const_spec((LANES, GLA_H * GLA_DK)), _const_spec((1, GLA_H * GLA_DK)), _const_spec((1, GROUP))],
        out_specs=[ospec, pl.BlockSpec((1, GLA_H, GLA_DK, GLA_DV), lambda b, c: (b, 0, 0, 0))],
        out_shape=[jax.ShapeDtypeStruct((bp * t, GROUP), bf16), jax.ShapeDtypeStruct((bp, GLA_H, GLA_DK, GLA_DV), f32)],
        compiler_params=_cparams(("parallel", "arbitrary")), name="gla_prompt")(cols, gup, gb, ng)


def _rwkv_pre(x, prev, mu, wup, aup, gup, vec):
    xs = x + (prev - x) * mu
    r, k, v, lb = xs[:, 0:512], xs[:, 512:1024], xs[:, 1024:1536], xs[:, 1536:1664]
    w0, a0, k_k, k_a = vec[0:1, :], vec[1:2, :], vec[2:3, :], vec[3:4, :]
    lw = -RW_DECAY_SCALE * jax.nn.sigmoid(w0 + _ein(jnp.tanh(lb), wup))
    a = jax.nn.sigmoid(a0 + _ein(lb, aup))
    g = _ein(jax.nn.sigmoid(lb), gup)
    kk_raw = k * k_k
    k2 = k * (1.0 + (a - 1.0) * k_a)
    return r, k2, v, lw, a, g, kk_raw


def _rwkv_post(o, r, k2, v, g, vec, sl):
    r_k, n_g, n_b = vec[4:5, sl], vec[5:6, sl], vec[6:7, sl]
    mu_o = jnp.mean(o, -1, keepdims=True)
    oc = o - mu_o
    var_o = jnp.mean(oc * oc, -1, keepdims=True)
    on = oc * lax.rsqrt(var_o + RW_GN_EPS) * n_g + n_b
    bonus = jnp.sum(r[:, sl] * k2[:, sl] * r_k, -1, keepdims=True) * v[:, sl]
    return (on + bonus) * g[:, sl]


def _rwkv_prompt_body(cols_ref, mu_ref, wup_ref, aup_ref, gup_ref, vec_ref, h_ref, s_ref, prev_scr):
    @pl.when(pl.program_id(1) == 0)
    def _():
        s_ref[...] = jnp.zeros_like(s_ref)
        prev_scr[...] = jnp.zeros_like(prev_scr)

    L = CHUNK
    causal, strict = _tri_masks(L)
    tril = jnp.where(causal, 1.0, 0.0)
    x = cols_ref[...]
    prev = jnp.where(_iota(x.shape, 0) == 0, prev_scr[0:1, :], pltpu.roll(x, 1, 0))
    prev_scr[0:1, :] = x[L - 1:L, :]
    vec = vec_ref[...]
    r, k2, v, lw, a, g, kk_raw = _rwkv_pre(x, prev, mu_ref[...], wup_ref[...], aup_ref[...], gup_ref[...], vec)
    cum = _exact(tril, lw)
    cum_prev = cum - lw
    mid = cum[L // 2:L // 2 + 1, :]
    last = cum[L - 1:L, :]
    e_prev_mid, e_mid_cum, e_cum_mid = jnp.exp(cum_prev - mid), jnp.exp(mid - cum), jnp.exp(cum - mid)
    e_prev, e_cum, e_last_cum, e_last = jnp.exp(cum_prev), jnp.exp(cum), jnp.exp(last - cum), jnp.exp(last)
    hs = range(RW_H)
    sl = [slice(h * RW_N, (h + 1) * RW_N) for h in hs]
    kk = [_l2norm(kk_raw[:, sl[h]]) for h in hs]
    b = [kk[h] * a[:, sl[h]] for h in hs]
    kt = [kk[h] * e_prev_mid[:, sl[h]] for h in hs]
    rt = [r[:, sl[h]] * e_cum_mid[:, sl[h]] for h in hs]
    bh = [b[h] * e_mid_cum[:, sl[h]] for h in hs]
    kh = [k2[:, sl[h]] * e_mid_cum[:, sl[h]] for h in hs]
    vh = [v[:, sl[h]] for h in hs]
    s = [s_ref[0, h] for h in hs]
    a_bb = [jnp.where(strict, _ein_nt(kt[h], bh[h]), 0.0) for h in hs]
    a_bk = [jnp.where(strict, _ein_nt(kt[h], kh[h]), 0.0) for h in hs]
    rhs = [_ein(kk[h] * e_prev[:, sl[h]], s[h]) + _ein(a_bk[h], vh[h]) for h in hs]
    t_inv = _unit_lower_inverses(a_bb)
    a_rb = [jnp.where(causal, _ein_nt(rt[h], bh[h]), 0.0) for h in hs]
    a_rk = [jnp.where(causal, _ein_nt(rt[h], kh[h]), 0.0) for h in hs]
    o0 = [_ein(r[:, sl[h]] * e_cum[:, sl[h]], s[h]) + _ein(a_rk[h], vh[h]) for h in hs]
    s0 = [_col(e_last[:, sl[h]]) * s[h] + _ein_tn(k2[:, sl[h]] * e_last_cum[:, sl[h]], vh[h]) for h in hs]
    u = [-_dot3(t_inv[h], rhs[h]) for h in hs]
    o = [o0[h] + _ein(a_rb[h], u[h]) for h in hs]
    for h in hs:
        s_ref[0, h] = s0[h] + _ein_tn(b[h] * e_last_cum[:, sl[h]], u[h])
    outs = [_rwkv_post(o[h], r, k2, v, g, vec, sl[h]) for h in hs]
    h_ref[...] = jnp.concatenate(outs, axis=-1).astype(h_ref.dtype)


def _rwkv_prompt(cols, mu, wup, aup, gup, vec, bp, t):
    grid, cspec, ospec = _chunk_grid_specs(bp, t, W_RW)
    return pl.pallas_call(
        _rwkv_prompt_body, grid=grid,
        in_specs=[cspec, _const_spec((1, W_RW)), _const_spec((LANES, GROUP)), _const_spec((LANES, GROUP)),
                  _const_spec((LANES, GROUP)), _const_spec((8, GROUP))],
        out_specs=[ospec, pl.BlockSpec((1, RW_H, RW_N, RW_N), lambda b, c: (b, 0, 0, 0))],
        out_shape=[jax.ShapeDtypeStruct((bp * t, GROUP), bf16), jax.ShapeDtypeStruct((bp, RW_H, RW_N, RW_N), f32)],
        scratch_shapes=[pltpu.VMEM((8, W_RW), f32)],
        compiler_params=_cparams(("parallel", "arbitrary")), name="rwkv_prompt")(cols, mu, wup, aup, gup, vec)


def _gdn_gates(gb, alog, dtb):
    return jax.nn.sigmoid(gb), -jnp.exp(alog) * _softplus(gb + dtb)


def _l2norm(x):
    return x * lax.rsqrt(jnp.sum(x * x, -1, keepdims=True) + NORM_EPS)


def _gdn_prompt_body(cols_ref, cw_ref, alog_ref, dtb_ref, ng_ref, h_ref, s_ref, xp_scr):
    @pl.when(pl.program_id(1) == 0)
    def _():
        s_ref[...] = jnp.zeros_like(s_ref)
        xp_scr[0:8, :] = jnp.zeros((8, DN_QKV), f32)

    L = CHUNK
    causal, strict = _tri_masks(L)
    tril = jnp.where(causal, 1.0, 0.0)
    raw = cols_ref[:, 0:DN_QKV]
    xp_scr[8:8 + L, :] = raw
    conv = xp_scr[5:5 + L, :] * cw_ref[0:1, :]
    for w in range(1, DN_CONV):
        conv = conv + xp_scr[5 + w:5 + w + L, :] * cw_ref[w:w + 1, :]
    xp_scr[0:8, :] = raw[L - 8:L, :]
    act = _silu(conv)
    beta_all, g_all = _gdn_gates(cols_ref[:, 2048:2176], alog_ref[...], dtb_ref[...])
    gc = _exact(tril, g_all)
    gt = _rows_of(gc)
    hs = range(DN_H)
    q = [_l2norm(act[:, h * DN_D:(h + 1) * DN_D]) * DN_D ** -0.5 for h in hs]
    k = [_l2norm(act[:, GROUP + h * DN_D:GROUP + (h + 1) * DN_D]) for h in hs]
    v = [act[:, 2 * GROUP + h * DN_D:2 * GROUP + (h + 1) * DN_D] for h in hs]
    beta = [beta_all[:, h:h + 1] for h in hs]
    g_col = [gc[:, DN_H + h:DN_H + h + 1] for h in hs]
    decay = [jnp.exp(jnp.where(causal, g_col[h] - gt[DN_H + h:DN_H + h + 1, :], -jnp.inf)) for h in hs]
    kb = [k[h] * beta[h] for h in hs]
    eg = [jnp.exp(g_col[h]) for h in hs]
    s = [s_ref[0, h] for h in hs]
    t_inv = _unit_lower_inverses([jnp.where(strict, _ein_nt(kb[h], k[h]) * decay[h], 0.0) for h in hs])
    qk = [_ein_nt(q[h], k[h]) * decay[h] for h in hs]
    o0 = [_ein(q[h] * eg[h], s[h]) for h in hs]
    sol = [_dot3(t_inv[h], jnp.concatenate([v[h] * beta[h], kb[h] * eg[h]], axis=-1)) for h in hs]
    v_new = [sol[h][:, :DN_D] - _ein(sol[h][:, DN_D:], s[h]) for h in hs]
    o = [o0[h] + _ein(qk[h], v_new[h]) for h in hs]
    for h in hs:
        g_last = g_col[h][L - 1:L, :]
        s_ref[0, h] = jnp.exp(g_last) * s[h] + _ein_tn(k[h] * jnp.exp(g_last - g_col[h]), v_new[h])
    for h in hs:
        z = cols_ref[:, DN_QKV + h * DN_D:DN_QKV + (h + 1) * DN_D]
        on = o[h] * lax.rsqrt(jnp.mean(o[h] * o[h], -1, keepdims=True) + NORM_EPS) * ng_ref[...]
        h_ref[:, h * DN_D:(h + 1) * DN_D] = (on * _silu(z)).astype(h_ref.dtype)


def _gdn_prompt(cols, cw, alog, dtb, ng, bp, t):
    grid, cspec, ospec = _chunk_grid_specs(bp, t, W_DN)
    return pl.pallas_call(
        _gdn_prompt_body, grid=grid,
        in_specs=[cspec, _const_spec((8, DN_QKV)), _const_spec((1, LANES)), _const_spec((1, LANES)), _const_spec((1, DN_D))],
        out_specs=[ospec, pl.BlockSpec((1, DN_H, DN_D, DN_D), lambda b, c: (b, 0, 0, 0))],
        out_shape=[jax.ShapeDtypeStruct((bp * t, GROUP), bf16), jax.ShapeDtypeStruct((bp, DN_H, DN_D, DN_D), f32)],
        scratch_shapes=[pltpu.VMEM((8 + CHUNK, DN_QKV), f32)],
        compiler_params=_cparams(("parallel", "arbitrary")), name="gdn_prompt")(cols, cw, alog, dtb, ng)


def _mlstm_step_body(cols_ref, gb_ref, ng_ref, c_ref, n_ref, m_ref, h_ref, co_ref, no_ref, mo_ref,
                     q_scr, kw_scr, v_scr, f_scr, qc_scr):
    nb = DEC_BLOCK
    g = cols_ref[:, 1536:1664] + gb_ref[...]
    lf = _log_sigmoid(g)
    hs = []
    for h in range(ML_H):
        q = cols_ref[:, h * ML_DK:(h + 1) * ML_DK]
        k = cols_ref[:, 256 + h * ML_DK:256 + (h + 1) * ML_DK] * ML_DK ** -0.5
        v = cols_ref[:, 512 + h * ML_DV:512 + (h + 1) * ML_DV]
        i_pre, f_log = g[:, h:h + 1], lf[:, ML_H + h:ML_H + h + 1]
        m = m_ref[:, h:h + 1]
        inter = f_log + m
        m_t = jnp.maximum(inter, i_pre)
        w_inter = jnp.exp(inter - m_t)
        kw = k * jnp.exp(i_pre - m_t)
        q_scr[h], kw_scr[h], v_scr[h] = q, kw, v
        f_scr[h] = jnp.broadcast_to(w_inter, (nb, LANES))
        hs.append((q, k, v, i_pre, m_t, w_inter, kw))

    def row(b, carry):
        hr = range(ML_H)
        q_col = [_col(q_scr[h, pl.ds(b, 1), :]) for h in hr]
        kw_col = [_col(kw_scr[h, pl.ds(b, 1), :]) for h in hr]
        c = [c_ref[b, h] for h in hr]
        for h in hr:
            qc_scr[h, pl.ds(b, 1), :] = jnp.sum(q_col[h] * c[h], axis=0, keepdims=True)
            co_ref[b, h] = f_scr[h, pl.ds(b, 1), :] * c[h] + kw_col[h] * v_scr[h, pl.ds(b, 1), :]
        return carry

    lax.fori_loop(0, nb, row, 0)
    for h, (q, k, v, i_pre, m_t, w_inter, kw) in enumerate(hs):
        og = cols_ref[:, 1024 + h * ML_DV:1024 + (h + 1) * ML_DV]
        n = n_ref[:, h * ML_DK:(h + 1) * ML_DK]
        s = jnp.sum(q * k, -1, keepdims=True) * jnp.exp(i_pre - m_t)
        num = w_inter * qc_scr[h] + s * v
        den = w_inter * jnp.sum(q * n, -1, keepdims=True) + s
        hh = num / jnp.maximum(jnp.abs(den), jnp.exp(-m_t))
        no_ref[:, h * ML_DK:(h + 1) * ML_DK] = w_inter * n + kw
        mo_ref[:, h:h + 1] = m_t
        hn = hh * lax.rsqrt(jnp.mean(hh * hh, -1, keepdims=True) + NORM_EPS) * ng_ref[:, h * ML_DV:(h + 1) * ML_DV]
        h_ref[:, h * ML_DV:(h + 1) * ML_DV] = (hn * jax.nn.sigmoid(og)).astype(h_ref.dtype)


def _row_spec(width):
    return pl.BlockSpec((DEC_BLOCK, width), lambda i: (i, 0))


def _state_spec(h, dk, dv):
    return pl.BlockSpec((DEC_BLOCK, h, dk, dv), lambda i: (i, 0, 0, 0))


def _vec_spec(shape):
    return pl.BlockSpec(shape, lambda i: (0,) * len(shape))


def _mlstm_step(cols, gb, ng, c0, n0, m0):
    bs = cols.shape[0]
    nb = DEC_BLOCK
    return pl.pallas_call(
        _mlstm_step_body, grid=(bs // nb,),
        in_specs=[_row_spec(W_ML), _vec_spec((1, LANES)), _vec_spec((1, GROUP)), _state_spec(ML_H, ML_DK, ML_DV),
                  _row_spec(ML_H * ML_DK), _row_spec(ML_H)],
        out_specs=[_row_spec(GROUP), _state_spec(ML_H, ML_DK, ML_DV), _row_spec(ML_H * ML_DK), _row_spec(ML_H)],
        out_shape=[jax.ShapeDtypeStruct((bs, GROUP), bf16), jax.ShapeDtypeStruct(c0.shape, f32),
                   jax.ShapeDtypeStruct(n0.shape, f32), jax.ShapeDtypeStruct(m0.shape, f32)],
        scratch_shapes=[pltpu.VMEM((ML_H, nb, ML_DK), f32), pltpu.VMEM((ML_H, nb, ML_DK), f32),
                        pltpu.VMEM((ML_H, nb, ML_DV), f32), pltpu.VMEM((ML_H, nb, LANES), f32),
                        pltpu.VMEM((ML_H, nb, ML_DV), f32)],
        compiler_params=_cparams(("parallel",)), name="mlstm_step")(cols, gb, ng, c0, n0, m0)


def _gla_step_body(cols_ref, gup_ref, gb_ref, ng_ref, s_ref, h_ref, so_ref, qe_scr, ea_scr, k_scr, v_scr, qs_scr):
    nb = DEC_BLOCK
    la = _log_sigmoid(_ein(cols_ref[:, 1536:1664], gup_ref[...]) + gb_ref[...]) / GLA_TAU
    ea = jnp.exp(la)
    hs = []
    for h in range(GLA_H):
        q = cols_ref[:, h * GLA_DK:(h + 1) * GLA_DK] * GLA_DK ** -0.5
        k = cols_ref[:, 256 + h * GLA_DK:256 + (h + 1) * GLA_DK]
        v = cols_ref[:, 512 + h * GLA_DV:512 + (h + 1) * GLA_DV]
        ea_h = ea[:, h * GLA_DK:(h + 1) * GLA_DK]
        qe_scr[h], ea_scr[h], k_scr[h], v_scr[h] = q * ea_h, ea_h, k, v
        hs.append((q, k, v))

    def row(b, carry):
        hr = range(GLA_H)
        qe_col = [_col(qe_scr[h, pl.ds(b, 1), :]) for h in hr]
        ea_col = [_col(ea_scr[h, pl.ds(b, 1), :]) for h in hr]
        k_col = [_col(k_scr[h, pl.ds(b, 1), :]) for h in hr]
        s = [s_ref[b, h] for h in hr]
        for h in hr:
            qs_scr[h, pl.ds(b, 1), :] = jnp.sum(qe_col[h] * s[h], axis=0, keepdims=True)
            so_ref[b, h] = ea_col[h] * s[h] + k_col[h] * v_scr[h, pl.ds(b, 1), :]
        return carry

    lax.fori_loop(0, nb, row, 0)
    for h, (q, k, v) in enumerate(hs):
        og = cols_ref[:, 1024 + h * GLA_DV:1024 + (h + 1) * GLA_DV]
        o = qs_scr[h] + jnp.sum(q * k, -1, keepdims=True) * v
        on = o * lax.rsqrt(jnp.mean(o * o, -1, keepdims=True) + NORM_EPS) * ng_ref[:, h * GLA_DV:(h + 1) * GLA_DV]
        h_ref[:, h * GLA_DV:(h + 1) * GLA_DV] = (on * _silu(og)).astype(h_ref.dtype)


def _gla_step(cols, gup, gb, ng, s0):
    bs = cols.shape[0]
    nb = DEC_BLOCK
    return pl.pallas_call(
        _gla_step_body, grid=(bs // nb,),
        in_specs=[_row_spec(W_GLA), _vec_spec((LANES, GLA_H * GLA_DK)), _vec_spec((1, GLA_H * GLA_DK)),
                  _vec_spec((1, GROUP)), _state_spec(GLA_H, GLA_DK, GLA_DV)],
        out_specs=[_row_spec(GROUP), _state_spec(GLA_H, GLA_DK, GLA_DV)],
        out_shape=[jax.ShapeDtypeStruct((bs, GROUP), bf16), jax.ShapeDtypeStruct(s0.shape, f32)],
        scratch_shapes=[pltpu.VMEM((GLA_H, nb, GLA_DK), f32)] * 3 + [pltpu.VMEM((GLA_H, nb, GLA_DV), f32)] * 2,
        compiler_params=_cparams(("parallel",)), name="gla_step")(cols, gup, gb, ng, s0)


def _rwkv_step_body(cols_ref, shift_ref, mu_ref, wup_ref, aup_ref, gup_ref, vec_ref, s_ref, h_ref, so_ref,
                    kk_scr, w_scr, b_scr, k_scr, r_scr, v_scr, o_scr):
    nb = DEC_BLOCK
    vec = vec_ref[...]
    r, k2, v, lw, a, g, kk_raw = _rwkv_pre(cols_ref[...], shift_ref[...], mu_ref[...], wup_ref[...], aup_ref[...],
                                           gup_ref[...], vec)
    w = jnp.exp(lw)
    for h in range(RW_H):
        sl = slice(h * RW_N, (h + 1) * RW_N)
        kk = _l2norm(kk_raw[:, sl])
        kk_scr[h], w_scr[h], b_scr[h], k_scr[h], r_scr[h], v_scr[h] = kk, w[:, sl], kk * a[:, sl], k2[:, sl], r[:, sl], v[:, sl]

    def row(b, carry):
        hr = range(RW_H)
        kk_col = [_col(kk_scr[h, pl.ds(b, 1), :]) for h in hr]
        w_col = [_col(w_scr[h, pl.ds(b, 1), :]) for h in hr]
        b_col = [_col(b_scr[h, pl.ds(b, 1), :]) for h in hr]
        k_col = [_col(k_scr[h, pl.ds(b, 1), :]) for h in hr]
        r_col = [_col(r_scr[h, pl.ds(b, 1), :]) for h in hr]
        s = [s_ref[b, h] for h in hr]
        sk = [jnp.sum(kk_col[h] * s[h], axis=0, keepdims=True) for h in hr]
        s_new = [w_col[h] * s[h] - b_col[h] * sk[h] + k_col[h] * v_scr[h, pl.ds(b, 1), :] for h in hr]
        for h in hr:
            so_ref[b, h] = s_new[h]
            o_scr[h, pl.ds(b, 1), :] = jnp.sum(r_col[h] * s_new[h], axis=0, keepdims=True)
        return carry

    lax.fori_loop(0, nb, row, 0)
    outs = [_rwkv_post(o_scr[h], r, k2, v, g, vec, slice(h * RW_N, (h + 1) * RW_N)) for h in range(RW_H)]
    h_ref[...] = jnp.concatenate(outs, axis=-1).astype(h_ref.dtype)


def _rwkv_step(cols, shift0, mu, wup, aup, gup, vec, s0):
    bs = cols.shape[0]
    nb = DEC_BLOCK
    return pl.pallas_call(
        _rwkv_step_body, grid=(bs // nb,),
        in_specs=[_row_spec(W_RW), _row_spec(W_RW), _vec_spec((1, W_RW)), _vec_spec((LANES, GROUP)),
                  _vec_spec((LANES, GROUP)), _vec_spec((LANES, GROUP)), _vec_spec((8, GROUP)), _state_spec(RW_H, RW_N, RW_N)],
        out_specs=[_row_spec(GROUP), _state_spec(RW_H, RW_N, RW_N)],
        out_shape=[jax.ShapeDtypeStruct((bs, GROUP), bf16), jax.ShapeDtypeStruct(s0.shape, f32)],
        scratch_shapes=[pltpu.VMEM((RW_H, nb, RW_N), f32)] * 7,
        compiler_params=_cparams(("parallel",)), name="rwkv_step")(cols, shift0, mu, wup, aup, gup, vec, s0)


def _gdn_step_body(cols_ref, buf_ref, cw_ref, alog_ref, dtb_ref, ng_ref, s_ref, h_ref, so_ref,
                   q_scr, k_scr, v_scr, sc_scr, ks_scr, qs_scr, vn_scr):
    nb = DEC_BLOCK
    conv = cols_ref[:, 0:DN_QKV] * cw_ref[DN_CONV - 1:DN_CONV, :]
    for w in range(DN_CONV - 1):
        conv = conv + buf_ref[:, w * DN_QKV:(w + 1) * DN_QKV] * cw_ref[w:w + 1, :]
    act = _silu(conv)
    beta_all, g_all = _gdn_gates(cols_ref[:, 2048:2176], alog_ref[...], dtb_ref[...])
    eg_all = jnp.exp(g_all)
    hs = []
    for h in range(DN_H):
        q = _l2norm(act[:, h * DN_D:(h + 1) * DN_D]) * DN_D ** -0.5
        k = _l2norm(act[:, GROUP + h * DN_D:GROUP + (h + 1) * DN_D])
        v = act[:, 2 * GROUP + h * DN_D:2 * GROUP + (h + 1) * DN_D]
        beta, eg = beta_all[:, h:h + 1], eg_all[:, DN_H + h:DN_H + h + 1]
        q_scr[h], k_scr[h], v_scr[h] = q, k, v
        sc_scr[h] = jnp.where(_iota((nb, LANES), 1) == 0, beta, eg)
        hs.append((q, k, eg))

    def row(b, carry):
        hr = range(DN_H)
        k_col = [_col(k_scr[h, pl.ds(b, 1), :]) for h in hr]
        q_col = [_col(q_scr[h, pl.ds(b, 1), :]) for h in hr]
        s = [s_ref[b, h] for h in hr]
        sc = [sc_scr[h, pl.ds(b, 1), :] for h in hr]
        ks = [jnp.sum(k_col[h] * s[h], axis=0, keepdims=True) for h in hr]
        for h in hr:
            beta, eg = sc[h][:, 0:1], sc[h][:, 1:2]
            v_new = beta * (v_scr[h, pl.ds(b, 1), :] - eg * ks[h])
            qs_scr[h, pl.ds(b, 1), :] = jnp.sum(q_col[h] * s[h], axis=0, keepdims=True)
            vn_scr[h, pl.ds(b, 1), :] = v_new
            so_ref[b, h] = eg * s[h] + k_col[h] * v_new
        return carry

    lax.fori_loop(0, nb, row, 0)
    for h, (q, k, eg) in enumerate(hs):
        z = cols_ref[:, DN_QKV + h * DN_D:DN_QKV + (h + 1) * DN_D]
        o = eg * qs_scr[h] + jnp.sum(q * k, -1, keepdims=True) * vn_scr[h]
        on = o * lax.rsqrt(jnp.mean(o * o, -1, keepdims=True) + NORM_EPS) * ng_ref[...]
        h_ref[:, h * DN_D:(h + 1) * DN_D] = (on * _silu(z)).astype(h_ref.dtype)


def _gdn_step(cols, buf, cw, alog, dtb, ng, s0):
    bs = cols.shape[0]
    nb = DEC_BLOCK
    return pl.pallas_call(
        _gdn_step_body, grid=(bs // nb,),
        in_specs=[_row_spec(W_DN), _row_spec((DN_CONV - 1) * DN_QKV), _vec_spec((8, DN_QKV)), _vec_spec((1, LANES)),
                  _vec_spec((1, LANES)), _vec_spec((1, DN_D)), _state_spec(DN_H, DN_D, DN_D)],
        out_specs=[_row_spec(GROUP), _state_spec(DN_H, DN_D, DN_D)],
        out_shape=[jax.ShapeDtypeStruct((bs, GROUP), bf16), jax.ShapeDtypeStruct(s0.shape, f32)],
        scratch_shapes=[pltpu.VMEM((DN_H, nb, DN_D), f32)] * 3 + [pltpu.VMEM((DN_H, nb, LANES), f32)]
        + [pltpu.VMEM((DN_H, nb, DN_D), f32)] * 3,
        compiler_params=_cparams(("parallel",)), name="gdn_step")(cols, buf, cw, alog, dtb, ng, s0)


def _pad_cols(w, width):
    return jnp.pad(w, ((0, 0), (0, width - w.shape[1])))


def _pad_rows(w, start, total):
    return jnp.pad(w, ((start, total - start - w.shape[0]), (0, 0)))


def _lane_row(v, start):
    return jnp.pad(v, (start, LANES - start - v.shape[0]))[None, :]


def kernel(x_prompt, x_sample, state_mlstm_c, state_mlstm_n, state_mlstm_m, state_gla, state_rwkv, state_rwkv_shift, state_dn, state_dn_conv, w_in, w_out, w_up, w_down, ln1_g, ln1_b, ln2_g, ln2_b, mlstm_gate_b, mlstm_norm_g, gla_gate_up, gla_gate_b, gla_norm_g, rwkv_mu, rwkv_w_up, rwkv_w0, rwkv_a_up, rwkv_a0, rwkv_g_up, rwkv_k_k, rwkv_k_a, rwkv_r_k, rwkv_norm_g, rwkv_norm_b, dn_conv_w, dn_a_log, dn_dt_bias, dn_norm_g):
    bp, t, _ = x_prompt.shape
    bs = x_sample.shape[0]
    xp = x_prompt.reshape(bp * t, D_MODEL)
    xs = x_sample.reshape(bs, D_MODEL)
    xpb, xsb = xp.astype(bf16), xs.astype(bf16)
    tm_p, tm_s, tf = 512, bs, 512
    outs_p, outs_s = [], []
    for l in range(DEPTH):
        o1, o2, o3 = N_ML, N_ML + N_GLA, N_ML + N_GLA + N_RW
        w_secs = [_pad_cols(w_in[l][:, 0:o1], W_ML).astype(bf16), _pad_cols(w_in[l][:, o1:o2], W_GLA).astype(bf16),
                  w_in[l][:, o2:o3].astype(bf16), _pad_cols(w_in[l][:, o3:], W_DN).astype(bf16)]
        wo, wu, wd = w_out[l].astype(bf16), w_up[l].astype(bf16), w_down[l].astype(bf16)
        g1, b1, g2, b2 = ln1_g[l][None], ln1_b[l][None], ln2_g[l][None], ln2_b[l][None]
        ml_gb, ml_ng = _lane_row(mlstm_gate_b[l], 0), mlstm_norm_g[l][None]
        gla_gup, gla_gb, gla_ng = _pad_rows(gla_gate_up[l], 0, LANES), gla_gate_b[l][None], gla_norm_g[l][None]
        rw_mu = rwkv_mu[l][None]
        rw_wup, rw_aup, rw_gup = (_pad_rows(rwkv_w_up[l], 0, LANES), _pad_rows(rwkv_a_up[l], 32, LANES),
                                  _pad_rows(rwkv_g_up[l], 64, LANES))
        rw_vec = jnp.stack([rwkv_w0[l], rwkv_a0[l], rwkv_k_k[l], rwkv_k_a[l], rwkv_r_k[l], rwkv_norm_g[l],
                            rwkv_norm_b[l], jnp.zeros_like(rwkv_w0[l])])
        dn_cw = _pad_rows(dn_conv_w[l], 0, 8)
        dn_alog, dn_dtb, dn_ng = _lane_row(dn_a_log[l], DN_H), _lane_row(dn_dt_bias[l], DN_H), dn_norm_g[l][None]

        c_ml, c_gla, c_rw, c_dn = [_proj(xpb, w, tm_p) for w in w_secs]
        h_ml, mc, mn, mm = _mlstm_prompt(c_ml, ml_gb, ml_ng, bp, t)
        h_gla, gs = _gla_prompt(c_gla, gla_gup, gla_gb, gla_ng, bp, t)
        h_rw, rs = _rwkv_prompt(c_rw, rw_mu, rw_wup, rw_aup, rw_gup, rw_vec, bp, t)
        h_dn, ds = _gdn_prompt(c_dn, dn_cw, dn_alog, dn_dtb, dn_ng, bp, t)
        x1, x1b = _outproj_ln((h_ml, h_gla, h_rw, h_dn), wo, xp, g1, b1, tm_p)
        xp, xpb = _ffn_ln(x1, x1b, wu, wd, g2, b2, tm_p, tf)
        outs_p.append((mc, mn.reshape(bp, ML_H, ML_DK), mm[:, 0, :ML_H], gs, rs, c_rw.reshape(bp, t, W_RW)[:, -1],
                       ds, c_dn.reshape(bp, t, W_DN)[:, t - (DN_CONV - 1):, :DN_QKV]))

        c_ml, c_gla, c_rw, c_dn = [_proj(xsb, w, tm_s) for w in w_secs]
        h_ml, mc, mn, mm = _mlstm_step(c_ml, ml_gb, ml_ng, state_mlstm_c[l],
                                       state_mlstm_n[l].reshape(bs, ML_H * ML_DK), state_mlstm_m[l])
        h_gla, gs = _gla_step(c_gla, gla_gup, gla_gb, gla_ng, state_gla[l])
        h_rw, rs = _rwkv_step(c_rw, state_rwkv_shift[l], rw_mu, rw_wup, rw_aup, rw_gup, rw_vec, state_rwkv[l])
        h_dn, ds = _gdn_step(c_dn, state_dn_conv[l].reshape(bs, (DN_CONV - 1) * DN_QKV), dn_cw, dn_alog, dn_dtb,
                             dn_ng, state_dn[l])
        x1, x1b = _outproj_ln((h_ml, h_gla, h_rw, h_dn), wo, xs, g1, b1, tm_s)
        xs, xsb = _ffn_ln(x1, x1b, wu, wd, g2, b2, tm_s, tf)
        outs_s.append((mc, mn.reshape(bs, ML_H, ML_DK), mm, gs, rs, c_rw,
                       ds, jnp.concatenate([state_dn_conv[l][:, 1:], c_dn[:, None, :DN_QKV]], axis=1)))

    sp = [jnp.stack(z) for z in zip(*outs_p)]
    ss = [jnp.stack(z) for z in zip(*outs_s)]
    res = [xp.reshape(bp, t, D_MODEL), xs.reshape(bs, 1, D_MODEL)]
    for a, b in zip(sp, ss):
        res += [a, b]
    return tuple(res)
```

```python
import functools

import jax
import jax.numpy as jnp
from jax import lax
from jax.experimental import pallas as pl
from jax.experimental.pallas import tpu as pltpu

f32 = jnp.float32
bf16 = jnp.bfloat16
_HI = lax.Precision.HIGHEST

D_MODEL = 2048
DEPTH = 4
GROUP = 512
D_FF = 4 * D_MODEL
CHUNK = 64
ML_H, ML_DK, ML_DV = 4, 64, 128
GLA_H, GLA_DK, GLA_DV, GLA_RANK, GLA_TAU = 4, 64, 128, 16, 16.0
RW_H, RW_N = 8, 64
RW_DECAY_SCALE = 0.606531
RW_GN_EPS = 64e-5
DN_H, DN_D, DN_CONV, DN_QKV = 4, 128, 4, 1536
N_ML, N_GLA, N_RW, N_DN = 1544, 1552, 1664, 2056
W_ML, W_GLA, W_RW, W_DN = 1664, 1664, 1664, 2176
ALPHA = (2 * DEPTH) ** 0.25
LN_EPS = 1e-5
NORM_EPS = 1e-6
LANES = 128
DEC_BLOCK = 8
VMEM_LIMIT = 56 * 1024 * 1024


def _ein(a, b):
    return jnp.dot(a.astype(bf16), b.astype(bf16), preferred_element_type=f32)


def _ein_nt(a, b):
    return lax.dot_general(a.astype(bf16), b.astype(bf16), (((1,), (1,)), ((), ())), preferred_element_type=f32)


def _ein_tn(a, b):
    return lax.dot_general(a.astype(bf16), b.astype(bf16), (((0,), (0,)), ((), ())), preferred_element_type=f32)


def _exact(a, b):
    return jnp.dot(a, b, precision=_HI, preferred_element_type=f32)


def _exact_nt(a, b):
    return lax.dot_general(a, b, (((1,), (1,)), ((), ())), precision=_HI, preferred_element_type=f32)


def _log_sigmoid(x):
    return jnp.minimum(x, 0.0) - jnp.log1p(jnp.exp(-jnp.abs(x)))


def _softplus(x):
    return jnp.maximum(x, 0.0) + jnp.log1p(jnp.exp(-jnp.abs(x)))


def _silu(x):
    return x * jax.nn.sigmoid(x)


def _l2norm(x):
    return x * lax.rsqrt(jnp.sum(x * x, -1, keepdims=True) + NORM_EPS)


def _iota(shape, axis):
    return lax.broadcasted_iota(jnp.int32, shape, axis)


def _col(row):
    n = row.shape[1]
    eye = _iota((n, n), 0) == _iota((n, n), 1)
    return jnp.sum(jnp.where(eye, row, 0.0), axis=1, keepdims=True)


def _tri_masks(n):
    r, c = _iota((n, n), 0), _iota((n, n), 1)
    return c <= r, c < r


def _split2(a):
    hi = a.astype(bf16)
    return hi, (a - hi.astype(f32)).astype(bf16)


def _dot3(a, b):
    ah, al = _split2(a)
    bh, bl = _split2(b)
    d = functools.partial(jnp.dot, preferred_element_type=f32)
    return d(ah, bh) + d(ah, bl) + d(al, bh)


def _unit_lower_inverses(a_list):
    n = a_list[0].shape[0]
    eye = jnp.where(_iota((n, n), 0) == _iota((n, n), 1), 1.0, 0.0)
    xs = [-a for a in a_list]
    ps = [eye + x for x in xs]
    for stage in range(max(n.bit_length() - 2, 0)):
        mm = _dot3 if stage == 0 else _ein
        xs = [mm(x, x) for x in xs]
        ps = [p + mm(p, x) for p, x in zip(ps, xs)]
    return ps


def _rows_of(y):
    e8 = jnp.where(_iota((8, LANES), 0) == _iota((8, LANES), 1), 1.0, 0.0)
    return _exact_nt(e8, y)


def _cparams(sem):
    return pltpu.CompilerParams(dimension_semantics=sem, vmem_limit_bytes=VMEM_LIMIT)


def _layer_spec(l, shape, index=None):
    nd = len(shape)
    if index is None:
        return pl.BlockSpec((None,) + tuple(shape), lambda *g: (l,) + (0,) * nd)
    return pl.BlockSpec((None,) + tuple(shape), lambda *g: (l,) + tuple(index(*g)))


def _proj_body(x_ref, w_ref, o_ref):
    o_ref[...] = jnp.dot(x_ref[...], w_ref[...], preferred_element_type=f32)


def _proj(xb, w, l, tm):
    m, k = xb.shape
    n = w.shape[2]
    return pl.pallas_call(
        _proj_body, grid=(m // tm,),
        in_specs=[pl.BlockSpec((tm, k), lambda i: (i, 0)), _layer_spec(l, (k, n))],
        out_specs=pl.BlockSpec((tm, n), lambda i: (i, 0)),
        out_shape=jax.ShapeDtypeStruct((m, n), f32),
        compiler_params=_cparams(("parallel",)), name="proj")(xb, w)


def _layernorm(y, g, b):
    mu = jnp.mean(y, -1, keepdims=True)
    yc = y - mu
    var = jnp.mean(yc * yc, -1, keepdims=True)
    return yc * lax.rsqrt(var + LN_EPS) * g + b


def _outproj_ln_body(h0, h1, h2, h3, w_ref, x_ref, g_ref, b_ref, o_ref, ob_ref):
    acc = ALPHA * x_ref[...]
    for j, h in enumerate((h0, h1, h2, h3)):
        acc = acc + jnp.dot(h[...], w_ref[j * GROUP:(j + 1) * GROUP, :], preferred_element_type=f32)
    y = _layernorm(acc, g_ref[...], b_ref[...])
    o_ref[...] = y
    ob_ref[...] = y.astype(bf16)


def _outproj_ln(hs, w, x, g, b, l, tm):
    m = x.shape[0]
    hspec = pl.BlockSpec((tm, GROUP), lambda i: (i, 0))
    xspec = pl.BlockSpec((tm, D_MODEL), lambda i: (i, 0))
    vspec = _layer_spec(l, (1, D_MODEL))
    return pl.pallas_call(
        _outproj_ln_body, grid=(m // tm,),
        in_specs=[hspec] * 4 + [_layer_spec(l, (D_MODEL, D_MODEL)), xspec, vspec, vspec],
        out_specs=[xspec, xspec],
        out_shape=[jax.ShapeDtypeStruct((m, D_MODEL), f32), jax.ShapeDtypeStruct((m, D_MODEL), bf16)],
        compiler_params=_cparams(("parallel",)), name="outproj_ln")(*hs, w, x, g, b)


def _ffn_ln_body(x_ref, xb_ref, wu_ref, wd_ref, g_ref, b_ref, o_ref, ob_ref, acc_ref):
    f = pl.program_id(1)

    @pl.when(f == 0)
    def _():
        acc_ref[...] = ALPHA * x_ref[...]

    h = jnp.maximum(jnp.dot(xb_ref[...], wu_ref[...], preferred_element_type=f32), 0.0)
    acc_ref[...] += jnp.dot((h * h).astype(bf16), wd_ref[...], preferred_element_type=f32)

    @pl.when(f == pl.num_programs(1) - 1)
    def _():
        y = _layernorm(acc_ref[...], g_ref[...], b_ref[...])
        o_ref[...] = y
        ob_ref[...] = y.astype(bf16)


def _ffn_ln(x, xb, wu, wd, g, b, l, tm, tf):
    m = x.shape[0]
    xspec = pl.BlockSpec((tm, D_MODEL), lambda i, f: (i, 0))
    vspec = _layer_spec(l, (1, D_MODEL))
    return pl.pallas_call(
        _ffn_ln_body, grid=(m // tm, D_FF // tf),
        in_specs=[xspec, xspec, _layer_spec(l, (D_MODEL, tf), lambda i, f: (0, f)),
                  _layer_spec(l, (tf, D_MODEL), lambda i, f: (f, 0)), vspec, vspec],
        out_specs=[xspec, xspec],
        out_shape=[jax.ShapeDtypeStruct((m, D_MODEL), f32), jax.ShapeDtypeStruct((m, D_MODEL), bf16)],
        scratch_shapes=[pltpu.VMEM((tm, D_MODEL), f32)],
        compiler_params=_cparams(("parallel", "arbitrary")), name="ffn_ln")(x, xb, wu, wd, g, b)


def _mlstm_prompt_body(cols_ref, gb_ref, ng_ref, h_ref, c_ref, n_ref, m_ref):
    @pl.when(pl.program_id(1) == 0)
    def _():
        c_ref[...] = jnp.zeros_like(c_ref)
        n_ref[...] = jnp.zeros_like(n_ref)
        m_ref[...] = jnp.zeros_like(m_ref)

    L = CHUNK
    causal, _ = _tri_masks(L)
    tril = jnp.where(causal, 1.0, 0.0)
    g = cols_ref[:, 1536:1664] + gb_ref[...]
    bc = _exact(tril, _log_sigmoid(g))
    y = jnp.where(_iota((L, LANES), 1) < ML_H, g, bc)
    yt = _rows_of(y)
    hs = range(ML_H)
    q = [cols_ref[:, h * ML_DK:(h + 1) * ML_DK] for h in hs]
    k = [cols_ref[:, 256 + h * ML_DK:256 + (h + 1) * ML_DK] * ML_DK ** -0.5 for h in hs]
    v = [cols_ref[:, 512 + h * ML_DV:512 + (h + 1) * ML_DV] for h in hs]
    c = [c_ref[0, h] for h in hs]
    n = [n_ref[0, 0:1, h * ML_DK:(h + 1) * ML_DK] for h in hs]
    m = [m_ref[0, 0:1, h:h + 1] for h in hs]
    qk = [_ein_nt(q[h], k[h]) for h in hs]
    qc = [_ein(q[h], c[h]) for h in hs]
    i_col = [y[:, h:h + 1] for h in hs]
    b_col = [y[:, ML_H + h:ML_H + h + 1] for h in hs]
    d = [jnp.where(causal, b_col[h] - yt[ML_H + h:ML_H + h + 1, :] + yt[h:h + 1, :], -jnp.inf) for h in hs]
    inter = [b_col[h] + m[h] for h in hs]
    m_t = [jnp.maximum(inter[h], jnp.max(d[h], axis=-1, keepdims=True)) for h in hs]
    w_inter = [jnp.exp(inter[h] - m_t[h]) for h in hs]
    s = [qk[h] * jnp.exp(d[h] - m_t[h]) for h in hs]
    m_new = [m_t[h][L - 1:L, :] for h in hs]
    b_last = [b_col[h][L - 1:L, :] for h in hs]
    kw = [k[h] * jnp.exp(b_last[h] - b_col[h] + i_col[h] - m_new[h]) for h in hs]
    sv = [_ein(s[h], v[h]) for h in hs]
    kv = [_ein_tn(kw[h], v[h]) for h in hs]
    for h in hs:
        f_state = jnp.exp(b_last[h] + m[h] - m_new[h])
        c_ref[0, h] = f_state * c[h] + kv[h]
        n_ref[0, 0:1, h * ML_DK:(h + 1) * ML_DK] = f_state * n[h] + jnp.sum(kw[h], axis=0, keepdims=True)
        m_ref[0, 0:1, h:h + 1] = m_new[h]
    for h in hs:
        og = cols_ref[:, 1024 + h * ML_DV:1024 + (h + 1) * ML_DV]
        num = w_inter[h] * qc[h] + sv[h]
        den = w_inter[h] * jnp.sum(q[h] * n[h], -1, keepdims=True) + jnp.sum(s[h], -1, keepdims=True)
        hh = num / jnp.maximum(jnp.abs(den), jnp.exp(-m_t[h]))
        hn = hh * lax.rsqrt(jnp.mean(hh * hh, -1, keepdims=True) + NORM_EPS) * ng_ref[:, h * ML_DV:(h + 1) * ML_DV]
        h_ref[:, h * ML_DV:(h + 1) * ML_DV] = (hn * jax.nn.sigmoid(og)).astype(h_ref.dtype)


def _chunk_grid_specs(bp, t, width):
    nc = t // CHUNK
    cols = pl.BlockSpec((CHUNK, width), lambda b, c: (b * nc + c, 0))
    out = pl.BlockSpec((CHUNK, GROUP), lambda b, c: (b * nc + c, 0))
    return (bp, nc), cols, out


def _mlstm_prompt(cols, gb, ng, l, bp, t):
    grid, cspec, ospec = _chunk_grid_specs(bp, t, W_ML)
    return pl.pallas_call(
        _mlstm_prompt_body, grid=grid,
        in_specs=[cspec, _layer_spec(l, (1, LANES)), _layer_spec(l, (1, GROUP))],
        out_specs=[ospec, pl.BlockSpec((1, ML_H, ML_DK, ML_DV), lambda b, c: (b, 0, 0, 0)),
                   pl.BlockSpec((1, 1, ML_H * ML_DK), lambda b, c: (b, 0, 0)),
                   pl.BlockSpec((1, 1, LANES), lambda b, c: (b, 0, 0))],
        out_shape=[jax.ShapeDtypeStruct((bp * t, GROUP), bf16), jax.ShapeDtypeStruct((bp, ML_H, ML_DK, ML_DV), f32),
                   jax.ShapeDtypeStruct((bp, 1, ML_H * ML_DK), f32), jax.ShapeDtypeStruct((bp, 1, LANES), f32)],
        compiler_params=_cparams(("parallel", "arbitrary")), name="mlstm_prompt")(cols, gb, ng)


def _gla_prompt_body(cols_ref, gup_ref, gb_ref, ng_ref, h_ref, s_ref):
    @pl.when(pl.program_id(1) == 0)
    def _():
        s_ref[...] = jnp.zeros_like(s_ref)

    L = CHUNK
    causal, _ = _tri_masks(L)
    tril = jnp.where(causal, 1.0, 0.0)
    la = _log_sigmoid(_ein(cols_ref[:, 1536:1664], gup_ref[...]) + gb_ref[...]) / GLA_TAU
    bc = _exact(tril, la)
    mid = bc[L // 2:L // 2 + 1, :]
    last = bc[L - 1:L, :]
    e_q_mid, e_k_mid = jnp.exp(bc - mid), jnp.exp(mid - bc)
    e_q, e_k_last, e_last = jnp.exp(bc), jnp.exp(last - bc), jnp.exp(last)
    hs = range(GLA_H)
    sl = [slice(h * GLA_DK, (h + 1) * GLA_DK) for h in hs]
    q = [cols_ref[:, h * GLA_DK:(h + 1) * GLA_DK] * GLA_DK ** -0.5 for h in hs]
    k = [cols_ref[:, 256 + h * GLA_DK:256 + (h + 1) * GLA_DK] for h in hs]
    v = [cols_ref[:, 512 + h * GLA_DV:512 + (h + 1) * GLA_DV] for h in hs]
    s = [s_ref[0, h] for h in hs]
    a = [jnp.where(causal, _ein_nt(q[h] * e_q_mid[:, sl[h]], k[h] * e_k_mid[:, sl[h]]), 0.0) for h in hs]
    o0 = [_ein(q[h] * e_q[:, sl[h]], s[h]) for h in hs]
    kv = [_ein_tn(k[h] * e_k_last[:, sl[h]], v[h]) for h in hs]
    o = [o0[h] + _ein(a[h], v[h]) for h in hs]
    for h in hs:
        s_ref[0, h] = _col(e_last[:, sl[h]]) * s[h] + kv[h]
    for h in hs:
        og = cols_ref[:, 1024 + h * GLA_DV:1024 + (h + 1) * GLA_DV]
        on = o[h] * lax.rsqrt(jnp.mean(o[h] * o[h], -1, keepdims=True) + NORM_EPS) * ng_ref[:, h * GLA_DV:(h + 1) * GLA_DV]
        h_ref[:, h * GLA_DV:(h + 1) * GLA_DV] = (on * _silu(og)).astype(h_ref.dtype)


def _gla_prompt(cols, gup, gb, ng, l, bp, t):
    grid, cspec, ospec = _chunk_grid_specs(bp, t, W_GLA)
    return pl.pallas_call(
        _gla_prompt_body, grid=grid,
        in_specs=[cspec, _layer_spec(l, (LANES, GLA_H * GLA_DK)), _layer_spec(l, (1, GLA_H * GLA_DK)),
                  _layer_spec(l, (1, GROUP))],
        out_specs=[ospec, pl.BlockSpec((1, GLA_H, GLA_DK, GLA_DV), lambda b, c: (b, 0, 0, 0))],
        out_shape=[jax.ShapeDtypeStruct((bp * t, GROUP), bf16), jax.ShapeDtypeStruct((bp, GLA_H, GLA_DK, GLA_DV), f32)],
        compiler_params=_cparams(("parallel", "arbitrary")), name="gla_prompt")(cols, gup, gb, ng)


def _rwkv_pre(x, prev, mu, wup, aup, gup, vec):
    xs = x + (prev - x) * mu
    r, k, v, lb = xs[:, 0:512], xs[:, 512:1024], xs[:, 1024:1536], xs[:, 1536:1664]
    w0, a0, k_k, k_a = vec[0:1, :], vec[1:2, :], vec[2:3, :], vec[3:4, :]
    lw = -RW_DECAY_SCALE * jax.nn.sigmoid(w0 + _ein(jnp.tanh(lb), wup))
    a = jax.nn.sigmoid(a0 + _ein(lb, aup))
    g = _ein(jax.nn.sigmoid(lb), gup)
    kk_raw = k * k_k
    k2 = k * (1.0 + (a - 1.0) * k_a)
    return r, k2, v, lw, a, g, kk_raw


def _rwkv_post(o, r, k2, v, g, vec, sl):
    r_k, n_g, n_b = vec[4:5, sl], vec[5:6, sl], vec[6:7, sl]
    mu_o = jnp.mean(o, -1, keepdims=True)
    oc = o - mu_o
    var_o = jnp.mean(oc * oc, -1, keepdims=True)
    on = oc * lax.rsqrt(var_o + RW_GN_EPS) * n_g + n_b
    bonus = jnp.sum(r[:, sl] * k2[:, sl] * r_k, -1, keepdims=True) * v[:, sl]
    return (on + bonus) * g[:, sl]


def _rwkv_prompt_body(cols_ref, mu_ref, wup_ref, aup_ref, gup_ref, vec_ref, h_ref, s_ref, prev_scr):
    @pl.when(pl.program_id(1) == 0)
    def _():
        s_ref[...] = jnp.zeros_like(s_ref)
        prev_scr[...] = jnp.zeros_like(prev_scr)

    L, N = CHUNK, RW_N
    x = cols_ref[...]
    prev = jnp.where(_iota(x.shape, 0) == 0, prev_scr[0:1, :], pltpu.roll(x, 1, 0))
    prev_scr[0:1, :] = x[L - 1:L, :]
    vec = vec_ref[...]
    r, k2, v, lw, a, g, kk_raw = _rwkv_pre(x, prev, mu_ref[...], wup_ref[...], aup_ref[...], gup_ref[...], vec)
    causal, _ = _tri_masks(L)
    cum = _exact(jnp.where(causal, 1.0, 0.0), lw)
    cum_prev = cum - lw
    mid = cum[L // 2:L // 2 + 1, :]
    last = cum[L - 1:L, :]
    e_prev_mid, e_mid_cum, e_cum_mid = jnp.exp(cum_prev - mid), jnp.exp(mid - cum), jnp.exp(cum - mid)
    e_prev, e_cum, e_last_cum, e_last = jnp.exp(cum_prev), jnp.exp(cum), jnp.exp(last - cum), jnp.exp(last)
    row, col = _iota((2 * L, 2 * L), 0), _iota((2 * L, 2 * L), 1)
    tq, sq = row % L, col % L
    quad = sq < tq + jnp.where(row < L, 0, 1)
    left = col[:, :] < L
    hs = range(RW_H)
    sl = [slice(h * N, (h + 1) * N) for h in hs]
    kk = [_l2norm(kk_raw[:, sl[h]]) for h in hs]
    b = [kk[h] * a[:, sl[h]] for h in hs]
    vh = [v[:, sl[h]] for h in hs]
    s = [s_ref[0, h] for h in hs]
    lhs = [jnp.concatenate([kk[h] * e_prev_mid[:, sl[h]], r[:, sl[h]] * e_cum_mid[:, sl[h]]], axis=0) for h in hs]
    rhs = [jnp.concatenate([b[h] * e_mid_cum[:, sl[h]], k2[:, sl[h]] * e_mid_cum[:, sl[h]]], axis=0) for h in hs]
    a_all = [jnp.where(quad, _ein_nt(lhs[h], rhs[h]), 0.0) for h in hs]
    t_inv = _unit_lower_inverses([a_all[h][0:L, 0:L] for h in hs])
    x0 = [jnp.concatenate([kk[h] * e_prev[:, sl[h]], r[:, sl[h]] * e_cum[:, sl[h]]], axis=0) for h in hs]
    m2 = [jnp.where(left, jnp.concatenate([x0[h], x0[h]], axis=1), a_all[h]) for h in hs]
    y = [_ein(m2[h], jnp.concatenate([s[h], vh[h]], axis=0)) for h in hs]
    u = [-_dot3(t_inv[h], y[h][0:L, :]) for h in hs]
    o = [y[h][L:2 * L, :] + _ein(a_all[h][L:2 * L, 0:L], u[h]) for h in hs]
    for h in hs:
        decayed = jnp.concatenate([b[h] * e_last_cum[:, sl[h]], k2[:, sl[h]] * e_last_cum[:, sl[h]]], axis=0)
        s_ref[0, h] = _col(e_last[:, sl[h]]) * s[h] + _ein_tn(decayed, jnp.concatenate([u[h], vh[h]], axis=0))
    outs = [_rwkv_post(o[h], r, k2, v, g, vec, sl[h]) for h in hs]
    h_ref[...] = jnp.concatenate(outs, axis=-1).astype(h_ref.dtype)


def _rwkv_prompt(cols, mu, wup, aup, gup, vec, l, bp, t):
    grid, cspec, ospec = _chunk_grid_specs(bp, t, W_RW)
    return pl.pallas_call(
        _rwkv_prompt_body, grid=grid,
        in_specs=[cspec, _layer_spec(l, (1, W_RW)), _layer_spec(l, (LANES, GROUP)), _layer_spec(l, (LANES, GROUP)),
                  _layer_spec(l, (LANES, GROUP)), _layer_spec(l, (8, GROUP))],
        out_specs=[ospec, pl.BlockSpec((1, RW_H, RW_N, RW_N), lambda b, c: (b, 0, 0, 0))],
        out_shape=[jax.ShapeDtypeStruct((bp * t, GROUP), bf16), jax.ShapeDtypeStruct((bp, RW_H, RW_N, RW_N), f32)],
        scratch_shapes=[pltpu.VMEM((8, W_RW), f32)],
        compiler_params=_cparams(("parallel", "arbitrary")), name="rwkv_prompt")(cols, mu, wup, aup, gup, vec)


def _gdn_gates(gb, alog, dtb):
    return jax.nn.sigmoid(gb), -jnp.exp(alog) * _softplus(gb + dtb)


def _gdn_prompt_body(cols_ref, cw_ref, alog_ref, dtb_ref, ng_ref, h_ref, s_ref, xp_scr):
    @pl.when(pl.program_id(1) == 0)
    def _():
        s_ref[...] = jnp.zeros_like(s_ref)
        xp_scr[0:8, :] = jnp.zeros((8, DN_QKV), f32)

    L = CHUNK
    causal, strict = _tri_masks(L)
    tril = jnp.where(causal, 1.0, 0.0)
    raw = cols_ref[:, 0:DN_QKV]
    xp_scr[8:8 + L, :] = raw
    conv = xp_scr[5:5 + L, :] * cw_ref[0:1, :]
    for w in range(1, DN_CONV):
        conv = conv + xp_scr[5 + w:5 + w + L, :] * cw_ref[w:w + 1, :]
    xp_scr[0:8, :] = raw[L - 8:L, :]
    act = _silu(conv)
    beta_all, g_all = _gdn_gates(cols_ref[:, 2048:2176], alog_ref[...], dtb_ref[...])
    gc = _exact(tril, g_all)
    gt = _rows_of(gc)
    hs = range(DN_H)
    q = [_l2norm(act[:, h * DN_D:(h + 1) * DN_D]) * DN_D ** -0.5 for h in hs]
    k = [_l2norm(act[:, GROUP + h * DN_D:GROUP + (h + 1) * DN_D]) for h in hs]
    v = [act[:, 2 * GROUP + h * DN_D:2 * GROUP + (h + 1) * DN_D] for h in hs]
    beta = [beta_all[:, h:h + 1] for h in hs]
    g_col = [gc[:, DN_H + h:DN_H + h + 1] for h in hs]
    decay = [jnp.exp(jnp.where(causal, g_col[h] - gt[DN_H + h:DN_H + h + 1, :], -jnp.inf)) for h in hs]
    kb = [k[h] * beta[h] for h in hs]
    eg = [jnp.exp(g_col[h]) for h in hs]
    s = [s_ref[0, h] for h in hs]
    kq = [_ein_nt(jnp.concatenate([kb[h], q[h]], axis=0), k[h]) for h in hs]
    t_inv = _unit_lower_inverses([jnp.where(strict, kq[h][0:L, :] * decay[h], 0.0) for h in hs])
    qk = [kq[h][L:2 * L, :] * decay[h] for h in hs]
    o0 = [_ein(q[h] * eg[h], s[h]) for h in hs]
    sol = [_dot3(t_inv[h], jnp.concatenate([v[h] * beta[h], kb[h] * eg[h]], axis=-1)) for h in hs]
    v_new = [sol[h][:, :DN_D] - _ein(sol[h][:, DN_D:], s[h]) for h in hs]
    o = [o0[h] + _ein(qk[h], v_new[h]) for h in hs]
    for h in hs:
        g_last = g_col[h][L - 1:L, :]
        s_ref[0, h] = jnp.exp(g_last) * s[h] + _ein_tn(k[h] * jnp.exp(g_last - g_col[h]), v_new[h])
    for h in hs:
        z = cols_ref[:, DN_QKV + h * DN_D:DN_QKV + (h + 1) * DN_D]
        on = o[h] * lax.rsqrt(jnp.mean(o[h] * o[h], -1, keepdims=True) + NORM_EPS) * ng_ref[...]
        h_ref[:, h * DN_D:(h + 1) * DN_D] = (on * _silu(z)).astype(h_ref.dtype)


def _gdn_prompt(cols, cw, alog, dtb, ng, l, bp, t):
    grid, cspec, ospec = _chunk_grid_specs(bp, t, W_DN)
    return pl.pallas_call(
        _gdn_prompt_body, grid=grid,
        in_specs=[cspec, _layer_spec(l, (8, DN_QKV)), _layer_spec(l, (1, LANES)), _layer_spec(l, (1, LANES)),
                  _layer_spec(l, (1, DN_D))],
        out_specs=[ospec, pl.BlockSpec((1, DN_H, DN_D, DN_D), lambda b, c: (b, 0, 0, 0))],
        out_shape=[jax.ShapeDtypeStruct((bp * t, GROUP), bf16), jax.ShapeDtypeStruct((bp, DN_H, DN_D, DN_D), f32)],
        scratch_shapes=[pltpu.VMEM((8 + CHUNK, DN_QKV), f32)],
        compiler_params=_cparams(("parallel", "arbitrary")), name="gdn_prompt")(cols, cw, alog, dtb, ng)


def _mlstm_step_body(cols_ref, gb_ref, ng_ref, c_ref, n_ref, m_ref, h_ref, co_ref, no_ref, mo_ref,
                     q_scr, kw_scr, v_scr, f_scr, qc_scr):
    nb = DEC_BLOCK
    g = cols_ref[:, 1536:1664] + gb_ref[...]
    lf = _log_sigmoid(g)
    hs = []
    for h in range(ML_H):
        q = cols_ref[:, h * ML_DK:(h + 1) * ML_DK]
        k = cols_ref[:, 256 + h * ML_DK:256 + (h + 1) * ML_DK] * ML_DK ** -0.5
        v = cols_ref[:, 512 + h * ML_DV:512 + (h + 1) * ML_DV]
        i_pre, f_log = g[:, h:h + 1], lf[:, ML_H + h:ML_H + h + 1]
        m = m_ref[:, h:h + 1]
        inter = f_log + m
        m_t = jnp.maximum(inter, i_pre)
        w_inter = jnp.exp(inter - m_t)
        kw = k * jnp.exp(i_pre - m_t)
        q_scr[h], kw_scr[h], v_scr[h] = q, kw, v
        f_scr[h] = jnp.broadcast_to(w_inter, (nb, LANES))
        hs.append((q, k, v, i_pre, m_t, w_inter, kw))

    def row(b, carry):
        hr = range(ML_H)
        q_col = [_col(q_scr[h, pl.ds(b, 1), :]) for h in hr]
        kw_col = [_col(kw_scr[h, pl.ds(b, 1), :]) for h in hr]
        c = [c_ref[b, h] for h in hr]
        for h in hr:
            qc_scr[h, pl.ds(b, 1), :] = jnp.sum(q_col[h] * c[h], axis=0, keepdims=True)
            co_ref[b, h] = f_scr[h, pl.ds(b, 1), :] * c[h] + kw_col[h] * v_scr[h, pl.ds(b, 1), :]
        return carry

    lax.fori_loop(0, nb, row, 0)
    for h, (q, k, v, i_pre, m_t, w_inter, kw) in enumerate(hs):
        og = cols_ref[:, 1024 + h * ML_DV:1024 + (h + 1) * ML_DV]
        n = n_ref[:, h * ML_DK:(h + 1) * ML_DK]
        s = jnp.sum(q * k, -1, keepdims=True) * jnp.exp(i_pre - m_t)
        num = w_inter * qc_scr[h] + s * v
        den = w_inter * jnp.sum(q * n, -1, keepdims=True) + s
        hh = num / jnp.maximum(jnp.abs(den), jnp.exp(-m_t))
        no_ref[:, h * ML_DK:(h + 1) * ML_DK] = w_inter * n + kw
        mo_ref[:, h:h + 1] = m_t
        hn = hh * lax.rsqrt(jnp.mean(hh * hh, -1, keepdims=True) + NORM_EPS) * ng_ref[:, h * ML_DV:(h + 1) * ML_DV]
        h_ref[:, h * ML_DV:(h + 1) * ML_DV] = (hn * jax.nn.sigmoid(og)).astype(h_ref.dtype)


def _row_spec(width):
    return pl.BlockSpec((DEC_BLOCK, width), lambda i: (i, 0))


def _layer_rows(l, width):
    return _layer_spec(l, (DEC_BLOCK, width), lambda i: (i, 0))


def _layer_state(l, h, dk, dv):
    return _layer_spec(l, (DEC_BLOCK, h, dk, dv), lambda i: (i, 0, 0, 0))


def _ignore_first_ref(body):
    def with_handed_on_buffer(_stacked_out_so_far, *refs):
        body(*refs)
    return with_handed_on_buffer


def _step_call(body, l, stacked_out, in_specs, out_specs, out_shapes, state_out_index, **kw):
    if stacked_out is None:
        return lambda *args: pl.pallas_call(body, in_specs=in_specs, out_specs=out_specs, out_shape=out_shapes, **kw)(*args)
    call = pl.pallas_call(_ignore_first_ref(body), in_specs=[pl.BlockSpec(memory_space=pl.ANY)] + in_specs,
                          out_specs=out_specs, out_shape=out_shapes, input_output_aliases={0: state_out_index}, **kw)
    return lambda *args: call(stacked_out, *args)


def _mlstm_step(cols, gb, ng, c0, n0, m0, l, c_out):
    bs = cols.shape[0]
    nb = DEC_BLOCK
    call = _step_call(
        _mlstm_step_body, l, c_out,
        in_specs=[_row_spec(W_ML), _layer_spec(l, (1, LANES)), _layer_spec(l, (1, GROUP)),
                  _layer_state(l, ML_H, ML_DK, ML_DV), _layer_rows(l, ML_H * ML_DK), _layer_rows(l, ML_H)],
        out_specs=[_row_spec(GROUP), _layer_state(l, ML_H, ML_DK, ML_DV), _row_spec(ML_H * ML_DK), _row_spec(ML_H)],
        out_shapes=[jax.ShapeDtypeStruct((bs, GROUP), bf16), jax.ShapeDtypeStruct(c0.shape, f32),
                    jax.ShapeDtypeStruct((bs, ML_H * ML_DK), f32), jax.ShapeDtypeStruct((bs, ML_H), f32)],
        state_out_index=1, grid=(bs // nb,),
        scratch_shapes=[pltpu.VMEM((ML_H, nb, ML_DK), f32), pltpu.VMEM((ML_H, nb, ML_DK), f32),
                        pltpu.VMEM((ML_H, nb, ML_DV), f32), pltpu.VMEM((ML_H, nb, LANES), f32),
                        pltpu.VMEM((ML_H, nb, ML_DV), f32)],
        compiler_params=_cparams(("parallel",)), name="mlstm_step")
    return call(cols, gb, ng, c0, n0, m0)


def _gla_step_body(cols_ref, gup_ref, gb_ref, ng_ref, s_ref, h_ref, so_ref, qe_scr, ea_scr, k_scr, v_scr, qs_scr):
    nb = DEC_BLOCK
    la = _log_sigmoid(_ein(cols_ref[:, 1536:1664], gup_ref[...]) + gb_ref[...]) / GLA_TAU
    ea = jnp.exp(la)
    hs = []
    for h in range(GLA_H):
        q = cols_ref[:, h * GLA_DK:(h + 1) * GLA_DK] * GLA_DK ** -0.5
        k = cols_ref[:, 256 + h * GLA_DK:256 + (h + 1) * GLA_DK]
        v = cols_ref[:, 512 + h * GLA_DV:512 + (h + 1) * GLA_DV]
        ea_h = ea[:, h * GLA_DK:(h + 1) * GLA_DK]
        qe_scr[h], ea_scr[h], k_scr[h], v_scr[h] = q * ea_h, ea_h, k, v
        hs.append((q, k, v))

    def row(b, carry):
        hr = range(GLA_H)
        qe_col = [_col(qe_scr[h, pl.ds(b, 1), :]) for h in hr]
        ea_col = [_col(ea_scr[h, pl.ds(b, 1), :]) for h in hr]
        k_col = [_col(k_scr[h, pl.ds(b, 1), :]) for h in hr]
        s = [s_ref[b, h] for h in hr]
        for h in hr:
            qs_scr[h, pl.ds(b, 1), :] = jnp.sum(qe_col[h] * s[h], axis=0, keepdims=True)
            so_ref[b, h] = ea_col[h] * s[h] + k_col[h] * v_scr[h, pl.ds(b, 1), :]
        return carry

    lax.fori_loop(0, nb, row, 0)
    for h, (q, k, v) in enumerate(hs):
        og = cols_ref[:, 1024 + h * GLA_DV:1024 + (h + 1) * GLA_DV]
        o = qs_scr[h] + jnp.sum(q * k, -1, keepdims=True) * v
        on = o * lax.rsqrt(jnp.mean(o * o, -1, keepdims=True) + NORM_EPS) * ng_ref[:, h * GLA_DV:(h + 1) * GLA_DV]
        h_ref[:, h * GLA_DV:(h + 1) * GLA_DV] = (on * _silu(og)).astype(h_ref.dtype)


def _gla_step(cols, gup, gb, ng, s0, l, s_out):
    bs = cols.shape[0]
    nb = DEC_BLOCK
    call = _step_call(
        _gla_step_body, l, s_out,
        in_specs=[_row_spec(W_GLA), _layer_spec(l, (LANES, GLA_H * GLA_DK)), _layer_spec(l, (1, GLA_H * GLA_DK)),
                  _layer_spec(l, (1, GROUP)), _layer_state(l, GLA_H, GLA_DK, GLA_DV)],
        out_specs=[_row_spec(GROUP), _layer_state(l, GLA_H, GLA_DK, GLA_DV)],
        out_shapes=[jax.ShapeDtypeStruct((bs, GROUP), bf16), jax.ShapeDtypeStruct(s0.shape, f32)],
        state_out_index=1, grid=(bs // nb,),
        scratch_shapes=[pltpu.VMEM((GLA_H, nb, GLA_DK), f32)] * 3 + [pltpu.VMEM((GLA_H, nb, GLA_DV), f32)] * 2,
        compiler_params=_cparams(("parallel",)), name="gla_step")
    return call(cols, gup, gb, ng, s0)


def _rwkv_step_body(cols_ref, shift_ref, mu_ref, wup_ref, aup_ref, gup_ref, vec_ref, s_ref, h_ref, so_ref,
                    kk_scr, w_scr, b_scr, k_scr, r_scr, v_scr, o_scr):
    nb = DEC_BLOCK
    vec = vec_ref[...]
    r, k2, v, lw, a, g, kk_raw = _rwkv_pre(cols_ref[...], shift_ref[...], mu_ref[...], wup_ref[...], aup_ref[...],
                                           gup_ref[...], vec)
    w = jnp.exp(lw)
    for h in range(RW_H):
        sl = slice(h * RW_N, (h + 1) * RW_N)
        kk = _l2norm(kk_raw[:, sl])
        kk_scr[h], w_scr[h], b_scr[h], k_scr[h], r_scr[h], v_scr[h] = kk, w[:, sl], kk * a[:, sl], k2[:, sl], r[:, sl], v[:, sl]

    def row(b, carry):
        hr = range(RW_H)
        kk_col = [_col(kk_scr[h, pl.ds(b, 1), :]) for h in hr]
        w_col = [_col(w_scr[h, pl.ds(b, 1), :]) for h in hr]
        b_col = [_col(b_scr[h, pl.ds(b, 1), :]) for h in hr]
        k_col = [_col(k_scr[h, pl.ds(b, 1), :]) for h in hr]
        r_col = [_col(r_scr[h, pl.ds(b, 1), :]) for h in hr]
        s = [s_ref[b, h] for h in hr]
        sk = [jnp.sum(kk_col[h] * s[h], axis=0, keepdims=True) for h in hr]
        s_new = [w_col[h] * s[h] - b_col[h] * sk[h] + k_col[h] * v_scr[h, pl.ds(b, 1), :] for h in hr]
        for h in hr:
            so_ref[b, h] = s_new[h]
            o_scr[h, pl.ds(b, 1), :] = jnp.sum(r_col[h] * s_new[h], axis=0, keepdims=True)
        return carry

    lax.fori_loop(0, nb, row, 0)
    outs = [_rwkv_post(o_scr[h], r, k2, v, g, vec, slice(h * RW_N, (h + 1) * RW_N)) for h in range(RW_H)]
    h_ref[...] = jnp.concatenate(outs, axis=-1).astype(h_ref.dtype)


def _rwkv_step(cols, shift0, mu, wup, aup, gup, vec, s0, l, s_out):
    bs = cols.shape[0]
    nb = DEC_BLOCK
    call = _step_call(
        _rwkv_step_body, l, s_out,
        in_specs=[_row_spec(W_RW), _layer_rows(l, W_RW), _layer_spec(l, (1, W_RW)), _layer_spec(l, (LANES, GROUP)),
                  _layer_spec(l, (LANES, GROUP)), _layer_spec(l, (LANES, GROUP)), _layer_spec(l, (8, GROUP)),
                  _layer_state(l, RW_H, RW_N, RW_N)],
        out_specs=[_row_spec(GROUP), _layer_state(l, RW_H, RW_N, RW_N)],
        out_shapes=[jax.ShapeDtypeStruct((bs, GROUP), bf16), jax.ShapeDtypeStruct(s0.shape, f32)],
        state_out_index=1, grid=(bs // nb,),
        scratch_shapes=[pltpu.VMEM((RW_H, nb, RW_N), f32)] * 7,
        compiler_params=_cparams(("parallel",)), name="rwkv_step")
    return call(cols, shift0, mu, wup, aup, gup, vec, s0)


def _gdn_step_body(cols_ref, buf_ref, cw_ref, alog_ref, dtb_ref, ng_ref, s_ref, h_ref, so_ref,
                   q_scr, k_scr, v_scr, sc_scr, qs_scr, vn_scr):
    nb = DEC_BLOCK
    conv = cols_ref[:, 0:DN_QKV] * cw_ref[DN_CONV - 1:DN_CONV, :]
    for w in range(DN_CONV - 1):
        conv = conv + buf_ref[:, w * DN_QKV:(w + 1) * DN_QKV] * cw_ref[w:w + 1, :]
    act = _silu(conv)
    beta_all, g_all = _gdn_gates(cols_ref[:, 2048:2176], alog_ref[...], dtb_ref[...])
    eg_all = jnp.exp(g_all)
    hs = []
    for h in range(DN_H):
        q = _l2norm(act[:, h * DN_D:(h + 1) * DN_D]) * DN_D ** -0.5
        k = _l2norm(act[:, GROUP + h * DN_D:GROUP + (h + 1) * DN_D])
        v = act[:, 2 * GROUP + h * DN_D:2 * GROUP + (h + 1) * DN_D]
        beta, eg = beta_all[:, h:h + 1], eg_all[:, DN_H + h:DN_H + h + 1]
        q_scr[h], k_scr[h], v_scr[h] = q, k, v
        sc_scr[h] = jnp.where(_iota((nb, LANES), 1) == 0, beta, eg)
        hs.append((q, k, eg))

    def row(b, carry):
        hr = range(DN_H)
        k_col = [_col(k_scr[h, pl.ds(b, 1), :]) for h in hr]
        q_col = [_col(q_scr[h, pl.ds(b, 1), :]) for h in hr]
        s = [s_ref[b, h] for h in hr]
        sc = [sc_scr[h, pl.ds(b, 1), :] for h in hr]
        ks = [jnp.sum(k_col[h] * s[h], axis=0, keepdims=True) for h in hr]
        for h in hr:
            beta, eg = sc[h][:, 0:1], sc[h][:, 1:2]
            v_new = beta * (v_scr[h, pl.ds(b, 1), :] - eg * ks[h])
            qs_scr[h, pl.ds(b, 1), :] = jnp.sum(q_col[h] * s[h], axis=0, keepdims=True)
            vn_scr[h, pl.ds(b, 1), :] = v_new
            so_ref[b, h] = eg * s[h] + k_col[h] * v_new
        return carry

    lax.fori_loop(0, nb, row, 0)
    for h, (q, k, eg) in enumerate(hs):
        z = cols_ref[:, DN_QKV + h * DN_D:DN_QKV + (h + 1) * DN_D]
        o = eg * qs_scr[h] + jnp.sum(q * k, -1, keepdims=True) * vn_scr[h]
        on = o * lax.rsqrt(jnp.mean(o * o, -1, keepdims=True) + NORM_EPS) * ng_ref[...]
        h_ref[:, h * DN_D:(h + 1) * DN_D] = (on * _silu(z)).astype(h_ref.dtype)


def _gdn_step(cols, buf, cw, alog, dtb, ng, s0, l, s_out):
    bs = cols.shape[0]
    nb = DEC_BLOCK
    call = _step_call(
        _gdn_step_body, l, s_out,
        in_specs=[_row_spec(W_DN), _layer_rows(l, (DN_CONV - 1) * DN_QKV), _layer_spec(l, (8, DN_QKV)),
                  _layer_spec(l, (1, LANES)), _layer_spec(l, (1, LANES)), _layer_spec(l, (1, DN_D)),
                  _layer_state(l, DN_H, DN_D, DN_D)],
        out_specs=[_row_spec(GROUP), _layer_state(l, DN_H, DN_D, DN_D)],
        out_shapes=[jax.ShapeDtypeStruct((bs, GROUP), bf16), jax.ShapeDtypeStruct(s0.shape, f32)],
        state_out_index=1, grid=(bs // nb,),
        scratch_shapes=[pltpu.VMEM((DN_H, nb, DN_D), f32)] * 3 + [pltpu.VMEM((DN_H, nb, LANES), f32)]
        + [pltpu.VMEM((DN_H, nb, DN_D), f32)] * 2,
        compiler_params=_cparams(("parallel",)), name="gdn_step")
    return call(cols, buf, cw, alog, dtb, ng, s0)


def _pad_last(w, width):
    return jnp.pad(w, [(0, 0)] * (w.ndim - 1) + [(0, width - w.shape[-1])])


def _pad_rows(w, start, total):
    return jnp.pad(w, ((0, 0), (start, total - start - w.shape[1]), (0, 0)))


def _lane_rows(v, start):
    return jnp.pad(v, ((0, 0), (start, LANES - start - v.shape[1])))[:, None, :]


def kernel(x_prompt, x_sample, state_mlstm_c, state_mlstm_n, state_mlstm_m, state_gla, state_rwkv, state_rwkv_shift, state_dn, state_dn_conv, w_in, w_out, w_up, w_down, ln1_g, ln1_b, ln2_g, ln2_b, mlstm_gate_b, mlstm_norm_g, gla_gate_up, gla_gate_b, gla_norm_g, rwkv_mu, rwkv_w_up, rwkv_w0, rwkv_a_up, rwkv_a0, rwkv_g_up, rwkv_k_k, rwkv_k_a, rwkv_r_k, rwkv_norm_g, rwkv_norm_b, dn_conv_w, dn_a_log, dn_dt_bias, dn_norm_g):
    bp, t, _ = x_prompt.shape
    bs = x_sample.shape[0]
    xp = x_prompt.reshape(bp * t, D_MODEL)
    xs = x_sample.reshape(bs, D_MODEL)
    xpb, xsb = xp.astype(bf16), xs.astype(bf16)
    tm_p, tm_s, tf = 512, bs, 1024

    o1, o2, o3 = N_ML, N_ML + N_GLA, N_ML + N_GLA + N_RW
    w_secs = [_pad_last(w_in[:, :, 0:o1], W_ML).astype(bf16), _pad_last(w_in[:, :, o1:o2], W_GLA).astype(bf16),
              w_in[:, :, o2:o3].astype(bf16), _pad_last(w_in[:, :, o3:], W_DN).astype(bf16)]
    wo, wu, wd = w_out.astype(bf16), w_up.astype(bf16), w_down.astype(bf16)
    g1, b1, g2, b2 = ln1_g[:, None], ln1_b[:, None], ln2_g[:, None], ln2_b[:, None]
    ml_gb, ml_ng = _lane_rows(mlstm_gate_b, 0), mlstm_norm_g[:, None]
    gla_gup, gla_gb, gla_ng = _pad_rows(gla_gate_up, 0, LANES), gla_gate_b[:, None], gla_norm_g[:, None]
    rw_mu = rwkv_mu[:, None]
    rw_wup, rw_aup, rw_gup = _pad_rows(rwkv_w_up, 0, LANES), _pad_rows(rwkv_a_up, 32, LANES), _pad_rows(rwkv_g_up, 64, LANES)
    rw_vec = jnp.stack([rwkv_w0, rwkv_a0, rwkv_k_k, rwkv_k_a, rwkv_r_k, rwkv_norm_g, rwkv_norm_b,
                        jnp.zeros_like(rwkv_w0)], axis=1)
    dn_cw = _pad_rows(dn_conv_w, 0, 8)
    dn_alog, dn_dtb, dn_ng = _lane_rows(dn_a_log, DN_H), _lane_rows(dn_dt_bias, DN_H), dn_norm_g[:, None]
    st_ml_n = state_mlstm_n.reshape(DEPTH, bs, ML_H * ML_DK)
    st_dn_conv = state_dn_conv.reshape(DEPTH, bs, (DN_CONV - 1) * DN_QKV)

    outs_p, outs_s = [], []
    ml_c_s = gla_s = rw_s = dn_s = None
    for l in range(DEPTH):
        c_ml, c_gla, c_rw, c_dn = [_proj(xpb, w, l, tm_p) for w in w_secs]
        h_ml, mc, mn, mm = _mlstm_prompt(c_ml, ml_gb, ml_ng, l, bp, t)
        h_gla, gs = _gla_prompt(c_gla, gla_gup, gla_gb, gla_ng, l, bp, t)
        h_rw, rs = _rwkv_prompt(c_rw, rw_mu, rw_wup, rw_aup, rw_gup, rw_vec, l, bp, t)
        h_dn, ds = _gdn_prompt(c_dn, dn_cw, dn_alog, dn_dtb, dn_ng, l, bp, t)
        x1, x1b = _outproj_ln((h_ml, h_gla, h_rw, h_dn), wo, xp, g1, b1, l, tm_p)
        xp, xpb = _ffn_ln(x1, x1b, wu, wd, g2, b2, l, tm_p, tf)
        outs_p.append((mc, mn.reshape(bp, ML_H, ML_DK), mm[:, 0, :ML_H], gs, rs, c_rw.reshape(bp, t, W_RW)[:, -1],
                       ds, c_dn.reshape(bp, t, W_DN)[:, t - (DN_CONV - 1):, :DN_QKV]))

        c_ml, c_gla, c_rw, c_dn = [_proj(xsb, w, l, tm_s) for w in w_secs]
        h_ml, ml_c_s, mn, mm = _mlstm_step(c_ml, ml_gb, ml_ng, state_mlstm_c, st_ml_n, state_mlstm_m, l, ml_c_s)
        h_gla, gla_s = _gla_step(c_gla, gla_gup, gla_gb, gla_ng, state_gla, l, gla_s)
        h_rw, rw_s = _rwkv_step(c_rw, state_rwkv_shift, rw_mu, rw_wup, rw_aup, rw_gup, rw_vec, state_rwkv, l, rw_s)
        h_dn, dn_s = _gdn_step(c_dn, st_dn_conv, dn_cw, dn_alog, dn_dtb, dn_ng, state_dn, l, dn_s)
        x1, x1b = _outproj_ln((h_ml, h_gla, h_rw, h_dn), wo, xs, g1, b1, l, tm_s)
        xs, xsb = _ffn_ln(x1, x1b, wu, wd, g2, b2, l, tm_s, tf)
        outs_s.append((mn.reshape(bs, ML_H, ML_DK), mm, c_rw,
                       jnp.concatenate([state_dn_conv[l][:, 1:], c_dn[:, None, :DN_QKV]], axis=1)))

    (mlstm_c_p, mlstm_n_p, mlstm_m_p, gla_p, rwkv_p, rwkv_shift_p, dn_p, dn_conv_p) = [jnp.stack(z) for z in zip(*outs_p)]
    (mlstm_n_s, mlstm_m_s, rwkv_shift_s, dn_conv_s) = [jnp.stack(z) for z in zip(*outs_s)]
    return (xp.reshape(bp, t, D_MODEL), xs.reshape(bs, 1, D_MODEL), mlstm_c_p, ml_c_s, mlstm_n_p, mlstm_n_s,
            mlstm_m_p, mlstm_m_s, gla_p, gla_s, rwkv_p, rw_s, rwkv_shift_p, rwkv_shift_s, dn_p, dn_s,
            dn_conv_p, dn_conv_s)
```

```python
import functools

import jax
import jax.numpy as jnp
from jax import lax
from jax.experimental import pallas as pl
from jax.experimental.pallas import tpu as pltpu

f32 = jnp.float32
bf16 = jnp.bfloat16
_HI = lax.Precision.HIGHEST

D_MODEL = 2048
DEPTH = 4
GROUP = 512
D_FF = 4 * D_MODEL
CHUNK = 64
ML_H, ML_DK, ML_DV = 4, 64, 128
GLA_H, GLA_DK, GLA_DV, GLA_RANK, GLA_TAU = 4, 64, 128, 16, 16.0
RW_H, RW_N = 8, 64
RW_DECAY_SCALE = 0.606531
RW_GN_EPS = 64e-5
DN_H, DN_D, DN_CONV, DN_QKV = 4, 128, 4, 1536
N_ML, N_GLA, N_RW, N_DN = 1544, 1552, 1664, 2056
W_ML, W_GLA, W_RW, W_DN = 1664, 1664, 1664, 2176
ALPHA = (2 * DEPTH) ** 0.25
LN_EPS = 1e-5
NORM_EPS = 1e-6
LANES = 128
DEC_BLOCK = 8
VMEM_LIMIT = 56 * 1024 * 1024


def _ein(a, b):
    return jnp.dot(a.astype(bf16), b.astype(bf16), preferred_element_type=f32)


def _ein_nt(a, b):
    return lax.dot_general(a.astype(bf16), b.astype(bf16), (((1,), (1,)), ((), ())), preferred_element_type=f32)


def _ein_tn(a, b):
    return lax.dot_general(a.astype(bf16), b.astype(bf16), (((0,), (0,)), ((), ())), preferred_element_type=f32)


def _exact(a, b):
    return jnp.dot(a, b, precision=_HI, preferred_element_type=f32)


def _exact_nt(a, b):
    return lax.dot_general(a, b, (((1,), (1,)), ((), ())), precision=_HI, preferred_element_type=f32)


def _log_sigmoid(x):
    return jnp.minimum(x, 0.0) - jnp.log1p(jnp.exp(-jnp.abs(x)))


def _softplus(x):
    return jnp.maximum(x, 0.0) + jnp.log1p(jnp.exp(-jnp.abs(x)))


def _silu(x):
    return x * jax.nn.sigmoid(x)


def _l2norm(x):
    return x * lax.rsqrt(jnp.sum(x * x, -1, keepdims=True) + NORM_EPS)


def _iota(shape, axis):
    return lax.broadcasted_iota(jnp.int32, shape, axis)


def _col(row):
    n = row.shape[1]
    eye = _iota((n, n), 0) == _iota((n, n), 1)
    return jnp.sum(jnp.where(eye, row, 0.0), axis=1, keepdims=True)


def _tri_masks(n):
    r, c = _iota((n, n), 0), _iota((n, n), 1)
    return c <= r, c < r


def _split2(a):
    hi = a.astype(bf16)
    return hi, (a - hi.astype(f32)).astype(bf16)


def _dot3(a, b):
    ah, al = _split2(a)
    bh, bl = _split2(b)
    d = functools.partial(jnp.dot, preferred_element_type=f32)
    return d(ah, bh) + d(ah, bl) + d(al, bh)


def _unit_lower_inverses(a_list):
    n = a_list[0].shape[0]
    eye = jnp.where(_iota((n, n), 0) == _iota((n, n), 1), 1.0, 0.0)
    xs = [-a for a in a_list]
    ps = [eye + x for x in xs]
    for stage in range(max(n.bit_length() - 2, 0)):
        mm = _dot3 if stage == 0 else _ein
        xs = [mm(x, x) for x in xs]
        ps = [p + mm(p, x) for p, x in zip(ps, xs)]
    return ps


def _rows_of(y):
    e8 = jnp.where(_iota((8, LANES), 0) == _iota((8, LANES), 1), 1.0, 0.0)
    return _exact_nt(e8, y)


def _cparams(sem):
    return pltpu.CompilerParams(dimension_semantics=sem, vmem_limit_bytes=VMEM_LIMIT)


def _layer_spec(l, shape, index=None):
    nd = len(shape)
    if index is None:
        return pl.BlockSpec((None,) + tuple(shape), lambda *g: (l,) + (0,) * nd)
    return pl.BlockSpec((None,) + tuple(shape), lambda *g: (l,) + tuple(index(*g)))


def _proj_body(x_ref, w_ref, o_ref):
    o_ref[...] = jnp.dot(x_ref[...], w_ref[...], preferred_element_type=f32)


def _proj(xb, w, l, tm):
    m, k = xb.shape
    n = w.shape[2]
    return pl.pallas_call(
        _proj_body, grid=(m // tm,),
        in_specs=[pl.BlockSpec((tm, k), lambda i: (i, 0)), _layer_spec(l, (k, n))],
        out_specs=pl.BlockSpec((tm, n), lambda i: (i, 0)),
        out_shape=jax.ShapeDtypeStruct((m, n), f32),
        compiler_params=_cparams(("parallel",)), name="proj")(xb, w)


def _layernorm(y, g, b):
    mu = jnp.mean(y, -1, keepdims=True)
    yc = y - mu
    var = jnp.mean(yc * yc, -1, keepdims=True)
    return yc * lax.rsqrt(var + LN_EPS) * g + b


def _outproj_ln_body(h0, h1, h2, h3, w_ref, x_ref, g_ref, b_ref, o_ref, ob_ref):
    acc = ALPHA * x_ref[...]
    for j, h in enumerate((h0, h1, h2, h3)):
        acc = acc + jnp.dot(h[...], w_ref[j * GROUP:(j + 1) * GROUP, :], preferred_element_type=f32)
    y = _layernorm(acc, g_ref[...], b_ref[...])
    o_ref[...] = y
    ob_ref[...] = y.astype(bf16)


def _outproj_ln(hs, w, x, g, b, l, tm):
    m = x.shape[0]
    hspec = pl.BlockSpec((tm, GROUP), lambda i: (i, 0))
    xspec = pl.BlockSpec((tm, D_MODEL), lambda i: (i, 0))
    vspec = _layer_spec(l, (1, D_MODEL))
    return pl.pallas_call(
        _outproj_ln_body, grid=(m // tm,),
        in_specs=[hspec] * 4 + [_layer_spec(l, (D_MODEL, D_MODEL)), xspec, vspec, vspec],
        out_specs=[xspec, xspec],
        out_shape=[jax.ShapeDtypeStruct((m, D_MODEL), f32), jax.ShapeDtypeStruct((m, D_MODEL), bf16)],
        compiler_params=_cparams(("parallel",)), name="outproj_ln")(*hs, w, x, g, b)


def _ffn_ln_body(x_ref, xb_ref, wu_ref, wd_ref, g_ref, b_ref, o_ref, ob_ref, acc_ref):
    f = pl.program_id(1)

    @pl.when(f == 0)
    def _():
        acc_ref[...] = ALPHA * x_ref[...]

    h = jnp.maximum(jnp.dot(xb_ref[...], wu_ref[...], preferred_element_type=f32), 0.0)
    acc_ref[...] += jnp.dot((h * h).astype(bf16), wd_ref[...], preferred_element_type=f32)

    @pl.when(f == pl.num_programs(1) - 1)
    def _():
        y = _layernorm(acc_ref[...], g_ref[...], b_ref[...])
        o_ref[...] = y
        ob_ref[...] = y.astype(bf16)


def _ffn_ln(x, xb, wu, wd, g, b, l, tm, tf):
    m = x.shape[0]
    xspec = pl.BlockSpec((tm, D_MODEL), lambda i, f: (i, 0))
    vspec = _layer_spec(l, (1, D_MODEL))
    return pl.pallas_call(
        _ffn_ln_body, grid=(m // tm, D_FF // tf),
        in_specs=[xspec, xspec, _layer_spec(l, (D_MODEL, tf), lambda i, f: (0, f)),
                  _layer_spec(l, (tf, D_MODEL), lambda i, f: (f, 0)), vspec, vspec],
        out_specs=[xspec, xspec],
        out_shape=[jax.ShapeDtypeStruct((m, D_MODEL), f32), jax.ShapeDtypeStruct((m, D_MODEL), bf16)],
        scratch_shapes=[pltpu.VMEM((tm, D_MODEL), f32)],
        compiler_params=_cparams(("parallel", "arbitrary")), name="ffn_ln")(x, xb, wu, wd, g, b)


def _tril_blocks(nb, L):
    r, c = _iota((nb * L, nb * L), 0), _iota((nb * L, nb * L), 1)
    return jnp.where((c <= r) & (c >= r - (r & (L - 1))), 1.0, 0.0)


def _units(nb, heads):
    return [(b, h) for b in range(nb) for h in range(heads)]


def _mlstm_prompt_body(cols_ref, gb_ref, ng_ref, h_ref, c_ref, n_ref, m_ref):
    @pl.when(pl.program_id(0) == 0)
    def _():
        c_ref[...] = jnp.zeros_like(c_ref)
        n_ref[...] = jnp.zeros_like(n_ref)
        m_ref[...] = jnp.zeros_like(m_ref)

    nb, L = cols_ref.shape[0], CHUNK
    causal, _ = _tri_masks(L)
    g = cols_ref[:, :, 1536:1664].reshape(nb * L, LANES) + gb_ref[...]
    bc = _exact(_tril_blocks(nb, L), _log_sigmoid(g))
    y_all = jnp.where(_iota((nb * L, LANES), 1) < ML_H, g, bc)
    y = [y_all[b * L:(b + 1) * L, :] for b in range(nb)]
    yt = [_rows_of(y[b]) for b in range(nb)]
    us = _units(nb, ML_H)
    ix = range(len(us))
    q = [cols_ref[b, :, h * ML_DK:(h + 1) * ML_DK] for b, h in us]
    k = [cols_ref[b, :, 256 + h * ML_DK:256 + (h + 1) * ML_DK] * ML_DK ** -0.5 for b, h in us]
    v = [cols_ref[b, :, 512 + h * ML_DV:512 + (h + 1) * ML_DV] for b, h in us]
    c = [c_ref[b, h] for b, h in us]
    n = [n_ref[b, 0:1, h * ML_DK:(h + 1) * ML_DK] for b, h in us]
    m = [m_ref[b, 0:1, h:h + 1] for b, h in us]
    qk = [_ein_nt(q[i], k[i]) for i in ix]
    qc = [_ein(q[i], c[i]) for i in ix]
    i_col = [y[b][:, h:h + 1] for b, h in us]
    b_col = [y[b][:, ML_H + h:ML_H + h + 1] for b, h in us]
    d = [jnp.where(causal, b_col[i] - yt[b][ML_H + h:ML_H + h + 1, :] + yt[b][h:h + 1, :], -jnp.inf)
         for i, (b, h) in enumerate(us)]
    inter = [b_col[i] + m[i] for i in ix]
    m_t = [jnp.maximum(inter[i], jnp.max(d[i], axis=-1, keepdims=True)) for i in ix]
    w_inter = [jnp.exp(inter[i] - m_t[i]) for i in ix]
    s = [qk[i] * jnp.exp(d[i] - m_t[i]) for i in ix]
    m_new = [m_t[i][L - 1:L, :] for i in ix]
    b_last = [b_col[i][L - 1:L, :] for i in ix]
    kw = [k[i] * jnp.exp(b_last[i] - b_col[i] + i_col[i] - m_new[i]) for i in ix]
    sv = [_ein(s[i], v[i]) for i in ix]
    kv = [_ein_tn(kw[i], v[i]) for i in ix]
    for i, (b, h) in enumerate(us):
        f_state = jnp.exp(b_last[i] + m[i] - m_new[i])
        c_ref[b, h] = f_state * c[i] + kv[i]
        n_ref[b, 0:1, h * ML_DK:(h + 1) * ML_DK] = f_state * n[i] + jnp.sum(kw[i], axis=0, keepdims=True)
        m_ref[b, 0:1, h:h + 1] = m_new[i]
    qn = [jnp.sum(q[i] * n[i], -1, keepdims=True) for i in ix]
    s_sum = [jnp.sum(s[i], -1, keepdims=True) for i in ix]
    hh = [(w_inter[i] * qc[i] + sv[i]) / jnp.maximum(jnp.abs(w_inter[i] * qn[i] + s_sum[i]), jnp.exp(-m_t[i])) for i in ix]
    ms = [jnp.mean(hh[i] * hh[i], -1, keepdims=True) for i in ix]
    for i, (b, h) in enumerate(us):
        og = cols_ref[b, :, 1024 + h * ML_DV:1024 + (h + 1) * ML_DV]
        hn = hh[i] * lax.rsqrt(ms[i] + NORM_EPS) * ng_ref[:, h * ML_DV:(h + 1) * ML_DV]
        h_ref[b, :, h * ML_DV:(h + 1) * ML_DV] = (hn * jax.nn.sigmoid(og)).astype(h_ref.dtype)


def _chunk_grid_specs(bp, t, width):
    cols = pl.BlockSpec((bp, CHUNK, width), lambda c: (0, c, 0))
    out = pl.BlockSpec((bp, CHUNK, GROUP), lambda c: (0, c, 0))
    return (t // CHUNK,), cols, out


def _whole(shape):
    return pl.BlockSpec(shape, lambda c: (0,) * len(shape))


def _mlstm_prompt(cols, gb, ng, l, bp, t):
    grid, cspec, ospec = _chunk_grid_specs(bp, t, W_ML)
    return pl.pallas_call(
        _mlstm_prompt_body, grid=grid,
        in_specs=[cspec, _layer_spec(l, (1, LANES)), _layer_spec(l, (1, GROUP))],
        out_specs=[ospec, _whole((bp, ML_H, ML_DK, ML_DV)), _whole((bp, 1, ML_H * ML_DK)), _whole((bp, 1, LANES))],
        out_shape=[jax.ShapeDtypeStruct((bp, t, GROUP), bf16), jax.ShapeDtypeStruct((bp, ML_H, ML_DK, ML_DV), f32),
                   jax.ShapeDtypeStruct((bp, 1, ML_H * ML_DK), f32), jax.ShapeDtypeStruct((bp, 1, LANES), f32)],
        compiler_params=_cparams(("arbitrary",)), name="mlstm_prompt")(cols.reshape(bp, t, W_ML), gb, ng)


def _gla_prompt_body(cols_ref, gup_ref, gb_ref, ng_ref, h_ref, s_ref):
    @pl.when(pl.program_id(0) == 0)
    def _():
        s_ref[...] = jnp.zeros_like(s_ref)

    nb, L = cols_ref.shape[0], CHUNK
    causal, _ = _tri_masks(L)
    lr = cols_ref[:, :, 1536:1664].reshape(nb * L, LANES)
    la = _log_sigmoid(_ein(lr, gup_ref[...]) + gb_ref[...]) / GLA_TAU
    bc_all = _exact(_tril_blocks(nb, L), la)
    bc = [bc_all[b * L:(b + 1) * L, :] for b in range(nb)]
    mid = [bc[b][L // 2:L // 2 + 1, :] for b in range(nb)]
    last = [bc[b][L - 1:L, :] for b in range(nb)]
    e_q_mid = [jnp.exp(bc[b] - mid[b]) for b in range(nb)]
    e_k_mid = [jnp.exp(mid[b] - bc[b]) for b in range(nb)]
    e_q = [jnp.exp(bc[b]) for b in range(nb)]
    e_k_last = [jnp.exp(last[b] - bc[b]) for b in range(nb)]
    e_last = [jnp.exp(last[b]) for b in range(nb)]
    us = _units(nb, GLA_H)
    ix = range(len(us))
    sl = [slice(h * GLA_DK, (h + 1) * GLA_DK) for _, h in us]
    q = [cols_ref[b, :, h * GLA_DK:(h + 1) * GLA_DK] * GLA_DK ** -0.5 for b, h in us]
    k = [cols_ref[b, :, 256 + h * GLA_DK:256 + (h + 1) * GLA_DK] for b, h in us]
    v = [cols_ref[b, :, 512 + h * GLA_DV:512 + (h + 1) * GLA_DV] for b, h in us]
    s = [s_ref[b, h] for b, h in us]
    a = [jnp.where(causal, _ein_nt(q[i] * e_q_mid[b][:, sl[i]], k[i] * e_k_mid[b][:, sl[i]]), 0.0)
         for i, (b, _) in enumerate(us)]
    o0 = [_ein(q[i] * e_q[b][:, sl[i]], s[i]) for i, (b, _) in enumerate(us)]
    kv = [_ein_tn(k[i] * e_k_last[b][:, sl[i]], v[i]) for i, (b, _) in enumerate(us)]
    o = [o0[i] + _ein(a[i], v[i]) for i in ix]
    for i, (b, h) in enumerate(us):
        s_ref[b, h] = _col(e_last[b][:, sl[i]]) * s[i] + kv[i]
    ms = [jnp.mean(o[i] * o[i], -1, keepdims=True) for i in ix]
    for i, (b, h) in enumerate(us):
        og = cols_ref[b, :, 1024 + h * GLA_DV:1024 + (h + 1) * GLA_DV]
        on = o[i] * lax.rsqrt(ms[i] + NORM_EPS) * ng_ref[:, h * GLA_DV:(h + 1) * GLA_DV]
        h_ref[b, :, h * GLA_DV:(h + 1) * GLA_DV] = (on * _silu(og)).astype(h_ref.dtype)


def _gla_prompt(cols, gup, gb, ng, l, bp, t):
    grid, cspec, ospec = _chunk_grid_specs(bp, t, W_GLA)
    return pl.pallas_call(
        _gla_prompt_body, grid=grid,
        in_specs=[cspec, _layer_spec(l, (LANES, GLA_H * GLA_DK)), _layer_spec(l, (1, GLA_H * GLA_DK)),
                  _layer_spec(l, (1, GROUP))],
        out_specs=[ospec, _whole((bp, GLA_H, GLA_DK, GLA_DV))],
        out_shape=[jax.ShapeDtypeStruct((bp, t, GROUP), bf16), jax.ShapeDtypeStruct((bp, GLA_H, GLA_DK, GLA_DV), f32)],
        compiler_params=_cparams(("arbitrary",)), name="gla_prompt")(cols.reshape(bp, t, W_GLA), gup, gb, ng)


def _rwkv_pre(x, prev, mu, wup, aup, gup, vec):
    xs = x + (prev - x) * mu
    r, k, v, lb = xs[:, 0:512], xs[:, 512:1024], xs[:, 1024:1536], xs[:, 1536:1664]
    w0, a0, k_k, k_a = vec[0:1, :], vec[1:2, :], vec[2:3, :], vec[3:4, :]
    lw = -RW_DECAY_SCALE * jax.nn.sigmoid(w0 + _ein(jnp.tanh(lb), wup))
    a = jax.nn.sigmoid(a0 + _ein(lb, aup))
    g = _ein(jax.nn.sigmoid(lb), gup)
    kk_raw = k * k_k
    k2 = k * (1.0 + (a - 1.0) * k_a)
    return r, k2, v, lw, a, g, kk_raw


def _rwkv_post(o, r, k2, v, g, vec):
    hs = range(RW_H)
    sl = [slice(h * RW_N, (h + 1) * RW_N) for h in hs]
    mu_o = [jnp.mean(o[h], -1, keepdims=True) for h in hs]
    bonus_w = [jnp.sum(r[:, sl[h]] * k2[:, sl[h]] * vec[4:5, sl[h]], -1, keepdims=True) for h in hs]
    oc = [o[h] - mu_o[h] for h in hs]
    var_o = [jnp.mean(oc[h] * oc[h], -1, keepdims=True) for h in hs]
    outs = [(oc[h] * lax.rsqrt(var_o[h] + RW_GN_EPS) * vec[5:6, sl[h]] + vec[6:7, sl[h]]
             + bonus_w[h] * v[:, sl[h]]) * g[:, sl[h]] for h in hs]
    return jnp.concatenate(outs, axis=-1)


def _rwkv_prompt_body(cols_ref, mu_ref, wup_ref, aup_ref, gup_ref, vec_ref, h_ref, s_ref, prev_scr):
    @pl.when(pl.program_id(0) == 0)
    def _():
        s_ref[...] = jnp.zeros_like(s_ref)
        prev_scr[...] = jnp.zeros_like(prev_scr)

    nb, L, N = cols_ref.shape[0], CHUNK, RW_N
    first_row = _iota((L, W_RW), 0) == 0
    xb = [cols_ref[b] for b in range(nb)]
    prevs = [jnp.where(first_row, prev_scr[b, 0:1, :], pltpu.roll(xb[b], 1, 0)) for b in range(nb)]
    for b in range(nb):
        prev_scr[b, 0:1, :] = xb[b][L - 1:L, :]
    vec = vec_ref[...]
    r, k2, v, lw, a, g, kk_raw = _rwkv_pre(jnp.concatenate(xb, axis=0), jnp.concatenate(prevs, axis=0), mu_ref[...],
                                           wup_ref[...], aup_ref[...], gup_ref[...], vec)
    cum = _exact(_tril_blocks(nb, L), lw)
    cum_prev = cum - lw
    rep = lambda row_of: jnp.concatenate(
        [jnp.broadcast_to(cum[b * L + row_of:b * L + row_of + 1, :], (L, RW_H * N)) for b in range(nb)], axis=0)
    mid, last = rep(L // 2), rep(L - 1)
    e_prev_mid, e_mid_cum, e_cum_mid = jnp.exp(cum_prev - mid), jnp.exp(mid - cum), jnp.exp(cum - mid)
    e_prev, e_cum, e_last_cum, e_last = jnp.exp(cum_prev), jnp.exp(cum), jnp.exp(last - cum), jnp.exp(last)
    row, col = _iota((2 * L, 2 * L), 0), _iota((2 * L, 2 * L), 1)
    tq, sq = row % L, col % L
    quad = sq < tq + jnp.where(row < L, 0, 1)
    left = col[:, :] < L
    hs = range(RW_H)
    sl = [slice(h * N, (h + 1) * N) for h in hs]
    kk_h = [_l2norm(kk_raw[:, sl[h]]) for h in hs]
    b_h = [kk_h[h] * a[:, sl[h]] for h in hs]
    f_lhs = [(kk_h[h] * e_prev_mid[:, sl[h]], r[:, sl[h]] * e_cum_mid[:, sl[h]]) for h in hs]
    f_rhs = [(b_h[h] * e_mid_cum[:, sl[h]], k2[:, sl[h]] * e_mid_cum[:, sl[h]]) for h in hs]
    f_x0 = [(kk_h[h] * e_prev[:, sl[h]], r[:, sl[h]] * e_cum[:, sl[h]]) for h in hs]
    f_dec = [(b_h[h] * e_last_cum[:, sl[h]], k2[:, sl[h]] * e_last_cum[:, sl[h]]) for h in hs]
    us = _units(nb, RW_H)
    ix = range(len(us))
    rows = lambda z, b: z[b * L:(b + 1) * L, :]
    pair = lambda f, b, h: jnp.concatenate([rows(f[h][0], b), rows(f[h][1], b)], axis=0)
    vh = [rows(v[:, sl[h]], b) for b, h in us]
    s = [s_ref[b, h] for b, h in us]
    a_all = [jnp.where(quad, _ein_nt(pair(f_lhs, b, h), pair(f_rhs, b, h)), 0.0) for b, h in us]
    t_inv = _unit_lower_inverses([a_all[i][0:L, 0:L] for i in ix])
    x0 = [pair(f_x0, b, h) for b, h in us]
    m2 = [jnp.where(left, jnp.concatenate([x0[i], x0[i]], axis=1), a_all[i]) for i in ix]
    y = [_ein(m2[i], jnp.concatenate([s[i], vh[i]], axis=0)) for i in ix]
    u = [-_dot3(t_inv[i], y[i][0:L, :]) for i in ix]
    o = [y[i][L:2 * L, :] + _ein(a_all[i][L:2 * L, 0:L], u[i]) for i in ix]
    for i, (b, h) in enumerate(us):
        s_ref[b, h] = (_col(e_last[b * L:b * L + 1, sl[h]]) * s[i]
                       + _ein_tn(pair(f_dec, b, h), jnp.concatenate([u[i], vh[i]], axis=0)))
    o_heads = [jnp.concatenate([o[b * RW_H + h] for b in range(nb)], axis=0) for h in hs]
    h_ref[...] = _rwkv_post(o_heads, r, k2, v, g, vec).reshape(nb, L, GROUP).astype(h_ref.dtype)


def _rwkv_prompt(cols, mu, wup, aup, gup, vec, l, bp, t):
    grid, cspec, ospec = _chunk_grid_specs(bp, t, W_RW)
    return pl.pallas_call(
        _rwkv_prompt_body, grid=grid,
        in_specs=[cspec, _layer_spec(l, (1, W_RW)), _layer_spec(l, (LANES, GROUP)), _layer_spec(l, (LANES, GROUP)),
                  _layer_spec(l, (LANES, GROUP)), _layer_spec(l, (8, GROUP))],
        out_specs=[ospec, _whole((bp, RW_H, RW_N, RW_N))],
        out_shape=[jax.ShapeDtypeStruct((bp, t, GROUP), bf16), jax.ShapeDtypeStruct((bp, RW_H, RW_N, RW_N), f32)],
        scratch_shapes=[pltpu.VMEM((bp, 8, W_RW), f32)],
        compiler_params=_cparams(("arbitrary",)), name="rwkv_prompt")(cols.reshape(bp, t, W_RW), mu, wup, aup, gup, vec)


def _gdn_gates(gb, alog, dtb):
    return jax.nn.sigmoid(gb), -jnp.exp(alog) * _softplus(gb + dtb)


def _gdn_prompt_body(cols_ref, cw_ref, alog_ref, dtb_ref, ng_ref, h_ref, s_ref, xp_scr):
    @pl.when(pl.program_id(0) == 0)
    def _():
        s_ref[...] = jnp.zeros_like(s_ref)
        xp_scr[:, 0:8, :] = jnp.zeros((xp_scr.shape[0], 8, DN_QKV), f32)

    nb, L = cols_ref.shape[0], CHUNK
    causal, strict = _tri_masks(L)
    act = []
    for b in range(nb):
        raw = cols_ref[b, :, 0:DN_QKV]
        xp_scr[b, 8:8 + L, :] = raw
        conv = xp_scr[b, 5:5 + L, :] * cw_ref[0:1, :]
        for w in range(1, DN_CONV):
            conv = conv + xp_scr[b, 5 + w:5 + w + L, :] * cw_ref[w:w + 1, :]
        xp_scr[b, 0:8, :] = raw[L - 8:L, :]
        act.append(_silu(conv))
    beta_all, g_all = _gdn_gates(cols_ref[:, :, 2048:2176].reshape(nb * L, LANES), alog_ref[...], dtb_ref[...])
    gc_all = _exact(_tril_blocks(nb, L), g_all)
    gc = [gc_all[b * L:(b + 1) * L, :] for b in range(nb)]
    gt = [_rows_of(gc[b]) for b in range(nb)]
    us = _units(nb, DN_H)
    ix = range(len(us))
    q = [_l2norm(act[b][:, h * DN_D:(h + 1) * DN_D]) * DN_D ** -0.5 for b, h in us]
    k = [_l2norm(act[b][:, GROUP + h * DN_D:GROUP + (h + 1) * DN_D]) for b, h in us]
    v = [act[b][:, 2 * GROUP + h * DN_D:2 * GROUP + (h + 1) * DN_D] for b, h in us]
    beta = [beta_all[b * L:(b + 1) * L, h:h + 1] for b, h in us]
    g_col = [gc[b][:, DN_H + h:DN_H + h + 1] for b, h in us]
    decay = [jnp.exp(jnp.where(causal, g_col[i] - gt[b][DN_H + h:DN_H + h + 1, :], -jnp.inf))
             for i, (b, h) in enumerate(us)]
    kb = [k[i] * beta[i] for i in ix]
    eg = [jnp.exp(g_col[i]) for i in ix]
    s = [s_ref[b, h] for b, h in us]
    kq = [_ein_nt(jnp.concatenate([kb[i], q[i]], axis=0), k[i]) for i in ix]
    t_inv = _unit_lower_inverses([jnp.where(strict, kq[i][0:L, :] * decay[i], 0.0) for i in ix])
    qk = [kq[i][L:2 * L, :] * decay[i] for i in ix]
    o0 = [_ein(q[i] * eg[i], s[i]) for i in ix]
    sol = [_dot3(t_inv[i], jnp.concatenate([v[i] * beta[i], kb[i] * eg[i]], axis=-1)) for i in ix]
    v_new = [sol[i][:, :DN_D] - _ein(sol[i][:, DN_D:], s[i]) for i in ix]
    o = [o0[i] + _ein(qk[i], v_new[i]) for i in ix]
    for i, (b, h) in enumerate(us):
        g_last = g_col[i][L - 1:L, :]
        s_ref[b, h] = jnp.exp(g_last) * s[i] + _ein_tn(k[i] * jnp.exp(g_last - g_col[i]), v_new[i])
    ms = [jnp.mean(o[i] * o[i], -1, keepdims=True) for i in ix]
    for i, (b, h) in enumerate(us):
        z = cols_ref[b, :, DN_QKV + h * DN_D:DN_QKV + (h + 1) * DN_D]
        on = o[i] * lax.rsqrt(ms[i] + NORM_EPS) * ng_ref[...]
        h_ref[b, :, h * DN_D:(h + 1) * DN_D] = (on * _silu(z)).astype(h_ref.dtype)


def _gdn_prompt(cols, cw, alog, dtb, ng, l, bp, t):
    grid, cspec, ospec = _chunk_grid_specs(bp, t, W_DN)
    return pl.pallas_call(
        _gdn_prompt_body, grid=grid,
        in_specs=[cspec, _layer_spec(l, (8, DN_QKV)), _layer_spec(l, (1, LANES)), _layer_spec(l, (1, LANES)),
                  _layer_spec(l, (1, DN_D))],
        out_specs=[ospec, _whole((bp, DN_H, DN_D, DN_D))],
        out_shape=[jax.ShapeDtypeStruct((bp, t, GROUP), bf16), jax.ShapeDtypeStruct((bp, DN_H, DN_D, DN_D), f32)],
        scratch_shapes=[pltpu.VMEM((bp, 8 + CHUNK, DN_QKV), f32)],
        compiler_params=_cparams(("arbitrary",)), name="gdn_prompt")(cols.reshape(bp, t, W_DN), cw, alog, dtb, ng)


def _mlstm_step_body(cols_ref, gb_ref, ng_ref, c_ref, n_ref, m_ref, h_ref, co_ref, no_ref, mo_ref,
                     q_scr, kw_scr, v_scr, f_scr, qc_scr):
    nb = DEC_BLOCK
    g = cols_ref[:, 1536:1664] + gb_ref[...]
    lf = _log_sigmoid(g)
    hs = []
    for h in range(ML_H):
        q = cols_ref[:, h * ML_DK:(h + 1) * ML_DK]
        k = cols_ref[:, 256 + h * ML_DK:256 + (h + 1) * ML_DK] * ML_DK ** -0.5
        v = cols_ref[:, 512 + h * ML_DV:512 + (h + 1) * ML_DV]
        i_pre, f_log = g[:, h:h + 1], lf[:, ML_H + h:ML_H + h + 1]
        m = m_ref[:, h:h + 1]
        inter = f_log + m
        m_t = jnp.maximum(inter, i_pre)
        w_inter = jnp.exp(inter - m_t)
        kw = k * jnp.exp(i_pre - m_t)
        q_scr[h], kw_scr[h], v_scr[h] = q, kw, v
        f_scr[h] = jnp.broadcast_to(w_inter, (nb, LANES))
        hs.append((q, k, v, i_pre, m_t, w_inter, kw))

    def row(b, carry):
        hr = range(ML_H)
        q_col = [_col(q_scr[h, pl.ds(b, 1), :]) for h in hr]
        kw_col = [_col(kw_scr[h, pl.ds(b, 1), :]) for h in hr]
        c = [c_ref[b, h] for h in hr]
        for h in hr:
            qc_scr[h, pl.ds(b, 1), :] = jnp.sum(q_col[h] * c[h], axis=0, keepdims=True)
            co_ref[b, h] = f_scr[h, pl.ds(b, 1), :] * c[h] + kw_col[h] * v_scr[h, pl.ds(b, 1), :]
        return carry

    lax.fori_loop(0, nb, row, 0)
    for h, (q, k, v, i_pre, m_t, w_inter, kw) in enumerate(hs):
        og = cols_ref[:, 1024 + h * ML_DV:1024 + (h + 1) * ML_DV]
        n = n_ref[:, h * ML_DK:(h + 1) * ML_DK]
        s = jnp.sum(q * k, -1, keepdims=True) * jnp.exp(i_pre - m_t)
        num = w_inter * qc_scr[h] + s * v
        den = w_inter * jnp.sum(q * n, -1, keepdims=True) + s
        hh = num / jnp.maximum(jnp.abs(den), jnp.exp(-m_t))
        no_ref[:, h * ML_DK:(h + 1) * ML_DK] = w_inter * n + kw
        mo_ref[:, h:h + 1] = m_t
        hn = hh * lax.rsqrt(jnp.mean(hh * hh, -1, keepdims=True) + NORM_EPS) * ng_ref[:, h * ML_DV:(h + 1) * ML_DV]
        h_ref[:, h * ML_DV:(h + 1) * ML_DV] = (hn * jax.nn.sigmoid(og)).astype(h_ref.dtype)


def _row_spec(width):
    return pl.BlockSpec((DEC_BLOCK, width), lambda i: (i, 0))


def _layer_rows(l, width):
    return _layer_spec(l, (DEC_BLOCK, width), lambda i: (i, 0))


def _layer_state(l, h, dk, dv):
    return _layer_spec(l, (DEC_BLOCK, h, dk, dv), lambda i: (i, 0, 0, 0))


def _ignore_first_ref(body):
    def with_handed_on_buffer(_stacked_out_so_far, *refs):
        body(*refs)
    return with_handed_on_buffer


def _step_call(body, l, stacked_out, in_specs, out_specs, out_shapes, state_out_index, **kw):
    if stacked_out is None:
        return lambda *args: pl.pallas_call(body, in_specs=in_specs, out_specs=out_specs, out_shape=out_shapes, **kw)(*args)
    call = pl.pallas_call(_ignore_first_ref(body), in_specs=[pl.BlockSpec(memory_space=pl.ANY)] + in_specs,
                          out_specs=out_specs, out_shape=out_shapes, input_output_aliases={0: state_out_index}, **kw)
    return lambda *args: call(stacked_out, *args)


def _mlstm_step(cols, gb, ng, c0, n0, m0, l, c_out):
    bs = cols.shape[0]
    nb = DEC_BLOCK
    call = _step_call(
        _mlstm_step_body, l, c_out,
        in_specs=[_row_spec(W_ML), _layer_spec(l, (1, LANES)), _layer_spec(l, (1, GROUP)),
                  _layer_state(l, ML_H, ML_DK, ML_DV), _layer_rows(l, ML_H * ML_DK), _layer_rows(l, ML_H)],
        out_specs=[_row_spec(GROUP), _layer_state(l, ML_H, ML_DK, ML_DV), _row_spec(ML_H * ML_DK), _row_spec(ML_H)],
        out_shapes=[jax.ShapeDtypeStruct((bs, GROUP), bf16), jax.ShapeDtypeStruct(c0.shape, f32),
                    jax.ShapeDtypeStruct((bs, ML_H * ML_DK), f32), jax.ShapeDtypeStruct((bs, ML_H), f32)],
        state_out_index=1, grid=(bs // nb,),
        scratch_shapes=[pltpu.VMEM((ML_H, nb, ML_DK), f32), pltpu.VMEM((ML_H, nb, ML_DK), f32),
                        pltpu.VMEM((ML_H, nb, ML_DV), f32), pltpu.VMEM((ML_H, nb, LANES), f32),
                        pltpu.VMEM((ML_H, nb, ML_DV), f32)],
        compiler_params=_cparams(("parallel",)), name="mlstm_step")
    return call(cols, gb, ng, c0, n0, m0)


def _gla_step_body(cols_ref, gup_ref, gb_ref, ng_ref, s_ref, h_ref, so_ref, qe_scr, ea_scr, k_scr, v_scr, qs_scr):
    nb = DEC_BLOCK
    la = _log_sigmoid(_ein(cols_ref[:, 1536:1664], gup_ref[...]) + gb_ref[...]) / GLA_TAU
    ea = jnp.exp(la)
    hs = []
    for h in range(GLA_H):
        q = cols_ref[:, h * GLA_DK:(h + 1) * GLA_DK] * GLA_DK ** -0.5
        k = cols_ref[:, 256 + h * GLA_DK:256 + (h + 1) * GLA_DK]
        v = cols_ref[:, 512 + h * GLA_DV:512 + (h + 1) * GLA_DV]
        ea_h = ea[:, h * GLA_DK:(h + 1) * GLA_DK]
        qe_scr[h], ea_scr[h], k_scr[h], v_scr[h] = q * ea_h, ea_h, k, v
        hs.append((q, k, v))

    def row(b, carry):
        hr = range(GLA_H)
        qe_col = [_col(qe_scr[h, pl.ds(b, 1), :]) for h in hr]
        ea_col = [_col(ea_scr[h, pl.ds(b, 1), :]) for h in hr]
        k_col = [_col(k_scr[h, pl.ds(b, 1), :]) for h in hr]
        s = [s_ref[b, h] for h in hr]
        for h in hr:
            qs_scr[h, pl.ds(b, 1), :] = jnp.sum(qe_col[h] * s[h], axis=0, keepdims=True)
            so_ref[b, h] = ea_col[h] * s[h] + k_col[h] * v_scr[h, pl.ds(b, 1), :]
        return carry

    lax.fori_loop(0, nb, row, 0)
    for h, (q, k, v) in enumerate(hs):
        og = cols_ref[:, 1024 + h * GLA_DV:1024 + (h + 1) * GLA_DV]
        o = qs_scr[h] + jnp.sum(q * k, -1, keepdims=True) * v
        on = o * lax.rsqrt(jnp.mean(o * o, -1, keepdims=True) + NORM_EPS) * ng_ref[:, h * GLA_DV:(h + 1) * GLA_DV]
        h_ref[:, h * GLA_DV:(h + 1) * GLA_DV] = (on * _silu(og)).astype(h_ref.dtype)


def _gla_step(cols, gup, gb, ng, s0, l, s_out):
    bs = cols.shape[0]
    nb = DEC_BLOCK
    call = _step_call(
        _gla_step_body, l, s_out,
        in_specs=[_row_spec(W_GLA), _layer_spec(l, (LANES, GLA_H * GLA_DK)), _layer_spec(l, (1, GLA_H * GLA_DK)),
                  _layer_spec(l, (1, GROUP)), _layer_state(l, GLA_H, GLA_DK, GLA_DV)],
        out_specs=[_row_spec(GROUP), _layer_state(l, GLA_H, GLA_DK, GLA_DV)],
        out_shapes=[jax.ShapeDtypeStruct((bs, GROUP), bf16), jax.ShapeDtypeStruct(s0.shape, f32)],
        state_out_index=1, grid=(bs // nb,),
        scratch_shapes=[pltpu.VMEM((GLA_H, nb, GLA_DK), f32)] * 3 + [pltpu.VMEM((GLA_H, nb, GLA_DV), f32)] * 2,
        compiler_params=_cparams(("parallel",)), name="gla_step")
    return call(cols, gup, gb, ng, s0)


def _rwkv_step_body(cols_ref, shift_ref, mu_ref, wup_ref, aup_ref, gup_ref, vec_ref, s_ref, h_ref, so_ref,
                    kk_scr, w_scr, b_scr, k_scr, r_scr, v_scr, o_scr):
    nb = DEC_BLOCK
    vec = vec_ref[...]
    r, k2, v, lw, a, g, kk_raw = _rwkv_pre(cols_ref[...], shift_ref[...], mu_ref[...], wup_ref[...], aup_ref[...],
                                           gup_ref[...], vec)
    w = jnp.exp(lw)
    for h in range(RW_H):
        sl = slice(h * RW_N, (h + 1) * RW_N)
        kk = _l2norm(kk_raw[:, sl])
        kk_scr[h], w_scr[h], b_scr[h], k_scr[h], r_scr[h], v_scr[h] = kk, w[:, sl], kk * a[:, sl], k2[:, sl], r[:, sl], v[:, sl]

    def row(b, carry):
        hr = range(RW_H)
        kk_col = [_col(kk_scr[h, pl.ds(b, 1), :]) for h in hr]
        w_col = [_col(w_scr[h, pl.ds(b, 1), :]) for h in hr]
        b_col = [_col(b_scr[h, pl.ds(b, 1), :]) for h in hr]
        k_col = [_col(k_scr[h, pl.ds(b, 1), :]) for h in hr]
        r_col = [_col(r_scr[h, pl.ds(b, 1), :]) for h in hr]
        s = [s_ref[b, h] for h in hr]
        sk = [jnp.sum(kk_col[h] * s[h], axis=0, keepdims=True) for h in hr]
        s_new = [w_col[h] * s[h] - b_col[h] * sk[h] + k_col[h] * v_scr[h, pl.ds(b, 1), :] for h in hr]
        for h in hr:
            so_ref[b, h] = s_new[h]
            o_scr[h, pl.ds(b, 1), :] = jnp.sum(r_col[h] * s_new[h], axis=0, keepdims=True)
        return carry

    lax.fori_loop(0, nb, row, 0)
    h_ref[...] = _rwkv_post([o_scr[h] for h in range(RW_H)], r, k2, v, g, vec).astype(h_ref.dtype)


def _rwkv_step(cols, shift0, mu, wup, aup, gup, vec, s0, l, s_out):
    bs = cols.shape[0]
    nb = DEC_BLOCK
    call = _step_call(
        _rwkv_step_body, l, s_out,
        in_specs=[_row_spec(W_RW), _layer_rows(l, W_RW), _layer_spec(l, (1, W_RW)), _layer_spec(l, (LANES, GROUP)),
                  _layer_spec(l, (LANES, GROUP)), _layer_spec(l, (LANES, GROUP)), _layer_spec(l, (8, GROUP)),
                  _layer_state(l, RW_H, RW_N, RW_N)],
        out_specs=[_row_spec(GROUP), _layer_state(l, RW_H, RW_N, RW_N)],
        out_shapes=[jax.ShapeDtypeStruct((bs, GROUP), bf16), jax.ShapeDtypeStruct(s0.shape, f32)],
        state_out_index=1, grid=(bs // nb,),
        scratch_shapes=[pltpu.VMEM((RW_H, nb, RW_N), f32)] * 7,
        compiler_params=_cparams(("parallel",)), name="rwkv_step")
    return call(cols, shift0, mu, wup, aup, gup, vec, s0)


def _gdn_step_body(cols_ref, buf_ref, cw_ref, alog_ref, dtb_ref, ng_ref, s_ref, h_ref, so_ref,
                   q_scr, k_scr, v_scr, sc_scr, qs_scr, vn_scr):
    nb = DEC_BLOCK
    conv = cols_ref[:, 0:DN_QKV] * cw_ref[DN_CONV - 1:DN_CONV, :]
    for w in range(DN_CONV - 1):
        conv = conv + buf_ref[:, w * DN_QKV:(w + 1) * DN_QKV] * cw_ref[w:w + 1, :]
    act = _silu(conv)
    beta_all, g_all = _gdn_gates(cols_ref[:, 2048:2176], alog_ref[...], dtb_ref[...])
    eg_all = jnp.exp(g_all)
    hs = []
    for h in range(DN_H):
        q = _l2norm(act[:, h * DN_D:(h + 1) * DN_D]) * DN_D ** -0.5
        k = _l2norm(act[:, GROUP + h * DN_D:GROUP + (h + 1) * DN_D])
        v = act[:, 2 * GROUP + h * DN_D:2 * GROUP + (h + 1) * DN_D]
        beta, eg = beta_all[:, h:h + 1], eg_all[:, DN_H + h:DN_H + h + 1]
        q_scr[h], k_scr[h], v_scr[h] = q, k, v
        sc_scr[h] = jnp.where(_iota((nb, LANES), 1) == 0, beta, eg)
        hs.append((q, k, eg))

    def row(b, carry):
        hr = range(DN_H)
        k_col = [_col(k_scr[h, pl.ds(b, 1), :]) for h in hr]
        q_col = [_col(q_scr[h, pl.ds(b, 1), :]) for h in hr]
        s = [s_ref[b, h] for h in hr]
        sc = [sc_scr[h, pl.ds(b, 1), :] for h in hr]
        ks = [jnp.sum(k_col[h] * s[h], axis=0, keepdims=True) for h in hr]
        for h in hr:
            beta, eg = sc[h][:, 0:1], sc[h][:, 1:2]
            v_new = beta * (v_scr[h, pl.ds(b, 1), :] - eg * ks[h])
            qs_scr[h, pl.ds(b, 1), :] = jnp.sum(q_col[h] * s[h], axis=0, keepdims=True)
            vn_scr[h, pl.ds(b, 1), :] = v_new
            so_ref[b, h] = eg * s[h] + k_col[h] * v_new
        return carry

    lax.fori_loop(0, nb, row, 0)
    for h, (q, k, eg) in enumerate(hs):
        z = cols_ref[:, DN_QKV + h * DN_D:DN_QKV + (h + 1) * DN_D]
        o = eg * qs_scr[h] + jnp.sum(q * k, -1, keepdims=True) * vn_scr[h]
        on = o * lax.rsqrt(jnp.mean(o * o, -1, keepdims=True) + NORM_EPS) * ng_ref[...]
        h_ref[:, h * DN_D:(h + 1) * DN_D] = (on * _silu(z)).astype(h_ref.dtype)


def _gdn_step(cols, buf, cw, alog, dtb, ng, s0, l, s_out):
    bs = cols.shape[0]
    nb = DEC_BLOCK
    call = _step_call(
        _gdn_step_body, l, s_out,
        in_specs=[_row_spec(W_DN), _layer_rows(l, (DN_CONV - 1) * DN_QKV), _layer_spec(l, (8, DN_QKV)),
                  _layer_spec(l, (1, LANES)), _layer_spec(l, (1, LANES)), _layer_spec(l, (1, DN_D)),
                  _layer_state(l, DN_H, DN_D, DN_D)],
        out_specs=[_row_spec(GROUP), _layer_state(l, DN_H, DN_D, DN_D)],
        out_shapes=[jax.ShapeDtypeStruct((bs, GROUP), bf16), jax.ShapeDtypeStruct(s0.shape, f32)],
        state_out_index=1, grid=(bs // nb,),
        scratch_shapes=[pltpu.VMEM((DN_H, nb, DN_D), f32)] * 3 + [pltpu.VMEM((DN_H, nb, LANES), f32)]
        + [pltpu.VMEM((DN_H, nb, DN_D), f32)] * 2,
        compiler_params=_cparams(("parallel",)), name="gdn_step")
    return call(cols, buf, cw, alog, dtb, ng, s0)


def _pad_last(w, width):
    return jnp.pad(w, [(0, 0)] * (w.ndim - 1) + [(0, width - w.shape[-1])])


def _pad_rows(w, start, total):
    return jnp.pad(w, ((0, 0), (start, total - start - w.shape[1]), (0, 0)))


def _lane_rows(v, start):
    return jnp.pad(v, ((0, 0), (start, LANES - start - v.shape[1])))[:, None, :]


def kernel(x_prompt, x_sample, state_mlstm_c, state_mlstm_n, state_mlstm_m, state_gla, state_rwkv, state_rwkv_shift, state_dn, state_dn_conv, w_in, w_out, w_up, w_down, ln1_g, ln1_b, ln2_g, ln2_b, mlstm_gate_b, mlstm_norm_g, gla_gate_up, gla_gate_b, gla_norm_g, rwkv_mu, rwkv_w_up, rwkv_w0, rwkv_a_up, rwkv_a0, rwkv_g_up, rwkv_k_k, rwkv_k_a, rwkv_r_k, rwkv_norm_g, rwkv_norm_b, dn_conv_w, dn_a_log, dn_dt_bias, dn_norm_g):
    bp, t, _ = x_prompt.shape
    bs = x_sample.shape[0]
    xp = x_prompt.reshape(bp * t, D_MODEL)
    xs = x_sample.reshape(bs, D_MODEL)
    xpb, xsb = xp.astype(bf16), xs.astype(bf16)
    tm_p, tm_s, tf = 512, bs, 1024

    o1, o2, o3 = N_ML, N_ML + N_GLA, N_ML + N_GLA + N_RW
    w_secs = [_pad_last(w_in[:, :, 0:o1], W_ML).astype(bf16), _pad_last(w_in[:, :, o1:o2], W_GLA).astype(bf16),
              w_in[:, :, o2:o3].astype(bf16), _pad_last(w_in[:, :, o3:], W_DN).astype(bf16)]
    wo, wu, wd = w_out.astype(bf16), w_up.astype(bf16), w_down.astype(bf16)
    g1, b1, g2, b2 = ln1_g[:, None], ln1_b[:, None], ln2_g[:, None], ln2_b[:, None]
    ml_gb, ml_ng = _lane_rows(mlstm_gate_b, 0), mlstm_norm_g[:, None]
    gla_gup, gla_gb, gla_ng = _pad_rows(gla_gate_up, 0, LANES), gla_gate_b[:, None], gla_norm_g[:, None]
    rw_mu = rwkv_mu[:, None]
    rw_wup, rw_aup, rw_gup = _pad_rows(rwkv_w_up, 0, LANES), _pad_rows(rwkv_a_up, 32, LANES), _pad_rows(rwkv_g_up, 64, LANES)
    rw_vec = jnp.stack([rwkv_w0, rwkv_a0, rwkv_k_k, rwkv_k_a, rwkv_r_k, rwkv_norm_g, rwkv_norm_b,
                        jnp.zeros_like(rwkv_w0)], axis=1)
    dn_cw = _pad_rows(dn_conv_w, 0, 8)
    dn_alog, dn_dtb, dn_ng = _lane_rows(dn_a_log, DN_H), _lane_rows(dn_dt_bias, DN_H), dn_norm_g[:, None]
    st_ml_n = state_mlstm_n.reshape(DEPTH, bs, ML_H * ML_DK)
    st_dn_conv = state_dn_conv.reshape(DEPTH, bs, (DN_CONV - 1) * DN_QKV)

    outs_p, outs_s = [], []
    ml_c_s = gla_s = rw_s = dn_s = None
    for l in range(DEPTH):
        c_ml, c_gla, c_rw, c_dn = [_proj(xpb, w, l, tm_p) for w in w_secs]
        h_ml, mc, mn, mm = _mlstm_prompt(c_ml, ml_gb, ml_ng, l, bp, t)
        h_gla, gs = _gla_prompt(c_gla, gla_gup, gla_gb, gla_ng, l, bp, t)
        h_rw, rs = _rwkv_prompt(c_rw, rw_mu, rw_wup, rw_aup, rw_gup, rw_vec, l, bp, t)
        h_dn, ds = _gdn_prompt(c_dn, dn_cw, dn_alog, dn_dtb, dn_ng, l, bp, t)
        hs_p = [h.reshape(bp * t, GROUP) for h in (h_ml, h_gla, h_rw, h_dn)]
        x1, x1b = _outproj_ln(hs_p, wo, xp, g1, b1, l, tm_p)
        xp, xpb = _ffn_ln(x1, x1b, wu, wd, g2, b2, l, tm_p, tf)
        outs_p.append((mc, mn.reshape(bp, ML_H, ML_DK), mm[:, 0, :ML_H], gs, rs, c_rw.reshape(bp, t, W_RW)[:, -1],
                       ds, c_dn.reshape(bp, t, W_DN)[:, t - (DN_CONV - 1):, :DN_QKV]))

        c_ml, c_gla, c_rw, c_dn = [_proj(xsb, w, l, tm_s) for w in w_secs]
        h_ml, ml_c_s, mn, mm = _mlstm_step(c_ml, ml_gb, ml_ng, state_mlstm_c, st_ml_n, state_mlstm_m, l, ml_c_s)
        h_gla, gla_s = _gla_step(c_gla, gla_gup, gla_gb, gla_ng, state_gla, l, gla_s)
        h_rw, rw_s = _rwkv_step(c_rw, state_rwkv_shift, rw_mu, rw_wup, rw_aup, rw_gup, rw_vec, state_rwkv, l, rw_s)
        h_dn, dn_s = _gdn_step(c_dn, st_dn_conv, dn_cw, dn_alog, dn_dtb, dn_ng, state_dn, l, dn_s)
        x1, x1b = _outproj_ln((h_ml, h_gla, h_rw, h_dn), wo, xs, g1, b1, l, tm_s)
        xs, xsb = _ffn_ln(x1, x1b, wu, wd, g2, b2, l, tm_s, tf)
        outs_s.append((mn.reshape(bs, ML_H, ML_DK), mm, c_rw,
                       jnp.concatenate([state_dn_conv[l][:, 1:], c_dn[:, None, :DN_QKV]], axis=1)))

    (mlstm_c_p, mlstm_n_p, mlstm_m_p, gla_p, rwkv_p, rwkv_shift_p, dn_p, dn_conv_p) = [jnp.stack(z) for z in zip(*outs_p)]
    (mlstm_n_s, mlstm_m_s, rwkv_shift_s, dn_conv_s) = [jnp.stack(z) for z in zip(*outs_s)]
    return (xp.reshape(bp, t, D_MODEL), xs.reshape(bs, 1, D_MODEL), mlstm_c_p, ml_c_s, mlstm_n_p, mlstm_n_s,
            mlstm_m_p, mlstm_m_s, gla_p, gla_s, rwkv_p, rw_s, rwkv_shift_p, rwkv_shift_s, dn_p, dn_s,
            dn_conv_p, dn_conv_s)
```

```python
import functools

import jax
import jax.numpy as jnp
from jax import lax
from jax.experimental import pallas as pl
from jax.experimental.pallas import tpu as pltpu

f32 = jnp.float32
bf16 = jnp.bfloat16
_HI = lax.Precision.HIGHEST

D_MODEL = 2048
DEPTH = 4
GROUP = 512
D_FF = 4 * D_MODEL
CHUNK = 64
ML_H, ML_DK, ML_DV = 4, 64, 128
GLA_H, GLA_DK, GLA_DV, GLA_RANK, GLA_TAU = 4, 64, 128, 16, 16.0
RW_H, RW_N = 8, 64
RW_DECAY_SCALE = 0.606531
RW_GN_EPS = 64e-5
DN_H, DN_D, DN_CONV, DN_QKV = 4, 128, 4, 1536
N_ML, N_GLA, N_RW, N_DN = 1544, 1552, 1664, 2056
W_ML, W_GLA, W_RW, W_DN = 1664, 1664, 1664, 2176
ALPHA = (2 * DEPTH) ** 0.25
LN_EPS = 1e-5
NORM_EPS = 1e-6
LANES = 128
DEC_BLOCK = 8
VMEM_LIMIT = 56 * 1024 * 1024


def _ein(a, b):
    return jnp.dot(a.astype(bf16), b.astype(bf16), preferred_element_type=f32)


def _ein_nt(a, b):
    return lax.dot_general(a.astype(bf16), b.astype(bf16), (((1,), (1,)), ((), ())), preferred_element_type=f32)


def _ein_tn(a, b):
    return lax.dot_general(a.astype(bf16), b.astype(bf16), (((0,), (0,)), ((), ())), preferred_element_type=f32)


def _exact(a, b):
    return jnp.dot(a, b, precision=_HI, preferred_element_type=f32)


def _exact_nt(a, b):
    return lax.dot_general(a, b, (((1,), (1,)), ((), ())), precision=_HI, preferred_element_type=f32)


def _log_sigmoid(x):
    return jnp.minimum(x, 0.0) - jnp.log1p(jnp.exp(-jnp.abs(x)))


def _softplus(x):
    return jnp.maximum(x, 0.0) + jnp.log1p(jnp.exp(-jnp.abs(x)))


def _silu(x):
    return x * jax.nn.sigmoid(x)


def _l2norm(x):
    return x * lax.rsqrt(jnp.sum(x * x, -1, keepdims=True) + NORM_EPS)


def _iota(shape, axis):
    return lax.broadcasted_iota(jnp.int32, shape, axis)


def _col(row):
    n = row.shape[1]
    eye = _iota((n, n), 0) == _iota((n, n), 1)
    return jnp.sum(jnp.where(eye, row, 0.0), axis=1, keepdims=True)


def _row_of(x, b):
    return jnp.sum(jnp.where(_iota(x.shape, 0) == b, x, 0.0), axis=0, keepdims=True)


def _cols_of(row, n, width=None):
    width = n if width is None else width
    tiles = []
    for g in range(row.shape[1] // LANES):
        t = jnp.broadcast_to(row[:, g * LANES:(g + 1) * LANES], (LANES, LANES)).T
        tiles += [t[j * n:(j + 1) * n, 0:width] for j in range(LANES // n)]
    return tiles


def _tri_masks(n):
    r, c = _iota((n, n), 0), _iota((n, n), 1)
    return c <= r, c < r


def _split2(a):
    hi = a.astype(bf16)
    return hi, (a - hi.astype(f32)).astype(bf16)


def _dot3(a, b):
    ah, al = _split2(a)
    bh, bl = _split2(b)
    d = functools.partial(jnp.dot, preferred_element_type=f32)
    return d(ah, bh) + d(ah, bl) + d(al, bh)


def _unit_lower_inverses(a_list):
    n = a_list[0].shape[0]
    eye = jnp.where(_iota((n, n), 0) == _iota((n, n), 1), 1.0, 0.0)
    xs = [-a for a in a_list]
    ps = [eye + x for x in xs]
    for stage in range(max(n.bit_length() - 2, 0)):
        mm = _dot3 if stage == 0 else _ein
        xs = [mm(x, x) for x in xs]
        ps = [p + mm(p, x) for p, x in zip(ps, xs)]
    return ps


def _rows_of(y):
    e8 = jnp.where(_iota((8, LANES), 0) == _iota((8, LANES), 1), 1.0, 0.0)
    return _exact_nt(e8, y)


def _cparams(sem):
    return pltpu.CompilerParams(dimension_semantics=sem, vmem_limit_bytes=VMEM_LIMIT)


def _layer_spec(l, shape, index=None):
    nd = len(shape)
    if index is None:
        return pl.BlockSpec((None,) + tuple(shape), lambda *g: (l,) + (0,) * nd)
    return pl.BlockSpec((None,) + tuple(shape), lambda *g: (l,) + tuple(index(*g)))


def _proj_body(x_ref, w_ref, o_ref):
    o_ref[...] = jnp.dot(x_ref[...].astype(bf16), w_ref[...], preferred_element_type=f32)


def _proj(xb, w, l, tm):
    m, k = xb.shape
    n = w.shape[2]
    return pl.pallas_call(
        _proj_body, grid=(m // tm,),
        in_specs=[pl.BlockSpec((tm, k), lambda i: (i, 0)), _layer_spec(l, (k, n))],
        out_specs=pl.BlockSpec((tm, n), lambda i: (i, 0)),
        out_shape=jax.ShapeDtypeStruct((m, n), f32),
        compiler_params=_cparams(("parallel",)), name="proj")(xb, w)


def _layernorm(y, g, b):
    mu = jnp.mean(y, -1, keepdims=True)
    yc = y - mu
    var = jnp.mean(yc * yc, -1, keepdims=True)
    return yc * lax.rsqrt(var + LN_EPS) * g + b


def _outproj_ln_body(h0, h1, h2, h3, w_ref, x_ref, g_ref, b_ref, o_ref, ob_ref):
    acc = ALPHA * x_ref[...]
    for j, h in enumerate((h0, h1, h2, h3)):
        acc = acc + jnp.dot(h[...], w_ref[j * GROUP:(j + 1) * GROUP, :], preferred_element_type=f32)
    y = _layernorm(acc, g_ref[...], b_ref[...])
    o_ref[...] = y
    ob_ref[...] = y.astype(bf16)


def _outproj_ln(hs, w, x, g, b, l, tm):
    m = x.shape[0]
    hspec = pl.BlockSpec((tm, GROUP), lambda i: (i, 0))
    xspec = pl.BlockSpec((tm, D_MODEL), lambda i: (i, 0))
    vspec = _layer_spec(l, (1, D_MODEL))
    return pl.pallas_call(
        _outproj_ln_body, grid=(m // tm,),
        in_specs=[hspec] * 4 + [_layer_spec(l, (D_MODEL, D_MODEL)), xspec, vspec, vspec],
        out_specs=[xspec, xspec],
        out_shape=[jax.ShapeDtypeStruct((m, D_MODEL), f32), jax.ShapeDtypeStruct((m, D_MODEL), bf16)],
        compiler_params=_cparams(("parallel",)), name="outproj_ln")(*hs, w, x, g, b)


def _ffn_ln_body(x_ref, xb_ref, wu_ref, wd_ref, g_ref, b_ref, o_ref, ob_ref, acc_ref):
    f = pl.program_id(1)

    @pl.when(f == 0)
    def _():
        acc_ref[...] = ALPHA * x_ref[...]

    h = jnp.maximum(jnp.dot(xb_ref[...], wu_ref[...], preferred_element_type=f32), 0.0)
    acc_ref[...] += jnp.dot((h * h).astype(bf16), wd_ref[...], preferred_element_type=f32)

    @pl.when(f == pl.num_programs(1) - 1)
    def _():
        y = _layernorm(acc_ref[...], g_ref[...], b_ref[...])
        o_ref[...] = y
        ob_ref[...] = y.astype(bf16)


def _ffn_ln(x, xb, wu, wd, g, b, l, tm, tf):
    m = x.shape[0]
    xspec = pl.BlockSpec((tm, D_MODEL), lambda i, f: (i, 0))
    vspec = _layer_spec(l, (1, D_MODEL))
    return pl.pallas_call(
        _ffn_ln_body, grid=(m // tm, D_FF // tf),
        in_specs=[xspec, xspec, _layer_spec(l, (D_MODEL, tf), lambda i, f: (0, f)),
                  _layer_spec(l, (tf, D_MODEL), lambda i, f: (f, 0)), vspec, vspec],
        out_specs=[xspec, xspec],
        out_shape=[jax.ShapeDtypeStruct((m, D_MODEL), f32), jax.ShapeDtypeStruct((m, D_MODEL), bf16)],
        scratch_shapes=[pltpu.VMEM((tm, D_MODEL), f32)],
        compiler_params=_cparams(("parallel", "arbitrary")), name="ffn_ln")(x, xb, wu, wd, g, b)


def _tril_blocks(nb, L):
    r, c = _iota((nb * L, nb * L), 0), _iota((nb * L, nb * L), 1)
    return jnp.where((c <= r) & (c >= r - (r & (L - 1))), 1.0, 0.0)


def _units(nb, heads):
    return [(b, h) for b in range(nb) for h in range(heads)]


def _mlstm_prompt_body(cols_ref, gb_ref, ng_ref, h_ref, c_ref, n_ref, m_ref):
    @pl.when(pl.program_id(0) == 0)
    def _():
        c_ref[...] = jnp.zeros_like(c_ref)
        n_ref[...] = jnp.zeros_like(n_ref)
        m_ref[...] = jnp.zeros_like(m_ref)

    nb, L = cols_ref.shape[0], CHUNK
    causal, _ = _tri_masks(L)
    g = cols_ref[:, :, 1536:1664].reshape(nb * L, LANES) + gb_ref[...]
    bc = _exact(_tril_blocks(nb, L), _log_sigmoid(g))
    y_all = jnp.where(_iota((nb * L, LANES), 1) < ML_H, g, bc)
    y = [y_all[b * L:(b + 1) * L, :] for b in range(nb)]
    yt = [_rows_of(y[b]) for b in range(nb)]
    us = _units(nb, ML_H)
    ix = range(len(us))
    q = [cols_ref[b, :, h * ML_DK:(h + 1) * ML_DK] for b, h in us]
    k = [cols_ref[b, :, 256 + h * ML_DK:256 + (h + 1) * ML_DK] * ML_DK ** -0.5 for b, h in us]
    v = [cols_ref[b, :, 512 + h * ML_DV:512 + (h + 1) * ML_DV] for b, h in us]
    c = [c_ref[b, h] for b, h in us]
    n = [n_ref[b, 0:1, h * ML_DK:(h + 1) * ML_DK] for b, h in us]
    m = [m_ref[b, 0:1, h:h + 1] for b, h in us]
    qk = [_ein_nt(q[i], k[i]) for i in ix]
    qc = [_ein(q[i], c[i]) for i in ix]
    i_col = [y[b][:, h:h + 1] for b, h in us]
    b_col = [y[b][:, ML_H + h:ML_H + h + 1] for b, h in us]
    d = [jnp.where(causal, b_col[i] - yt[b][ML_H + h:ML_H + h + 1, :] + yt[b][h:h + 1, :], -jnp.inf)
         for i, (b, h) in enumerate(us)]
    inter = [b_col[i] + m[i] for i in ix]
    m_t = [jnp.maximum(inter[i], jnp.max(d[i], axis=-1, keepdims=True)) for i in ix]
    w_inter = [jnp.exp(inter[i] - m_t[i]) for i in ix]
    s = [qk[i] * jnp.exp(d[i] - m_t[i]) for i in ix]
    m_new = [m_t[i][L - 1:L, :] for i in ix]
    b_last = [b_col[i][L - 1:L, :] for i in ix]
    kw = [k[i] * jnp.exp(b_last[i] - b_col[i] + i_col[i] - m_new[i]) for i in ix]
    sv = [_ein(s[i], v[i]) for i in ix]
    kv = [_ein_tn(kw[i], v[i]) for i in ix]
    for i, (b, h) in enumerate(us):
        f_state = jnp.exp(b_last[i] + m[i] - m_new[i])
        c_ref[b, h] = f_state * c[i] + kv[i]
        n_ref[b, 0:1, h * ML_DK:(h + 1) * ML_DK] = f_state * n[i] + jnp.sum(kw[i], axis=0, keepdims=True)
        m_ref[b, 0:1, h:h + 1] = m_new[i]
    qn = [jnp.sum(q[i] * n[i], -1, keepdims=True) for i in ix]
    s_sum = [jnp.sum(s[i], -1, keepdims=True) for i in ix]
    hh = [(w_inter[i] * qc[i] + sv[i]) / jnp.maximum(jnp.abs(w_inter[i] * qn[i] + s_sum[i]), jnp.exp(-m_t[i])) for i in ix]
    ms = [jnp.mean(hh[i] * hh[i], -1, keepdims=True) for i in ix]
    for i, (b, h) in enumerate(us):
        og = cols_ref[b, :, 1024 + h * ML_DV:1024 + (h + 1) * ML_DV]
        hn = hh[i] * lax.rsqrt(ms[i] + NORM_EPS) * ng_ref[:, h * ML_DV:(h + 1) * ML_DV]
        h_ref[b, :, h * ML_DV:(h + 1) * ML_DV] = (hn * jax.nn.sigmoid(og)).astype(h_ref.dtype)


def _chunk_grid_specs(bp, t, width):
    cols = pl.BlockSpec((bp, CHUNK, width), lambda c: (0, c, 0))
    out = pl.BlockSpec((bp, CHUNK, GROUP), lambda c: (0, c, 0))
    return (t // CHUNK,), cols, out


def _whole(shape):
    return pl.BlockSpec(shape, lambda c: (0,) * len(shape))


def _mlstm_prompt(cols, gb, ng, l, bp, t):
    grid, cspec, ospec = _chunk_grid_specs(bp, t, W_ML)
    return pl.pallas_call(
        _mlstm_prompt_body, grid=grid,
        in_specs=[cspec, _layer_spec(l, (1, LANES)), _layer_spec(l, (1, GROUP))],
        out_specs=[ospec, _whole((bp, ML_H, ML_DK, ML_DV)), _whole((bp, 1, ML_H * ML_DK)), _whole((bp, 1, LANES))],
        out_shape=[jax.ShapeDtypeStruct((bp, t, GROUP), bf16), jax.ShapeDtypeStruct((bp, ML_H, ML_DK, ML_DV), f32),
                   jax.ShapeDtypeStruct((bp, 1, ML_H * ML_DK), f32), jax.ShapeDtypeStruct((bp, 1, LANES), f32)],
        compiler_params=_cparams(("arbitrary",)), name="mlstm_prompt")(cols.reshape(bp, t, W_ML), gb, ng)


def _gla_prompt_body(cols_ref, gup_ref, gb_ref, ng_ref, h_ref, s_ref):
    @pl.when(pl.program_id(0) == 0)
    def _():
        s_ref[...] = jnp.zeros_like(s_ref)

    nb, L = cols_ref.shape[0], CHUNK
    causal, _ = _tri_masks(L)
    lr = cols_ref[:, :, 1536:1664].reshape(nb * L, LANES)
    la = _log_sigmoid(_ein(lr, gup_ref[...]) + gb_ref[...]) / GLA_TAU
    bc_all = _exact(_tril_blocks(nb, L), la)
    bc = [bc_all[b * L:(b + 1) * L, :] for b in range(nb)]
    mid = [bc[b][L // 2:L // 2 + 1, :] for b in range(nb)]
    last = [bc[b][L - 1:L, :] for b in range(nb)]
    e_q_mid = [jnp.exp(bc[b] - mid[b]) for b in range(nb)]
    e_k_mid = [jnp.exp(mid[b] - bc[b]) for b in range(nb)]
    e_q = [jnp.exp(bc[b]) for b in range(nb)]
    e_k_last = [jnp.exp(last[b] - bc[b]) for b in range(nb)]
    e_last = [jnp.exp(last[b]) for b in range(nb)]
    us = _units(nb, GLA_H)
    ix = range(len(us))
    sl = [slice(h * GLA_DK, (h + 1) * GLA_DK) for _, h in us]
    q = [cols_ref[b, :, h * GLA_DK:(h + 1) * GLA_DK] * GLA_DK ** -0.5 for b, h in us]
    k = [cols_ref[b, :, 256 + h * GLA_DK:256 + (h + 1) * GLA_DK] for b, h in us]
    v = [cols_ref[b, :, 512 + h * GLA_DV:512 + (h + 1) * GLA_DV] for b, h in us]
    s = [s_ref[b, h] for b, h in us]
    a = [jnp.where(causal, _ein_nt(q[i] * e_q_mid[b][:, sl[i]], k[i] * e_k_mid[b][:, sl[i]]), 0.0)
         for i, (b, _) in enumerate(us)]
    o0 = [_ein(q[i] * e_q[b][:, sl[i]], s[i]) for i, (b, _) in enumerate(us)]
    kv = [_ein_tn(k[i] * e_k_last[b][:, sl[i]], v[i]) for i, (b, _) in enumerate(us)]
    o = [o0[i] + _ein(a[i], v[i]) for i in ix]
    for i, (b, h) in enumerate(us):
        s_ref[b, h] = _col(e_last[b][:, sl[i]]) * s[i] + kv[i]
    ms = [jnp.mean(o[i] * o[i], -1, keepdims=True) for i in ix]
    for i, (b, h) in enumerate(us):
        og = cols_ref[b, :, 1024 + h * GLA_DV:1024 + (h + 1) * GLA_DV]
        on = o[i] * lax.rsqrt(ms[i] + NORM_EPS) * ng_ref[:, h * GLA_DV:(h + 1) * GLA_DV]
        h_ref[b, :, h * GLA_DV:(h + 1) * GLA_DV] = (on * _silu(og)).astype(h_ref.dtype)


def _gla_prompt(cols, gup, gb, ng, l, bp, t):
    grid, cspec, ospec = _chunk_grid_specs(bp, t, W_GLA)
    return pl.pallas_call(
        _gla_prompt_body, grid=grid,
        in_specs=[cspec, _layer_spec(l, (LANES, GLA_H * GLA_DK)), _layer_spec(l, (1, GLA_H * GLA_DK)),
                  _layer_spec(l, (1, GROUP))],
        out_specs=[ospec, _whole((bp, GLA_H, GLA_DK, GLA_DV))],
        out_shape=[jax.ShapeDtypeStruct((bp, t, GROUP), bf16), jax.ShapeDtypeStruct((bp, GLA_H, GLA_DK, GLA_DV), f32)],
        compiler_params=_cparams(("arbitrary",)), name="gla_prompt")(cols.reshape(bp, t, W_GLA), gup, gb, ng)


def _rwkv_pre(x, prev, mu, wup, aup, gup, vec):
    xs = x + (prev - x) * mu
    r, k, v, lb = xs[:, 0:512], xs[:, 512:1024], xs[:, 1024:1536], xs[:, 1536:1664]
    w0, a0, k_k, k_a = vec[0:1, :], vec[1:2, :], vec[2:3, :], vec[3:4, :]
    lw = -RW_DECAY_SCALE * jax.nn.sigmoid(w0 + _ein(jnp.tanh(lb), wup))
    a = jax.nn.sigmoid(a0 + _ein(lb, aup))
    g = _ein(jax.nn.sigmoid(lb), gup)
    kk_raw = k * k_k
    k2 = k * (1.0 + (a - 1.0) * k_a)
    return r, k2, v, lw, a, g, kk_raw


def _rwkv_post(o, r, k2, v, g, vec):
    hs = range(RW_H)
    sl = [slice(h * RW_N, (h + 1) * RW_N) for h in hs]
    mu_o = [jnp.mean(o[h], -1, keepdims=True) for h in hs]
    bonus_w = [jnp.sum(r[:, sl[h]] * k2[:, sl[h]] * vec[4:5, sl[h]], -1, keepdims=True) for h in hs]
    oc = [o[h] - mu_o[h] for h in hs]
    var_o = [jnp.mean(oc[h] * oc[h], -1, keepdims=True) for h in hs]
    outs = [(oc[h] * lax.rsqrt(var_o[h] + RW_GN_EPS) * vec[5:6, sl[h]] + vec[6:7, sl[h]]
             + bonus_w[h] * v[:, sl[h]]) * g[:, sl[h]] for h in hs]
    return jnp.concatenate(outs, axis=-1)


def _rwkv_prompt_body(cols_ref, mu_ref, wup_ref, aup_ref, gup_ref, vec_ref, h_ref, s_ref, prev_scr):
    @pl.when(pl.program_id(0) == 0)
    def _():
        s_ref[...] = jnp.zeros_like(s_ref)
        prev_scr[...] = jnp.zeros_like(prev_scr)

    nb, L, N = cols_ref.shape[0], CHUNK, RW_N
    first_row = _iota((L, W_RW), 0) == 0
    xb = [cols_ref[b] for b in range(nb)]
    prevs = [jnp.where(first_row, prev_scr[b, 0:1, :], pltpu.roll(xb[b], 1, 0)) for b in range(nb)]
    for b in range(nb):
        prev_scr[b, 0:1, :] = xb[b][L - 1:L, :]
    vec = vec_ref[...]
    r, k2, v, lw, a, g, kk_raw = _rwkv_pre(jnp.concatenate(xb, axis=0), jnp.concatenate(prevs, axis=0), mu_ref[...],
                                           wup_ref[...], aup_ref[...], gup_ref[...], vec)
    cum = _exact(_tril_blocks(nb, L), lw)
    cum_prev = cum - lw
    rep = lambda row_of: jnp.concatenate(
        [jnp.broadcast_to(cum[b * L + row_of:b * L + row_of + 1, :], (L, RW_H * N)) for b in range(nb)], axis=0)
    mid, last = rep(L // 2), rep(L - 1)
    e_prev_mid, e_mid_cum, e_cum_mid = jnp.exp(cum_prev - mid), jnp.exp(mid - cum), jnp.exp(cum - mid)
    e_prev, e_cum, e_last_cum, e_last = jnp.exp(cum_prev), jnp.exp(cum), jnp.exp(last - cum), jnp.exp(last)
    row, col = _iota((2 * L, 2 * L), 0), _iota((2 * L, 2 * L), 1)
    tq, sq = row % L, col % L
    quad = sq < tq + jnp.where(row < L, 0, 1)
    left = col[:, :] < L
    hs = range(RW_H)
    sl = [slice(h * N, (h + 1) * N) for h in hs]
    kk_h = [_l2norm(kk_raw[:, sl[h]]) for h in hs]
    b_h = [kk_h[h] * a[:, sl[h]] for h in hs]
    f_lhs = [(kk_h[h] * e_prev_mid[:, sl[h]], r[:, sl[h]] * e_cum_mid[:, sl[h]]) for h in hs]
    f_rhs = [(b_h[h] * e_mid_cum[:, sl[h]], k2[:, sl[h]] * e_mid_cum[:, sl[h]]) for h in hs]
    f_x0 = [(kk_h[h] * e_prev[:, sl[h]], r[:, sl[h]] * e_cum[:, sl[h]]) for h in hs]
    f_dec = [(b_h[h] * e_last_cum[:, sl[h]], k2[:, sl[h]] * e_last_cum[:, sl[h]]) for h in hs]
    us = _units(nb, RW_H)
    ix = range(len(us))
    rows = lambda z, b: z[b * L:(b + 1) * L, :]
    pair = lambda f, b, h: jnp.concatenate([rows(f[h][0], b), rows(f[h][1], b)], axis=0)
    vh = [rows(v[:, sl[h]], b) for b, h in us]
    s = [s_ref[b, h] for b, h in us]
    a_all = [jnp.where(quad, _ein_nt(pair(f_lhs, b, h), pair(f_rhs, b, h)), 0.0) for b, h in us]
    t_inv = _unit_lower_inverses([a_all[i][0:L, 0:L] for i in ix])
    x0 = [pair(f_x0, b, h) for b, h in us]
    m2 = [jnp.where(left, jnp.concatenate([x0[i], x0[i]], axis=1), a_all[i]) for i in ix]
    y = [_ein(m2[i], jnp.concatenate([s[i], vh[i]], axis=0)) for i in ix]
    u = [-_dot3(t_inv[i], y[i][0:L, :]) for i in ix]
    o = [y[i][L:2 * L, :] + _ein(a_all[i][L:2 * L, 0:L], u[i]) for i in ix]
    for i, (b, h) in enumerate(us):
        s_ref[b, h] = (_col(e_last[b * L:b * L + 1, sl[h]]) * s[i]
                       + _ein_tn(pair(f_dec, b, h), jnp.concatenate([u[i], vh[i]], axis=0)))
    o_heads = [jnp.concatenate([o[b * RW_H + h] for b in range(nb)], axis=0) for h in hs]
    h_ref[...] = _rwkv_post(o_heads, r, k2, v, g, vec).reshape(nb, L, GROUP).astype(h_ref.dtype)


def _rwkv_prompt(cols, mu, wup, aup, gup, vec, l, bp, t):
    grid, cspec, ospec = _chunk_grid_specs(bp, t, W_RW)
    return pl.pallas_call(
        _rwkv_prompt_body, grid=grid,
        in_specs=[cspec, _layer_spec(l, (1, W_RW)), _layer_spec(l, (LANES, GROUP)), _layer_spec(l, (LANES, GROUP)),
                  _layer_spec(l, (LANES, GROUP)), _layer_spec(l, (8, GROUP))],
        out_specs=[ospec, _whole((bp, RW_H, RW_N, RW_N))],
        out_shape=[jax.ShapeDtypeStruct((bp, t, GROUP), bf16), jax.ShapeDtypeStruct((bp, RW_H, RW_N, RW_N), f32)],
        scratch_shapes=[pltpu.VMEM((bp, 8, W_RW), f32)],
        compiler_params=_cparams(("arbitrary",)), name="rwkv_prompt")(cols.reshape(bp, t, W_RW), mu, wup, aup, gup, vec)


def _gdn_gates(gb, alog, dtb):
    return jax.nn.sigmoid(gb), -jnp.exp(alog) * _softplus(gb + dtb)


def _gdn_prompt_body(cols_ref, cw_ref, alog_ref, dtb_ref, ng_ref, h_ref, s_ref, xp_scr):
    @pl.when(pl.program_id(0) == 0)
    def _():
        s_ref[...] = jnp.zeros_like(s_ref)
        xp_scr[...] = jnp.zeros_like(xp_scr)

    nb, L = cols_ref.shape[0], CHUNK
    causal, strict = _tri_masks(L)
    act = []
    for b in range(nb):
        raw = cols_ref[b, :, 0:DN_QKV]
        ext = jnp.concatenate([xp_scr[b], raw], axis=0)
        conv = raw * cw_ref[DN_CONV - 1:DN_CONV, :]
        for j in range(1, DN_CONV):
            conv = conv + pltpu.roll(ext, j, 0)[8:8 + L, :] * cw_ref[DN_CONV - 1 - j:DN_CONV - j, :]
        xp_scr[b] = raw[L - 8:L, :]
        act.append(_silu(conv))
    beta_all, g_all = _gdn_gates(cols_ref[:, :, 2048:2176].reshape(nb * L, LANES), alog_ref[...], dtb_ref[...])
    gc_all = _exact(_tril_blocks(nb, L), g_all)
    gc = [gc_all[b * L:(b + 1) * L, :] for b in range(nb)]
    gt = [_rows_of(gc[b]) for b in range(nb)]
    us = _units(nb, DN_H)
    ix = range(len(us))
    q = [_l2norm(act[b][:, h * DN_D:(h + 1) * DN_D]) * DN_D ** -0.5 for b, h in us]
    k = [_l2norm(act[b][:, GROUP + h * DN_D:GROUP + (h + 1) * DN_D]) for b, h in us]
    v = [act[b][:, 2 * GROUP + h * DN_D:2 * GROUP + (h + 1) * DN_D] for b, h in us]
    beta = [beta_all[b * L:(b + 1) * L, h:h + 1] for b, h in us]
    g_col = [gc[b][:, DN_H + h:DN_H + h + 1] for b, h in us]
    decay = [jnp.exp(jnp.where(causal, g_col[i] - gt[b][DN_H + h:DN_H + h + 1, :], -jnp.inf))
             for i, (b, h) in enumerate(us)]
    kb = [k[i] * beta[i] for i in ix]
    eg = [jnp.exp(g_col[i]) for i in ix]
    s = [s_ref[b, h] for b, h in us]
    kq = [_ein_nt(jnp.concatenate([kb[i], q[i]], axis=0), k[i]) for i in ix]
    t_inv = _unit_lower_inverses([jnp.where(strict, kq[i][0:L, :] * decay[i], 0.0) for i in ix])
    qk = [kq[i][L:2 * L, :] * decay[i] for i in ix]
    o0 = [_ein(q[i] * eg[i], s[i]) for i in ix]
    sol = [_dot3(t_inv[i], jnp.concatenate([v[i] * beta[i], kb[i] * eg[i]], axis=-1)) for i in ix]
    v_new = [sol[i][:, :DN_D] - _ein(sol[i][:, DN_D:], s[i]) for i in ix]
    o = [o0[i] + _ein(qk[i], v_new[i]) for i in ix]
    for i, (b, h) in enumerate(us):
        g_last = g_col[i][L - 1:L, :]
        s_ref[b, h] = jnp.exp(g_last) * s[i] + _ein_tn(k[i] * jnp.exp(g_last - g_col[i]), v_new[i])
    ms = [jnp.mean(o[i] * o[i], -1, keepdims=True) for i in ix]
    for i, (b, h) in enumerate(us):
        z = cols_ref[b, :, DN_QKV + h * DN_D:DN_QKV + (h + 1) * DN_D]
        on = o[i] * lax.rsqrt(ms[i] + NORM_EPS) * ng_ref[...]
        h_ref[b, :, h * DN_D:(h + 1) * DN_D] = (on * _silu(z)).astype(h_ref.dtype)


def _gdn_prompt(cols, cw, alog, dtb, ng, l, bp, t):
    grid, cspec, ospec = _chunk_grid_specs(bp, t, W_DN)
    return pl.pallas_call(
        _gdn_prompt_body, grid=grid,
        in_specs=[cspec, _layer_spec(l, (8, DN_QKV)), _layer_spec(l, (1, LANES)), _layer_spec(l, (1, LANES)),
                  _layer_spec(l, (1, DN_D))],
        out_specs=[ospec, _whole((bp, DN_H, DN_D, DN_D))],
        out_shape=[jax.ShapeDtypeStruct((bp, t, GROUP), bf16), jax.ShapeDtypeStruct((bp, DN_H, DN_D, DN_D), f32)],
        scratch_shapes=[pltpu.VMEM((bp, 8, DN_QKV), f32)],
        compiler_params=_cparams(("arbitrary",)), name="gdn_prompt")(cols.reshape(bp, t, W_DN), cw, alog, dtb, ng)


def _mlstm_step_body(cols_ref, gb_ref, ng_ref, c_ref, n_ref, m_ref, h_ref, co_ref, no_ref, mo_ref,
                     q_scr, kw_scr, v_scr, f_scr, qc_scr):
    nb = DEC_BLOCK
    g = cols_ref[:, 1536:1664] + gb_ref[...]
    lf = _log_sigmoid(g)
    hs = []
    for h in range(ML_H):
        q = cols_ref[:, h * ML_DK:(h + 1) * ML_DK]
        k = cols_ref[:, 256 + h * ML_DK:256 + (h + 1) * ML_DK] * ML_DK ** -0.5
        v = cols_ref[:, 512 + h * ML_DV:512 + (h + 1) * ML_DV]
        i_pre, f_log = g[:, h:h + 1], lf[:, ML_H + h:ML_H + h + 1]
        m = m_ref[:, h:h + 1]
        inter = f_log + m
        m_t = jnp.maximum(inter, i_pre)
        w_inter = jnp.exp(inter - m_t)
        kw = k * jnp.exp(i_pre - m_t)
        q_scr[h], kw_scr[h], v_scr[h] = q, kw, v
        f_scr[h] = jnp.broadcast_to(w_inter, (nb, LANES))
        hs.append((q, k, v, i_pre, m_t, w_inter, kw))

    def row(b, carry):
        hr = range(ML_H)
        kw_col = [_col(kw_scr[h, pl.ds(b, 1), :]) for h in hr]
        c = [c_ref[b, h] for h in hr]
        qc = [_row_of(_ein(q_scr[h], c[h]), b) for h in hr]
        for h in hr:
            qc_scr[h, pl.ds(b, 1), :] = qc[h]
            co_ref[b, h] = f_scr[h, pl.ds(b, 1), :] * c[h] + kw_col[h] * v_scr[h, pl.ds(b, 1), :]
        return carry

    lax.fori_loop(0, nb, row, 0)
    for h, (q, k, v, i_pre, m_t, w_inter, kw) in enumerate(hs):
        og = cols_ref[:, 1024 + h * ML_DV:1024 + (h + 1) * ML_DV]
        n = n_ref[:, h * ML_DK:(h + 1) * ML_DK]
        s = jnp.sum(q * k, -1, keepdims=True) * jnp.exp(i_pre - m_t)
        num = w_inter * qc_scr[h] + s * v
        den = w_inter * jnp.sum(q * n, -1, keepdims=True) + s
        hh = num / jnp.maximum(jnp.abs(den), jnp.exp(-m_t))
        no_ref[:, h * ML_DK:(h + 1) * ML_DK] = w_inter * n + kw
        mo_ref[:, h:h + 1] = m_t
        hn = hh * lax.rsqrt(jnp.mean(hh * hh, -1, keepdims=True) + NORM_EPS) * ng_ref[:, h * ML_DV:(h + 1) * ML_DV]
        h_ref[:, h * ML_DV:(h + 1) * ML_DV] = (hn * jax.nn.sigmoid(og)).astype(h_ref.dtype)


def _row_spec(width):
    return pl.BlockSpec((DEC_BLOCK, width), lambda i: (i, 0))


def _layer_rows(l, width):
    return _layer_spec(l, (DEC_BLOCK, width), lambda i: (i, 0))


def _layer_state(l, h, dk, dv):
    return _layer_spec(l, (DEC_BLOCK, h, dk, dv), lambda i: (i, 0, 0, 0))


def _ignore_first_ref(body):
    def with_handed_on_buffer(_stacked_out_so_far, *refs):
        body(*refs)
    return with_handed_on_buffer


def _step_call(body, l, stacked_out, in_specs, out_specs, out_shapes, state_out_index, **kw):
    if stacked_out is None:
        return lambda *args: pl.pallas_call(body, in_specs=in_specs, out_specs=out_specs, out_shape=out_shapes, **kw)(*args)
    call = pl.pallas_call(_ignore_first_ref(body), in_specs=[pl.BlockSpec(memory_space=pl.ANY)] + in_specs,
                          out_specs=out_specs, out_shape=out_shapes, input_output_aliases={0: state_out_index}, **kw)
    return lambda *args: call(stacked_out, *args)


def _mlstm_step(cols, gb, ng, c0, n0, m0, l, c_out):
    bs = cols.shape[0]
    nb = DEC_BLOCK
    call = _step_call(
        _mlstm_step_body, l, c_out,
        in_specs=[_row_spec(W_ML), _layer_spec(l, (1, LANES)), _layer_spec(l, (1, GROUP)),
                  _layer_state(l, ML_H, ML_DK, ML_DV), _layer_rows(l, ML_H * ML_DK), _layer_rows(l, ML_H)],
        out_specs=[_row_spec(GROUP), _layer_state(l, ML_H, ML_DK, ML_DV), _row_spec(ML_H * ML_DK), _row_spec(ML_H)],
        out_shapes=[jax.ShapeDtypeStruct((bs, GROUP), bf16), jax.ShapeDtypeStruct(c0.shape, f32),
                    jax.ShapeDtypeStruct((bs, ML_H * ML_DK), f32), jax.ShapeDtypeStruct((bs, ML_H), f32)],
        state_out_index=1, grid=(bs // nb,),
        scratch_shapes=[pltpu.VMEM((ML_H, nb, ML_DK), f32), pltpu.VMEM((ML_H, nb, ML_DK), f32),
                        pltpu.VMEM((ML_H, nb, ML_DV), f32), pltpu.VMEM((ML_H, nb, LANES), f32),
                        pltpu.VMEM((ML_H, nb, ML_DV), f32)],
        compiler_params=_cparams(("parallel",)), name="mlstm_step")
    return call(cols, gb, ng, c0, n0, m0)


def _gla_step_body(cols_ref, gup_ref, gb_ref, ng_ref, s_ref, h_ref, so_ref, qe_scr, ea_scr, k_scr, v_scr, qs_scr):
    nb = DEC_BLOCK
    la = _log_sigmoid(_ein(cols_ref[:, 1536:1664], gup_ref[...]) + gb_ref[...]) / GLA_TAU
    ea = jnp.exp(la)
    hs = []
    for h in range(GLA_H):
        q = cols_ref[:, h * GLA_DK:(h + 1) * GLA_DK] * GLA_DK ** -0.5
        k = cols_ref[:, 256 + h * GLA_DK:256 + (h + 1) * GLA_DK]
        v = cols_ref[:, 512 + h * GLA_DV:512 + (h + 1) * GLA_DV]
        ea_h = ea[:, h * GLA_DK:(h + 1) * GLA_DK]
        qe_scr[h], ea_scr[h], k_scr[h], v_scr[h] = q * ea_h, ea_h, k, v
        hs.append((q, k, v))

    def row(b, carry):
        hr = range(GLA_H)
        ea_col = [_col(ea_scr[h, pl.ds(b, 1), :]) for h in hr]
        k_col = [_col(k_scr[h, pl.ds(b, 1), :]) for h in hr]
        s = [s_ref[b, h] for h in hr]
        qs = [_row_of(_ein(qe_scr[h], s[h]), b) for h in hr]
        for h in hr:
            qs_scr[h, pl.ds(b, 1), :] = qs[h]
            so_ref[b, h] = ea_col[h] * s[h] + k_col[h] * v_scr[h, pl.ds(b, 1), :]
        return carry

    lax.fori_loop(0, nb, row, 0)
    for h, (q, k, v) in enumerate(hs):
        og = cols_ref[:, 1024 + h * GLA_DV:1024 + (h + 1) * GLA_DV]
        o = qs_scr[h] + jnp.sum(q * k, -1, keepdims=True) * v
        on = o * lax.rsqrt(jnp.mean(o * o, -1, keepdims=True) + NORM_EPS) * ng_ref[:, h * GLA_DV:(h + 1) * GLA_DV]
        h_ref[:, h * GLA_DV:(h + 1) * GLA_DV] = (on * _silu(og)).astype(h_ref.dtype)


def _gla_step(cols, gup, gb, ng, s0, l, s_out):
    bs = cols.shape[0]
    nb = DEC_BLOCK
    call = _step_call(
        _gla_step_body, l, s_out,
        in_specs=[_row_spec(W_GLA), _layer_spec(l, (LANES, GLA_H * GLA_DK)), _layer_spec(l, (1, GLA_H * GLA_DK)),
                  _layer_spec(l, (1, GROUP)), _layer_state(l, GLA_H, GLA_DK, GLA_DV)],
        out_specs=[_row_spec(GROUP), _layer_state(l, GLA_H, GLA_DK, GLA_DV)],
        out_shapes=[jax.ShapeDtypeStruct((bs, GROUP), bf16), jax.ShapeDtypeStruct(s0.shape, f32)],
        state_out_index=1, grid=(bs // nb,),
        scratch_shapes=[pltpu.VMEM((GLA_H, nb, GLA_DK), f32)] * 3 + [pltpu.VMEM((GLA_H, nb, GLA_DV), f32)] * 2,
        compiler_params=_cparams(("parallel",)), name="gla_step")
    return call(cols, gup, gb, ng, s0)


def _rwkv_step_body(cols_ref, shift_ref, mu_ref, wup_ref, aup_ref, gup_ref, vec_ref, s_ref, h_ref, so_ref,
                    w_scr, b_scr, k_scr, kkrw_scr, v_scr, sk_scr, o_scr):
    nb = DEC_BLOCK
    vec = vec_ref[...]
    r, k2, v, lw, a, g, kk_raw = _rwkv_pre(cols_ref[...], shift_ref[...], mu_ref[...], wup_ref[...], aup_ref[...],
                                           gup_ref[...], vec)
    hr = range(RW_H)
    sl = [slice(h * RW_N, (h + 1) * RW_N) for h in hr]
    kk = [_l2norm(kk_raw[:, sl[h]]) for h in hr]
    w = jnp.exp(lw)
    bv = jnp.concatenate(kk, axis=-1) * a
    w_scr[...], b_scr[...], k_scr[...] = w, bv, k2
    for h in hr:
        kkrw_scr[h, 0:nb, :], kkrw_scr[h, nb:2 * nb, :] = kk[h], r[:, sl[h]] * w[:, sl[h]]
        v_scr[h] = v[:, sl[h]]

    def row(b, carry):
        w_col, b_col, k_col = (_cols_of(scr[pl.ds(b, 1), :], RW_N) for scr in (w_scr, b_scr, k_scr))
        s = [s_ref[b, h] for h in hr]
        mv = [_ein(kkrw_scr[h], s[h]) for h in hr]
        sk = [_row_of(mv[h][0:nb, :], b) for h in hr]
        rws = [_row_of(mv[h][nb:2 * nb, :], b) for h in hr]
        for h in hr:
            so_ref[b, h] = w_col[h] * s[h] - b_col[h] * sk[h] + k_col[h] * v_scr[h, pl.ds(b, 1), :]
            sk_scr[h, pl.ds(b, 1), :] = sk[h]
            o_scr[h, pl.ds(b, 1), :] = rws[h]
        return carry

    lax.fori_loop(0, nb, row, 0)
    rb = [jnp.sum(r[:, sl[h]] * bv[:, sl[h]], -1, keepdims=True) for h in hr]
    rk = [jnp.sum(r[:, sl[h]] * k2[:, sl[h]], -1, keepdims=True) for h in hr]
    o = [o_scr[h] - rb[h] * sk_scr[h] + rk[h] * v[:, sl[h]] for h in hr]
    h_ref[...] = _rwkv_post(o, r, k2, v, g, vec).astype(h_ref.dtype)


def _rwkv_step(cols, shift0, mu, wup, aup, gup, vec, s0, l, s_out):
    bs = cols.shape[0]
    nb = DEC_BLOCK
    call = _step_call(
        _rwkv_step_body, l, s_out,
        in_specs=[_row_spec(W_RW), _layer_rows(l, W_RW), _layer_spec(l, (1, W_RW)), _layer_spec(l, (LANES, GROUP)),
                  _layer_spec(l, (LANES, GROUP)), _layer_spec(l, (LANES, GROUP)), _layer_spec(l, (8, GROUP)),
                  _layer_state(l, RW_H, RW_N, RW_N)],
        out_specs=[_row_spec(GROUP), _layer_state(l, RW_H, RW_N, RW_N)],
        out_shapes=[jax.ShapeDtypeStruct((bs, GROUP), bf16), jax.ShapeDtypeStruct(s0.shape, f32)],
        state_out_index=1, grid=(bs // nb,),
        scratch_shapes=[pltpu.VMEM((nb, GROUP), f32)] * 3 + [pltpu.VMEM((RW_H, 2 * nb, RW_N), f32)]
        + [pltpu.VMEM((RW_H, nb, RW_N), f32)] * 3,
        compiler_params=_cparams(("parallel",)), name="rwkv_step")
    return call(cols, shift0, mu, wup, aup, gup, vec, s0)


def _gdn_step_body(cols_ref, buf_ref, cw_ref, alog_ref, dtb_ref, ng_ref, s_ref, h_ref, so_ref,
                   kq_scr, v_scr, sc_scr, qs_scr, vn_scr):
    nb = DEC_BLOCK
    conv = cols_ref[:, 0:DN_QKV] * cw_ref[DN_CONV - 1:DN_CONV, :]
    for w in range(DN_CONV - 1):
        conv = conv + buf_ref[:, w * DN_QKV:(w + 1) * DN_QKV] * cw_ref[w:w + 1, :]
    act = _silu(conv)
    beta_all, g_all = _gdn_gates(cols_ref[:, 2048:2176], alog_ref[...], dtb_ref[...])
    eg_all = jnp.exp(g_all)
    hs = []
    for h in range(DN_H):
        q = _l2norm(act[:, h * DN_D:(h + 1) * DN_D]) * DN_D ** -0.5
        k = _l2norm(act[:, GROUP + h * DN_D:GROUP + (h + 1) * DN_D])
        v = act[:, 2 * GROUP + h * DN_D:2 * GROUP + (h + 1) * DN_D]
        beta, eg = beta_all[:, h:h + 1], eg_all[:, DN_H + h:DN_H + h + 1]
        kq_scr[h, 0:nb, :], kq_scr[h, nb:2 * nb, :], v_scr[h] = k, q, v
        sc_scr[h] = jnp.where(_iota((nb, LANES), 1) == 0, beta, eg)
        hs.append((q, k, eg))

    def row(b, carry):
        hr = range(DN_H)
        k_col = [_col(kq_scr[h, pl.ds(b, 1), :]) for h in hr]
        s = [s_ref[b, h] for h in hr]
        sc = [sc_scr[h, pl.ds(b, 1), :] for h in hr]
        mv = [_ein(kq_scr[h], s[h]) for h in hr]
        ks = [_row_of(mv[h][0:nb, :], b) for h in hr]
        qs = [_row_of(mv[h][nb:2 * nb, :], b) for h in hr]
        for h in hr:
            beta, eg = sc[h][:, 0:1], sc[h][:, 1:2]
            v_new = beta * (v_scr[h, pl.ds(b, 1), :] - eg * ks[h])
            qs_scr[h, pl.ds(b, 1), :] = qs[h]
            vn_scr[h, pl.ds(b, 1), :] = v_new
            so_ref[b, h] = eg * s[h] + k_col[h] * v_new
        return carry

    lax.fori_loop(0, nb, row, 0)
    for h, (q, k, eg) in enumerate(hs):
        z = cols_ref[:, DN_QKV + h * DN_D:DN_QKV + (h + 1) * DN_D]
        o = eg * qs_scr[h] + jnp.sum(q * k, -1, keepdims=True) * vn_scr[h]
        on = o * lax.rsqrt(jnp.mean(o * o, -1, keepdims=True) + NORM_EPS) * ng_ref[...]
        h_ref[:, h * DN_D:(h + 1) * DN_D] = (on * _silu(z)).astype(h_ref.dtype)


def _gdn_step(cols, buf, cw, alog, dtb, ng, s0, l, s_out):
    bs = cols.shape[0]
    nb = DEC_BLOCK
    call = _step_call(
        _gdn_step_body, l, s_out,
        in_specs=[_row_spec(W_DN), _layer_rows(l, (DN_CONV - 1) * DN_QKV), _layer_spec(l, (8, DN_QKV)),
                  _layer_spec(l, (1, LANES)), _layer_spec(l, (1, LANES)), _layer_spec(l, (1, DN_D)),
                  _layer_state(l, DN_H, DN_D, DN_D)],
        out_specs=[_row_spec(GROUP), _layer_state(l, DN_H, DN_D, DN_D)],
        out_shapes=[jax.ShapeDtypeStruct((bs, GROUP), bf16), jax.ShapeDtypeStruct(s0.shape, f32)],
        state_out_index=1, grid=(bs // nb,),
        scratch_shapes=[pltpu.VMEM((DN_H, 2 * nb, DN_D), f32), pltpu.VMEM((DN_H, nb, DN_D), f32),
                        pltpu.VMEM((DN_H, nb, LANES), f32)] + [pltpu.VMEM((DN_H, nb, DN_D), f32)] * 2,
        compiler_params=_cparams(("parallel",)), name="gdn_step")
    return call(cols, buf, cw, alog, dtb, ng, s0)


def _pad_last(w, width):
    return jnp.pad(w, [(0, 0)] * (w.ndim - 1) + [(0, width - w.shape[-1])])


def _pad_rows(w, start, total):
    return jnp.pad(w, ((0, 0), (start, total - start - w.shape[1]), (0, 0)))


def _lane_rows(v, start):
    return jnp.pad(v, ((0, 0), (start, LANES - start - v.shape[1])))[:, None, :]


def kernel(x_prompt, x_sample, state_mlstm_c, state_mlstm_n, state_mlstm_m, state_gla, state_rwkv, state_rwkv_shift, state_dn, state_dn_conv, w_in, w_out, w_up, w_down, ln1_g, ln1_b, ln2_g, ln2_b, mlstm_gate_b, mlstm_norm_g, gla_gate_up, gla_gate_b, gla_norm_g, rwkv_mu, rwkv_w_up, rwkv_w0, rwkv_a_up, rwkv_a0, rwkv_g_up, rwkv_k_k, rwkv_k_a, rwkv_r_k, rwkv_norm_g, rwkv_norm_b, dn_conv_w, dn_a_log, dn_dt_bias, dn_norm_g):
    bp, t, _ = x_prompt.shape
    bs = x_sample.shape[0]
    xp = x_prompt.reshape(bp * t, D_MODEL)
    xs = x_sample.reshape(bs, D_MODEL)
    xpb, xsb = xp, xs
    tm_p, tm_s, tf = 512, bs, 1024

    o1, o2, o3 = N_ML, N_ML + N_GLA, N_ML + N_GLA + N_RW
    w_secs = [_pad_last(w_in[:, :, 0:o1].astype(bf16), W_ML), _pad_last(w_in[:, :, o1:o2].astype(bf16), W_GLA),
              w_in[:, :, o2:o3].astype(bf16), _pad_last(w_in[:, :, o3:].astype(bf16), W_DN)]
    wo, wu, wd = w_out.astype(bf16), w_up.astype(bf16), w_down.astype(bf16)
    g1, b1, g2, b2 = ln1_g[:, None], ln1_b[:, None], ln2_g[:, None], ln2_b[:, None]
    ml_gb, ml_ng = _lane_rows(mlstm_gate_b, 0), mlstm_norm_g[:, None]
    gla_gup, gla_gb, gla_ng = _pad_rows(gla_gate_up, 0, LANES), gla_gate_b[:, None], gla_norm_g[:, None]
    rw_mu = rwkv_mu[:, None]
    rw_wup, rw_aup, rw_gup = _pad_rows(rwkv_w_up, 0, LANES), _pad_rows(rwkv_a_up, 32, LANES), _pad_rows(rwkv_g_up, 64, LANES)
    rw_vec = jnp.stack([rwkv_w0, rwkv_a0, rwkv_k_k, rwkv_k_a, rwkv_r_k, rwkv_norm_g, rwkv_norm_b,
                        jnp.zeros_like(rwkv_w0)], axis=1)
    dn_cw = _pad_rows(dn_conv_w, 0, 8)
    dn_alog, dn_dtb, dn_ng = _lane_rows(dn_a_log, DN_H), _lane_rows(dn_dt_bias, DN_H), dn_norm_g[:, None]
    st_ml_n = state_mlstm_n.reshape(DEPTH, bs, ML_H * ML_DK)
    st_dn_conv = state_dn_conv.reshape(DEPTH, bs, (DN_CONV - 1) * DN_QKV)

    outs_p, outs_s = [], []
    ml_c_s = gla_s = rw_s = dn_s = None
    for l in range(DEPTH):
        c_ml, c_gla, c_rw, c_dn = [_proj(xpb, w, l, tm_p) for w in w_secs]
        h_ml, mc, mn, mm = _mlstm_prompt(c_ml, ml_gb, ml_ng, l, bp, t)
        h_gla, gs = _gla_prompt(c_gla, gla_gup, gla_gb, gla_ng, l, bp, t)
        h_rw, rs = _rwkv_prompt(c_rw, rw_mu, rw_wup, rw_aup, rw_gup, rw_vec, l, bp, t)
        h_dn, ds = _gdn_prompt(c_dn, dn_cw, dn_alog, dn_dtb, dn_ng, l, bp, t)
        hs_p = [h.reshape(bp * t, GROUP) for h in (h_ml, h_gla, h_rw, h_dn)]
        x1, x1b = _outproj_ln(hs_p, wo, xp, g1, b1, l, tm_p)
        xp, xpb = _ffn_ln(x1, x1b, wu, wd, g2, b2, l, tm_p, tf)
        outs_p.append((mc, mn.reshape(bp, ML_H, ML_DK), mm[:, 0, :ML_H], gs, rs, c_rw.reshape(bp, t, W_RW)[:, -1],
                       ds, c_dn.reshape(bp, t, W_DN)[:, t - (DN_CONV - 1):, :DN_QKV]))

        c_ml, c_gla, c_rw, c_dn = [_proj(xsb, w, l, tm_s) for w in w_secs]
        h_ml, ml_c_s, mn, mm = _mlstm_step(c_ml, ml_gb, ml_ng, state_mlstm_c, st_ml_n, state_mlstm_m, l, ml_c_s)
        h_gla, gla_s = _gla_step(c_gla, gla_gup, gla_gb, gla_ng, state_gla, l, gla_s)
        h_rw, rw_s = _rwkv_step(c_rw, state_rwkv_shift, rw_mu, rw_wup, rw_aup, rw_gup, rw_vec, state_rwkv, l, rw_s)
        h_dn, dn_s = _gdn_step(c_dn, st_dn_conv, dn_cw, dn_alog, dn_dtb, dn_ng, state_dn, l, dn_s)
        x1, x1b = _outproj_ln((h_ml, h_gla, h_rw, h_dn), wo, xs, g1, b1, l, tm_s)
        xs, xsb = _ffn_ln(x1, x1b, wu, wd, g2, b2, l, tm_s, tf)
        outs_s.append((mn.reshape(bs, ML_H, ML_DK), mm, c_rw,
                       jnp.concatenate([state_dn_conv[l][:, 1:], c_dn[:, None, :DN_QKV]], axis=1)))

    (mlstm_c_p, mlstm_n_p, mlstm_m_p, gla_p, rwkv_p, rwkv_shift_p, dn_p, dn_conv_p) = [jnp.stack(z) for z in zip(*outs_p)]
    (mlstm_n_s, mlstm_m_s, rwkv_shift_s, dn_conv_s) = [jnp.stack(z) for z in zip(*outs_s)]
    return (xp.reshape(bp, t, D_MODEL), xs.reshape(bs, 1, D_MODEL), mlstm_c_p, ml_c_s, mlstm_n_p, mlstm_n_s,
            mlstm_m_p, mlstm_m_s, gla_p, gla_s, rwkv_p, rw_s, rwkv_shift_p, rwkv_shift_s, dn_p, dn_s,
            dn_conv_p, dn_conv_s)
```

```python
import functools

import jax
import jax.numpy as jnp
from jax import lax
from jax.experimental import pallas as pl
from jax.experimental.pallas import tpu as pltpu

f32 = jnp.float32
bf16 = jnp.bfloat16

D_MODEL = 2048
DEPTH = 4
GROUP = 512
D_FF = 4 * D_MODEL
CHUNK = 64
ML_H, ML_DK, ML_DV = 4, 64, 128
GLA_H, GLA_DK, GLA_DV, GLA_RANK, GLA_TAU = 4, 64, 128, 16, 16.0
RW_H, RW_N = 8, 64
RW_DECAY_SCALE = 0.606531
RW_GN_EPS = 64e-5
DN_H, DN_D, DN_CONV, DN_QKV = 4, 128, 4, 1536
N_ML, N_GLA, N_RW, N_DN = 1544, 1552, 1664, 2056
W_ML, W_GLA, W_RW, W_DN = 1664, 1664, 1664, 2176
ALPHA = (2 * DEPTH) ** 0.25
LN_EPS = 1e-5
NORM_EPS = 1e-6
LANES = 128
DEC_BLOCK = 32
VMEM_LIMIT = 56 * 1024 * 1024


def _ein(a, b):
    return jnp.dot(a.astype(bf16), b.astype(bf16), preferred_element_type=f32)


def _ein_nt(a, b):
    return lax.dot_general(a.astype(bf16), b.astype(bf16), (((1,), (1,)), ((), ())), preferred_element_type=f32)


def _ein_tn(a, b):
    return lax.dot_general(a.astype(bf16), b.astype(bf16), (((0,), (0,)), ((), ())), preferred_element_type=f32)


def _split3(x):
    hi = x.astype(bf16)
    r1 = x - hi.astype(f32)
    mid = r1.astype(bf16)
    return hi, mid, (r1 - mid.astype(f32)).astype(bf16)


def _select_dot(sel, x):
    s = sel.astype(bf16)
    hi, mid, lo = _split3(x)
    d = functools.partial(jnp.dot, preferred_element_type=f32)
    return d(s, hi) + d(s, mid) + d(s, lo)


def _select_dot_right(x, sel):
    s = sel.astype(bf16)
    hi, mid, lo = _split3(x)
    d = functools.partial(jnp.dot, preferred_element_type=f32)
    return d(hi, s) + d(mid, s) + d(lo, s)


def _select_dot_nt(sel, x):
    s = sel.astype(bf16)
    hi, mid, lo = _split3(x)
    d = functools.partial(lax.dot_general, dimension_numbers=(((1,), (1,)), ((), ())), preferred_element_type=f32)
    return d(s, hi) + d(s, mid) + d(s, lo)


def _log_sigmoid(x):
    return jnp.minimum(x, 0.0) - jnp.log1p(jnp.exp(-jnp.abs(x)))


def _softplus(x):
    return jnp.maximum(x, 0.0) + jnp.log1p(jnp.exp(-jnp.abs(x)))


def _silu(x):
    return x * jax.nn.sigmoid(x)


def _l2norm(x):
    return x * lax.rsqrt(jnp.sum(x * x, -1, keepdims=True) + NORM_EPS)


def _iota(shape, axis):
    return lax.broadcasted_iota(jnp.int32, shape, axis)


def _col(row):
    n = row.shape[1]
    eye = _iota((n, n), 0) == _iota((n, n), 1)
    return jnp.sum(jnp.where(eye, row, 0.0), axis=1, keepdims=True)


def _row_of(x, b):
    return jnp.sum(jnp.where(_iota(x.shape, 0) == b, x, 0.0), axis=0, keepdims=True)


def _cols_of(row, n, width=None):
    width = n if width is None else width
    tiles = []
    for g in range(row.shape[1] // LANES):
        t = jnp.broadcast_to(row[:, g * LANES:(g + 1) * LANES], (LANES, LANES)).T
        tiles += [t[j * n:(j + 1) * n, 0:width] for j in range(LANES // n)]
    return tiles


def _tri_masks(n):
    r, c = _iota((n, n), 0), _iota((n, n), 1)
    return c <= r, c < r


def _split2(a):
    hi = a.astype(bf16)
    return hi, (a - hi.astype(f32)).astype(bf16)


def _dot3(a, b):
    ah, al = _split2(a)
    bh, bl = _split2(b)
    d = functools.partial(jnp.dot, preferred_element_type=f32)
    return d(ah, bh) + d(ah, bl) + d(al, bh)


def _unit_lower_inverses(a_list):
    n = a_list[0].shape[0]
    eye = jnp.where(_iota((n, n), 0) == _iota((n, n), 1), 1.0, 0.0)
    xs = [-a for a in a_list]
    ps = [eye + x for x in xs]
    for stage in range(max(n.bit_length() - 2, 0)):
        mm = _dot3 if stage == 0 else _ein
        xs = [mm(x, x) for x in xs]
        ps = [p + mm(p, x) for p, x in zip(ps, xs)]
    return ps


def _rows_of(y):
    e8 = jnp.where(_iota((8, LANES), 0) == _iota((8, LANES), 1), 1.0, 0.0)
    return _select_dot_nt(e8, y)


def _cparams(sem):
    return pltpu.CompilerParams(dimension_semantics=sem, vmem_limit_bytes=VMEM_LIMIT)


def _layer_spec(l, shape, index=None):
    nd = len(shape)
    if index is None:
        return pl.BlockSpec((None,) + tuple(shape), lambda *g: (l,) + (0,) * nd)
    return pl.BlockSpec((None,) + tuple(shape), lambda *g: (l,) + tuple(index(*g)))


def _proj_body(x_ref, w_ref, o_ref):
    o_ref[...] = jnp.dot(x_ref[...].astype(bf16), w_ref[...], preferred_element_type=f32)


def _proj(xb, w, l, tm):
    m, k = xb.shape
    n = w.shape[2]
    return pl.pallas_call(
        _proj_body, grid=(m // tm,),
        in_specs=[pl.BlockSpec((tm, k), lambda i: (i, 0)), _layer_spec(l, (k, n))],
        out_specs=pl.BlockSpec((tm, n), lambda i: (i, 0)),
        out_shape=jax.ShapeDtypeStruct((m, n), f32),
        compiler_params=_cparams(("parallel",)), name="proj")(xb, w)


def _layernorm(y, g, b):
    mu = jnp.mean(y, -1, keepdims=True)
    yc = y - mu
    var = jnp.mean(yc * yc, -1, keepdims=True)
    return yc * lax.rsqrt(var + LN_EPS) * g + b


def _outproj_ln_body(h0, h1, h2, h3, w_ref, x_ref, g_ref, b_ref, o_ref, ob_ref):
    acc = ALPHA * x_ref[...]
    for j, h in enumerate((h0, h1, h2, h3)):
        acc = acc + jnp.dot(h[...], w_ref[j * GROUP:(j + 1) * GROUP, :], preferred_element_type=f32)
    y = _layernorm(acc, g_ref[...], b_ref[...])
    o_ref[...] = y
    ob_ref[...] = y.astype(bf16)


def _outproj_ln(hs, w, x, g, b, l, tm):
    m = x.shape[0]
    hspec = pl.BlockSpec((tm, GROUP), lambda i: (i, 0))
    xspec = pl.BlockSpec((tm, D_MODEL), lambda i: (i, 0))
    vspec = _layer_spec(l, (1, D_MODEL))
    return pl.pallas_call(
        _outproj_ln_body, grid=(m // tm,),
        in_specs=[hspec] * 4 + [_layer_spec(l, (D_MODEL, D_MODEL)), xspec, vspec, vspec],
        out_specs=[xspec, xspec],
        out_shape=[jax.ShapeDtypeStruct((m, D_MODEL), f32), jax.ShapeDtypeStruct((m, D_MODEL), bf16)],
        compiler_params=_cparams(("parallel",)), name="outproj_ln")(*hs, w, x, g, b)


def _ffn_ln_body(x_ref, xb_ref, wu_ref, wd_ref, g_ref, b_ref, o_ref, ob_ref, acc_ref):
    f = pl.program_id(1)

    @pl.when(f == 0)
    def _():
        acc_ref[...] = ALPHA * x_ref[...]

    h = jnp.maximum(jnp.dot(xb_ref[...], wu_ref[...], preferred_element_type=f32), 0.0)
    acc_ref[...] += jnp.dot((h * h).astype(bf16), wd_ref[...], preferred_element_type=f32)

    @pl.when(f == pl.num_programs(1) - 1)
    def _():
        y = _layernorm(acc_ref[...], g_ref[...], b_ref[...])
        o_ref[...] = y
        ob_ref[...] = y.astype(bf16)


def _ffn_ln(x, xb, wu, wd, g, b, l, tm, tf):
    m = x.shape[0]
    xspec = pl.BlockSpec((tm, D_MODEL), lambda i, f: (i, 0))
    vspec = _layer_spec(l, (1, D_MODEL))
    return pl.pallas_call(
        _ffn_ln_body, grid=(m // tm, D_FF // tf),
        in_specs=[xspec, xspec, _layer_spec(l, (D_MODEL, tf), lambda i, f: (0, f)),
                  _layer_spec(l, (tf, D_MODEL), lambda i, f: (f, 0)), vspec, vspec],
        out_specs=[xspec, xspec],
        out_shape=[jax.ShapeDtypeStruct((m, D_MODEL), f32), jax.ShapeDtypeStruct((m, D_MODEL), bf16)],
        scratch_shapes=[pltpu.VMEM((tm, D_MODEL), f32)],
        compiler_params=_cparams(("parallel", "arbitrary")), name="ffn_ln")(x, xb, wu, wd, g, b)


def _tril_blocks(nb, L):
    r, c = _iota((nb * L, nb * L), 0), _iota((nb * L, nb * L), 1)
    return jnp.where((c <= r) & (c >= r - (r & (L - 1))), 1.0, 0.0)


def _units(nb, heads):
    return [(b, h) for b in range(nb) for h in range(heads)]


def _mlstm_prompt_body(cols_ref, gb_ref, ng_ref, h_ref, c_ref, n_ref, m_ref):
    @pl.when(pl.program_id(0) == 0)
    def _():
        c_ref[...] = jnp.zeros_like(c_ref)
        n_ref[...] = jnp.zeros_like(n_ref)
        m_ref[...] = jnp.zeros_like(m_ref)

    nb, L = cols_ref.shape[0], CHUNK
    causal, _ = _tri_masks(L)
    g = cols_ref[:, :, 1536:1664].reshape(nb * L, LANES) + gb_ref[...]
    bc = _select_dot(_tril_blocks(nb, L), _log_sigmoid(g))
    y_all = jnp.where(_iota((nb * L, LANES), 1) < ML_H, g, bc)
    y = [y_all[b * L:(b + 1) * L, :] for b in range(nb)]
    yt = [_rows_of(y[b]) for b in range(nb)]
    us = _units(nb, ML_H)
    ix = range(len(us))
    q = [cols_ref[b, :, h * ML_DK:(h + 1) * ML_DK] for b, h in us]
    k = [cols_ref[b, :, 256 + h * ML_DK:256 + (h + 1) * ML_DK] * ML_DK ** -0.5 for b, h in us]
    v = [cols_ref[b, :, 512 + h * ML_DV:512 + (h + 1) * ML_DV] for b, h in us]
    c = [c_ref[b, h] for b, h in us]
    n = [n_ref[b, h:h + 1, :] for b, h in us]
    m = [m_ref[b, h:h + 1, :] for b, h in us]
    qk = [_ein_nt(q[i], k[i]) for i in ix]
    qc = [_ein(q[i], c[i]) for i in ix]
    qn = [_ein_nt(q[i], jnp.broadcast_to(n[i], (LANES, ML_DK))) for i in ix]
    spread_shape = (LANES, 2 * ML_H * LANES)
    spread = jnp.where(_iota(spread_shape, 0) == jnp.right_shift(_iota(spread_shape, 1), LANES.bit_length() - 1), 1.0, 0.0)
    rep = [_select_dot_right(y[b], spread) for b in range(nb)]
    i_rep = [rep[b][:, h * LANES:(h + 1) * LANES] for b, h in us]
    b_rep = [rep[b][:, (ML_H + h) * LANES:(ML_H + h + 1) * LANES] for b, h in us]
    d = [jnp.where(causal, b_rep[i][:, :L] - yt[b][ML_H + h:ML_H + h + 1, :] + yt[b][h:h + 1, :], -jnp.inf)
         for i, (b, h) in enumerate(us)]
    inter = [b_rep[i] + m[i] for i in ix]
    m_t = [jnp.maximum(inter[i], jnp.max(d[i], axis=-1, keepdims=True)) for i in ix]
    w_inter = [jnp.exp(inter[i] - m_t[i]) for i in ix]
    s = [qk[i] * jnp.exp(d[i] - m_t[i][:, :L]) for i in ix]
    m_new = [m_t[i][L - 1:L, :] for i in ix]
    b_last = [b_rep[i][L - 1:L, :] for i in ix]
    kw = [k[i] * jnp.exp(b_last[i] - b_rep[i] + i_rep[i] - m_new[i])[:, :ML_DK] for i in ix]
    sv = [_ein(s[i], v[i]) for i in ix]
    kv = [_ein_tn(kw[i], v[i]) for i in ix]
    s_sum = [jnp.sum(s[i], -1, keepdims=True) for i in ix]
    f_state = [jnp.exp(b_last[i] + m[i] - m_new[i]) for i in ix]
    c_new = [f_state[i] * c[i] + kv[i] for i in ix]
    n_new = [f_state[i][:, :ML_DK] * n[i] + jnp.sum(kw[i], axis=0, keepdims=True) for i in ix]
    for i, (b, h) in enumerate(us):
        c_ref[b, h] = c_new[i]
        n_ref[b, h:h + 1, :] = n_new[i]
        m_ref[b, h:h + 1, :] = m_new[i]
    hh = [(w_inter[i] * qc[i] + sv[i]) / jnp.maximum(jnp.abs(w_inter[i] * qn[i] + s_sum[i]), jnp.exp(-m_t[i])) for i in ix]
    ms = [jnp.mean(hh[i] * hh[i], -1, keepdims=True) for i in ix]
    for i, (b, h) in enumerate(us):
        og = cols_ref[b, :, 1024 + h * ML_DV:1024 + (h + 1) * ML_DV]
        hn = hh[i] * lax.rsqrt(ms[i] + NORM_EPS) * ng_ref[:, h * ML_DV:(h + 1) * ML_DV]
        h_ref[b, :, h * ML_DV:(h + 1) * ML_DV] = (hn * jax.nn.sigmoid(og)).astype(h_ref.dtype)


def _chunk_grid_specs(bp, t, width):
    cols = pl.BlockSpec((bp, CHUNK, width), lambda c: (0, c, 0))
    out = pl.BlockSpec((bp, CHUNK, GROUP), lambda c: (0, c, 0))
    return (t // CHUNK,), cols, out


def _whole(shape):
    return pl.BlockSpec(shape, lambda c: (0,) * len(shape))


def _mlstm_prompt(cols, gb, ng, l, bp, t):
    grid, cspec, ospec = _chunk_grid_specs(bp, t, W_ML)
    return pl.pallas_call(
        _mlstm_prompt_body, grid=grid,
        in_specs=[cspec, _layer_spec(l, (1, LANES)), _layer_spec(l, (1, GROUP))],
        out_specs=[ospec, _whole((bp, ML_H, ML_DK, ML_DV)), _whole((bp, ML_H, ML_DK)), _whole((bp, ML_H, LANES))],
        out_shape=[jax.ShapeDtypeStruct((bp, t, GROUP), bf16), jax.ShapeDtypeStruct((bp, ML_H, ML_DK, ML_DV), f32),
                   jax.ShapeDtypeStruct((bp, ML_H, ML_DK), f32), jax.ShapeDtypeStruct((bp, ML_H, LANES), f32)],
        compiler_params=_cparams(("arbitrary",)), name="mlstm_prompt")(cols.reshape(bp, t, W_ML), gb, ng)


def _gla_prompt_body(cols_ref, gup_ref, gb_ref, ng_ref, h_ref, s_ref):
    @pl.when(pl.program_id(0) == 0)
    def _():
        s_ref[...] = jnp.zeros_like(s_ref)

    nb, L = cols_ref.shape[0], CHUNK
    causal, _ = _tri_masks(L)
    lr = cols_ref[:, :, 1536:1664].reshape(nb * L, LANES)
    la = _log_sigmoid(_ein(lr, gup_ref[...]) + gb_ref[...]) / GLA_TAU
    bc_all = _select_dot(_tril_blocks(nb, L), la)
    bc = [bc_all[b * L:(b + 1) * L, :] for b in range(nb)]
    mid = [bc[b][L // 2:L // 2 + 1, :] for b in range(nb)]
    last = [bc[b][L - 1:L, :] for b in range(nb)]
    e_q_mid = [jnp.exp(bc[b] - mid[b]) for b in range(nb)]
    e_k_mid = [jnp.exp(mid[b] - bc[b]) for b in range(nb)]
    e_q = [jnp.exp(bc[b]) for b in range(nb)]
    e_k_last = [jnp.exp(last[b] - bc[b]) for b in range(nb)]
    e_last = [jnp.exp(last[b]) for b in range(nb)]
    us = _units(nb, GLA_H)
    ix = range(len(us))
    sl = [slice(h * GLA_DK, (h + 1) * GLA_DK) for _, h in us]
    q = [cols_ref[b, :, h * GLA_DK:(h + 1) * GLA_DK] * GLA_DK ** -0.5 for b, h in us]
    k = [cols_ref[b, :, 256 + h * GLA_DK:256 + (h + 1) * GLA_DK] for b, h in us]
    v = [cols_ref[b, :, 512 + h * GLA_DV:512 + (h + 1) * GLA_DV] for b, h in us]
    s = [s_ref[b, h] for b, h in us]
    a = [jnp.where(causal, _ein_nt(q[i] * e_q_mid[b][:, sl[i]], k[i] * e_k_mid[b][:, sl[i]]), 0.0)
         for i, (b, _) in enumerate(us)]
    o0 = [_ein(q[i] * e_q[b][:, sl[i]], s[i]) for i, (b, _) in enumerate(us)]
    kv = [_ein_tn(k[i] * e_k_last[b][:, sl[i]], v[i]) for i, (b, _) in enumerate(us)]
    o = [o0[i] + _ein(a[i], v[i]) for i in ix]
    for i, (b, h) in enumerate(us):
        s_ref[b, h] = _col(e_last[b][:, sl[i]]) * s[i] + kv[i]
    ms = [jnp.mean(o[i] * o[i], -1, keepdims=True) for i in ix]
    for i, (b, h) in enumerate(us):
        og = cols_ref[b, :, 1024 + h * GLA_DV:1024 + (h + 1) * GLA_DV]
        on = o[i] * lax.rsqrt(ms[i] + NORM_EPS) * ng_ref[:, h * GLA_DV:(h + 1) * GLA_DV]
        h_ref[b, :, h * GLA_DV:(h + 1) * GLA_DV] = (on * _silu(og)).astype(h_ref.dtype)


def _gla_prompt(cols, gup, gb, ng, l, bp, t):
    grid, cspec, ospec = _chunk_grid_specs(bp, t, W_GLA)
    return pl.pallas_call(
        _gla_prompt_body, grid=grid,
        in_specs=[cspec, _layer_spec(l, (LANES, GLA_H * GLA_DK)), _layer_spec(l, (1, GLA_H * GLA_DK)),
                  _layer_spec(l, (1, GROUP))],
        out_specs=[ospec, _whole((bp, GLA_H, GLA_DK, GLA_DV))],
        out_shape=[jax.ShapeDtypeStruct((bp, t, GROUP), bf16), jax.ShapeDtypeStruct((bp, GLA_H, GLA_DK, GLA_DV), f32)],
        compiler_params=_cparams(("arbitrary",)), name="gla_prompt")(cols.reshape(bp, t, W_GLA), gup, gb, ng)


def _rwkv_pre(x, prev, mu, wup, aup, gup, vec):
    xs = x + (prev - x) * mu
    r, k, v, lb = xs[:, 0:512], xs[:, 512:1024], xs[:, 1024:1536], xs[:, 1536:1664]
    w0, a0, k_k, k_a = vec[0:1, :], vec[1:2, :], vec[2:3, :], vec[3:4, :]
    lw = -RW_DECAY_SCALE * jax.nn.sigmoid(w0 + _ein(jnp.tanh(lb), wup))
    a = jax.nn.sigmoid(a0 + _ein(lb, aup))
    g = _ein(jax.nn.sigmoid(lb), gup)
    kk_raw = k * k_k
    k2 = k * (1.0 + (a - 1.0) * k_a)
    return r, k2, v, lw, a, g, kk_raw


def _rwkv_post(o, r, k2, v, g, vec):
    hs = range(RW_H)
    sl = [slice(h * RW_N, (h + 1) * RW_N) for h in hs]
    mu_o = [jnp.mean(o[h], -1, keepdims=True) for h in hs]
    bonus_w = [jnp.sum(r[:, sl[h]] * k2[:, sl[h]] * vec[4:5, sl[h]], -1, keepdims=True) for h in hs]
    oc = [o[h] - mu_o[h] for h in hs]
    var_o = [jnp.mean(oc[h] * oc[h], -1, keepdims=True) for h in hs]
    outs = [(oc[h] * lax.rsqrt(var_o[h] + RW_GN_EPS) * vec[5:6, sl[h]] + vec[6:7, sl[h]]
             + bonus_w[h] * v[:, sl[h]]) * g[:, sl[h]] for h in hs]
    return jnp.concatenate(outs, axis=-1)


def _rwkv_prompt_body(cols_ref, mu_ref, wup_ref, aup_ref, gup_ref, vec_ref, h_ref, s_ref, prev_scr):
    @pl.when(pl.program_id(0) == 0)
    def _():
        s_ref[...] = jnp.zeros_like(s_ref)
        prev_scr[...] = jnp.zeros_like(prev_scr)

    nb, L, N = cols_ref.shape[0], CHUNK, RW_N
    first_row = _iota((L, W_RW), 0) == 0
    xb = [cols_ref[b] for b in range(nb)]
    prevs = [jnp.where(first_row, prev_scr[b, 0:1, :], pltpu.roll(xb[b], 1, 0)) for b in range(nb)]
    for b in range(nb):
        prev_scr[b, 0:1, :] = xb[b][L - 1:L, :]
    vec = vec_ref[...]
    r, k2, v, lw, a, g, kk_raw = _rwkv_pre(jnp.concatenate(xb, axis=0), jnp.concatenate(prevs, axis=0), mu_ref[...],
                                           wup_ref[...], aup_ref[...], gup_ref[...], vec)
    cum = _select_dot(_tril_blocks(nb, L), lw)
    cum_prev = cum - lw
    rep = lambda row_of: jnp.concatenate(
        [jnp.broadcast_to(cum[b * L + row_of:b * L + row_of + 1, :], (L, RW_H * N)) for b in range(nb)], axis=0)
    mid, last = rep(L // 2), rep(L - 1)
    e_prev_mid, e_mid_cum, e_cum_mid = jnp.exp(cum_prev - mid), jnp.exp(mid - cum), jnp.exp(cum - mid)
    e_prev, e_cum, e_last_cum, e_last = jnp.exp(cum_prev), jnp.exp(cum), jnp.exp(last - cum), jnp.exp(last)
    row, col = _iota((2 * L, 2 * L), 0), _iota((2 * L, 2 * L), 1)
    tq, sq = row % L, col % L
    quad = sq < tq + jnp.where(row < L, 0, 1)
    left = col[:, :] < L
    hs = range(RW_H)
    sl = [slice(h * N, (h + 1) * N) for h in hs]
    kk_h = [_l2norm(kk_raw[:, sl[h]]) for h in hs]
    b_h = [kk_h[h] * a[:, sl[h]] for h in hs]
    f_lhs = [(kk_h[h] * e_prev_mid[:, sl[h]], r[:, sl[h]] * e_cum_mid[:, sl[h]]) for h in hs]
    f_rhs = [(b_h[h] * e_mid_cum[:, sl[h]], k2[:, sl[h]] * e_mid_cum[:, sl[h]]) for h in hs]
    f_x0 = [(kk_h[h] * e_prev[:, sl[h]], r[:, sl[h]] * e_cum[:, sl[h]]) for h in hs]
    f_dec = [(b_h[h] * e_last_cum[:, sl[h]], k2[:, sl[h]] * e_last_cum[:, sl[h]]) for h in hs]
    us = _units(nb, RW_H)
    ix = range(len(us))
    rows = lambda z, b: z[b * L:(b + 1) * L, :]
    pair = lambda f, b, h: jnp.concatenate([rows(f[h][0], b), rows(f[h][1], b)], axis=0)
    vh = [rows(v[:, sl[h]], b) for b, h in us]
    s = [s_ref[b, h] for b, h in us]
    a_all = [jnp.where(quad, _ein_nt(pair(f_lhs, b, h), pair(f_rhs, b, h)), 0.0) for b, h in us]
    t_inv = _unit_lower_inverses([a_all[i][0:L, 0:L] for i in ix])
    x0 = [pair(f_x0, b, h) for b, h in us]
    m2 = [jnp.where(left, jnp.concatenate([x0[i], x0[i]], axis=1), a_all[i]) for i in ix]
    y = [_ein(m2[i], jnp.concatenate([s[i], vh[i]], axis=0)) for i in ix]
    u = [-_dot3(t_inv[i], y[i][0:L, :]) for i in ix]
    o = [y[i][L:2 * L, :] + _ein(a_all[i][L:2 * L, 0:L], u[i]) for i in ix]
    for i, (b, h) in enumerate(us):
        s_ref[b, h] = (_col(e_last[b * L:b * L + 1, sl[h]]) * s[i]
                       + _ein_tn(pair(f_dec, b, h), jnp.concatenate([u[i], vh[i]], axis=0)))
    o_heads = [jnp.concatenate([o[b * RW_H + h] for b in range(nb)], axis=0) for h in hs]
    h_ref[...] = _rwkv_post(o_heads, r, k2, v, g, vec).reshape(nb, L, GROUP).astype(h_ref.dtype)


def _rwkv_prompt(cols, mu, wup, aup, gup, vec, l, bp, t):
    grid, cspec, ospec = _chunk_grid_specs(bp, t, W_RW)
    return pl.pallas_call(
        _rwkv_prompt_body, grid=grid,
        in_specs=[cspec, _layer_spec(l, (1, W_RW)), _layer_spec(l, (LANES, GROUP)), _layer_spec(l, (LANES, GROUP)),
                  _layer_spec(l, (LANES, GROUP)), _layer_spec(l, (8, GROUP))],
        out_specs=[ospec, _whole((bp, RW_H, RW_N, RW_N))],
        out_shape=[jax.ShapeDtypeStruct((bp, t, GROUP), bf16), jax.ShapeDtypeStruct((bp, RW_H, RW_N, RW_N), f32)],
        scratch_shapes=[pltpu.VMEM((bp, 8, W_RW), f32)],
        compiler_params=_cparams(("arbitrary",)), name="rwkv_prompt")(cols.reshape(bp, t, W_RW), mu, wup, aup, gup, vec)


def _gdn_gates(gb, alog, dtb):
    return jax.nn.sigmoid(gb), -jnp.exp(alog) * _softplus(gb + dtb)


def _gdn_prompt_body(cols_ref, cw_ref, alog_ref, dtb_ref, ng_ref, h_ref, s_ref, xp_scr):
    @pl.when(pl.program_id(0) == 0)
    def _():
        s_ref[...] = jnp.zeros_like(s_ref)
        xp_scr[...] = jnp.zeros_like(xp_scr)

    nb, L = cols_ref.shape[0], CHUNK
    causal, strict = _tri_masks(L)
    act = []
    for b in range(nb):
        raw = cols_ref[b, :, 0:DN_QKV]
        ext = jnp.concatenate([xp_scr[b], raw], axis=0)
        conv = raw * cw_ref[DN_CONV - 1:DN_CONV, :]
        for j in range(1, DN_CONV):
            conv = conv + pltpu.roll(ext, j, 0)[8:8 + L, :] * cw_ref[DN_CONV - 1 - j:DN_CONV - j, :]
        xp_scr[b] = raw[L - 8:L, :]
        act.append(_silu(conv))
    beta_all, g_all = _gdn_gates(cols_ref[:, :, 2048:2176].reshape(nb * L, LANES), alog_ref[...], dtb_ref[...])
    gc_all = _select_dot(_tril_blocks(nb, L), g_all)
    gc = [gc_all[b * L:(b + 1) * L, :] for b in range(nb)]
    gt = [_rows_of(gc[b]) for b in range(nb)]
    us = _units(nb, DN_H)
    ix = range(len(us))
    q = [_l2norm(act[b][:, h * DN_D:(h + 1) * DN_D]) * DN_D ** -0.5 for b, h in us]
    k = [_l2norm(act[b][:, GROUP + h * DN_D:GROUP + (h + 1) * DN_D]) for b, h in us]
    v = [act[b][:, 2 * GROUP + h * DN_D:2 * GROUP + (h + 1) * DN_D] for b, h in us]
    beta = [beta_all[b * L:(b + 1) * L, h:h + 1] for b, h in us]
    g_col = [gc[b][:, DN_H + h:DN_H + h + 1] for b, h in us]
    decay = [jnp.exp(jnp.where(causal, g_col[i] - gt[b][DN_H + h:DN_H + h + 1, :], -jnp.inf))
             for i, (b, h) in enumerate(us)]
    kb = [k[i] * beta[i] for i in ix]
    eg = [jnp.exp(g_col[i]) for i in ix]
    s = [s_ref[b, h] for b, h in us]
    kq = [_ein_nt(jnp.concatenate([kb[i], q[i]], axis=0), k[i]) for i in ix]
    t_inv = _unit_lower_inverses([jnp.where(strict, kq[i][0:L, :] * decay[i], 0.0) for i in ix])
    qk = [kq[i][L:2 * L, :] * decay[i] for i in ix]
    o0 = [_ein(q[i] * eg[i], s[i]) for i in ix]
    sol = [_dot3(t_inv[i], jnp.concatenate([v[i] * beta[i], kb[i] * eg[i]], axis=-1)) for i in ix]
    v_new = [sol[i][:, :DN_D] - _ein(sol[i][:, DN_D:], s[i]) for i in ix]
    o = [o0[i] + _ein(qk[i], v_new[i]) for i in ix]
    for i, (b, h) in enumerate(us):
        g_last = g_col[i][L - 1:L, :]
        s_ref[b, h] = jnp.exp(g_last) * s[i] + _ein_tn(k[i] * jnp.exp(g_last - g_col[i]), v_new[i])
    ms = [jnp.mean(o[i] * o[i], -1, keepdims=True) for i in ix]
    for i, (b, h) in enumerate(us):
        z = cols_ref[b, :, DN_QKV + h * DN_D:DN_QKV + (h + 1) * DN_D]
        on = o[i] * lax.rsqrt(ms[i] + NORM_EPS) * ng_ref[...]
        h_ref[b, :, h * DN_D:(h + 1) * DN_D] = (on * _silu(z)).astype(h_ref.dtype)


def _gdn_prompt(cols, cw, alog, dtb, ng, l, bp, t):
    grid, cspec, ospec = _chunk_grid_specs(bp, t, W_DN)
    return pl.pallas_call(
        _gdn_prompt_body, grid=grid,
        in_specs=[cspec, _layer_spec(l, (8, DN_QKV)), _layer_spec(l, (1, LANES)), _layer_spec(l, (1, LANES)),
                  _layer_spec(l, (1, DN_D))],
        out_specs=[ospec, _whole((bp, DN_H, DN_D, DN_D))],
        out_shape=[jax.ShapeDtypeStruct((bp, t, GROUP), bf16), jax.ShapeDtypeStruct((bp, DN_H, DN_D, DN_D), f32)],
        scratch_shapes=[pltpu.VMEM((bp, 8, DN_QKV), f32)],
        compiler_params=_cparams(("arbitrary",)), name="gdn_prompt")(cols.reshape(bp, t, W_DN), cw, alog, dtb, ng)


def _mlstm_step_body(cols_ref, gb_ref, ng_ref, c_ref, n_ref, m_ref, h_ref, co_ref, no_ref, mo_ref,
                     q_scr, kw_scr, v_scr, f_scr, qc_scr):
    nb = DEC_BLOCK
    g = cols_ref[:, 1536:1664] + gb_ref[...]
    lf = _log_sigmoid(g)
    hs = []
    for h in range(ML_H):
        q = cols_ref[:, h * ML_DK:(h + 1) * ML_DK]
        k = cols_ref[:, 256 + h * ML_DK:256 + (h + 1) * ML_DK] * ML_DK ** -0.5
        v = cols_ref[:, 512 + h * ML_DV:512 + (h + 1) * ML_DV]
        i_pre, f_log = g[:, h:h + 1], lf[:, ML_H + h:ML_H + h + 1]
        m = m_ref[:, h:h + 1]
        inter = f_log + m
        m_t = jnp.maximum(inter, i_pre)
        w_inter = jnp.exp(inter - m_t)
        kw = k * jnp.exp(i_pre - m_t)
        q_scr[h], kw_scr[h], v_scr[h] = q, kw, v
        f_scr[h] = jnp.broadcast_to(w_inter, (nb, LANES))
        hs.append((q, k, v, i_pre, m_t, w_inter, kw))

    def row(b, carry):
        hr = range(ML_H)
        kw_col = [_col(kw_scr[h, pl.ds(b, 1), :]) for h in hr]
        c = [c_ref[b, h] for h in hr]
        qc = [_row_of(_ein(q_scr[h], c[h]), b) for h in hr]
        for h in hr:
            qc_scr[h, pl.ds(b, 1), :] = qc[h]
            co_ref[b, h] = f_scr[h, pl.ds(b, 1), :] * c[h] + kw_col[h] * v_scr[h, pl.ds(b, 1), :]
        return carry

    lax.fori_loop(0, nb, row, 0)
    for h, (q, k, v, i_pre, m_t, w_inter, kw) in enumerate(hs):
        og = cols_ref[:, 1024 + h * ML_DV:1024 + (h + 1) * ML_DV]
        n = n_ref[:, h * ML_DK:(h + 1) * ML_DK]
        s = jnp.sum(q * k, -1, keepdims=True) * jnp.exp(i_pre - m_t)
        num = w_inter * qc_scr[h] + s * v
        den = w_inter * jnp.sum(q * n, -1, keepdims=True) + s
        hh = num / jnp.maximum(jnp.abs(den), jnp.exp(-m_t))
        no_ref[:, h * ML_DK:(h + 1) * ML_DK] = w_inter * n + kw
        mo_ref[:, h:h + 1] = m_t
        hn = hh * lax.rsqrt(jnp.mean(hh * hh, -1, keepdims=True) + NORM_EPS) * ng_ref[:, h * ML_DV:(h + 1) * ML_DV]
        h_ref[:, h * ML_DV:(h + 1) * ML_DV] = (hn * jax.nn.sigmoid(og)).astype(h_ref.dtype)


def _row_spec(width):
    return pl.BlockSpec((DEC_BLOCK, width), lambda i: (i, 0))


def _layer_rows(l, width):
    return _layer_spec(l, (DEC_BLOCK, width), lambda i: (i, 0))


def _layer_state(l, h, dk, dv):
    return _layer_spec(l, (DEC_BLOCK, h, dk, dv), lambda i: (i, 0, 0, 0))


def _ignore_first_ref(body):
    def with_handed_on_buffer(_stacked_out_so_far, *refs):
        body(*refs)
    return with_handed_on_buffer


def _step_call(body, l, stacked_out, in_specs, out_specs, out_shapes, state_out_index, **kw):
    if stacked_out is None:
        return lambda *args: pl.pallas_call(body, in_specs=in_specs, out_specs=out_specs, out_shape=out_shapes, **kw)(*args)
    call = pl.pallas_call(_ignore_first_ref(body), in_specs=[pl.BlockSpec(memory_space=pl.ANY)] + in_specs,
                          out_specs=out_specs, out_shape=out_shapes, input_output_aliases={0: state_out_index}, **kw)
    return lambda *args: call(stacked_out, *args)


def _mlstm_step(cols, gb, ng, c0, n0, m0, l, c_out):
    bs = cols.shape[0]
    nb = DEC_BLOCK
    call = _step_call(
        _mlstm_step_body, l, c_out,
        in_specs=[_row_spec(W_ML), _layer_spec(l, (1, LANES)), _layer_spec(l, (1, GROUP)),
                  _layer_state(l, ML_H, ML_DK, ML_DV), _layer_rows(l, ML_H * ML_DK), _layer_rows(l, ML_H)],
        out_specs=[_row_spec(GROUP), _layer_state(l, ML_H, ML_DK, ML_DV), _row_spec(ML_H * ML_DK), _row_spec(ML_H)],
        out_shapes=[jax.ShapeDtypeStruct((bs, GROUP), bf16), jax.ShapeDtypeStruct(c0.shape, f32),
                    jax.ShapeDtypeStruct((bs, ML_H * ML_DK), f32), jax.ShapeDtypeStruct((bs, ML_H), f32)],
        state_out_index=1, grid=(bs // nb,),
        scratch_shapes=[pltpu.VMEM((ML_H, nb, ML_DK), f32), pltpu.VMEM((ML_H, nb, ML_DK), f32),
                        pltpu.VMEM((ML_H, nb, ML_DV), f32), pltpu.VMEM((ML_H, nb, LANES), f32),
                        pltpu.VMEM((ML_H, nb, ML_DV), f32)],
        compiler_params=_cparams(("parallel",)), name="mlstm_step")
    return call(cols, gb, ng, c0, n0, m0)


def _gla_step_body(cols_ref, gup_ref, gb_ref, ng_ref, s_ref, h_ref, so_ref, qe_scr, ea_scr, k_scr, v_scr, qs_scr):
    nb = DEC_BLOCK
    la = _log_sigmoid(_ein(cols_ref[:, 1536:1664], gup_ref[...]) + gb_ref[...]) / GLA_TAU
    ea = jnp.exp(la)
    hs = []
    for h in range(GLA_H):
        q = cols_ref[:, h * GLA_DK:(h + 1) * GLA_DK] * GLA_DK ** -0.5
        k = cols_ref[:, 256 + h * GLA_DK:256 + (h + 1) * GLA_DK]
        v = cols_ref[:, 512 + h * GLA_DV:512 + (h + 1) * GLA_DV]
        ea_h = ea[:, h * GLA_DK:(h + 1) * GLA_DK]
        qe_scr[h], ea_scr[h], k_scr[h], v_scr[h] = q * ea_h, ea_h, k, v
        hs.append((q, k, v))

    def row(b, carry):
        hr = range(GLA_H)
        ea_col = [_col(ea_scr[h, pl.ds(b, 1), :]) for h in hr]
        k_col = [_col(k_scr[h, pl.ds(b, 1), :]) for h in hr]
        s = [s_ref[b, h] for h in hr]
        qs = [_row_of(_ein(qe_scr[h], s[h]), b) for h in hr]
        for h in hr:
            qs_scr[h, pl.ds(b, 1), :] = qs[h]
            so_ref[b, h] = ea_col[h] * s[h] + k_col[h] * v_scr[h, pl.ds(b, 1), :]
        return carry

    lax.fori_loop(0, nb, row, 0)
    for h, (q, k, v) in enumerate(hs):
        og = cols_ref[:, 1024 + h * GLA_DV:1024 + (h + 1) * GLA_DV]
        o = qs_scr[h] + jnp.sum(q * k, -1, keepdims=True) * v
        on = o * lax.rsqrt(jnp.mean(o * o, -1, keepdims=True) + NORM_EPS) * ng_ref[:, h * GLA_DV:(h + 1) * GLA_DV]
        h_ref[:, h * GLA_DV:(h + 1) * GLA_DV] = (on * _silu(og)).astype(h_ref.dtype)


def _gla_step(cols, gup, gb, ng, s0, l, s_out):
    bs = cols.shape[0]
    nb = DEC_BLOCK
    call = _step_call(
        _gla_step_body, l, s_out,
        in_specs=[_row_spec(W_GLA), _layer_spec(l, (LANES, GLA_H * GLA_DK)), _layer_spec(l, (1, GLA_H * GLA_DK)),
                  _layer_spec(l, (1, GROUP)), _layer_state(l, GLA_H, GLA_DK, GLA_DV)],
        out_specs=[_row_spec(GROUP), _layer_state(l, GLA_H, GLA_DK, GLA_DV)],
        out_shapes=[jax.ShapeDtypeStruct((bs, GROUP), bf16), jax.ShapeDtypeStruct(s0.shape, f32)],
        state_out_index=1, grid=(bs // nb,),
        scratch_shapes=[pltpu.VMEM((GLA_H, nb, GLA_DK), f32)] * 3 + [pltpu.VMEM((GLA_H, nb, GLA_DV), f32)] * 2,
        compiler_params=_cparams(("parallel",)), name="gla_step")
    return call(cols, gup, gb, ng, s0)


def _rwkv_step_body(cols_ref, shift_ref, mu_ref, wup_ref, aup_ref, gup_ref, vec_ref, s_ref, h_ref, so_ref,
                    w_scr, b_scr, k_scr, kkrw_scr, v_scr, sk_scr, o_scr):
    nb = DEC_BLOCK
    vec = vec_ref[...]
    r, k2, v, lw, a, g, kk_raw = _rwkv_pre(cols_ref[...], shift_ref[...], mu_ref[...], wup_ref[...], aup_ref[...],
                                           gup_ref[...], vec)
    hr = range(RW_H)
    sl = [slice(h * RW_N, (h + 1) * RW_N) for h in hr]
    kk = [_l2norm(kk_raw[:, sl[h]]) for h in hr]
    w = jnp.exp(lw)
    bv = jnp.concatenate(kk, axis=-1) * a
    w_scr[...], b_scr[...], k_scr[...] = w, bv, k2
    for h in hr:
        kkrw_scr[h, 0:nb, :], kkrw_scr[h, nb:2 * nb, :] = kk[h], r[:, sl[h]] * w[:, sl[h]]
        v_scr[h] = v[:, sl[h]]

    def row(b, carry):
        w_col, b_col, k_col = (_cols_of(scr[pl.ds(b, 1), :], RW_N) for scr in (w_scr, b_scr, k_scr))
        s = [s_ref[b, h] for h in hr]
        mv = [_ein(kkrw_scr[h], s[h]) for h in hr]
        sk = [_row_of(mv[h][0:nb, :], b) for h in hr]
        rws = [_row_of(mv[h][nb:2 * nb, :], b) for h in hr]
        for h in hr:
            so_ref[b, h] = w_col[h] * s[h] - b_col[h] * sk[h] + k_col[h] * v_scr[h, pl.ds(b, 1), :]
            sk_scr[h, pl.ds(b, 1), :] = sk[h]
            o_scr[h, pl.ds(b, 1), :] = rws[h]
        return carry

    lax.fori_loop(0, nb, row, 0)
    rb = [jnp.sum(r[:, sl[h]] * bv[:, sl[h]], -1, keepdims=True) for h in hr]
    rk = [jnp.sum(r[:, sl[h]] * k2[:, sl[h]], -1, keepdims=True) for h in hr]
    o = [o_scr[h] - rb[h] * sk_scr[h] + rk[h] * v[:, sl[h]] for h in hr]
    h_ref[...] = _rwkv_post(o, r, k2, v, g, vec).astype(h_ref.dtype)


def _rwkv_step(cols, shift0, mu, wup, aup, gup, vec, s0, l, s_out):
    bs = cols.shape[0]
    nb = DEC_BLOCK
    call = _step_call(
        _rwkv_step_body, l, s_out,
        in_specs=[_row_spec(W_RW), _layer_rows(l, W_RW), _layer_spec(l, (1, W_RW)), _layer_spec(l, (LANES, GROUP)),
                  _layer_spec(l, (LANES, GROUP)), _layer_spec(l, (LANES, GROUP)), _layer_spec(l, (8, GROUP)),
                  _layer_state(l, RW_H, RW_N, RW_N)],
        out_specs=[_row_spec(GROUP), _layer_state(l, RW_H, RW_N, RW_N)],
        out_shapes=[jax.ShapeDtypeStruct((bs, GROUP), bf16), jax.ShapeDtypeStruct(s0.shape, f32)],
        state_out_index=1, grid=(bs // nb,),
        scratch_shapes=[pltpu.VMEM((nb, GROUP), f32)] * 3 + [pltpu.VMEM((RW_H, 2 * nb, RW_N), f32)]
        + [pltpu.VMEM((RW_H, nb, RW_N), f32)] * 3,
        compiler_params=_cparams(("parallel",)), name="rwkv_step")
    return call(cols, shift0, mu, wup, aup, gup, vec, s0)


def _gdn_step_body(cols_ref, buf_ref, cw_ref, alog_ref, dtb_ref, ng_ref, s_ref, h_ref, so_ref,
                   kq_scr, v_scr, sc_scr, qs_scr, vn_scr):
    nb = DEC_BLOCK
    conv = cols_ref[:, 0:DN_QKV] * cw_ref[DN_CONV - 1:DN_CONV, :]
    for w in range(DN_CONV - 1):
        conv = conv + buf_ref[:, w * DN_QKV:(w + 1) * DN_QKV] * cw_ref[w:w + 1, :]
    act = _silu(conv)
    beta_all, g_all = _gdn_gates(cols_ref[:, 2048:2176], alog_ref[...], dtb_ref[...])
    eg_all = jnp.exp(g_all)
    hs = []
    for h in range(DN_H):
        q = _l2norm(act[:, h * DN_D:(h + 1) * DN_D]) * DN_D ** -0.5
        k = _l2norm(act[:, GROUP + h * DN_D:GROUP + (h + 1) * DN_D])
        v = act[:, 2 * GROUP + h * DN_D:2 * GROUP + (h + 1) * DN_D]
        beta, eg = beta_all[:, h:h + 1], eg_all[:, DN_H + h:DN_H + h + 1]
        kq_scr[h, 0:nb, :], kq_scr[h, nb:2 * nb, :], v_scr[h] = k, q, v
        sc_scr[h] = jnp.where(_iota((nb, LANES), 1) == 0, beta, eg)
        hs.append((q, k, eg))

    def row(b, carry):
        hr = range(DN_H)
        k_col = [_col(kq_scr[h, pl.ds(b, 1), :]) for h in hr]
        s = [s_ref[b, h] for h in hr]
        sc = [sc_scr[h, pl.ds(b, 1), :] for h in hr]
        mv = [_ein(kq_scr[h], s[h]) for h in hr]
        ks = [_row_of(mv[h][0:nb, :], b) for h in hr]
        qs = [_row_of(mv[h][nb:2 * nb, :], b) for h in hr]
        for h in hr:
            beta, eg = sc[h][:, 0:1], sc[h][:, 1:2]
            v_new = beta * (v_scr[h, pl.ds(b, 1), :] - eg * ks[h])
            qs_scr[h, pl.ds(b, 1), :] = qs[h]
            vn_scr[h, pl.ds(b, 1), :] = v_new
            so_ref[b, h] = eg * s[h] + k_col[h] * v_new
        return carry

    lax.fori_loop(0, nb, row, 0)
    for h, (q, k, eg) in enumerate(hs):
        z = cols_ref[:, DN_QKV + h * DN_D:DN_QKV + (h + 1) * DN_D]
        o = eg * qs_scr[h] + jnp.sum(q * k, -1, keepdims=True) * vn_scr[h]
        on = o * lax.rsqrt(jnp.mean(o * o, -1, keepdims=True) + NORM_EPS) * ng_ref[...]
        h_ref[:, h * DN_D:(h + 1) * DN_D] = (on * _silu(z)).astype(h_ref.dtype)


def _gdn_step(cols, buf, cw, alog, dtb, ng, s0, l, s_out):
    bs = cols.shape[0]
    nb = DEC_BLOCK
    call = _step_call(
        _gdn_step_body, l, s_out,
        in_specs=[_row_spec(W_DN), _layer_rows(l, (DN_CONV - 1) * DN_QKV), _layer_spec(l, (8, DN_QKV)),
                  _layer_spec(l, (1, LANES)), _layer_spec(l, (1, LANES)), _layer_spec(l, (1, DN_D)),
                  _layer_state(l, DN_H, DN_D, DN_D)],
        out_specs=[_row_spec(GROUP), _layer_state(l, DN_H, DN_D, DN_D)],
        out_shapes=[jax.ShapeDtypeStruct((bs, GROUP), bf16), jax.ShapeDtypeStruct(s0.shape, f32)],
        state_out_index=1, grid=(bs // nb,),
        scratch_shapes=[pltpu.VMEM((DN_H, 2 * nb, DN_D), f32), pltpu.VMEM((DN_H, nb, DN_D), f32),
                        pltpu.VMEM((DN_H, nb, LANES), f32)] + [pltpu.VMEM((DN_H, nb, DN_D), f32)] * 2,
        compiler_params=_cparams(("parallel",)), name="gdn_step")
    return call(cols, buf, cw, alog, dtb, ng, s0)


def _pad_last(w, width):
    return jnp.pad(w, [(0, 0)] * (w.ndim - 1) + [(0, width - w.shape[-1])])


def _pad_rows(w, start, total):
    return jnp.pad(w, ((0, 0), (start, total - start - w.shape[1]), (0, 0)))


def _lane_rows(v, start):
    return jnp.pad(v, ((0, 0), (start, LANES - start - v.shape[1])))[:, None, :]


def kernel(x_prompt, x_sample, state_mlstm_c, state_mlstm_n, state_mlstm_m, state_gla, state_rwkv, state_rwkv_shift, state_dn, state_dn_conv, w_in, w_out, w_up, w_down, ln1_g, ln1_b, ln2_g, ln2_b, mlstm_gate_b, mlstm_norm_g, gla_gate_up, gla_gate_b, gla_norm_g, rwkv_mu, rwkv_w_up, rwkv_w0, rwkv_a_up, rwkv_a0, rwkv_g_up, rwkv_k_k, rwkv_k_a, rwkv_r_k, rwkv_norm_g, rwkv_norm_b, dn_conv_w, dn_a_log, dn_dt_bias, dn_norm_g):
    bp, t, _ = x_prompt.shape
    bs = x_sample.shape[0]
    xp = x_prompt.reshape(bp * t, D_MODEL)
    xs = x_sample.reshape(bs, D_MODEL)
    xpb, xsb = xp, xs
    tm_p, tm_s, tf = 512, bs, 1024

    o1, o2, o3 = N_ML, N_ML + N_GLA, N_ML + N_GLA + N_RW
    w_secs = [_pad_last(w_in[:, :, 0:o1].astype(bf16), W_ML), _pad_last(w_in[:, :, o1:o2].astype(bf16), W_GLA),
              w_in[:, :, o2:o3].astype(bf16), _pad_last(w_in[:, :, o3:].astype(bf16), W_DN)]
    wo, wu, wd = w_out.astype(bf16), w_up.astype(bf16), w_down.astype(bf16)
    g1, b1, g2, b2 = ln1_g[:, None], ln1_b[:, None], ln2_g[:, None], ln2_b[:, None]
    ml_gb, ml_ng = _lane_rows(mlstm_gate_b, 0), mlstm_norm_g[:, None]
    gla_gup, gla_gb, gla_ng = _pad_rows(gla_gate_up, 0, LANES), gla_gate_b[:, None], gla_norm_g[:, None]
    rw_mu = rwkv_mu[:, None]
    rw_wup, rw_aup, rw_gup = _pad_rows(rwkv_w_up, 0, LANES), _pad_rows(rwkv_a_up, 32, LANES), _pad_rows(rwkv_g_up, 64, LANES)
    rw_vec = jnp.stack([rwkv_w0, rwkv_a0, rwkv_k_k, rwkv_k_a, rwkv_r_k, rwkv_norm_g, rwkv_norm_b,
                        jnp.zeros_like(rwkv_w0)], axis=1)
    dn_cw = _pad_rows(dn_conv_w, 0, 8)
    dn_alog, dn_dtb, dn_ng = _lane_rows(dn_a_log, DN_H), _lane_rows(dn_dt_bias, DN_H), dn_norm_g[:, None]
    st_ml_n = state_mlstm_n.reshape(DEPTH, bs, ML_H * ML_DK)
    st_dn_conv = state_dn_conv.reshape(DEPTH, bs, (DN_CONV - 1) * DN_QKV)

    outs_p, outs_s = [], []
    ml_c_s = gla_s = rw_s = dn_s = None
    for l in range(DEPTH):
        c_ml, c_gla, c_rw, c_dn = [_proj(xpb, w, l, tm_p) for w in w_secs]
        h_ml, mc, mn, mm = _mlstm_prompt(c_ml, ml_gb, ml_ng, l, bp, t)
        h_gla, gs = _gla_prompt(c_gla, gla_gup, gla_gb, gla_ng, l, bp, t)
        h_rw, rs = _rwkv_prompt(c_rw, rw_mu, rw_wup, rw_aup, rw_gup, rw_vec, l, bp, t)
        h_dn, ds = _gdn_prompt(c_dn, dn_cw, dn_alog, dn_dtb, dn_ng, l, bp, t)
        hs_p = [h.reshape(bp * t, GROUP) for h in (h_ml, h_gla, h_rw, h_dn)]
        x1, x1b = _outproj_ln(hs_p, wo, xp, g1, b1, l, tm_p)
        xp, xpb = _ffn_ln(x1, x1b, wu, wd, g2, b2, l, tm_p, tf)
        outs_p.append((mc, mn, mm[:, :, 0], gs, rs, c_rw.reshape(bp, t, W_RW)[:, -1],
                       ds, c_dn.reshape(bp, t, W_DN)[:, t - (DN_CONV - 1):, :DN_QKV]))

        c_ml, c_gla, c_rw, c_dn = [_proj(xsb, w, l, tm_s) for w in w_secs]
        h_ml, ml_c_s, mn, mm = _mlstm_step(c_ml, ml_gb, ml_ng, state_mlstm_c, st_ml_n, state_mlstm_m, l, ml_c_s)
        h_gla, gla_s = _gla_step(c_gla, gla_gup, gla_gb, gla_ng, state_gla, l, gla_s)
        h_rw, rw_s = _rwkv_step(c_rw, state_rwkv_shift, rw_mu, rw_wup, rw_aup, rw_gup, rw_vec, state_rwkv, l, rw_s)
        h_dn, dn_s = _gdn_step(c_dn, st_dn_conv, dn_cw, dn_alog, dn_dtb, dn_ng, state_dn, l, dn_s)
        x1, x1b = _outproj_ln((h_ml, h_gla, h_rw, h_dn), wo, xs, g1, b1, l, tm_s)
        xs, xsb = _ffn_ln(x1, x1b, wu, wd, g2, b2, l, tm_s, tf)
        outs_s.append((mn.reshape(bs, ML_H, ML_DK), mm, c_rw,
                       jnp.concatenate([state_dn_conv[l][:, 1:], c_dn[:, None, :DN_QKV]], axis=1)))

    (mlstm_c_p, mlstm_n_p, mlstm_m_p, gla_p, rwkv_p, rwkv_shift_p, dn_p, dn_conv_p) = [jnp.stack(z) for z in zip(*outs_p)]
    (mlstm_n_s, mlstm_m_s, rwkv_shift_s, dn_conv_s) = [jnp.stack(z) for z in zip(*outs_s)]
    return (xp.reshape(bp, t, D_MODEL), xs.reshape(bs, 1, D_MODEL), mlstm_c_p, ml_c_s, mlstm_n_p, mlstm_n_s,
            mlstm_m_p, mlstm_m_s, gla_p, gla_s, rwkv_p, rw_s, rwkv_shift_p, rwkv_shift_s, dn_p, dn_s,
            dn_conv_p, dn_conv_s)
```

```python
import functools

import jax
import jax.numpy as jnp
from jax import lax
from jax.experimental import pallas as pl
from jax.experimental.pallas import tpu as pltpu

f32 = jnp.float32
bf16 = jnp.bfloat16

D_MODEL = 2048
DEPTH = 4
GROUP = 512
D_FF = 4 * D_MODEL
CHUNK = 64
ML_H, ML_DK, ML_DV = 4, 64, 128
GLA_H, GLA_DK, GLA_DV, GLA_RANK, GLA_TAU = 4, 64, 128, 16, 16.0
RW_H, RW_N = 8, 64
RW_DECAY_SCALE = 0.606531
RW_GN_EPS = 64e-5
DN_H, DN_D, DN_CONV, DN_QKV = 4, 128, 4, 1536
N_ML, N_GLA, N_RW, N_DN = 1544, 1552, 1664, 2056
W_ML, W_GLA, W_RW, W_DN = 1664, 1664, 1664, 2176
ALPHA = (2 * DEPTH) ** 0.25
LN_EPS = 1e-5
NORM_EPS = 1e-6
LANES = 128
DEC_BLOCK = 16
VMEM_LIMIT = 56 * 1024 * 1024


def _ein(a, b):
    return jnp.dot(a.astype(bf16), b.astype(bf16), preferred_element_type=f32)


def _ein_nt(a, b):
    return lax.dot_general(a.astype(bf16), b.astype(bf16), (((1,), (1,)), ((), ())), preferred_element_type=f32)


def _ein_tn(a, b):
    return lax.dot_general(a.astype(bf16), b.astype(bf16), (((0,), (0,)), ((), ())), preferred_element_type=f32)


def _split3(x):
    hi = x.astype(bf16)
    r1 = x - hi.astype(f32)
    mid = r1.astype(bf16)
    return hi, mid, (r1 - mid.astype(f32)).astype(bf16)


def _select_dot(sel, x):
    s = sel.astype(bf16)
    hi, mid, lo = _split3(x)
    d = functools.partial(jnp.dot, preferred_element_type=f32)
    return d(s, hi) + d(s, mid) + d(s, lo)


def _select_dot_right(x, sel):
    s = sel.astype(bf16)
    hi, mid, lo = _split3(x)
    d = functools.partial(jnp.dot, preferred_element_type=f32)
    return d(hi, s) + d(mid, s) + d(lo, s)


def _select_dot_nt(sel, x):
    s = sel.astype(bf16)
    hi, mid, lo = _split3(x)
    d = functools.partial(lax.dot_general, dimension_numbers=(((1,), (1,)), ((), ())), preferred_element_type=f32)
    return d(s, hi) + d(s, mid) + d(s, lo)


def _log_sigmoid(x):
    return jnp.minimum(x, 0.0) - jnp.log1p(jnp.exp(-jnp.abs(x)))


def _softplus(x):
    return jnp.maximum(x, 0.0) + jnp.log1p(jnp.exp(-jnp.abs(x)))


def _silu(x):
    return x * jax.nn.sigmoid(x)


def _l2norm(x):
    return x * lax.rsqrt(jnp.sum(x * x, -1, keepdims=True) + NORM_EPS)


def _iota(shape, axis):
    return lax.broadcasted_iota(jnp.int32, shape, axis)


def _col(row):
    n = row.shape[1]
    eye = _iota((n, n), 0) == _iota((n, n), 1)
    return jnp.sum(jnp.where(eye, row, 0.0), axis=1, keepdims=True)


def _row_of(x, b):
    return jnp.sum(jnp.where(_iota(x.shape, 0) == b, x, 0.0), axis=0, keepdims=True)


def _cols_of(row, n, width=None):
    width = n if width is None else width
    tiles = []
    for g in range(row.shape[1] // LANES):
        t = jnp.broadcast_to(row[:, g * LANES:(g + 1) * LANES], (LANES, LANES)).T
        tiles += [t[j * n:(j + 1) * n, 0:width] for j in range(LANES // n)]
    return tiles


def _tri_masks(n):
    r, c = _iota((n, n), 0), _iota((n, n), 1)
    return c <= r, c < r


def _split2(a):
    hi = a.astype(bf16)
    return hi, (a - hi.astype(f32)).astype(bf16)


def _dot3(a, b):
    ah, al = _split2(a)
    bh, bl = _split2(b)
    d = functools.partial(jnp.dot, preferred_element_type=f32)
    return d(ah, bh) + d(ah, bl) + d(al, bh)


def _unit_lower_inverses(a_list):
    n = a_list[0].shape[0]
    eye = jnp.where(_iota((n, n), 0) == _iota((n, n), 1), 1.0, 0.0)
    xs = [-a for a in a_list]
    ps = [eye + x for x in xs]
    for stage in range(max(n.bit_length() - 2, 0)):
        mm = _dot3 if stage == 0 else _ein
        xs = [mm(x, x) for x in xs]
        ps = [p + mm(p, x) for p, x in zip(ps, xs)]
    return ps


def _rows_of(y):
    e8 = jnp.where(_iota((8, LANES), 0) == _iota((8, LANES), 1), 1.0, 0.0)
    return _select_dot_nt(e8, y)


def _cparams(sem):
    return pltpu.CompilerParams(dimension_semantics=sem, vmem_limit_bytes=VMEM_LIMIT)


def _layer_spec(l, shape, index=None):
    nd = len(shape)
    if index is None:
        return pl.BlockSpec((None,) + tuple(shape), lambda *g: (l,) + (0,) * nd)
    return pl.BlockSpec((None,) + tuple(shape), lambda *g: (l,) + tuple(index(*g)))


def _proj_body(x_ref, w_ref, o_ref):
    o_ref[...] = jnp.dot(x_ref[...].astype(bf16), w_ref[...], preferred_element_type=f32)


def _proj(xb, w, l, tm):
    m, k = xb.shape
    n = w.shape[2]
    return pl.pallas_call(
        _proj_body, grid=(m // tm,),
        in_specs=[pl.BlockSpec((tm, k), lambda i: (i, 0)), _layer_spec(l, (k, n))],
        out_specs=pl.BlockSpec((tm, n), lambda i: (i, 0)),
        out_shape=jax.ShapeDtypeStruct((m, n), f32),
        compiler_params=_cparams(("parallel",)), name="proj")(xb, w)


def _layernorm(y, g, b):
    mu = jnp.mean(y, -1, keepdims=True)
    yc = y - mu
    var = jnp.mean(yc * yc, -1, keepdims=True)
    return yc * lax.rsqrt(var + LN_EPS) * g + b


def _outproj_ln_body(h0, h1, h2, h3, w_ref, x_ref, g_ref, b_ref, o_ref, ob_ref):
    acc = ALPHA * x_ref[...]
    for j, h in enumerate((h0, h1, h2, h3)):
        acc = acc + jnp.dot(h[...], w_ref[j * GROUP:(j + 1) * GROUP, :], preferred_element_type=f32)
    y = _layernorm(acc, g_ref[...], b_ref[...])
    o_ref[...] = y
    ob_ref[...] = y.astype(bf16)


def _outproj_ln(hs, w, x, g, b, l, tm):
    m = x.shape[0]
    hspec = pl.BlockSpec((tm, GROUP), lambda i: (i, 0))
    xspec = pl.BlockSpec((tm, D_MODEL), lambda i: (i, 0))
    vspec = _layer_spec(l, (1, D_MODEL))
    return pl.pallas_call(
        _outproj_ln_body, grid=(m // tm,),
        in_specs=[hspec] * 4 + [_layer_spec(l, (D_MODEL, D_MODEL)), xspec, vspec, vspec],
        out_specs=[xspec, xspec],
        out_shape=[jax.ShapeDtypeStruct((m, D_MODEL), f32), jax.ShapeDtypeStruct((m, D_MODEL), bf16)],
        compiler_params=_cparams(("parallel",)), name="outproj_ln")(*hs, w, x, g, b)


def _ffn_ln_body(x_ref, xb_ref, wu_ref, wd_ref, g_ref, b_ref, o_ref, ob_ref, acc_ref):
    f = pl.program_id(1)

    @pl.when(f == 0)
    def _():
        acc_ref[...] = ALPHA * x_ref[...]

    h = jnp.maximum(jnp.dot(xb_ref[...], wu_ref[...], preferred_element_type=f32), 0.0)
    acc_ref[...] += jnp.dot((h * h).astype(bf16), wd_ref[...], preferred_element_type=f32)

    @pl.when(f == pl.num_programs(1) - 1)
    def _():
        y = _layernorm(acc_ref[...], g_ref[...], b_ref[...])
        o_ref[...] = y
        ob_ref[...] = y.astype(bf16)


def _ffn_ln(x, xb, wu, wd, g, b, l, tm, tf):
    m = x.shape[0]
    xspec = pl.BlockSpec((tm, D_MODEL), lambda i, f: (i, 0))
    vspec = _layer_spec(l, (1, D_MODEL))
    return pl.pallas_call(
        _ffn_ln_body, grid=(m // tm, D_FF // tf),
        in_specs=[xspec, xspec, _layer_spec(l, (D_MODEL, tf), lambda i, f: (0, f)),
                  _layer_spec(l, (tf, D_MODEL), lambda i, f: (f, 0)), vspec, vspec],
        out_specs=[xspec, xspec],
        out_shape=[jax.ShapeDtypeStruct((m, D_MODEL), f32), jax.ShapeDtypeStruct((m, D_MODEL), bf16)],
        scratch_shapes=[pltpu.VMEM((tm, D_MODEL), f32)],
        compiler_params=_cparams(("parallel", "arbitrary")), name="ffn_ln")(x, xb, wu, wd, g, b)


def _tril_blocks(nb, L):
    r, c = _iota((nb * L, nb * L), 0), _iota((nb * L, nb * L), 1)
    return jnp.where((c <= r) & (c >= r - (r & (L - 1))), 1.0, 0.0)


def _units(nb, heads):
    return [(b, h) for b in range(nb) for h in range(heads)]


def _mlstm_prompt_body(cols_ref, gb_ref, ng_ref, h_ref, c_ref, n_ref, m_ref):
    @pl.when(pl.program_id(0) == 0)
    def _():
        c_ref[...] = jnp.zeros_like(c_ref)
        n_ref[...] = jnp.zeros_like(n_ref)
        m_ref[...] = jnp.zeros_like(m_ref)

    nb, L = cols_ref.shape[0], CHUNK
    causal, _ = _tri_masks(L)
    g = cols_ref[:, :, 1536:1664].reshape(nb * L, LANES) + gb_ref[...]
    bc = _select_dot(_tril_blocks(nb, L), _log_sigmoid(g))
    y_all = jnp.where(_iota((nb * L, LANES), 1) < ML_H, g, bc)
    y = [y_all[b * L:(b + 1) * L, :] for b in range(nb)]
    yt = [_rows_of(y[b]) for b in range(nb)]
    us = _units(nb, ML_H)
    ix = range(len(us))
    q = [cols_ref[b, :, h * ML_DK:(h + 1) * ML_DK] for b, h in us]
    k = [cols_ref[b, :, 256 + h * ML_DK:256 + (h + 1) * ML_DK] * ML_DK ** -0.5 for b, h in us]
    v = [cols_ref[b, :, 512 + h * ML_DV:512 + (h + 1) * ML_DV] for b, h in us]
    c = [c_ref[b, h] for b, h in us]
    n = [n_ref[b, h:h + 1, :] for b, h in us]
    m = [m_ref[b, h:h + 1, :] for b, h in us]
    qk = [_ein_nt(q[i], k[i]) for i in ix]
    qc = [_ein(q[i], c[i]) for i in ix]
    qn = [_ein_nt(q[i], jnp.broadcast_to(n[i], (LANES, ML_DK))) for i in ix]
    spread_shape = (LANES, 2 * ML_H * LANES)
    spread = jnp.where(_iota(spread_shape, 0) == jnp.right_shift(_iota(spread_shape, 1), LANES.bit_length() - 1), 1.0, 0.0)
    rep = [_select_dot_right(y[b], spread) for b in range(nb)]
    i_rep = [rep[b][:, h * LANES:(h + 1) * LANES] for b, h in us]
    b_rep = [rep[b][:, (ML_H + h) * LANES:(ML_H + h + 1) * LANES] for b, h in us]
    d = [jnp.where(causal, b_rep[i][:, :L] - yt[b][ML_H + h:ML_H + h + 1, :] + yt[b][h:h + 1, :], -jnp.inf)
         for i, (b, h) in enumerate(us)]
    inter = [b_rep[i] + m[i] for i in ix]
    m_t = [jnp.maximum(inter[i], jnp.max(d[i], axis=-1, keepdims=True)) for i in ix]
    w_inter = [jnp.exp(inter[i] - m_t[i]) for i in ix]
    s = [qk[i] * jnp.exp(d[i] - m_t[i][:, :L]) for i in ix]
    m_new = [m_t[i][L - 1:L, :] for i in ix]
    b_last = [b_rep[i][L - 1:L, :] for i in ix]
    kw = [k[i] * jnp.exp(b_last[i] - b_rep[i] + i_rep[i] - m_new[i])[:, :ML_DK] for i in ix]
    sv = [_ein(s[i], v[i]) for i in ix]
    kv = [_ein_tn(kw[i], v[i]) for i in ix]
    s_sum = [jnp.sum(s[i], -1, keepdims=True) for i in ix]
    f_state = [jnp.exp(b_last[i] + m[i] - m_new[i]) for i in ix]
    c_new = [f_state[i] * c[i] + kv[i] for i in ix]
    n_new = [f_state[i][:, :ML_DK] * n[i] + jnp.sum(kw[i], axis=0, keepdims=True) for i in ix]
    for i, (b, h) in enumerate(us):
        c_ref[b, h] = c_new[i]
        n_ref[b, h:h + 1, :] = n_new[i]
        m_ref[b, h:h + 1, :] = m_new[i]
    hh = [(w_inter[i] * qc[i] + sv[i]) / jnp.maximum(jnp.abs(w_inter[i] * qn[i] + s_sum[i]), jnp.exp(-m_t[i])) for i in ix]
    ms = [jnp.mean(hh[i] * hh[i], -1, keepdims=True) for i in ix]
    for i, (b, h) in enumerate(us):
        og = cols_ref[b, :, 1024 + h * ML_DV:1024 + (h + 1) * ML_DV]
        hn = hh[i] * lax.rsqrt(ms[i] + NORM_EPS) * ng_ref[:, h * ML_DV:(h + 1) * ML_DV]
        h_ref[b, :, h * ML_DV:(h + 1) * ML_DV] = (hn * jax.nn.sigmoid(og)).astype(h_ref.dtype)


def _chunk_grid_specs(bp, t, width):
    cols = pl.BlockSpec((bp, CHUNK, width), lambda c: (0, c, 0))
    out = pl.BlockSpec((bp, CHUNK, GROUP), lambda c: (0, c, 0))
    return (t // CHUNK,), cols, out


def _whole(shape):
    return pl.BlockSpec(shape, lambda c: (0,) * len(shape))


def _mlstm_prompt(cols, gb, ng, l, bp, t):
    grid, cspec, ospec = _chunk_grid_specs(bp, t, W_ML)
    return pl.pallas_call(
        _mlstm_prompt_body, grid=grid,
        in_specs=[cspec, _layer_spec(l, (1, LANES)), _layer_spec(l, (1, GROUP))],
        out_specs=[ospec, _whole((bp, ML_H, ML_DK, ML_DV)), _whole((bp, ML_H, ML_DK)), _whole((bp, ML_H, LANES))],
        out_shape=[jax.ShapeDtypeStruct((bp, t, GROUP), bf16), jax.ShapeDtypeStruct((bp, ML_H, ML_DK, ML_DV), f32),
                   jax.ShapeDtypeStruct((bp, ML_H, ML_DK), f32), jax.ShapeDtypeStruct((bp, ML_H, LANES), f32)],
        compiler_params=_cparams(("arbitrary",)), name="mlstm_prompt")(cols.reshape(bp, t, W_ML), gb, ng)


def _gla_prompt_body(cols_ref, gup_ref, gb_ref, ng_ref, h_ref, s_ref):
    @pl.when(pl.program_id(0) == 0)
    def _():
        s_ref[...] = jnp.zeros_like(s_ref)

    nb, L = cols_ref.shape[0], CHUNK
    causal, _ = _tri_masks(L)
    lr = cols_ref[:, :, 1536:1664].reshape(nb * L, LANES)
    la = _log_sigmoid(_ein(lr, gup_ref[...]) + gb_ref[...]) / GLA_TAU
    bc_all = _select_dot(_tril_blocks(nb, L), la)
    bc = [bc_all[b * L:(b + 1) * L, :] for b in range(nb)]
    mid = [bc[b][L // 2:L // 2 + 1, :] for b in range(nb)]
    last = [bc[b][L - 1:L, :] for b in range(nb)]
    e_q_mid = [jnp.exp(bc[b] - mid[b]) for b in range(nb)]
    e_k_mid = [jnp.exp(mid[b] - bc[b]) for b in range(nb)]
    e_q = [jnp.exp(bc[b]) for b in range(nb)]
    e_k_last = [jnp.exp(last[b] - bc[b]) for b in range(nb)]
    e_last = [jnp.exp(last[b]) for b in range(nb)]
    us = _units(nb, GLA_H)
    ix = range(len(us))
    sl = [slice(h * GLA_DK, (h + 1) * GLA_DK) for _, h in us]
    q = [cols_ref[b, :, h * GLA_DK:(h + 1) * GLA_DK] * GLA_DK ** -0.5 for b, h in us]
    k = [cols_ref[b, :, 256 + h * GLA_DK:256 + (h + 1) * GLA_DK] for b, h in us]
    v = [cols_ref[b, :, 512 + h * GLA_DV:512 + (h + 1) * GLA_DV] for b, h in us]
    s = [s_ref[b, h] for b, h in us]
    a = [jnp.where(causal, _ein_nt(q[i] * e_q_mid[b][:, sl[i]], k[i] * e_k_mid[b][:, sl[i]]), 0.0)
         for i, (b, _) in enumerate(us)]
    o0 = [_ein(q[i] * e_q[b][:, sl[i]], s[i]) for i, (b, _) in enumerate(us)]
    kv = [_ein_tn(k[i] * e_k_last[b][:, sl[i]], v[i]) for i, (b, _) in enumerate(us)]
    o = [o0[i] + _ein(a[i], v[i]) for i in ix]
    for i, (b, h) in enumerate(us):
        s_ref[b, h] = _col(e_last[b][:, sl[i]]) * s[i] + kv[i]
    ms = [jnp.mean(o[i] * o[i], -1, keepdims=True) for i in ix]
    for i, (b, h) in enumerate(us):
        og = cols_ref[b, :, 1024 + h * GLA_DV:1024 + (h + 1) * GLA_DV]
        on = o[i] * lax.rsqrt(ms[i] + NORM_EPS) * ng_ref[:, h * GLA_DV:(h + 1) * GLA_DV]
        h_ref[b, :, h * GLA_DV:(h + 1) * GLA_DV] = (on * _silu(og)).astype(h_ref.dtype)


def _gla_prompt(cols, gup, gb, ng, l, bp, t):
    grid, cspec, ospec = _chunk_grid_specs(bp, t, W_GLA)
    return pl.pallas_call(
        _gla_prompt_body, grid=grid,
        in_specs=[cspec, _layer_spec(l, (LANES, GLA_H * GLA_DK)), _layer_spec(l, (1, GLA_H * GLA_DK)),
                  _layer_spec(l, (1, GROUP))],
        out_specs=[ospec, _whole((bp, GLA_H, GLA_DK, GLA_DV))],
        out_shape=[jax.ShapeDtypeStruct((bp, t, GROUP), bf16), jax.ShapeDtypeStruct((bp, GLA_H, GLA_DK, GLA_DV), f32)],
        compiler_params=_cparams(("arbitrary",)), name="gla_prompt")(cols.reshape(bp, t, W_GLA), gup, gb, ng)


def _rwkv_pre(x, prev, mu, wup, aup, gup, vec):
    xs = x + (prev - x) * mu
    r, k, v, lb = xs[:, 0:512], xs[:, 512:1024], xs[:, 1024:1536], xs[:, 1536:1664]
    w0, a0, k_k, k_a = vec[0:1, :], vec[1:2, :], vec[2:3, :], vec[3:4, :]
    lw = -RW_DECAY_SCALE * jax.nn.sigmoid(w0 + _ein(jnp.tanh(lb), wup))
    a = jax.nn.sigmoid(a0 + _ein(lb, aup))
    g = _ein(jax.nn.sigmoid(lb), gup)
    kk_raw = k * k_k
    k2 = k * (1.0 + (a - 1.0) * k_a)
    return r, k2, v, lw, a, g, kk_raw


def _rwkv_post(o, r, k2, v, g, vec):
    hs = range(RW_H)
    sl = [slice(h * RW_N, (h + 1) * RW_N) for h in hs]
    mu_o = [jnp.mean(o[h], -1, keepdims=True) for h in hs]
    bonus_w = [jnp.sum(r[:, sl[h]] * k2[:, sl[h]] * vec[4:5, sl[h]], -1, keepdims=True) for h in hs]
    oc = [o[h] - mu_o[h] for h in hs]
    var_o = [jnp.mean(oc[h] * oc[h], -1, keepdims=True) for h in hs]
    outs = [(oc[h] * lax.rsqrt(var_o[h] + RW_GN_EPS) * vec[5:6, sl[h]] + vec[6:7, sl[h]]
             + bonus_w[h] * v[:, sl[h]]) * g[:, sl[h]] for h in hs]
    return jnp.concatenate(outs, axis=-1)


def _rwkv_prompt_body(cols_ref, mu_ref, wup_ref, aup_ref, gup_ref, vec_ref, h_ref, s_ref, prev_scr):
    @pl.when(pl.program_id(0) == 0)
    def _():
        s_ref[...] = jnp.zeros_like(s_ref)
        prev_scr[...] = jnp.zeros_like(prev_scr)

    nb, L, N = cols_ref.shape[0], CHUNK, RW_N
    first_row = _iota((L, W_RW), 0) == 0
    xb = [cols_ref[b] for b in range(nb)]
    prevs = [jnp.where(first_row, prev_scr[b, 0:1, :], pltpu.roll(xb[b], 1, 0)) for b in range(nb)]
    for b in range(nb):
        prev_scr[b, 0:1, :] = xb[b][L - 1:L, :]
    vec = vec_ref[...]
    r, k2, v, lw, a, g, kk_raw = _rwkv_pre(jnp.concatenate(xb, axis=0), jnp.concatenate(prevs, axis=0), mu_ref[...],
                                           wup_ref[...], aup_ref[...], gup_ref[...], vec)
    cum = _select_dot(_tril_blocks(nb, L), lw)
    cum_prev = cum - lw
    rep = lambda row_of: jnp.concatenate(
        [jnp.broadcast_to(cum[b * L + row_of:b * L + row_of + 1, :], (L, RW_H * N)) for b in range(nb)], axis=0)
    mid, last = rep(L // 2), rep(L - 1)
    e_prev_mid, e_mid_cum, e_cum_mid = jnp.exp(cum_prev - mid), jnp.exp(mid - cum), jnp.exp(cum - mid)
    e_prev, e_cum, e_last_cum, e_last = jnp.exp(cum_prev), jnp.exp(cum), jnp.exp(last - cum), jnp.exp(last)
    row, col = _iota((2 * L, 2 * L), 0), _iota((2 * L, 2 * L), 1)
    tq, sq = row % L, col % L
    quad = sq < tq + jnp.where(row < L, 0, 1)
    left = col[:, :] < L
    hs = range(RW_H)
    sl = [slice(h * N, (h + 1) * N) for h in hs]
    kk_h = [_l2norm(kk_raw[:, sl[h]]) for h in hs]
    b_h = [kk_h[h] * a[:, sl[h]] for h in hs]
    f_lhs = [(kk_h[h] * e_prev_mid[:, sl[h]], r[:, sl[h]] * e_cum_mid[:, sl[h]]) for h in hs]
    f_rhs = [(b_h[h] * e_mid_cum[:, sl[h]], k2[:, sl[h]] * e_mid_cum[:, sl[h]]) for h in hs]
    f_x0 = [(kk_h[h] * e_prev[:, sl[h]], r[:, sl[h]] * e_cum[:, sl[h]]) for h in hs]
    f_dec = [(b_h[h] * e_last_cum[:, sl[h]], k2[:, sl[h]] * e_last_cum[:, sl[h]]) for h in hs]
    us = _units(nb, RW_H)
    ix = range(len(us))
    rows = lambda z, b: z[b * L:(b + 1) * L, :]
    pair = lambda f, b, h: jnp.concatenate([rows(f[h][0], b), rows(f[h][1], b)], axis=0)
    vh = [rows(v[:, sl[h]], b) for b, h in us]
    s = [s_ref[b, h] for b, h in us]
    a_all = [jnp.where(quad, _ein_nt(pair(f_lhs, b, h), pair(f_rhs, b, h)), 0.0) for b, h in us]
    t_inv = _unit_lower_inverses([a_all[i][0:L, 0:L] for i in ix])
    x0 = [pair(f_x0, b, h) for b, h in us]
    m2 = [jnp.where(left, jnp.concatenate([x0[i], x0[i]], axis=1), a_all[i]) for i in ix]
    y = [_ein(m2[i], jnp.concatenate([s[i], vh[i]], axis=0)) for i in ix]
    u = [-_dot3(t_inv[i], y[i][0:L, :]) for i in ix]
    o = [y[i][L:2 * L, :] + _ein(a_all[i][L:2 * L, 0:L], u[i]) for i in ix]
    for i, (b, h) in enumerate(us):
        s_ref[b, h] = (_col(e_last[b * L:b * L + 1, sl[h]]) * s[i]
                       + _ein_tn(pair(f_dec, b, h), jnp.concatenate([u[i], vh[i]], axis=0)))
    o_heads = [jnp.concatenate([o[b * RW_H + h] for b in range(nb)], axis=0) for h in hs]
    h_ref[...] = _rwkv_post(o_heads, r, k2, v, g, vec).reshape(nb, L, GROUP).astype(h_ref.dtype)


def _rwkv_prompt(cols, mu, wup, aup, gup, vec, l, bp, t):
    grid, cspec, ospec = _chunk_grid_specs(bp, t, W_RW)
    return pl.pallas_call(
        _rwkv_prompt_body, grid=grid,
        in_specs=[cspec, _layer_spec(l, (1, W_RW)), _layer_spec(l, (LANES, GROUP)), _layer_spec(l, (LANES, GROUP)),
                  _layer_spec(l, (LANES, GROUP)), _layer_spec(l, (8, GROUP))],
        out_specs=[ospec, _whole((bp, RW_H, RW_N, RW_N))],
        out_shape=[jax.ShapeDtypeStruct((bp, t, GROUP), bf16), jax.ShapeDtypeStruct((bp, RW_H, RW_N, RW_N), f32)],
        scratch_shapes=[pltpu.VMEM((bp, 8, W_RW), f32)],
        compiler_params=_cparams(("arbitrary",)), name="rwkv_prompt")(cols.reshape(bp, t, W_RW), mu, wup, aup, gup, vec)


def _gdn_gates(gb, alog, dtb):
    return jax.nn.sigmoid(gb), -jnp.exp(alog) * _softplus(gb + dtb)


def _gdn_prompt_body(cols_ref, cw_ref, alog_ref, dtb_ref, ng_ref, h_ref, s_ref, xp_scr):
    @pl.when(pl.program_id(0) == 0)
    def _():
        s_ref[...] = jnp.zeros_like(s_ref)
        xp_scr[...] = jnp.zeros_like(xp_scr)

    nb, L = cols_ref.shape[0], CHUNK
    causal, strict = _tri_masks(L)
    act = []
    for b in range(nb):
        raw = cols_ref[b, :, 0:DN_QKV]
        ext = jnp.concatenate([xp_scr[b], raw], axis=0)
        conv = raw * cw_ref[DN_CONV - 1:DN_CONV, :]
        for j in range(1, DN_CONV):
            conv = conv + pltpu.roll(ext, j, 0)[8:8 + L, :] * cw_ref[DN_CONV - 1 - j:DN_CONV - j, :]
        xp_scr[b] = raw[L - 8:L, :]
        act.append(_silu(conv))
    beta_all, g_all = _gdn_gates(cols_ref[:, :, 2048:2176].reshape(nb * L, LANES), alog_ref[...], dtb_ref[...])
    gc_all = _select_dot(_tril_blocks(nb, L), g_all)
    gc = [gc_all[b * L:(b + 1) * L, :] for b in range(nb)]
    gt = [_rows_of(gc[b]) for b in range(nb)]
    us = _units(nb, DN_H)
    ix = range(len(us))
    q = [_l2norm(act[b][:, h * DN_D:(h + 1) * DN_D]) * DN_D ** -0.5 for b, h in us]
    k = [_l2norm(act[b][:, GROUP + h * DN_D:GROUP + (h + 1) * DN_D]) for b, h in us]
    v = [act[b][:, 2 * GROUP + h * DN_D:2 * GROUP + (h + 1) * DN_D] for b, h in us]
    beta = [beta_all[b * L:(b + 1) * L, h:h + 1] for b, h in us]
    g_col = [gc[b][:, DN_H + h:DN_H + h + 1] for b, h in us]
    decay = [jnp.exp(jnp.where(causal, g_col[i] - gt[b][DN_H + h:DN_H + h + 1, :], -jnp.inf))
             for i, (b, h) in enumerate(us)]
    kb = [k[i] * beta[i] for i in ix]
    eg = [jnp.exp(g_col[i]) for i in ix]
    s = [s_ref[b, h] for b, h in us]
    kq = [_ein_nt(jnp.concatenate([kb[i], q[i]], axis=0), k[i]) for i in ix]
    t_inv = _unit_lower_inverses([jnp.where(strict, kq[i][0:L, :] * decay[i], 0.0) for i in ix])
    qk = [kq[i][L:2 * L, :] * decay[i] for i in ix]
    o0 = [_ein(q[i] * eg[i], s[i]) for i in ix]
    sol = [_dot3(t_inv[i], jnp.concatenate([v[i] * beta[i], kb[i] * eg[i]], axis=-1)) for i in ix]
    v_new = [sol[i][:, :DN_D] - _ein(sol[i][:, DN_D:], s[i]) for i in ix]
    o = [o0[i] + _ein(qk[i], v_new[i]) for i in ix]
    for i, (b, h) in enumerate(us):
        g_last = g_col[i][L - 1:L, :]
        s_ref[b, h] = jnp.exp(g_last) * s[i] + _ein_tn(k[i] * jnp.exp(g_last - g_col[i]), v_new[i])
    ms = [jnp.mean(o[i] * o[i], -1, keepdims=True) for i in ix]
    for i, (b, h) in enumerate(us):
        z = cols_ref[b, :, DN_QKV + h * DN_D:DN_QKV + (h + 1) * DN_D]
        on = o[i] * lax.rsqrt(ms[i] + NORM_EPS) * ng_ref[...]
        h_ref[b, :, h * DN_D:(h + 1) * DN_D] = (on * _silu(z)).astype(h_ref.dtype)


def _gdn_prompt(cols, cw, alog, dtb, ng, l, bp, t):
    grid, cspec, ospec = _chunk_grid_specs(bp, t, W_DN)
    return pl.pallas_call(
        _gdn_prompt_body, grid=grid,
        in_specs=[cspec, _layer_spec(l, (8, DN_QKV)), _layer_spec(l, (1, LANES)), _layer_spec(l, (1, LANES)),
                  _layer_spec(l, (1, DN_D))],
        out_specs=[ospec, _whole((bp, DN_H, DN_D, DN_D))],
        out_shape=[jax.ShapeDtypeStruct((bp, t, GROUP), bf16), jax.ShapeDtypeStruct((bp, DN_H, DN_D, DN_D), f32)],
        scratch_shapes=[pltpu.VMEM((bp, 8, DN_QKV), f32)],
        compiler_params=_cparams(("arbitrary",)), name="gdn_prompt")(cols.reshape(bp, t, W_DN), cw, alog, dtb, ng)


def _mlstm_step_body(cols_ref, gb_ref, ng_ref, c_ref, n_ref, m_ref, h_ref, co_ref, no_ref, mo_ref,
                     q_scr, kw_scr, v_scr, f_scr, qc_scr):
    nb = DEC_BLOCK
    g = cols_ref[:, 1536:1664] + gb_ref[...]
    lf = _log_sigmoid(g)
    hs = []
    for h in range(ML_H):
        q = cols_ref[:, h * ML_DK:(h + 1) * ML_DK]
        k = cols_ref[:, 256 + h * ML_DK:256 + (h + 1) * ML_DK] * ML_DK ** -0.5
        v = cols_ref[:, 512 + h * ML_DV:512 + (h + 1) * ML_DV]
        i_pre, f_log = g[:, h:h + 1], lf[:, ML_H + h:ML_H + h + 1]
        m = m_ref[:, h:h + 1]
        inter = f_log + m
        m_t = jnp.maximum(inter, i_pre)
        w_inter = jnp.exp(inter - m_t)
        kw = k * jnp.exp(i_pre - m_t)
        q_scr[h], kw_scr[h], v_scr[h] = q, kw, v
        f_scr[h] = jnp.broadcast_to(w_inter, (nb, LANES))
        hs.append((q, k, v, i_pre, m_t, w_inter, kw))

    def row(b, carry):
        hr = range(ML_H)
        kw_col = [_col(kw_scr[h, pl.ds(b, 1), :]) for h in hr]
        c = [c_ref[b, h] for h in hr]
        qc = [_row_of(_ein(q_scr[h], c[h]), b) for h in hr]
        for h in hr:
            qc_scr[h, pl.ds(b, 1), :] = qc[h]
            co_ref[b, h] = f_scr[h, pl.ds(b, 1), :] * c[h] + kw_col[h] * v_scr[h, pl.ds(b, 1), :]
        return carry

    lax.fori_loop(0, nb, row, 0)
    for h, (q, k, v, i_pre, m_t, w_inter, kw) in enumerate(hs):
        og = cols_ref[:, 1024 + h * ML_DV:1024 + (h + 1) * ML_DV]
        n = n_ref[:, h * ML_DK:(h + 1) * ML_DK]
        s = jnp.sum(q * k, -1, keepdims=True) * jnp.exp(i_pre - m_t)
        num = w_inter * qc_scr[h] + s * v
        den = w_inter * jnp.sum(q * n, -1, keepdims=True) + s
        hh = num / jnp.maximum(jnp.abs(den), jnp.exp(-m_t))
        no_ref[:, h * ML_DK:(h + 1) * ML_DK] = w_inter * n + kw
        mo_ref[:, h:h + 1] = m_t
        hn = hh * lax.rsqrt(jnp.mean(hh * hh, -1, keepdims=True) + NORM_EPS) * ng_ref[:, h * ML_DV:(h + 1) * ML_DV]
        h_ref[:, h * ML_DV:(h + 1) * ML_DV] = (hn * jax.nn.sigmoid(og)).astype(h_ref.dtype)


def _row_spec(width):
    return pl.BlockSpec((DEC_BLOCK, width), lambda i: (i, 0))


def _layer_rows(l, width):
    return _layer_spec(l, (DEC_BLOCK, width), lambda i: (i, 0))


def _layer_state(l, h, dk, dv):
    return _layer_spec(l, (DEC_BLOCK, h, dk, dv), lambda i: (i, 0, 0, 0))


def _ignore_first_ref(body):
    def with_handed_on_buffer(_stacked_out_so_far, *refs):
        body(*refs)
    return with_handed_on_buffer


def _step_call(body, l, stacked_out, in_specs, out_specs, out_shapes, state_out_index, **kw):
    if stacked_out is None:
        return lambda *args: pl.pallas_call(body, in_specs=in_specs, out_specs=out_specs, out_shape=out_shapes, **kw)(*args)
    call = pl.pallas_call(_ignore_first_ref(body), in_specs=[pl.BlockSpec(memory_space=pl.ANY)] + in_specs,
                          out_specs=out_specs, out_shape=out_shapes, input_output_aliases={0: state_out_index}, **kw)
    return lambda *args: call(stacked_out, *args)


def _mlstm_step(cols, gb, ng, c0, n0, m0, l, c_out):
    bs = cols.shape[0]
    nb = DEC_BLOCK
    call = _step_call(
        _mlstm_step_body, l, c_out,
        in_specs=[_row_spec(W_ML), _layer_spec(l, (1, LANES)), _layer_spec(l, (1, GROUP)),
                  _layer_state(l, ML_H, ML_DK, ML_DV), _layer_rows(l, ML_H * ML_DK), _layer_rows(l, ML_H)],
        out_specs=[_row_spec(GROUP), _layer_state(l, ML_H, ML_DK, ML_DV), _row_spec(ML_H * ML_DK), _row_spec(ML_H)],
        out_shapes=[jax.ShapeDtypeStruct((bs, GROUP), bf16), jax.ShapeDtypeStruct(c0.shape, f32),
                    jax.ShapeDtypeStruct((bs, ML_H * ML_DK), f32), jax.ShapeDtypeStruct((bs, ML_H), f32)],
        state_out_index=1, grid=(bs // nb,),
        scratch_shapes=[pltpu.VMEM((ML_H, nb, ML_DK), f32), pltpu.VMEM((ML_H, nb, ML_DK), f32),
                        pltpu.VMEM((ML_H, nb, ML_DV), f32), pltpu.VMEM((ML_H, nb, LANES), f32),
                        pltpu.VMEM((ML_H, nb, ML_DV), f32)],
        compiler_params=_cparams(("parallel",)), name="mlstm_step")
    return call(cols, gb, ng, c0, n0, m0)


def _gla_step_body(cols_ref, gup_ref, gb_ref, ng_ref, s_ref, h_ref, so_ref, qe_scr, ea_scr, k_scr, v_scr, qs_scr):
    nb = DEC_BLOCK
    la = _log_sigmoid(_ein(cols_ref[:, 1536:1664], gup_ref[...]) + gb_ref[...]) / GLA_TAU
    ea = jnp.exp(la)
    hs = []
    for h in range(GLA_H):
        q = cols_ref[:, h * GLA_DK:(h + 1) * GLA_DK] * GLA_DK ** -0.5
        k = cols_ref[:, 256 + h * GLA_DK:256 + (h + 1) * GLA_DK]
        v = cols_ref[:, 512 + h * GLA_DV:512 + (h + 1) * GLA_DV]
        ea_h = ea[:, h * GLA_DK:(h + 1) * GLA_DK]
        qe_scr[h], ea_scr[h], k_scr[h], v_scr[h] = q * ea_h, ea_h, k, v
        hs.append((q, k, v))

    def row(b, carry):
        hr = range(GLA_H)
        ea_col = [_col(ea_scr[h, pl.ds(b, 1), :]) for h in hr]
        k_col = [_col(k_scr[h, pl.ds(b, 1), :]) for h in hr]
        s = [s_ref[b, h] for h in hr]
        qs = [_row_of(_ein(qe_scr[h], s[h]), b) for h in hr]
        for h in hr:
            qs_scr[h, pl.ds(b, 1), :] = qs[h]
            so_ref[b, h] = ea_col[h] * s[h] + k_col[h] * v_scr[h, pl.ds(b, 1), :]
        return carry

    lax.fori_loop(0, nb, row, 0)
    for h, (q, k, v) in enumerate(hs):
        og = cols_ref[:, 1024 + h * GLA_DV:1024 + (h + 1) * GLA_DV]
        o = qs_scr[h] + jnp.sum(q * k, -1, keepdims=True) * v
        on = o * lax.rsqrt(jnp.mean(o * o, -1, keepdims=True) + NORM_EPS) * ng_ref[:, h * GLA_DV:(h + 1) * GLA_DV]
        h_ref[:, h * GLA_DV:(h + 1) * GLA_DV] = (on * _silu(og)).astype(h_ref.dtype)


def _gla_step(cols, gup, gb, ng, s0, l, s_out):
    bs = cols.shape[0]
    nb = DEC_BLOCK
    call = _step_call(
        _gla_step_body, l, s_out,
        in_specs=[_row_spec(W_GLA), _layer_spec(l, (LANES, GLA_H * GLA_DK)), _layer_spec(l, (1, GLA_H * GLA_DK)),
                  _layer_spec(l, (1, GROUP)), _layer_state(l, GLA_H, GLA_DK, GLA_DV)],
        out_specs=[_row_spec(GROUP), _layer_state(l, GLA_H, GLA_DK, GLA_DV)],
        out_shapes=[jax.ShapeDtypeStruct((bs, GROUP), bf16), jax.ShapeDtypeStruct(s0.shape, f32)],
        state_out_index=1, grid=(bs // nb,),
        scratch_shapes=[pltpu.VMEM((GLA_H, nb, GLA_DK), f32)] * 3 + [pltpu.VMEM((GLA_H, nb, GLA_DV), f32)] * 2,
        compiler_params=_cparams(("parallel",)), name="gla_step")
    return call(cols, gup, gb, ng, s0)


def _rwkv_step_body(cols_ref, shift_ref, mu_ref, wup_ref, aup_ref, gup_ref, vec_ref, s_ref, h_ref, so_ref,
                    w_scr, b_scr, k_scr, kkrw_scr, v_scr, sk_scr, o_scr):
    nb = DEC_BLOCK
    vec = vec_ref[...]
    r, k2, v, lw, a, g, kk_raw = _rwkv_pre(cols_ref[...], shift_ref[...], mu_ref[...], wup_ref[...], aup_ref[...],
                                           gup_ref[...], vec)
    hr = range(RW_H)
    sl = [slice(h * RW_N, (h + 1) * RW_N) for h in hr]
    kk = [_l2norm(kk_raw[:, sl[h]]) for h in hr]
    w = jnp.exp(lw)
    bv = jnp.concatenate(kk, axis=-1) * a
    w_scr[...], b_scr[...], k_scr[...] = w, bv, k2
    for h in hr:
        kkrw_scr[h, 0:nb, :], kkrw_scr[h, nb:2 * nb, :] = kk[h], r[:, sl[h]] * w[:, sl[h]]
        v_scr[h] = v[:, sl[h]]

    def row(b, carry):
        w_col, b_col, k_col = (_cols_of(scr[pl.ds(b, 1), :], RW_N) for scr in (w_scr, b_scr, k_scr))
        s = [s_ref[b, h] for h in hr]
        mv = [_ein(kkrw_scr[h], s[h]) for h in hr]
        sk = [_row_of(mv[h][0:nb, :], b) for h in hr]
        rws = [_row_of(mv[h][nb:2 * nb, :], b) for h in hr]
        for h in hr:
            so_ref[b, h] = w_col[h] * s[h] - b_col[h] * sk[h] + k_col[h] * v_scr[h, pl.ds(b, 1), :]
            sk_scr[h, pl.ds(b, 1), :] = sk[h]
            o_scr[h, pl.ds(b, 1), :] = rws[h]
        return carry

    lax.fori_loop(0, nb, row, 0)
    rb = [jnp.sum(r[:, sl[h]] * bv[:, sl[h]], -1, keepdims=True) for h in hr]
    rk = [jnp.sum(r[:, sl[h]] * k2[:, sl[h]], -1, keepdims=True) for h in hr]
    o = [o_scr[h] - rb[h] * sk_scr[h] + rk[h] * v[:, sl[h]] for h in hr]
    h_ref[...] = _rwkv_post(o, r, k2, v, g, vec).astype(h_ref.dtype)


def _rwkv_step(cols, shift0, mu, wup, aup, gup, vec, s0, l, s_out):
    bs = cols.shape[0]
    nb = DEC_BLOCK
    call = _step_call(
        _rwkv_step_body, l, s_out,
        in_specs=[_row_spec(W_RW), _layer_rows(l, W_RW), _layer_spec(l, (1, W_RW)), _layer_spec(l, (LANES, GROUP)),
                  _layer_spec(l, (LANES, GROUP)), _layer_spec(l, (LANES, GROUP)), _layer_spec(l, (8, GROUP)),
                  _layer_state(l, RW_H, RW_N, RW_N)],
        out_specs=[_row_spec(GROUP), _layer_state(l, RW_H, RW_N, RW_N)],
        out_shapes=[jax.ShapeDtypeStruct((bs, GROUP), bf16), jax.ShapeDtypeStruct(s0.shape, f32)],
        state_out_index=1, grid=(bs // nb,),
        scratch_shapes=[pltpu.VMEM((nb, GROUP), f32)] * 3 + [pltpu.VMEM((RW_H, 2 * nb, RW_N), f32)]
        + [pltpu.VMEM((RW_H, nb, RW_N), f32)] * 3,
        compiler_params=_cparams(("parallel",)), name="rwkv_step")
    return call(cols, shift0, mu, wup, aup, gup, vec, s0)


def _gdn_step_body(cols_ref, buf_ref, cw_ref, alog_ref, dtb_ref, ng_ref, s_ref, h_ref, so_ref,
                   kq_scr, v_scr, sc_scr, qs_scr, vn_scr):
    nb = DEC_BLOCK
    conv = cols_ref[:, 0:DN_QKV] * cw_ref[DN_CONV - 1:DN_CONV, :]
    for w in range(DN_CONV - 1):
        conv = conv + buf_ref[:, w * DN_QKV:(w + 1) * DN_QKV] * cw_ref[w:w + 1, :]
    act = _silu(conv)
    beta_all, g_all = _gdn_gates(cols_ref[:, 2048:2176], alog_ref[...], dtb_ref[...])
    eg_all = jnp.exp(g_all)
    hs = []
    for h in range(DN_H):
        q = _l2norm(act[:, h * DN_D:(h + 1) * DN_D]) * DN_D ** -0.5
        k = _l2norm(act[:, GROUP + h * DN_D:GROUP + (h + 1) * DN_D])
        v = act[:, 2 * GROUP + h * DN_D:2 * GROUP + (h + 1) * DN_D]
        beta, eg = beta_all[:, h:h + 1], eg_all[:, DN_H + h:DN_H + h + 1]
        kq_scr[h, 0:nb, :], kq_scr[h, nb:2 * nb, :], v_scr[h] = k, q, v
        sc_scr[h] = jnp.where(_iota((nb, LANES), 1) == 0, beta, eg)
        hs.append((q, k, eg))

    def row(b, carry):
        hr = range(DN_H)
        k_col = [_col(kq_scr[h, pl.ds(b, 1), :]) for h in hr]
        s = [s_ref[b, h] for h in hr]
        sc = [sc_scr[h, pl.ds(b, 1), :] for h in hr]
        mv = [_ein(kq_scr[h], s[h]) for h in hr]
        ks = [_row_of(mv[h][0:nb, :], b) for h in hr]
        qs = [_row_of(mv[h][nb:2 * nb, :], b) for h in hr]
        for h in hr:
            beta, eg = sc[h][:, 0:1], sc[h][:, 1:2]
            v_new = beta * (v_scr[h, pl.ds(b, 1), :] - eg * ks[h])
            qs_scr[h, pl.ds(b, 1), :] = qs[h]
            vn_scr[h, pl.ds(b, 1), :] = v_new
            so_ref[b, h] = eg * s[h] + k_col[h] * v_new
        return carry

    lax.fori_loop(0, nb, row, 0)
    for h, (q, k, eg) in enumerate(hs):
        z = cols_ref[:, DN_QKV + h * DN_D:DN_QKV + (h + 1) * DN_D]
        o = eg * qs_scr[h] + jnp.sum(q * k, -1, keepdims=True) * vn_scr[h]
        on = o * lax.rsqrt(jnp.mean(o * o, -1, keepdims=True) + NORM_EPS) * ng_ref[...]
        h_ref[:, h * DN_D:(h + 1) * DN_D] = (on * _silu(z)).astype(h_ref.dtype)


def _gdn_step(cols, buf, cw, alog, dtb, ng, s0, l, s_out):
    bs = cols.shape[0]
    nb = DEC_BLOCK
    call = _step_call(
        _gdn_step_body, l, s_out,
        in_specs=[_row_spec(W_DN), _layer_rows(l, (DN_CONV - 1) * DN_QKV), _layer_spec(l, (8, DN_QKV)),
                  _layer_spec(l, (1, LANES)), _layer_spec(l, (1, LANES)), _layer_spec(l, (1, DN_D)),
                  _layer_state(l, DN_H, DN_D, DN_D)],
        out_specs=[_row_spec(GROUP), _layer_state(l, DN_H, DN_D, DN_D)],
        out_shapes=[jax.ShapeDtypeStruct((bs, GROUP), bf16), jax.ShapeDtypeStruct(s0.shape, f32)],
        state_out_index=1, grid=(bs // nb,),
        scratch_shapes=[pltpu.VMEM((DN_H, 2 * nb, DN_D), f32), pltpu.VMEM((DN_H, nb, DN_D), f32),
                        pltpu.VMEM((DN_H, nb, LANES), f32)] + [pltpu.VMEM((DN_H, nb, DN_D), f32)] * 2,
        compiler_params=_cparams(("parallel",)), name="gdn_step")
    return call(cols, buf, cw, alog, dtb, ng, s0)


def _pad_last(w, width):
    return jnp.pad(w, [(0, 0)] * (w.ndim - 1) + [(0, width - w.shape[-1])])


def _pad_rows(w, start, total):
    return jnp.pad(w, ((0, 0), (start, total - start - w.shape[1]), (0, 0)))


def _lane_rows(v, start):
    return jnp.pad(v, ((0, 0), (start, LANES - start - v.shape[1])))[:, None, :]


def kernel(x_prompt, x_sample, state_mlstm_c, state_mlstm_n, state_mlstm_m, state_gla, state_rwkv, state_rwkv_shift, state_dn, state_dn_conv, w_in, w_out, w_up, w_down, ln1_g, ln1_b, ln2_g, ln2_b, mlstm_gate_b, mlstm_norm_g, gla_gate_up, gla_gate_b, gla_norm_g, rwkv_mu, rwkv_w_up, rwkv_w0, rwkv_a_up, rwkv_a0, rwkv_g_up, rwkv_k_k, rwkv_k_a, rwkv_r_k, rwkv_norm_g, rwkv_norm_b, dn_conv_w, dn_a_log, dn_dt_bias, dn_norm_g):
    bp, t, _ = x_prompt.shape
    bs = x_sample.shape[0]
    xp = x_prompt.reshape(bp * t, D_MODEL)
    xs = x_sample.reshape(bs, D_MODEL)
    xpb, xsb = xp, xs
    tm_p, tm_s, tf = 512, bs, 1024

    o1, o2, o3 = N_ML, N_ML + N_GLA, N_ML + N_GLA + N_RW
    w_secs = [_pad_last(w_in[:, :, 0:o1].astype(bf16), W_ML), _pad_last(w_in[:, :, o1:o2].astype(bf16), W_GLA),
              w_in[:, :, o2:o3].astype(bf16), _pad_last(w_in[:, :, o3:].astype(bf16), W_DN)]
    wo, wu, wd = w_out.astype(bf16), w_up.astype(bf16), w_down.astype(bf16)
    g1, b1, g2, b2 = ln1_g[:, None], ln1_b[:, None], ln2_g[:, None], ln2_b[:, None]
    ml_gb, ml_ng = _lane_rows(mlstm_gate_b, 0), mlstm_norm_g[:, None]
    gla_gup, gla_gb, gla_ng = _pad_rows(gla_gate_up, 0, LANES), gla_gate_b[:, None], gla_norm_g[:, None]
    rw_mu = rwkv_mu[:, None]
    rw_wup, rw_aup, rw_gup = _pad_rows(rwkv_w_up, 0, LANES), _pad_rows(rwkv_a_up, 32, LANES), _pad_rows(rwkv_g_up, 64, LANES)
    rw_vec = jnp.stack([rwkv_w0, rwkv_a0, rwkv_k_k, rwkv_k_a, rwkv_r_k, rwkv_norm_g, rwkv_norm_b,
                        jnp.zeros_like(rwkv_w0)], axis=1)
    dn_cw = _pad_rows(dn_conv_w, 0, 8)
    dn_alog, dn_dtb, dn_ng = _lane_rows(dn_a_log, DN_H), _lane_rows(dn_dt_bias, DN_H), dn_norm_g[:, None]
    st_ml_n = state_mlstm_n.reshape(DEPTH, bs, ML_H * ML_DK)
    st_dn_conv = state_dn_conv.reshape(DEPTH, bs, (DN_CONV - 1) * DN_QKV)

    outs_p, outs_s = [], []
    ml_c_s = gla_s = rw_s = dn_s = None
    for l in range(DEPTH):
        c_ml, c_gla, c_rw, c_dn = [_proj(xpb, w, l, 2 * tm_p) for w in w_secs]
        h_ml, mc, mn, mm = _mlstm_prompt(c_ml, ml_gb, ml_ng, l, bp, t)
        h_gla, gs = _gla_prompt(c_gla, gla_gup, gla_gb, gla_ng, l, bp, t)
        h_rw, rs = _rwkv_prompt(c_rw, rw_mu, rw_wup, rw_aup, rw_gup, rw_vec, l, bp, t)
        h_dn, ds = _gdn_prompt(c_dn, dn_cw, dn_alog, dn_dtb, dn_ng, l, bp, t)
        hs_p = [h.reshape(bp * t, GROUP) for h in (h_ml, h_gla, h_rw, h_dn)]
        x1, x1b = _outproj_ln(hs_p, wo, xp, g1, b1, l, tm_p)
        xp, xpb = _ffn_ln(x1, x1b, wu, wd, g2, b2, l, tm_p, tf)
        outs_p.append((mc, mn, mm[:, :, 0], gs, rs, c_rw.reshape(bp, t, W_RW)[:, -1],
                       ds, c_dn.reshape(bp, t, W_DN)[:, t - (DN_CONV - 1):, :DN_QKV]))

        c_ml, c_gla, c_rw, c_dn = [_proj(xsb, w, l, tm_s) for w in w_secs]
        h_ml, ml_c_s, mn, mm = _mlstm_step(c_ml, ml_gb, ml_ng, state_mlstm_c, st_ml_n, state_mlstm_m, l, ml_c_s)
        h_gla, gla_s = _gla_step(c_gla, gla_gup, gla_gb, gla_ng, state_gla, l, gla_s)
        h_rw, rw_s = _rwkv_step(c_rw, state_rwkv_shift, rw_mu, rw_wup, rw_aup, rw_gup, rw_vec, state_rwkv, l, rw_s)
        h_dn, dn_s = _gdn_step(c_dn, st_dn_conv, dn_cw, dn_alog, dn_dtb, dn_ng, state_dn, l, dn_s)
        x1, x1b = _outproj_ln((h_ml, h_gla, h_rw, h_dn), wo, xs, g1, b1, l, tm_s)
        xs, xsb = _ffn_ln(x1, x1b, wu, wd, g2, b2, l, tm_s, tf)
        outs_s.append((mn.reshape(bs, ML_H, ML_DK), mm, c_rw,
                       jnp.concatenate([state_dn_conv[l][:, 1:], c_dn[:, None, :DN_QKV]], axis=1)))

    (mlstm_c_p, mlstm_n_p, mlstm_m_p, gla_p, rwkv_p, rwkv_shift_p, dn_p, dn_conv_p) = [jnp.stack(z) for z in zip(*outs_p)]
    (mlstm_n_s, mlstm_m_s, rwkv_shift_s, dn_conv_s) = [jnp.stack(z) for z in zip(*outs_s)]
    return (xp.reshape(bp, t, D_MODEL), xs.reshape(bs, 1, D_MODEL), mlstm_c_p, ml_c_s, mlstm_n_p, mlstm_n_s,
            mlstm_m_p, mlstm_m_s, gla_p, gla_s, rwkv_p, rw_s, rwkv_shift_p, rwkv_shift_s, dn_p, dn_s,
            dn_conv_p, dn_conv_s)
```

```python
import functools

import jax
import jax.numpy as jnp
from jax import lax
from jax.experimental import pallas as pl
from jax.experimental.pallas import tpu as pltpu

f32 = jnp.float32
bf16 = jnp.bfloat16

D_MODEL = 2048
DEPTH = 4
GROUP = 512
D_FF = 4 * D_MODEL
CHUNK = 64
ML_H, ML_DK, ML_DV = 4, 64, 128
GLA_H, GLA_DK, GLA_DV, GLA_RANK, GLA_TAU = 4, 64, 128, 16, 16.0
RW_H, RW_N = 8, 64
RW_DECAY_SCALE = 0.606531
RW_GN_EPS = 64e-5
DN_H, DN_D, DN_CONV, DN_QKV = 4, 128, 4, 1536
N_ML, N_GLA, N_RW, N_DN = 1544, 1552, 1664, 2056
W_ML, W_GLA, W_RW, W_DN = 1664, 1664, 1664, 2176
W_A, W_B = W_ML + W_GLA, W_DN + W_RW
RW_OFF = W_DN
ALPHA = (2 * DEPTH) ** 0.25
LN_EPS = 1e-5
NORM_EPS = 1e-6
LANES = 128
DEC_BLOCK = 16
VMEM_LIMIT = 56 * 1024 * 1024
LN_SUBTILE = 256


def _ein(a, b):
    return jnp.dot(a.astype(bf16), b.astype(bf16), preferred_element_type=f32)


def _ein_nt(a, b):
    return lax.dot_general(a.astype(bf16), b.astype(bf16), (((1,), (1,)), ((), ())), preferred_element_type=f32)


def _ein_tn(a, b):
    return lax.dot_general(a.astype(bf16), b.astype(bf16), (((0,), (0,)), ((), ())), preferred_element_type=f32)


def _split3(x):
    hi = x.astype(bf16)
    r1 = x - hi.astype(f32)
    mid = r1.astype(bf16)
    return hi, mid, (r1 - mid.astype(f32)).astype(bf16)


def _select_dot(sel, x):
    s = sel.astype(bf16)
    hi, mid, lo = _split3(x)
    d = functools.partial(jnp.dot, preferred_element_type=f32)
    return d(s, hi) + d(s, mid) + d(s, lo)


def _select_dot_right(x, sel):
    s = sel.astype(bf16)
    hi, mid, lo = _split3(x)
    d = functools.partial(jnp.dot, preferred_element_type=f32)
    return d(hi, s) + d(mid, s) + d(lo, s)


def _select_dot_nt(sel, x):
    s = sel.astype(bf16)
    hi, mid, lo = _split3(x)
    d = functools.partial(lax.dot_general, dimension_numbers=(((1,), (1,)), ((), ())), preferred_element_type=f32)
    return d(s, hi) + d(s, mid) + d(s, lo)


def _log_sigmoid(x):
    return jnp.minimum(x, 0.0) - jnp.log1p(jnp.exp(-jnp.abs(x)))


def _softplus(x):
    return jnp.maximum(x, 0.0) + jnp.log1p(jnp.exp(-jnp.abs(x)))


def _silu(x):
    return x * jax.nn.sigmoid(x)


def _l2norm(x):
    return x * lax.rsqrt(jnp.sum(x * x, -1, keepdims=True) + NORM_EPS)


def _iota(shape, axis):
    return lax.broadcasted_iota(jnp.int32, shape, axis)


def _col(row):
    n = row.shape[1]
    eye = _iota((n, n), 0) == _iota((n, n), 1)
    return jnp.sum(jnp.where(eye, row, 0.0), axis=1, keepdims=True)


def _row_of(x, b):
    return jnp.sum(jnp.where(_iota(x.shape, 0) == b, x, 0.0), axis=0, keepdims=True)


def _cols_of(row, n, width=None):
    width = n if width is None else width
    tiles = []
    for g in range(row.shape[1] // LANES):
        t = jnp.broadcast_to(row[:, g * LANES:(g + 1) * LANES], (LANES, LANES)).T
        tiles += [t[j * n:(j + 1) * n, 0:width] for j in range(LANES // n)]
    return tiles


def _tri_masks(n):
    r, c = _iota((n, n), 0), _iota((n, n), 1)
    return c <= r, c < r


def _split2(a):
    hi = a.astype(bf16)
    return hi, (a - hi.astype(f32)).astype(bf16)


def _dot3(a, b):
    ah, al = _split2(a)
    bh, bl = _split2(b)
    d = functools.partial(jnp.dot, preferred_element_type=f32)
    return d(ah, bh) + d(ah, bl) + d(al, bh)


def _unit_lower_inverses(a_list):
    n = a_list[0].shape[0]
    eye = jnp.where(_iota((n, n), 0) == _iota((n, n), 1), 1.0, 0.0)
    xs = [-a for a in a_list]
    ps = [eye + x for x in xs]
    for stage in range(max(n.bit_length() - 2, 0)):
        mm = _dot3 if stage == 0 else _ein
        xs = [mm(x, x) for x in xs]
        ps = [p + mm(p, x) for p, x in zip(ps, xs)]
    return ps


def _rows_of(y):
    e8 = jnp.where(_iota((8, LANES), 0) == _iota((8, LANES), 1), 1.0, 0.0)
    return _select_dot_nt(e8, y)


def _cparams(sem):
    return pltpu.CompilerParams(dimension_semantics=sem, vmem_limit_bytes=VMEM_LIMIT)


def _layer_spec(l, shape, index=None):
    nd = len(shape)
    if index is None:
        return pl.BlockSpec((None,) + tuple(shape), lambda *g: (l,) + (0,) * nd)
    return pl.BlockSpec((None,) + tuple(shape), lambda *g: (l,) + tuple(index(*g)))


def _proj_body(x_ref, w_ref, o_ref):
    o_ref[...] = jnp.dot(x_ref[...].astype(bf16), w_ref[...], preferred_element_type=f32)


def _proj(xb, w, l, tm):
    m, k = xb.shape
    n = w.shape[2]
    return pl.pallas_call(
        _proj_body, grid=(m // tm,),
        in_specs=[pl.BlockSpec((tm, k), lambda i: (i, 0)),
                  pl.BlockSpec((None, k, n), lambda i: (l, 0, 0), pipeline_mode=pl.Buffered(1))],
        out_specs=pl.BlockSpec((tm, n), lambda i: (i, 0)),
        out_shape=jax.ShapeDtypeStruct((m, n), f32),
        compiler_params=_cparams(("parallel",)), name="proj")(xb, w)


def _layernorm(y, g, b):
    mu = jnp.mean(y, -1, keepdims=True)
    yc = y - mu
    var = jnp.mean(yc * yc, -1, keepdims=True)
    return yc * lax.rsqrt(var + LN_EPS) * g + b


def _outproj_ln_body(h0, h1, h2, h3, w_ref, x_ref, g_ref, b_ref, o_ref, ob_ref):
    sub = min(LN_SUBTILE, x_ref.shape[0])
    for r in range(x_ref.shape[0] // sub):
        rows = slice(r * sub, (r + 1) * sub)
        mix = jnp.concatenate([h[rows, :] for h in (h0, h1, h2, h3)], axis=1)
        acc = ALPHA * x_ref[rows, :] + jnp.dot(mix, w_ref[...], preferred_element_type=f32)
        y = _layernorm(acc, g_ref[...], b_ref[...])
        o_ref[rows, :] = y
        ob_ref[rows, :] = y.astype(bf16)


def _outproj_ln(hs, w, x, g, b, l, tm):
    m = x.shape[0]
    hspec = pl.BlockSpec((tm, GROUP), lambda i: (i, 0))
    xspec = pl.BlockSpec((tm, D_MODEL), lambda i: (i, 0))
    vspec = _layer_spec(l, (1, D_MODEL))
    return pl.pallas_call(
        _outproj_ln_body, grid=(m // tm,),
        in_specs=[hspec] * 4 + [_layer_spec(l, (D_MODEL, D_MODEL)), xspec, vspec, vspec],
        out_specs=[xspec, xspec],
        out_shape=[jax.ShapeDtypeStruct((m, D_MODEL), f32), jax.ShapeDtypeStruct((m, D_MODEL), bf16)],
        compiler_params=_cparams(("parallel",)), name="outproj_ln")(*hs, w, x, g, b)


def _ffn_ln_body(x_ref, xb_ref, wu_ref, wd_ref, g_ref, b_ref, o_ref, ob_ref, acc_ref):
    f = pl.program_id(1)

    @pl.when(f == 0)
    def _():
        acc_ref[...] = ALPHA * x_ref[...]

    h = jnp.maximum(jnp.dot(xb_ref[...], wu_ref[...], preferred_element_type=f32), 0.0)
    acc_ref[...] += jnp.dot((h * h).astype(bf16), wd_ref[...], preferred_element_type=f32)

    @pl.when(f == pl.num_programs(1) - 1)
    def _():
        y = _layernorm(acc_ref[...], g_ref[...], b_ref[...])
        o_ref[...] = y
        ob_ref[...] = y.astype(bf16)


def _ffn_ln(x, xb, wu, wd, g, b, l, tm, tf):
    m = x.shape[0]
    xspec = pl.BlockSpec((tm, D_MODEL), lambda i, f: (i, 0))
    vspec = _layer_spec(l, (1, D_MODEL))
    return pl.pallas_call(
        _ffn_ln_body, grid=(m // tm, D_FF // tf),
        in_specs=[xspec, xspec, _layer_spec(l, (D_MODEL, tf), lambda i, f: (0, f)),
                  _layer_spec(l, (tf, D_MODEL), lambda i, f: (f, 0)), vspec, vspec],
        out_specs=[xspec, xspec],
        out_shape=[jax.ShapeDtypeStruct((m, D_MODEL), f32), jax.ShapeDtypeStruct((m, D_MODEL), bf16)],
        scratch_shapes=[pltpu.VMEM((tm, D_MODEL), f32)],
        compiler_params=_cparams(("parallel", "arbitrary")), name="ffn_ln")(x, xb, wu, wd, g, b)


def _tril_blocks(nb, L):
    r, c = _iota((nb * L, nb * L), 0), _iota((nb * L, nb * L), 1)
    return jnp.where((c <= r) & (c >= r - (r & (L - 1))), 1.0, 0.0)


def _units(nb, heads):
    return [(b, h) for b in range(nb) for h in range(heads)]


def _mlstm_prompt_body(cols_ref, gb_ref, ng_ref, h_ref, c_ref, n_ref, m_ref):
    @pl.when(pl.program_id(0) == 0)
    def _():
        c_ref[...] = jnp.zeros_like(c_ref)
        n_ref[...] = jnp.zeros_like(n_ref)
        m_ref[...] = jnp.zeros_like(m_ref)

    nb, L = cols_ref.shape[0], CHUNK
    causal, _ = _tri_masks(L)
    g = cols_ref[:, :, 1536:1664].reshape(nb * L, LANES) + gb_ref[...]
    bc = _select_dot(_tril_blocks(nb, L), _log_sigmoid(g))
    y_all = jnp.where(_iota((nb * L, LANES), 1) < ML_H, g, bc)
    y = [y_all[b * L:(b + 1) * L, :] for b in range(nb)]
    yt = [_rows_of(y[b]) for b in range(nb)]
    us = _units(nb, ML_H)
    ix = range(len(us))
    q = [cols_ref[b, :, h * ML_DK:(h + 1) * ML_DK] for b, h in us]
    k = [cols_ref[b, :, 256 + h * ML_DK:256 + (h + 1) * ML_DK] * ML_DK ** -0.5 for b, h in us]
    v = [cols_ref[b, :, 512 + h * ML_DV:512 + (h + 1) * ML_DV] for b, h in us]
    c = [c_ref[b, h] for b, h in us]
    n = [n_ref[b, h:h + 1, :] for b, h in us]
    m = [m_ref[b, h:h + 1, :] for b, h in us]
    qk = [_ein_nt(q[i], k[i]) for i in ix]
    qc = [_ein(q[i], c[i]) for i in ix]
    qn = [_ein_nt(q[i], jnp.broadcast_to(n[i], (LANES, ML_DK))) for i in ix]
    spread_shape = (LANES, 2 * ML_H * LANES)
    spread = jnp.where(_iota(spread_shape, 0) == jnp.right_shift(_iota(spread_shape, 1), LANES.bit_length() - 1), 1.0, 0.0)
    rep = [_select_dot_right(y[b], spread) for b in range(nb)]
    i_rep = [rep[b][:, h * LANES:(h + 1) * LANES] for b, h in us]
    b_rep = [rep[b][:, (ML_H + h) * LANES:(ML_H + h + 1) * LANES] for b, h in us]
    d = [jnp.where(causal, b_rep[i][:, :L] - yt[b][ML_H + h:ML_H + h + 1, :] + yt[b][h:h + 1, :], -jnp.inf)
         for i, (b, h) in enumerate(us)]
    inter = [b_rep[i] + m[i] for i in ix]
    m_t = [jnp.maximum(inter[i], jnp.max(d[i], axis=-1, keepdims=True)) for i in ix]
    w_inter = [jnp.exp(inter[i] - m_t[i]) for i in ix]
    s = [qk[i] * jnp.exp(d[i] - m_t[i][:, :L]) for i in ix]
    m_new = [m_t[i][L - 1:L, :] for i in ix]
    b_last = [b_rep[i][L - 1:L, :] for i in ix]
    kw = [k[i] * jnp.exp(b_last[i] - b_rep[i] + i_rep[i] - m_new[i])[:, :ML_DK] for i in ix]
    sv = [_ein(s[i], v[i]) for i in ix]
    kv = [_ein_tn(kw[i], v[i]) for i in ix]
    s_sum = [jnp.sum(s[i], -1, keepdims=True) for i in ix]
    f_state = [jnp.exp(b_last[i] + m[i] - m_new[i]) for i in ix]
    c_new = [f_state[i] * c[i] + kv[i] for i in ix]
    n_new = [f_state[i][:, :ML_DK] * n[i] + jnp.sum(kw[i], axis=0, keepdims=True) for i in ix]
    for i, (b, h) in enumerate(us):
        c_ref[b, h] = c_new[i]
        n_ref[b, h:h + 1, :] = n_new[i]
        m_ref[b, h:h + 1, :] = m_new[i]
    hh = [(w_inter[i] * qc[i] + sv[i]) / jnp.maximum(jnp.abs(w_inter[i] * qn[i] + s_sum[i]), jnp.exp(-m_t[i])) for i in ix]
    ms = [jnp.mean(hh[i] * hh[i], -1, keepdims=True) for i in ix]
    for i, (b, h) in enumerate(us):
        og = cols_ref[b, :, 1024 + h * ML_DV:1024 + (h + 1) * ML_DV]
        hn = hh[i] * lax.rsqrt(ms[i] + NORM_EPS) * ng_ref[:, h * ML_DV:(h + 1) * ML_DV]
        h_ref[b, :, h * ML_DV:(h + 1) * ML_DV] = (hn * jax.nn.sigmoid(og)).astype(h_ref.dtype)


def _chunk_grid_specs(bp, t, width, col=0):
    cols = pl.BlockSpec((bp, CHUNK, width), lambda c: (0, c, col))
    out = pl.BlockSpec((bp, CHUNK, GROUP), lambda c: (0, c, 0))
    return (t // CHUNK,), cols, out


def _whole(shape):
    return pl.BlockSpec(shape, lambda c: (0,) * len(shape))


def _mlstm_prompt(cols, gb, ng, l, bp, t):
    grid, cspec, ospec = _chunk_grid_specs(bp, t, W_ML)
    return pl.pallas_call(
        _mlstm_prompt_body, grid=grid,
        in_specs=[cspec, _layer_spec(l, (1, LANES)), _layer_spec(l, (1, GROUP))],
        out_specs=[ospec, _whole((bp, ML_H, ML_DK, ML_DV)), _whole((bp, ML_H, ML_DK)), _whole((bp, ML_H, LANES))],
        out_shape=[jax.ShapeDtypeStruct((bp, t, GROUP), bf16), jax.ShapeDtypeStruct((bp, ML_H, ML_DK, ML_DV), f32),
                   jax.ShapeDtypeStruct((bp, ML_H, ML_DK), f32), jax.ShapeDtypeStruct((bp, ML_H, LANES), f32)],
        compiler_params=_cparams(("arbitrary",)), name="mlstm_prompt")(cols.reshape(bp, t, W_A), gb, ng)


def _gla_prompt_body(cols_ref, gup_ref, gb_ref, ng_ref, h_ref, s_ref):
    @pl.when(pl.program_id(0) == 0)
    def _():
        s_ref[...] = jnp.zeros_like(s_ref)

    nb, L = cols_ref.shape[0], CHUNK
    causal, _ = _tri_masks(L)
    lr = cols_ref[:, :, 1536:1664].reshape(nb * L, LANES)
    la = _log_sigmoid(_ein(lr, gup_ref[...]) + gb_ref[...]) / GLA_TAU
    bc_all = _select_dot(_tril_blocks(nb, L), la)
    bc = [bc_all[b * L:(b + 1) * L, :] for b in range(nb)]
    mid = [bc[b][L // 2:L // 2 + 1, :] for b in range(nb)]
    last = [bc[b][L - 1:L, :] for b in range(nb)]
    e_q_mid = [jnp.exp(bc[b] - mid[b]) for b in range(nb)]
    e_k_mid = [jnp.exp(mid[b] - bc[b]) for b in range(nb)]
    e_q = [jnp.exp(bc[b]) for b in range(nb)]
    e_k_last = [jnp.exp(last[b] - bc[b]) for b in range(nb)]
    e_last = [jnp.exp(last[b]) for b in range(nb)]
    us = _units(nb, GLA_H)
    ix = range(len(us))
    sl = [slice(h * GLA_DK, (h + 1) * GLA_DK) for _, h in us]
    q = [cols_ref[b, :, h * GLA_DK:(h + 1) * GLA_DK] * GLA_DK ** -0.5 for b, h in us]
    k = [cols_ref[b, :, 256 + h * GLA_DK:256 + (h + 1) * GLA_DK] for b, h in us]
    v = [cols_ref[b, :, 512 + h * GLA_DV:512 + (h + 1) * GLA_DV] for b, h in us]
    s = [s_ref[b, h] for b, h in us]
    a = [jnp.where(causal, _ein_nt(q[i] * e_q_mid[b][:, sl[i]], k[i] * e_k_mid[b][:, sl[i]]), 0.0)
         for i, (b, _) in enumerate(us)]
    o0 = [_ein(q[i] * e_q[b][:, sl[i]], s[i]) for i, (b, _) in enumerate(us)]
    kv = [_ein_tn(k[i] * e_k_last[b][:, sl[i]], v[i]) for i, (b, _) in enumerate(us)]
    o = [o0[i] + _ein(a[i], v[i]) for i in ix]
    for i, (b, h) in enumerate(us):
        s_ref[b, h] = _col(e_last[b][:, sl[i]]) * s[i] + kv[i]
    ms = [jnp.mean(o[i] * o[i], -1, keepdims=True) for i in ix]
    for i, (b, h) in enumerate(us):
        og = cols_ref[b, :, 1024 + h * GLA_DV:1024 + (h + 1) * GLA_DV]
        on = o[i] * lax.rsqrt(ms[i] + NORM_EPS) * ng_ref[:, h * GLA_DV:(h + 1) * GLA_DV]
        h_ref[b, :, h * GLA_DV:(h + 1) * GLA_DV] = (on * _silu(og)).astype(h_ref.dtype)


def _gla_prompt(cols, gup, gb, ng, l, bp, t):
    grid, cspec, ospec = _chunk_grid_specs(bp, t, W_GLA, col=1)
    return pl.pallas_call(
        _gla_prompt_body, grid=grid,
        in_specs=[cspec, _layer_spec(l, (LANES, GLA_H * GLA_DK)), _layer_spec(l, (1, GLA_H * GLA_DK)),
                  _layer_spec(l, (1, GROUP))],
        out_specs=[ospec, _whole((bp, GLA_H, GLA_DK, GLA_DV))],
        out_shape=[jax.ShapeDtypeStruct((bp, t, GROUP), bf16), jax.ShapeDtypeStruct((bp, GLA_H, GLA_DK, GLA_DV), f32)],
        compiler_params=_cparams(("arbitrary",)), name="gla_prompt")(cols.reshape(bp, t, W_A), gup, gb, ng)


def _rwkv_pre(x, prev, mu, wup, aup, gup, vec):
    xs = x + (prev - x) * mu
    r, k, v, lb = xs[:, 0:512], xs[:, 512:1024], xs[:, 1024:1536], xs[:, 1536:1664]
    w0, a0, k_k, k_a = vec[0:1, :], vec[1:2, :], vec[2:3, :], vec[3:4, :]
    lw = -RW_DECAY_SCALE * jax.nn.sigmoid(w0 + _ein(jnp.tanh(lb), wup))
    a = jax.nn.sigmoid(a0 + _ein(lb, aup))
    g = _ein(jax.nn.sigmoid(lb), gup)
    kk_raw = k * k_k
    k2 = k * (1.0 + (a - 1.0) * k_a)
    return r, k2, v, lw, a, g, kk_raw


def _rwkv_post(o, r, k2, v, g, vec):
    hs = range(RW_H)
    sl = [slice(h * RW_N, (h + 1) * RW_N) for h in hs]
    mu_o = [jnp.mean(o[h], -1, keepdims=True) for h in hs]
    bonus_w = [jnp.sum(r[:, sl[h]] * k2[:, sl[h]] * vec[4:5, sl[h]], -1, keepdims=True) for h in hs]
    oc = [o[h] - mu_o[h] for h in hs]
    var_o = [jnp.mean(oc[h] * oc[h], -1, keepdims=True) for h in hs]
    outs = [(oc[h] * lax.rsqrt(var_o[h] + RW_GN_EPS) * vec[5:6, sl[h]] + vec[6:7, sl[h]]
             + bonus_w[h] * v[:, sl[h]]) * g[:, sl[h]] for h in hs]
    return jnp.concatenate(outs, axis=-1)


def _rwkv_prompt_body(cols_ref, mu_ref, wup_ref, aup_ref, gup_ref, vec_ref, h_ref, s_ref, prev_scr):
    @pl.when(pl.program_id(0) == 0)
    def _():
        s_ref[...] = jnp.zeros_like(s_ref)
        prev_scr[...] = jnp.zeros_like(prev_scr)

    nb, L, N = cols_ref.shape[0], CHUNK, RW_N
    first_row = _iota((L, W_RW), 0) == 0
    xb = [cols_ref[b, :, RW_OFF:RW_OFF + W_RW] for b in range(nb)]
    prevs = [jnp.where(first_row, prev_scr[b, 0:1, :], pltpu.roll(xb[b], 1, 0)) for b in range(nb)]
    for b in range(nb):
        prev_scr[b, 0:1, :] = xb[b][L - 1:L, :]
    vec = vec_ref[...]
    r, k2, v, lw, a, g, kk_raw = _rwkv_pre(jnp.concatenate(xb, axis=0), jnp.concatenate(prevs, axis=0), mu_ref[...],
                                           wup_ref[...], aup_ref[...], gup_ref[...], vec)
    cum = _select_dot(_tril_blocks(nb, L), lw)
    cum_prev = cum - lw
    rep = lambda row_of: jnp.concatenate(
        [jnp.broadcast_to(cum[b * L + row_of:b * L + row_of + 1, :], (L, RW_H * N)) for b in range(nb)], axis=0)
    mid, last = rep(L // 2), rep(L - 1)
    e_prev_mid, e_mid_cum, e_cum_mid = jnp.exp(cum_prev - mid), jnp.exp(mid - cum), jnp.exp(cum - mid)
    e_prev, e_cum, e_last_cum, e_last = jnp.exp(cum_prev), jnp.exp(cum), jnp.exp(last - cum), jnp.exp(last)
    row, col = _iota((2 * L, 2 * L), 0), _iota((2 * L, 2 * L), 1)
    tq, sq = row % L, col % L
    quad = sq < tq + jnp.where(row < L, 0, 1)
    left = col[:, :] < L
    hs = range(RW_H)
    sl = [slice(h * N, (h + 1) * N) for h in hs]
    kk_h = [_l2norm(kk_raw[:, sl[h]]) for h in hs]
    b_h = [kk_h[h] * a[:, sl[h]] for h in hs]
    f_lhs = [(kk_h[h] * e_prev_mid[:, sl[h]], r[:, sl[h]] * e_cum_mid[:, sl[h]]) for h in hs]
    f_rhs = [(b_h[h] * e_mid_cum[:, sl[h]], k2[:, sl[h]] * e_mid_cum[:, sl[h]]) for h in hs]
    f_x0 = [(kk_h[h] * e_prev[:, sl[h]], r[:, sl[h]] * e_cum[:, sl[h]]) for h in hs]
    f_dec = [(b_h[h] * e_last_cum[:, sl[h]], k2[:, sl[h]] * e_last_cum[:, sl[h]]) for h in hs]
    us = _units(nb, RW_H)
    ix = range(len(us))
    rows = lambda z, b: z[b * L:(b + 1) * L, :]
    pair = lambda f, b, h: jnp.concatenate([rows(f[h][0], b), rows(f[h][1], b)], axis=0)
    vh = [rows(v[:, sl[h]], b) for b, h in us]
    s = [s_ref[b, h] for b, h in us]
    a_all = [jnp.where(quad, _ein_nt(pair(f_lhs, b, h), pair(f_rhs, b, h)), 0.0) for b, h in us]
    t_inv = _unit_lower_inverses([a_all[i][0:L, 0:L] for i in ix])
    x0 = [pair(f_x0, b, h) for b, h in us]
    m2 = [jnp.where(left, jnp.concatenate([x0[i], x0[i]], axis=1), a_all[i]) for i in ix]
    y = [_ein(m2[i], jnp.concatenate([s[i], vh[i]], axis=0)) for i in ix]
    u = [-_dot3(t_inv[i], y[i][0:L, :]) for i in ix]
    o = [y[i][L:2 * L, :] + _ein(a_all[i][L:2 * L, 0:L], u[i]) for i in ix]
    for i, (b, h) in enumerate(us):
        s_ref[b, h] = (_col(e_last[b * L:b * L + 1, sl[h]]) * s[i]
                       + _ein_tn(pair(f_dec, b, h), jnp.concatenate([u[i], vh[i]], axis=0)))
    o_heads = [jnp.concatenate([o[b * RW_H + h] for b in range(nb)], axis=0) for h in hs]
    h_ref[...] = _rwkv_post(o_heads, r, k2, v, g, vec).reshape(nb, L, GROUP).astype(h_ref.dtype)


def _rwkv_prompt(cols, mu, wup, aup, gup, vec, l, bp, t):
    grid, cspec, ospec = _chunk_grid_specs(bp, t, W_B)
    return pl.pallas_call(
        _rwkv_prompt_body, grid=grid,
        in_specs=[cspec, _layer_spec(l, (1, W_RW)), _layer_spec(l, (LANES, GROUP)), _layer_spec(l, (LANES, GROUP)),
                  _layer_spec(l, (LANES, GROUP)), _layer_spec(l, (8, GROUP))],
        out_specs=[ospec, _whole((bp, RW_H, RW_N, RW_N))],
        out_shape=[jax.ShapeDtypeStruct((bp, t, GROUP), bf16), jax.ShapeDtypeStruct((bp, RW_H, RW_N, RW_N), f32)],
        scratch_shapes=[pltpu.VMEM((bp, 8, W_RW), f32)],
        compiler_params=_cparams(("arbitrary",)), name="rwkv_prompt")(cols.reshape(bp, t, W_B), mu, wup, aup, gup, vec)


def _gdn_gates(gb, alog, dtb):
    return jax.nn.sigmoid(gb), -jnp.exp(alog) * _softplus(gb + dtb)


def _gdn_prompt_body(cols_ref, cw_ref, alog_ref, dtb_ref, ng_ref, h_ref, s_ref, xp_scr):
    @pl.when(pl.program_id(0) == 0)
    def _():
        s_ref[...] = jnp.zeros_like(s_ref)
        xp_scr[...] = jnp.zeros_like(xp_scr)

    nb, L = cols_ref.shape[0], CHUNK
    causal, strict = _tri_masks(L)
    act = []
    for b in range(nb):
        raw = cols_ref[b, :, 0:DN_QKV]
        ext = jnp.concatenate([xp_scr[b], raw], axis=0)
        conv = raw * cw_ref[DN_CONV - 1:DN_CONV, :]
        for j in range(1, DN_CONV):
            conv = conv + pltpu.roll(ext, j, 0)[8:8 + L, :] * cw_ref[DN_CONV - 1 - j:DN_CONV - j, :]
        xp_scr[b] = raw[L - 8:L, :]
        act.append(_silu(conv))
    beta_all, g_all = _gdn_gates(cols_ref[:, :, 2048:2176].reshape(nb * L, LANES), alog_ref[...], dtb_ref[...])
    gc_all = _select_dot(_tril_blocks(nb, L), g_all)
    gc = [gc_all[b * L:(b + 1) * L, :] for b in range(nb)]
    gt = [_rows_of(gc[b]) for b in range(nb)]
    us = _units(nb, DN_H)
    ix = range(len(us))
    q = [_l2norm(act[b][:, h * DN_D:(h + 1) * DN_D]) * DN_D ** -0.5 for b, h in us]
    k = [_l2norm(act[b][:, GROUP + h * DN_D:GROUP + (h + 1) * DN_D]) for b, h in us]
    v = [act[b][:, 2 * GROUP + h * DN_D:2 * GROUP + (h + 1) * DN_D] for b, h in us]
    beta = [beta_all[b * L:(b + 1) * L, h:h + 1] for b, h in us]
    g_col = [gc[b][:, DN_H + h:DN_H + h + 1] for b, h in us]
    decay = [jnp.exp(jnp.where(causal, g_col[i] - gt[b][DN_H + h:DN_H + h + 1, :], -jnp.inf))
             for i, (b, h) in enumerate(us)]
    kb = [k[i] * beta[i] for i in ix]
    eg = [jnp.exp(g_col[i]) for i in ix]
    s = [s_ref[b, h] for b, h in us]
    kq = [_ein_nt(jnp.concatenate([kb[i], q[i]], axis=0), k[i]) for i in ix]
    t_inv = _unit_lower_inverses([jnp.where(strict, kq[i][0:L, :] * decay[i], 0.0) for i in ix])
    qk = [kq[i][L:2 * L, :] * decay[i] for i in ix]
    o0 = [_ein(q[i] * eg[i], s[i]) for i in ix]
    sol = [_dot3(t_inv[i], jnp.concatenate([v[i] * beta[i], kb[i] * eg[i]], axis=-1)) for i in ix]
    v_new = [sol[i][:, :DN_D] - _ein(sol[i][:, DN_D:], s[i]) for i in ix]
    o = [o0[i] + _ein(qk[i], v_new[i]) for i in ix]
    for i, (b, h) in enumerate(us):
        g_last = g_col[i][L - 1:L, :]
        s_ref[b, h] = jnp.exp(g_last) * s[i] + _ein_tn(k[i] * jnp.exp(g_last - g_col[i]), v_new[i])
    ms = [jnp.mean(o[i] * o[i], -1, keepdims=True) for i in ix]
    for i, (b, h) in enumerate(us):
        z = cols_ref[b, :, DN_QKV + h * DN_D:DN_QKV + (h + 1) * DN_D]
        on = o[i] * lax.rsqrt(ms[i] + NORM_EPS) * ng_ref[...]
        h_ref[b, :, h * DN_D:(h + 1) * DN_D] = (on * _silu(z)).astype(h_ref.dtype)


def _gdn_prompt(cols, cw, alog, dtb, ng, l, bp, t):
    grid, cspec, ospec = _chunk_grid_specs(bp, t, W_DN)
    return pl.pallas_call(
        _gdn_prompt_body, grid=grid,
        in_specs=[cspec, _layer_spec(l, (8, DN_QKV)), _layer_spec(l, (1, LANES)), _layer_spec(l, (1, LANES)),
                  _layer_spec(l, (1, DN_D))],
        out_specs=[ospec, _whole((bp, DN_H, DN_D, DN_D))],
        out_shape=[jax.ShapeDtypeStruct((bp, t, GROUP), bf16), jax.ShapeDtypeStruct((bp, DN_H, DN_D, DN_D), f32)],
        scratch_shapes=[pltpu.VMEM((bp, 8, DN_QKV), f32)],
        compiler_params=_cparams(("arbitrary",)), name="gdn_prompt")(cols.reshape(bp, t, W_B), cw, alog, dtb, ng)


def _mlstm_step_body(cols_ref, gb_ref, ng_ref, c_ref, n_ref, m_ref, h_ref, co_ref, no_ref, mo_ref,
                     q_scr, kw_scr, v_scr, f_scr, qc_scr):
    nb = DEC_BLOCK
    g = cols_ref[:, 1536:1664] + gb_ref[...]
    lf = _log_sigmoid(g)
    hs = []
    for h in range(ML_H):
        q = cols_ref[:, h * ML_DK:(h + 1) * ML_DK]
        k = cols_ref[:, 256 + h * ML_DK:256 + (h + 1) * ML_DK] * ML_DK ** -0.5
        v = cols_ref[:, 512 + h * ML_DV:512 + (h + 1) * ML_DV]
        i_pre, f_log = g[:, h:h + 1], lf[:, ML_H + h:ML_H + h + 1]
        m = m_ref[:, h:h + 1]
        inter = f_log + m
        m_t = jnp.maximum(inter, i_pre)
        w_inter = jnp.exp(inter - m_t)
        kw = k * jnp.exp(i_pre - m_t)
        q_scr[h], kw_scr[h], v_scr[h] = q, kw, v
        f_scr[h] = jnp.broadcast_to(w_inter, (nb, LANES))
        hs.append((q, k, v, i_pre, m_t, w_inter, kw))

    def row(b, carry):
        hr = range(ML_H)
        kw_col = [_col(kw_scr[h, pl.ds(b, 1), :]) for h in hr]
        c = [c_ref[b, h] for h in hr]
        qc = [_row_of(_ein(q_scr[h], c[h]), b) for h in hr]
        for h in hr:
            qc_scr[h, pl.ds(b, 1), :] = qc[h]
            co_ref[b, h] = f_scr[h, pl.ds(b, 1), :] * c[h] + kw_col[h] * v_scr[h, pl.ds(b, 1), :]
        return carry

    lax.fori_loop(0, nb, row, 0)
    for h, (q, k, v, i_pre, m_t, w_inter, kw) in enumerate(hs):
        og = cols_ref[:, 1024 + h * ML_DV:1024 + (h + 1) * ML_DV]
        n = n_ref[:, h * ML_DK:(h + 1) * ML_DK]
        s = jnp.sum(q * k, -1, keepdims=True) * jnp.exp(i_pre - m_t)
        num = w_inter * qc_scr[h] + s * v
        den = w_inter * jnp.sum(q * n, -1, keepdims=True) + s
        hh = num / jnp.maximum(jnp.abs(den), jnp.exp(-m_t))
        no_ref[:, h * ML_DK:(h + 1) * ML_DK] = w_inter * n + kw
        mo_ref[:, h:h + 1] = m_t
        hn = hh * lax.rsqrt(jnp.mean(hh * hh, -1, keepdims=True) + NORM_EPS) * ng_ref[:, h * ML_DV:(h + 1) * ML_DV]
        h_ref[:, h * ML_DV:(h + 1) * ML_DV] = (hn * jax.nn.sigmoid(og)).astype(h_ref.dtype)


def _row_spec(width, col=0):
    return pl.BlockSpec((DEC_BLOCK, width), lambda i: (i, col))


def _layer_rows(l, width):
    return _layer_spec(l, (DEC_BLOCK, width), lambda i: (i, 0))


def _layer_state(l, h, dk, dv):
    return _layer_spec(l, (DEC_BLOCK, h, dk, dv), lambda i: (i, 0, 0, 0))


def _ignore_first_ref(body):
    def with_handed_on_buffer(_stacked_out_so_far, *refs):
        body(*refs)
    return with_handed_on_buffer


def _step_call(body, l, stacked_out, in_specs, out_specs, out_shapes, state_out_index, **kw):
    if stacked_out is None:
        return lambda *args: pl.pallas_call(body, in_specs=in_specs, out_specs=out_specs, out_shape=out_shapes, **kw)(*args)
    call = pl.pallas_call(_ignore_first_ref(body), in_specs=[pl.BlockSpec(memory_space=pl.ANY)] + in_specs,
                          out_specs=out_specs, out_shape=out_shapes, input_output_aliases={0: state_out_index}, **kw)
    return lambda *args: call(stacked_out, *args)


def _mlstm_step(cols, gb, ng, c0, n0, m0, l, c_out):
    bs = cols.shape[0]
    nb = DEC_BLOCK
    call = _step_call(
        _mlstm_step_body, l, c_out,
        in_specs=[_row_spec(W_ML), _layer_spec(l, (1, LANES)), _layer_spec(l, (1, GROUP)),
                  _layer_state(l, ML_H, ML_DK, ML_DV), _layer_rows(l, ML_H * ML_DK), _layer_rows(l, ML_H)],
        out_specs=[_row_spec(GROUP), _layer_state(l, ML_H, ML_DK, ML_DV), _row_spec(ML_H * ML_DK), _row_spec(ML_H)],
        out_shapes=[jax.ShapeDtypeStruct((bs, GROUP), bf16), jax.ShapeDtypeStruct(c0.shape, f32),
                    jax.ShapeDtypeStruct((bs, ML_H * ML_DK), f32), jax.ShapeDtypeStruct((bs, ML_H), f32)],
        state_out_index=1, grid=(bs // nb,),
        scratch_shapes=[pltpu.VMEM((ML_H, nb, ML_DK), f32), pltpu.VMEM((ML_H, nb, ML_DK), f32),
                        pltpu.VMEM((ML_H, nb, ML_DV), f32), pltpu.VMEM((ML_H, nb, LANES), f32),
                        pltpu.VMEM((ML_H, nb, ML_DV), f32)],
        compiler_params=_cparams(("parallel",)), name="mlstm_step")
    return call(cols, gb, ng, c0, n0, m0)


def _gla_step_body(cols_ref, gup_ref, gb_ref, ng_ref, s_ref, h_ref, so_ref, qe_scr, ea_scr, k_scr, v_scr, qs_scr):
    nb = DEC_BLOCK
    la = _log_sigmoid(_ein(cols_ref[:, 1536:1664], gup_ref[...]) + gb_ref[...]) / GLA_TAU
    ea = jnp.exp(la)
    hs = []
    for h in range(GLA_H):
        q = cols_ref[:, h * GLA_DK:(h + 1) * GLA_DK] * GLA_DK ** -0.5
        k = cols_ref[:, 256 + h * GLA_DK:256 + (h + 1) * GLA_DK]
        v = cols_ref[:, 512 + h * GLA_DV:512 + (h + 1) * GLA_DV]
        ea_h = ea[:, h * GLA_DK:(h + 1) * GLA_DK]
        qe_scr[h], ea_scr[h], k_scr[h], v_scr[h] = q * ea_h, ea_h, k, v
        hs.append((q, k, v))

    def row(b, carry):
        hr = range(GLA_H)
        ea_col = [_col(ea_scr[h, pl.ds(b, 1), :]) for h in hr]
        k_col = [_col(k_scr[h, pl.ds(b, 1), :]) for h in hr]
        s = [s_ref[b, h] for h in hr]
        qs = [_row_of(_ein(qe_scr[h], s[h]), b) for h in hr]
        for h in hr:
            qs_scr[h, pl.ds(b, 1), :] = qs[h]
            so_ref[b, h] = ea_col[h] * s[h] + k_col[h] * v_scr[h, pl.ds(b, 1), :]
        return carry

    lax.fori_loop(0, nb, row, 0)
    for h, (q, k, v) in enumerate(hs):
        og = cols_ref[:, 1024 + h * GLA_DV:1024 + (h + 1) * GLA_DV]
        o = qs_scr[h] + jnp.sum(q * k, -1, keepdims=True) * v
        on = o * lax.rsqrt(jnp.mean(o * o, -1, keepdims=True) + NORM_EPS) * ng_ref[:, h * GLA_DV:(h + 1) * GLA_DV]
        h_ref[:, h * GLA_DV:(h + 1) * GLA_DV] = (on * _silu(og)).astype(h_ref.dtype)


def _gla_step(cols, gup, gb, ng, s0, l, s_out):
    bs = cols.shape[0]
    nb = DEC_BLOCK
    call = _step_call(
        _gla_step_body, l, s_out,
        in_specs=[_row_spec(W_GLA, col=1), _layer_spec(l, (LANES, GLA_H * GLA_DK)), _layer_spec(l, (1, GLA_H * GLA_DK)),
                  _layer_spec(l, (1, GROUP)), _layer_state(l, GLA_H, GLA_DK, GLA_DV)],
        out_specs=[_row_spec(GROUP), _layer_state(l, GLA_H, GLA_DK, GLA_DV)],
        out_shapes=[jax.ShapeDtypeStruct((bs, GROUP), bf16), jax.ShapeDtypeStruct(s0.shape, f32)],
        state_out_index=1, grid=(bs // nb,),
        scratch_shapes=[pltpu.VMEM((GLA_H, nb, GLA_DK), f32)] * 3 + [pltpu.VMEM((GLA_H, nb, GLA_DV), f32)] * 2,
        compiler_params=_cparams(("parallel",)), name="gla_step")
    return call(cols, gup, gb, ng, s0)


def _rwkv_step_body(cols_ref, shift_ref, mu_ref, wup_ref, aup_ref, gup_ref, vec_ref, s_ref, h_ref, so_ref,
                    w_scr, b_scr, k_scr, kkrw_scr, v_scr, sk_scr, o_scr):
    nb = DEC_BLOCK
    vec = vec_ref[...]
    r, k2, v, lw, a, g, kk_raw = _rwkv_pre(cols_ref[:, RW_OFF:RW_OFF + W_RW], shift_ref[...], mu_ref[...], wup_ref[...], aup_ref[...],
                                           gup_ref[...], vec)
    hr = range(RW_H)
    sl = [slice(h * RW_N, (h + 1) * RW_N) for h in hr]
    kk = [_l2norm(kk_raw[:, sl[h]]) for h in hr]
    w = jnp.exp(lw)
    bv = jnp.concatenate(kk, axis=-1) * a
    w_scr[...], b_scr[...], k_scr[...] = w, bv, k2
    for h in hr:
        kkrw_scr[h, 0:nb, :], kkrw_scr[h, nb:2 * nb, :] = kk[h], r[:, sl[h]] * w[:, sl[h]]
        v_scr[h] = v[:, sl[h]]

    def row(b, carry):
        w_col, b_col, k_col = (_cols_of(scr[pl.ds(b, 1), :], RW_N) for scr in (w_scr, b_scr, k_scr))
        s = [s_ref[b, h] for h in hr]
        mv = [_ein(kkrw_scr[h], s[h]) for h in hr]
        sk = [_row_of(mv[h][0:nb, :], b) for h in hr]
        rws = [_row_of(mv[h][nb:2 * nb, :], b) for h in hr]
        for h in hr:
            so_ref[b, h] = w_col[h] * s[h] - b_col[h] * sk[h] + k_col[h] * v_scr[h, pl.ds(b, 1), :]
            sk_scr[h, pl.ds(b, 1), :] = sk[h]
            o_scr[h, pl.ds(b, 1), :] = rws[h]
        return carry

    lax.fori_loop(0, nb, row, 0)
    rb = [jnp.sum(r[:, sl[h]] * bv[:, sl[h]], -1, keepdims=True) for h in hr]
    rk = [jnp.sum(r[:, sl[h]] * k2[:, sl[h]], -1, keepdims=True) for h in hr]
    o = [o_scr[h] - rb[h] * sk_scr[h] + rk[h] * v[:, sl[h]] for h in hr]
    h_ref[...] = _rwkv_post(o, r, k2, v, g, vec).astype(h_ref.dtype)


def _rwkv_step(cols, shift0, mu, wup, aup, gup, vec, s0, l, s_out):
    bs = cols.shape[0]
    nb = DEC_BLOCK
    call = _step_call(
        _rwkv_step_body, l, s_out,
        in_specs=[_row_spec(W_B), _layer_rows(l, W_RW), _layer_spec(l, (1, W_RW)), _layer_spec(l, (LANES, GROUP)),
                  _layer_spec(l, (LANES, GROUP)), _layer_spec(l, (LANES, GROUP)), _layer_spec(l, (8, GROUP)),
                  _layer_state(l, RW_H, RW_N, RW_N)],
        out_specs=[_row_spec(GROUP), _layer_state(l, RW_H, RW_N, RW_N)],
        out_shapes=[jax.ShapeDtypeStruct((bs, GROUP), bf16), jax.ShapeDtypeStruct(s0.shape, f32)],
        state_out_index=1, grid=(bs // nb,),
        scratch_shapes=[pltpu.VMEM((nb, GROUP), f32)] * 3 + [pltpu.VMEM((RW_H, 2 * nb, RW_N), f32)]
        + [pltpu.VMEM((RW_H, nb, RW_N), f32)] * 3,
        compiler_params=_cparams(("parallel",)), name="rwkv_step")
    return call(cols, shift0, mu, wup, aup, gup, vec, s0)


def _gdn_step_body(cols_ref, buf_ref, cw_ref, alog_ref, dtb_ref, ng_ref, s_ref, h_ref, so_ref,
                   kq_scr, v_scr, sc_scr, qs_scr, vn_scr):
    nb = DEC_BLOCK
    conv = cols_ref[:, 0:DN_QKV] * cw_ref[DN_CONV - 1:DN_CONV, :]
    for w in range(DN_CONV - 1):
        conv = conv + buf_ref[:, w * DN_QKV:(w + 1) * DN_QKV] * cw_ref[w:w + 1, :]
    act = _silu(conv)
    beta_all, g_all = _gdn_gates(cols_ref[:, 2048:2176], alog_ref[...], dtb_ref[...])
    eg_all = jnp.exp(g_all)
    hs = []
    for h in range(DN_H):
        q = _l2norm(act[:, h * DN_D:(h + 1) * DN_D]) * DN_D ** -0.5
        k = _l2norm(act[:, GROUP + h * DN_D:GROUP + (h + 1) * DN_D])
        v = act[:, 2 * GROUP + h * DN_D:2 * GROUP + (h + 1) * DN_D]
        beta, eg = beta_all[:, h:h + 1], eg_all[:, DN_H + h:DN_H + h + 1]
        kq_scr[h, 0:nb, :], kq_scr[h, nb:2 * nb, :], v_scr[h] = k, q, v
        sc_scr[h] = jnp.where(_iota((nb, LANES), 1) == 0, beta, eg)
        hs.append((q, k, eg))

    def row(b, carry):
        hr = range(DN_H)
        k_col = [_col(kq_scr[h, pl.ds(b, 1), :]) for h in hr]
        s = [s_ref[b, h] for h in hr]
        sc = [sc_scr[h, pl.ds(b, 1), :] for h in hr]
        mv = [_ein(kq_scr[h], s[h]) for h in hr]
        ks = [_row_of(mv[h][0:nb, :], b) for h in hr]
        qs = [_row_of(mv[h][nb:2 * nb, :], b) for h in hr]
        for h in hr:
            beta, eg = sc[h][:, 0:1], sc[h][:, 1:2]
            v_new = beta * (v_scr[h, pl.ds(b, 1), :] - eg * ks[h])
            qs_scr[h, pl.ds(b, 1), :] = qs[h]
            vn_scr[h, pl.ds(b, 1), :] = v_new
            so_ref[b, h] = eg * s[h] + k_col[h] * v_new
        return carry

    lax.fori_loop(0, nb, row, 0)
    for h, (q, k, eg) in enumerate(hs):
        z = cols_ref[:, DN_QKV + h * DN_D:DN_QKV + (h + 1) * DN_D]
        o = eg * qs_scr[h] + jnp.sum(q * k, -1, keepdims=True) * vn_scr[h]
        on = o * lax.rsqrt(jnp.mean(o * o, -1, keepdims=True) + NORM_EPS) * ng_ref[...]
        h_ref[:, h * DN_D:(h + 1) * DN_D] = (on * _silu(z)).astype(h_ref.dtype)


def _gdn_step(cols, buf, cw, alog, dtb, ng, s0, l, s_out):
    bs = cols.shape[0]
    nb = DEC_BLOCK
    call = _step_call(
        _gdn_step_body, l, s_out,
        in_specs=[_row_spec(W_DN), _layer_rows(l, (DN_CONV - 1) * DN_QKV), _layer_spec(l, (8, DN_QKV)),
                  _layer_spec(l, (1, LANES)), _layer_spec(l, (1, LANES)), _layer_spec(l, (1, DN_D)),
                  _layer_state(l, DN_H, DN_D, DN_D)],
        out_specs=[_row_spec(GROUP), _layer_state(l, DN_H, DN_D, DN_D)],
        out_shapes=[jax.ShapeDtypeStruct((bs, GROUP), bf16), jax.ShapeDtypeStruct(s0.shape, f32)],
        state_out_index=1, grid=(bs // nb,),
        scratch_shapes=[pltpu.VMEM((DN_H, 2 * nb, DN_D), f32), pltpu.VMEM((DN_H, nb, DN_D), f32),
                        pltpu.VMEM((DN_H, nb, LANES), f32)] + [pltpu.VMEM((DN_H, nb, DN_D), f32)] * 2,
        compiler_params=_cparams(("parallel",)), name="gdn_step")
    return call(cols, buf, cw, alog, dtb, ng, s0)


def _pad_last(w, width):
    return jnp.pad(w, [(0, 0)] * (w.ndim - 1) + [(0, width - w.shape[-1])])


def _pad_rows(w, start, total):
    return jnp.pad(w, ((0, 0), (start, total - start - w.shape[1]), (0, 0)))


def _lane_rows(v, start):
    return jnp.pad(v, ((0, 0), (start, LANES - start - v.shape[1])))[:, None, :]


def kernel(x_prompt, x_sample, state_mlstm_c, state_mlstm_n, state_mlstm_m, state_gla, state_rwkv, state_rwkv_shift, state_dn, state_dn_conv, w_in, w_out, w_up, w_down, ln1_g, ln1_b, ln2_g, ln2_b, mlstm_gate_b, mlstm_norm_g, gla_gate_up, gla_gate_b, gla_norm_g, rwkv_mu, rwkv_w_up, rwkv_w0, rwkv_a_up, rwkv_a0, rwkv_g_up, rwkv_k_k, rwkv_k_a, rwkv_r_k, rwkv_norm_g, rwkv_norm_b, dn_conv_w, dn_a_log, dn_dt_bias, dn_norm_g):
    bp, t, _ = x_prompt.shape
    bs = x_sample.shape[0]
    xp = x_prompt.reshape(bp * t, D_MODEL)
    xs = x_sample.reshape(bs, D_MODEL)
    xpb, xsb = xp, xs
    tm_p, tm_s, tf = 512, bs, 1024

    o1, o2, o3 = N_ML, N_ML + N_GLA, N_ML + N_GLA + N_RW
    w_ab = [jnp.concatenate([_pad_last(w_in[:, :, 0:o1].astype(bf16), W_ML),
                             _pad_last(w_in[:, :, o1:o2].astype(bf16), W_GLA)], axis=-1),
            jnp.concatenate([_pad_last(w_in[:, :, o3:].astype(bf16), W_DN), w_in[:, :, o2:o3].astype(bf16)], axis=-1)]
    wo, wu, wd = w_out.astype(bf16), w_up.astype(bf16), w_down.astype(bf16)
    g1, b1, g2, b2 = ln1_g[:, None], ln1_b[:, None], ln2_g[:, None], ln2_b[:, None]
    ml_gb, ml_ng = _lane_rows(mlstm_gate_b, 0), mlstm_norm_g[:, None]
    gla_gup, gla_gb, gla_ng = _pad_rows(gla_gate_up, 0, LANES), gla_gate_b[:, None], gla_norm_g[:, None]
    rw_mu = rwkv_mu[:, None]
    rw_wup, rw_aup, rw_gup = _pad_rows(rwkv_w_up, 0, LANES), _pad_rows(rwkv_a_up, 32, LANES), _pad_rows(rwkv_g_up, 64, LANES)
    rw_vec = jnp.stack([rwkv_w0, rwkv_a0, rwkv_k_k, rwkv_k_a, rwkv_r_k, rwkv_norm_g, rwkv_norm_b,
                        jnp.zeros_like(rwkv_w0)], axis=1)
    dn_cw = _pad_rows(dn_conv_w, 0, 8)
    dn_alog, dn_dtb, dn_ng = _lane_rows(dn_a_log, DN_H), _lane_rows(dn_dt_bias, DN_H), dn_norm_g[:, None]
    st_ml_n = state_mlstm_n.reshape(DEPTH, bs, ML_H * ML_DK)
    st_dn_conv = state_dn_conv.reshape(DEPTH, bs, (DN_CONV - 1) * DN_QKV)

    outs_p, outs_s = [], []
    ml_c_s = gla_s = rw_s = dn_s = None
    for l in range(DEPTH):
        c_a, c_b = [_proj(xpb, w, l, tm_p) for w in w_ab]
        h_ml, mc, mn, mm = _mlstm_prompt(c_a, ml_gb, ml_ng, l, bp, t)
        h_gla, gs = _gla_prompt(c_a, gla_gup, gla_gb, gla_ng, l, bp, t)
        h_rw, rs = _rwkv_prompt(c_b, rw_mu, rw_wup, rw_aup, rw_gup, rw_vec, l, bp, t)
        h_dn, ds = _gdn_prompt(c_b, dn_cw, dn_alog, dn_dtb, dn_ng, l, bp, t)
        hs_p = [h.reshape(bp * t, GROUP) for h in (h_ml, h_gla, h_rw, h_dn)]
        x1, x1b = _outproj_ln(hs_p, wo, xp, g1, b1, l, tm_p)
        xp, xpb = _ffn_ln(x1, x1b, wu, wd, g2, b2, l, tm_p, tf)
        last_rows = c_b.reshape(bp, t, W_B)[:, t - (DN_CONV - 1):]
        outs_p.append((mc, mn, mm[:, :, 0], gs, rs, last_rows[:, -1, RW_OFF:], ds, last_rows[:, :, :DN_QKV]))

        c_a, c_b = [_proj(xsb, w, l, tm_s) for w in w_ab]
        h_ml, ml_c_s, mn, mm = _mlstm_step(c_a, ml_gb, ml_ng, state_mlstm_c, st_ml_n, state_mlstm_m, l, ml_c_s)
        h_gla, gla_s = _gla_step(c_a, gla_gup, gla_gb, gla_ng, state_gla, l, gla_s)
        h_rw, rw_s = _rwkv_step(c_b, state_rwkv_shift, rw_mu, rw_wup, rw_aup, rw_gup, rw_vec, state_rwkv, l, rw_s)
        h_dn, dn_s = _gdn_step(c_b, st_dn_conv, dn_cw, dn_alog, dn_dtb, dn_ng, state_dn, l, dn_s)
        x1, x1b = _outproj_ln((h_ml, h_gla, h_rw, h_dn), wo, xs, g1, b1, l, tm_s)
        xs, xsb = _ffn_ln(x1, x1b, wu, wd, g2, b2, l, tm_s, tf)
        outs_s.append((mn.reshape(bs, ML_H, ML_DK), mm, c_b[:, RW_OFF:],
                       jnp.concatenate([state_dn_conv[l][:, 1:], c_b[:, None, :DN_QKV]], axis=1)))

    (mlstm_c_p, mlstm_n_p, mlstm_m_p, gla_p, rwkv_p, rwkv_shift_p, dn_p, dn_conv_p) = [jnp.stack(z) for z in zip(*outs_p)]
    (mlstm_n_s, mlstm_m_s, rwkv_shift_s, dn_conv_s) = [jnp.stack(z) for z in zip(*outs_s)]
    return (xp.reshape(bp, t, D_MODEL), xs.reshape(bs, 1, D_MODEL), mlstm_c_p, ml_c_s, mlstm_n_p, mlstm_n_s,
            mlstm_m_p, mlstm_m_s, gla_p, gla_s, rwkv_p, rw_s, rwkv_shift_p, rwkv_shift_s, dn_p, dn_s,
            dn_conv_p, dn_conv_s)
```

```python
import functools

import jax
import jax.numpy as jnp
from jax import lax
from jax.experimental import pallas as pl
from jax.experimental.pallas import tpu as pltpu

f32 = jnp.float32
bf16 = jnp.bfloat16

D_MODEL = 2048
DEPTH = 4
GROUP = 512
D_FF = 4 * D_MODEL
CHUNK = 64
WIDE_CHUNK = 128
ML_H, ML_DK, ML_DV = 4, 64, 128
GLA_H, GLA_DK, GLA_DV, GLA_RANK, GLA_TAU = 4, 64, 128, 16, 16.0
RW_H, RW_N = 8, 64
RW_DECAY_SCALE = 0.606531
RW_GN_EPS = 64e-5
DN_H, DN_D, DN_CONV, DN_QKV = 4, 128, 4, 1536
N_ML, N_GLA, N_RW, N_DN = 1544, 1552, 1664, 2056
W_ML, W_GLA, W_RW, W_DN = 1664, 1664, 1664, 2176
W_A, W_B = W_ML + W_GLA, W_DN + W_RW
RW_OFF = W_DN
ALPHA = (2 * DEPTH) ** 0.25
LN_EPS = 1e-5
NORM_EPS = 1e-6
LANES = 128
DEC_BLOCK = 16
VMEM_LIMIT = 56 * 1024 * 1024
LN_SUBTILE = 256


def _ein(a, b):
    return jnp.dot(a.astype(bf16), b.astype(bf16), preferred_element_type=f32)


def _ein_nt(a, b):
    return lax.dot_general(a.astype(bf16), b.astype(bf16), (((1,), (1,)), ((), ())), preferred_element_type=f32)


def _ein_tn(a, b):
    return lax.dot_general(a.astype(bf16), b.astype(bf16), (((0,), (0,)), ((), ())), preferred_element_type=f32)


def _split3(x):
    hi = x.astype(bf16)
    r1 = x - hi.astype(f32)
    mid = r1.astype(bf16)
    return hi, mid, (r1 - mid.astype(f32)).astype(bf16)


def _select_dot(sel, x):
    s = sel.astype(bf16)
    hi, mid, lo = _split3(x)
    d = functools.partial(jnp.dot, preferred_element_type=f32)
    return d(s, hi) + d(s, mid) + d(s, lo)


def _select_dot_right(x, sel):
    s = sel.astype(bf16)
    hi, mid, lo = _split3(x)
    d = functools.partial(jnp.dot, preferred_element_type=f32)
    return d(hi, s) + d(mid, s) + d(lo, s)


def _select_dot_nt(sel, x):
    s = sel.astype(bf16)
    hi, mid, lo = _split3(x)
    d = functools.partial(lax.dot_general, dimension_numbers=(((1,), (1,)), ((), ())), preferred_element_type=f32)
    return d(s, hi) + d(s, mid) + d(s, lo)


def _log_sigmoid(x):
    return jnp.minimum(x, 0.0) - jnp.log1p(jnp.exp(-jnp.abs(x)))


def _softplus(x):
    return jnp.maximum(x, 0.0) + jnp.log1p(jnp.exp(-jnp.abs(x)))


def _silu(x):
    return x * jax.nn.sigmoid(x)


def _l2norm(x):
    return x * lax.rsqrt(jnp.sum(x * x, -1, keepdims=True) + NORM_EPS)


def _iota(shape, axis):
    return lax.broadcasted_iota(jnp.int32, shape, axis)


def _col(row):
    n = row.shape[1]
    eye = _iota((n, n), 0) == _iota((n, n), 1)
    return jnp.sum(jnp.where(eye, row, 0.0), axis=1, keepdims=True)


def _row_of(x, b):
    return jnp.sum(jnp.where(_iota(x.shape, 0) == b, x, 0.0), axis=0, keepdims=True)


def _cols_of(row, n, width=None):
    width = n if width is None else width
    tiles = []
    for g in range(row.shape[1] // LANES):
        t = jnp.broadcast_to(row[:, g * LANES:(g + 1) * LANES], (LANES, LANES)).T
        tiles += [t[j * n:(j + 1) * n, 0:width] for j in range(LANES // n)]
    return tiles


def _tri_masks(n):
    r, c = _iota((n, n), 0), _iota((n, n), 1)
    return c <= r, c < r


def _split2(a):
    hi = a.astype(bf16)
    return hi, (a - hi.astype(f32)).astype(bf16)


def _dot3(a, b):
    ah, al = _split2(a)
    bh, bl = _split2(b)
    d = functools.partial(jnp.dot, preferred_element_type=f32)
    return d(ah, bh) + d(ah, bl) + d(al, bh)


def _unit_lower_inverses(a_list):
    n = a_list[0].shape[0]
    eye = jnp.where(_iota((n, n), 0) == _iota((n, n), 1), 1.0, 0.0)
    xs = [-a for a in a_list]
    ps = [eye + x for x in xs]
    for stage in range(max(n.bit_length() - 2, 0)):
        mm = _dot3 if stage == 0 else _ein
        xs = [mm(x, x) for x in xs]
        ps = [p + mm(p, x) for p, x in zip(ps, xs)]
    return ps


def _rows_of(y):
    e8 = jnp.where(_iota((8, LANES), 0) == _iota((8, LANES), 1), 1.0, 0.0)
    return _select_dot_nt(e8, y)


def _cparams(sem):
    return pltpu.CompilerParams(dimension_semantics=sem, vmem_limit_bytes=VMEM_LIMIT)


def _layer_spec(l, shape, index=None):
    nd = len(shape)
    if index is None:
        return pl.BlockSpec((None,) + tuple(shape), lambda *g: (l,) + (0,) * nd)
    return pl.BlockSpec((None,) + tuple(shape), lambda *g: (l,) + tuple(index(*g)))


def _proj_body(x_ref, w_ref, o_ref):
    o_ref[...] = jnp.dot(x_ref[...].astype(bf16), w_ref[...], preferred_element_type=f32)


def _proj(xb, w, l, tm):
    m, k = xb.shape
    n = w.shape[2]
    return pl.pallas_call(
        _proj_body, grid=(m // tm,),
        in_specs=[pl.BlockSpec((tm, k), lambda i: (i, 0)),
                  pl.BlockSpec((None, k, n), lambda i: (l, 0, 0), pipeline_mode=pl.Buffered(1))],
        out_specs=pl.BlockSpec((tm, n), lambda i: (i, 0)),
        out_shape=jax.ShapeDtypeStruct((m, n), f32),
        compiler_params=_cparams(("parallel",)), name="proj")(xb, w)


def _layernorm(y, g, b):
    mu = jnp.mean(y, -1, keepdims=True)
    yc = y - mu
    var = jnp.mean(yc * yc, -1, keepdims=True)
    return yc * lax.rsqrt(var + LN_EPS) * g + b


def _outproj_ln_body(h0, h1, h2, h3, w_ref, x_ref, g_ref, b_ref, o_ref, ob_ref):
    sub = min(LN_SUBTILE, x_ref.shape[0])
    for r in range(x_ref.shape[0] // sub):
        rows = slice(r * sub, (r + 1) * sub)
        mix = jnp.concatenate([h[rows, :] for h in (h0, h1, h2, h3)], axis=1)
        acc = ALPHA * x_ref[rows, :] + jnp.dot(mix, w_ref[...], preferred_element_type=f32)
        y = _layernorm(acc, g_ref[...], b_ref[...])
        o_ref[rows, :] = y
        ob_ref[rows, :] = y.astype(bf16)


def _outproj_ln(hs, w, x, g, b, l, tm):
    m = x.shape[0]
    hspec = pl.BlockSpec((tm, GROUP), lambda i: (i, 0))
    xspec = pl.BlockSpec((tm, D_MODEL), lambda i: (i, 0))
    vspec = _layer_spec(l, (1, D_MODEL))
    return pl.pallas_call(
        _outproj_ln_body, grid=(m // tm,),
        in_specs=[hspec] * 4 + [_layer_spec(l, (D_MODEL, D_MODEL)), xspec, vspec, vspec],
        out_specs=[xspec, xspec],
        out_shape=[jax.ShapeDtypeStruct((m, D_MODEL), f32), jax.ShapeDtypeStruct((m, D_MODEL), bf16)],
        compiler_params=_cparams(("parallel",)), name="outproj_ln")(*hs, w, x, g, b)


def _ffn_ln_body(x_ref, xb_ref, wu_ref, wd_ref, g_ref, b_ref, o_ref, ob_ref, acc_ref):
    f = pl.program_id(1)

    @pl.when(f == 0)
    def _():
        acc_ref[...] = ALPHA * x_ref[...]

    h = jnp.maximum(jnp.dot(xb_ref[...], wu_ref[...], preferred_element_type=f32), 0.0)
    acc_ref[...] += jnp.dot((h * h).astype(bf16), wd_ref[...], preferred_element_type=f32)

    @pl.when(f == pl.num_programs(1) - 1)
    def _():
        y = _layernorm(acc_ref[...], g_ref[...], b_ref[...])
        o_ref[...] = y
        ob_ref[...] = y.astype(bf16)


def _ffn_ln(x, xb, wu, wd, g, b, l, tm, tf):
    m = x.shape[0]
    xspec = pl.BlockSpec((tm, D_MODEL), lambda i, f: (i, 0))
    vspec = _layer_spec(l, (1, D_MODEL))
    return pl.pallas_call(
        _ffn_ln_body, grid=(m // tm, D_FF // tf),
        in_specs=[xspec, xspec, _layer_spec(l, (D_MODEL, tf), lambda i, f: (0, f)),
                  _layer_spec(l, (tf, D_MODEL), lambda i, f: (f, 0)), vspec, vspec],
        out_specs=[xspec, xspec],
        out_shape=[jax.ShapeDtypeStruct((m, D_MODEL), f32), jax.ShapeDtypeStruct((m, D_MODEL), bf16)],
        scratch_shapes=[pltpu.VMEM((tm, D_MODEL), f32)],
        compiler_params=_cparams(("parallel", "arbitrary")), name="ffn_ln")(x, xb, wu, wd, g, b)


def _tril_blocks(nb, L):
    r, c = _iota((nb * L, nb * L), 0), _iota((nb * L, nb * L), 1)
    return jnp.where((c <= r) & (c >= r - (r & (L - 1))), 1.0, 0.0)


def _units(nb, heads):
    return [(b, h) for b in range(nb) for h in range(heads)]


def _mlstm_prompt_body(cols_ref, gb_ref, ng_ref, h_ref, c_ref, n_ref, m_ref):
    @pl.when(pl.program_id(0) == 0)
    def _():
        c_ref[...] = jnp.zeros_like(c_ref)
        n_ref[...] = jnp.zeros_like(n_ref)
        m_ref[...] = jnp.zeros_like(m_ref)

    nb, L = cols_ref.shape[0], cols_ref.shape[1]
    causal, _ = _tri_masks(L)
    g = cols_ref[:, :, 1536:1664].reshape(nb * L, LANES) + gb_ref[...]
    bc = _select_dot(_tril_blocks(nb, L), _log_sigmoid(g))
    y_all = jnp.where(_iota((nb * L, LANES), 1) < ML_H, g, bc)
    y = [y_all[b * L:(b + 1) * L, :] for b in range(nb)]
    yt = [_rows_of(y[b]) for b in range(nb)]
    us = _units(nb, ML_H)
    ix = range(len(us))
    q = [cols_ref[b, :, h * ML_DK:(h + 1) * ML_DK] for b, h in us]
    k = [cols_ref[b, :, 256 + h * ML_DK:256 + (h + 1) * ML_DK] * ML_DK ** -0.5 for b, h in us]
    v = [cols_ref[b, :, 512 + h * ML_DV:512 + (h + 1) * ML_DV] for b, h in us]
    c = [c_ref[b, h] for b, h in us]
    n = [n_ref[b, h:h + 1, :] for b, h in us]
    m = [m_ref[b, h:h + 1, :] for b, h in us]
    qk = [_ein_nt(q[i], k[i]) for i in ix]
    qc = [_ein(q[i], c[i]) for i in ix]
    qn = [_ein_nt(q[i], jnp.broadcast_to(n[i], (LANES, ML_DK))) for i in ix]
    spread_shape = (LANES, 2 * ML_H * LANES)
    spread = jnp.where(_iota(spread_shape, 0) == jnp.right_shift(_iota(spread_shape, 1), LANES.bit_length() - 1), 1.0, 0.0)
    rep = [_select_dot_right(y[b], spread) for b in range(nb)]
    i_rep = [rep[b][:, h * LANES:(h + 1) * LANES] for b, h in us]
    b_rep = [rep[b][:, (ML_H + h) * LANES:(ML_H + h + 1) * LANES] for b, h in us]
    d = [jnp.where(causal, b_rep[i][:, :L] - yt[b][ML_H + h:ML_H + h + 1, :] + yt[b][h:h + 1, :], -jnp.inf)
         for i, (b, h) in enumerate(us)]
    inter = [b_rep[i] + m[i] for i in ix]
    m_t = [jnp.maximum(inter[i], jnp.max(d[i], axis=-1, keepdims=True)) for i in ix]
    w_inter = [jnp.exp(inter[i] - m_t[i]) for i in ix]
    s = [qk[i] * jnp.exp(d[i] - m_t[i][:, :L]) for i in ix]
    m_new = [m_t[i][L - 1:L, :] for i in ix]
    b_last = [b_rep[i][L - 1:L, :] for i in ix]
    kw = [k[i] * jnp.exp(b_last[i] - b_rep[i] + i_rep[i] - m_new[i])[:, :ML_DK] for i in ix]
    sv = [_ein(s[i], v[i]) for i in ix]
    kv = [_ein_tn(kw[i], v[i]) for i in ix]
    s_sum = [jnp.sum(s[i], -1, keepdims=True) for i in ix]
    f_state = [jnp.exp(b_last[i] + m[i] - m_new[i]) for i in ix]
    c_new = [f_state[i] * c[i] + kv[i] for i in ix]
    n_new = [f_state[i][:, :ML_DK] * n[i] + jnp.sum(kw[i], axis=0, keepdims=True) for i in ix]
    for i, (b, h) in enumerate(us):
        c_ref[b, h] = c_new[i]
        n_ref[b, h:h + 1, :] = n_new[i]
        m_ref[b, h:h + 1, :] = m_new[i]
    hh = [(w_inter[i] * qc[i] + sv[i]) / jnp.maximum(jnp.abs(w_inter[i] * qn[i] + s_sum[i]), jnp.exp(-m_t[i])) for i in ix]
    ms = [jnp.mean(hh[i] * hh[i], -1, keepdims=True) for i in ix]
    for i, (b, h) in enumerate(us):
        og = cols_ref[b, :, 1024 + h * ML_DV:1024 + (h + 1) * ML_DV]
        hn = hh[i] * lax.rsqrt(ms[i] + NORM_EPS) * ng_ref[:, h * ML_DV:(h + 1) * ML_DV]
        h_ref[b, :, h * ML_DV:(h + 1) * ML_DV] = (hn * jax.nn.sigmoid(og)).astype(h_ref.dtype)


def _chunk_grid_specs(bp, t, width, col=0, chunk=CHUNK):
    cols = pl.BlockSpec((bp, chunk, width), lambda c: (0, c, col))
    out = pl.BlockSpec((bp, chunk, GROUP), lambda c: (0, c, 0))
    return (t // chunk,), cols, out


def _whole(shape):
    return pl.BlockSpec(shape, lambda c: (0,) * len(shape))


def _mlstm_prompt(cols, gb, ng, l, bp, t):
    grid, cspec, ospec = _chunk_grid_specs(bp, t, W_ML, chunk=WIDE_CHUNK)
    return pl.pallas_call(
        _mlstm_prompt_body, grid=grid,
        in_specs=[cspec, _layer_spec(l, (1, LANES)), _layer_spec(l, (1, GROUP))],
        out_specs=[ospec, _whole((bp, ML_H, ML_DK, ML_DV)), _whole((bp, ML_H, ML_DK)), _whole((bp, ML_H, LANES))],
        out_shape=[jax.ShapeDtypeStruct((bp, t, GROUP), bf16), jax.ShapeDtypeStruct((bp, ML_H, ML_DK, ML_DV), f32),
                   jax.ShapeDtypeStruct((bp, ML_H, ML_DK), f32), jax.ShapeDtypeStruct((bp, ML_H, LANES), f32)],
        compiler_params=_cparams(("arbitrary",)), name="mlstm_prompt")(cols.reshape(bp, t, W_A), gb, ng)


def _gla_prompt_body(cols_ref, gup_ref, gb_ref, ng_ref, h_ref, s_ref):
    @pl.when(pl.program_id(0) == 0)
    def _():
        s_ref[...] = jnp.zeros_like(s_ref)

    nb, L = cols_ref.shape[0], cols_ref.shape[1]
    causal, _ = _tri_masks(L)
    lr = cols_ref[:, :, 1536:1664].reshape(nb * L, LANES)
    la = _log_sigmoid(_ein(lr, gup_ref[...]) + gb_ref[...]) / GLA_TAU
    bc_all = _select_dot(_tril_blocks(nb, L), la)
    bc = [bc_all[b * L:(b + 1) * L, :] for b in range(nb)]
    mid = [bc[b][L // 2:L // 2 + 1, :] for b in range(nb)]
    last = [bc[b][L - 1:L, :] for b in range(nb)]
    e_q_mid = [jnp.exp(bc[b] - mid[b]) for b in range(nb)]
    e_k_mid = [jnp.exp(mid[b] - bc[b]) for b in range(nb)]
    e_q = [jnp.exp(bc[b]) for b in range(nb)]
    e_k_last = [jnp.exp(last[b] - bc[b]) for b in range(nb)]
    e_last = [jnp.exp(last[b]) for b in range(nb)]
    us = _units(nb, GLA_H)
    ix = range(len(us))
    sl = [slice(h * GLA_DK, (h + 1) * GLA_DK) for _, h in us]
    q = [cols_ref[b, :, h * GLA_DK:(h + 1) * GLA_DK] * GLA_DK ** -0.5 for b, h in us]
    k = [cols_ref[b, :, 256 + h * GLA_DK:256 + (h + 1) * GLA_DK] for b, h in us]
    v = [cols_ref[b, :, 512 + h * GLA_DV:512 + (h + 1) * GLA_DV] for b, h in us]
    s = [s_ref[b, h] for b, h in us]
    a = [jnp.where(causal, _ein_nt(q[i] * e_q_mid[b][:, sl[i]], k[i] * e_k_mid[b][:, sl[i]]), 0.0)
         for i, (b, _) in enumerate(us)]
    o0 = [_ein(q[i] * e_q[b][:, sl[i]], s[i]) for i, (b, _) in enumerate(us)]
    kv = [_ein_tn(k[i] * e_k_last[b][:, sl[i]], v[i]) for i, (b, _) in enumerate(us)]
    o = [o0[i] + _ein(a[i], v[i]) for i in ix]
    for i, (b, h) in enumerate(us):
        s_ref[b, h] = _col(e_last[b][:, sl[i]]) * s[i] + kv[i]
    ms = [jnp.mean(o[i] * o[i], -1, keepdims=True) for i in ix]
    for i, (b, h) in enumerate(us):
        og = cols_ref[b, :, 1024 + h * GLA_DV:1024 + (h + 1) * GLA_DV]
        on = o[i] * lax.rsqrt(ms[i] + NORM_EPS) * ng_ref[:, h * GLA_DV:(h + 1) * GLA_DV]
        h_ref[b, :, h * GLA_DV:(h + 1) * GLA_DV] = (on * _silu(og)).astype(h_ref.dtype)


def _gla_prompt(cols, gup, gb, ng, l, bp, t):
    grid, cspec, ospec = _chunk_grid_specs(bp, t, W_GLA, col=1, chunk=WIDE_CHUNK)
    return pl.pallas_call(
        _gla_prompt_body, grid=grid,
        in_specs=[cspec, _layer_spec(l, (LANES, GLA_H * GLA_DK)), _layer_spec(l, (1, GLA_H * GLA_DK)),
                  _layer_spec(l, (1, GROUP))],
        out_specs=[ospec, _whole((bp, GLA_H, GLA_DK, GLA_DV))],
        out_shape=[jax.ShapeDtypeStruct((bp, t, GROUP), bf16), jax.ShapeDtypeStruct((bp, GLA_H, GLA_DK, GLA_DV), f32)],
        compiler_params=_cparams(("arbitrary",)), name="gla_prompt")(cols.reshape(bp, t, W_A), gup, gb, ng)


def _rwkv_pre(x, prev, mu, wup, aup, gup, vec):
    xs = x + (prev - x) * mu
    r, k, v, lb = xs[:, 0:512], xs[:, 512:1024], xs[:, 1024:1536], xs[:, 1536:1664]
    w0, a0, k_k, k_a = vec[0:1, :], vec[1:2, :], vec[2:3, :], vec[3:4, :]
    lw = -RW_DECAY_SCALE * jax.nn.sigmoid(w0 + _ein(jnp.tanh(lb), wup))
    a = jax.nn.sigmoid(a0 + _ein(lb, aup))
    g = _ein(jax.nn.sigmoid(lb), gup)
    kk_raw = k * k_k
    k2 = k * (1.0 + (a - 1.0) * k_a)
    return r, k2, v, lw, a, g, kk_raw


def _rwkv_post(o, r, k2, v, g, vec):
    hs = range(RW_H)
    sl = [slice(h * RW_N, (h + 1) * RW_N) for h in hs]
    mu_o = [jnp.mean(o[h], -1, keepdims=True) for h in hs]
    bonus_w = [jnp.sum(r[:, sl[h]] * k2[:, sl[h]] * vec[4:5, sl[h]], -1, keepdims=True) for h in hs]
    oc = [o[h] - mu_o[h] for h in hs]
    var_o = [jnp.mean(oc[h] * oc[h], -1, keepdims=True) for h in hs]
    outs = [(oc[h] * lax.rsqrt(var_o[h] + RW_GN_EPS) * vec[5:6, sl[h]] + vec[6:7, sl[h]]
             + bonus_w[h] * v[:, sl[h]]) * g[:, sl[h]] for h in hs]
    return jnp.concatenate(outs, axis=-1)


def _rwkv_prompt_body(cols_ref, mu_ref, wup_ref, aup_ref, gup_ref, vec_ref, h_ref, s_ref, prev_scr):
    @pl.when(pl.program_id(0) == 0)
    def _():
        s_ref[...] = jnp.zeros_like(s_ref)
        prev_scr[...] = jnp.zeros_like(prev_scr)

    nb, L, N = cols_ref.shape[0], CHUNK, RW_N
    first_row = _iota((L, W_RW), 0) == 0
    xb = [cols_ref[b, :, RW_OFF:RW_OFF + W_RW] for b in range(nb)]
    prevs = [jnp.where(first_row, prev_scr[b, 0:1, :], pltpu.roll(xb[b], 1, 0)) for b in range(nb)]
    for b in range(nb):
        prev_scr[b, 0:1, :] = xb[b][L - 1:L, :]
    vec = vec_ref[...]
    r, k2, v, lw, a, g, kk_raw = _rwkv_pre(jnp.concatenate(xb, axis=0), jnp.concatenate(prevs, axis=0), mu_ref[...],
                                           wup_ref[...], aup_ref[...], gup_ref[...], vec)
    cum = _select_dot(_tril_blocks(nb, L), lw)
    cum_prev = cum - lw
    rep = lambda row_of: jnp.concatenate(
        [jnp.broadcast_to(cum[b * L + row_of:b * L + row_of + 1, :], (L, RW_H * N)) for b in range(nb)], axis=0)
    mid, last = rep(L // 2), rep(L - 1)
    e_prev_mid, e_mid_cum, e_cum_mid = jnp.exp(cum_prev - mid), jnp.exp(mid - cum), jnp.exp(cum - mid)
    e_prev, e_cum, e_last_cum, e_last = jnp.exp(cum_prev), jnp.exp(cum), jnp.exp(last - cum), jnp.exp(last)
    row, col = _iota((2 * L, 2 * L), 0), _iota((2 * L, 2 * L), 1)
    tq, sq = row % L, col % L
    quad = sq < tq + jnp.where(row < L, 0, 1)
    left = col[:, :] < L
    hs = range(RW_H)
    sl = [slice(h * N, (h + 1) * N) for h in hs]
    kk_h = [_l2norm(kk_raw[:, sl[h]]) for h in hs]
    b_h = [kk_h[h] * a[:, sl[h]] for h in hs]
    f_lhs = [(kk_h[h] * e_prev_mid[:, sl[h]], r[:, sl[h]] * e_cum_mid[:, sl[h]]) for h in hs]
    f_rhs = [(b_h[h] * e_mid_cum[:, sl[h]], k2[:, sl[h]] * e_mid_cum[:, sl[h]]) for h in hs]
    f_x0 = [(kk_h[h] * e_prev[:, sl[h]], r[:, sl[h]] * e_cum[:, sl[h]]) for h in hs]
    f_dec = [(b_h[h] * e_last_cum[:, sl[h]], k2[:, sl[h]] * e_last_cum[:, sl[h]]) for h in hs]
    us = _units(nb, RW_H)
    ix = range(len(us))
    rows = lambda z, b: z[b * L:(b + 1) * L, :]
    pair = lambda f, b, h: jnp.concatenate([rows(f[h][0], b), rows(f[h][1], b)], axis=0)
    vh = [rows(v[:, sl[h]], b) for b, h in us]
    s = [s_ref[b, h] for b, h in us]
    a_all = [jnp.where(quad, _ein_nt(pair(f_lhs, b, h), pair(f_rhs, b, h)), 0.0) for b, h in us]
    t_inv = _unit_lower_inverses([a_all[i][0:L, 0:L] for i in ix])
    x0 = [pair(f_x0, b, h) for b, h in us]
    m2 = [jnp.where(left, jnp.concatenate([x0[i], x0[i]], axis=1), a_all[i]) for i in ix]
    y = [_ein(m2[i], jnp.concatenate([s[i], vh[i]], axis=0)) for i in ix]
    u = [-_dot3(t_inv[i], y[i][0:L, :]) for i in ix]
    o = [y[i][L:2 * L, :] + _ein(a_all[i][L:2 * L, 0:L], u[i]) for i in ix]
    for i, (b, h) in enumerate(us):
        s_ref[b, h] = (_col(e_last[b * L:b * L + 1, sl[h]]) * s[i]
                       + _ein_tn(pair(f_dec, b, h), jnp.concatenate([u[i], vh[i]], axis=0)))
    o_heads = [jnp.concatenate([o[b * RW_H + h] for b in range(nb)], axis=0) for h in hs]
    h_ref[...] = _rwkv_post(o_heads, r, k2, v, g, vec).reshape(nb, L, GROUP).astype(h_ref.dtype)


def _rwkv_prompt(cols, mu, wup, aup, gup, vec, l, bp, t):
    grid, cspec, ospec = _chunk_grid_specs(bp, t, W_B)
    return pl.pallas_call(
        _rwkv_prompt_body, grid=grid,
        in_specs=[cspec, _layer_spec(l, (1, W_RW)), _layer_spec(l, (LANES, GROUP)), _layer_spec(l, (LANES, GROUP)),
                  _layer_spec(l, (LANES, GROUP)), _layer_spec(l, (8, GROUP))],
        out_specs=[ospec, _whole((bp, RW_H, RW_N, RW_N))],
        out_shape=[jax.ShapeDtypeStruct((bp, t, GROUP), bf16), jax.ShapeDtypeStruct((bp, RW_H, RW_N, RW_N), f32)],
        scratch_shapes=[pltpu.VMEM((bp, 8, W_RW), f32)],
        compiler_params=_cparams(("arbitrary",)), name="rwkv_prompt")(cols.reshape(bp, t, W_B), mu, wup, aup, gup, vec)


def _gdn_gates(gb, alog, dtb):
    return jax.nn.sigmoid(gb), -jnp.exp(alog) * _softplus(gb + dtb)


def _gdn_prompt_body(cols_ref, cw_ref, alog_ref, dtb_ref, ng_ref, h_ref, s_ref, xp_scr):
    @pl.when(pl.program_id(0) == 0)
    def _():
        s_ref[...] = jnp.zeros_like(s_ref)
        xp_scr[...] = jnp.zeros_like(xp_scr)

    nb, L = cols_ref.shape[0], CHUNK
    causal, strict = _tri_masks(L)
    act = []
    for b in range(nb):
        raw = cols_ref[b, :, 0:DN_QKV]
        ext = jnp.concatenate([xp_scr[b], raw], axis=0)
        conv = raw * cw_ref[DN_CONV - 1:DN_CONV, :]
        for j in range(1, DN_CONV):
            conv = conv + pltpu.roll(ext, j, 0)[8:8 + L, :] * cw_ref[DN_CONV - 1 - j:DN_CONV - j, :]
        xp_scr[b] = raw[L - 8:L, :]
        act.append(_silu(conv))
    beta_all, g_all = _gdn_gates(cols_ref[:, :, 2048:2176].reshape(nb * L, LANES), alog_ref[...], dtb_ref[...])
    gc_all = _select_dot(_tril_blocks(nb, L), g_all)
    gc = [gc_all[b * L:(b + 1) * L, :] for b in range(nb)]
    gt = [_rows_of(gc[b]) for b in range(nb)]
    us = _units(nb, DN_H)
    ix = range(len(us))
    q = [_l2norm(act[b][:, h * DN_D:(h + 1) * DN_D]) * DN_D ** -0.5 for b, h in us]
    k = [_l2norm(act[b][:, GROUP + h * DN_D:GROUP + (h + 1) * DN_D]) for b, h in us]
    v = [act[b][:, 2 * GROUP + h * DN_D:2 * GROUP + (h + 1) * DN_D] for b, h in us]
    beta = [beta_all[b * L:(b + 1) * L, h:h + 1] for b, h in us]
    g_col = [gc[b][:, DN_H + h:DN_H + h + 1] for b, h in us]
    decay = [jnp.exp(jnp.where(causal, g_col[i] - gt[b][DN_H + h:DN_H + h + 1, :], -jnp.inf))
             for i, (b, h) in enumerate(us)]
    kb = [k[i] * beta[i] for i in ix]
    eg = [jnp.exp(g_col[i]) for i in ix]
    s = [s_ref[b, h] for b, h in us]
    kq = [_ein_nt(jnp.concatenate([kb[i], q[i]], axis=0), k[i]) for i in ix]
    t_inv = _unit_lower_inverses([jnp.where(strict, kq[i][0:L, :] * decay[i], 0.0) for i in ix])
    qk = [kq[i][L:2 * L, :] * decay[i] for i in ix]
    o0 = [_ein(q[i] * eg[i], s[i]) for i in ix]
    sol = [_dot3(t_inv[i], jnp.concatenate([v[i] * beta[i], kb[i] * eg[i]], axis=-1)) for i in ix]
    v_new = [sol[i][:, :DN_D] - _ein(sol[i][:, DN_D:], s[i]) for i in ix]
    o = [o0[i] + _ein(qk[i], v_new[i]) for i in ix]
    for i, (b, h) in enumerate(us):
        g_last = g_col[i][L - 1:L, :]
        s_ref[b, h] = jnp.exp(g_last) * s[i] + _ein_tn(k[i] * jnp.exp(g_last - g_col[i]), v_new[i])
    ms = [jnp.mean(o[i] * o[i], -1, keepdims=True) for i in ix]
    for i, (b, h) in enumerate(us):
        z = cols_ref[b, :, DN_QKV + h * DN_D:DN_QKV + (h + 1) * DN_D]
        on = o[i] * lax.rsqrt(ms[i] + NORM_EPS) * ng_ref[...]
        h_ref[b, :, h * DN_D:(h + 1) * DN_D] = (on * _silu(z)).astype(h_ref.dtype)


def _gdn_prompt(cols, cw, alog, dtb, ng, l, bp, t):
    grid, cspec, ospec = _chunk_grid_specs(bp, t, W_DN)
    return pl.pallas_call(
        _gdn_prompt_body, grid=grid,
        in_specs=[cspec, _layer_spec(l, (8, DN_QKV)), _layer_spec(l, (1, LANES)), _layer_spec(l, (1, LANES)),
                  _layer_spec(l, (1, DN_D))],
        out_specs=[ospec, _whole((bp, DN_H, DN_D, DN_D))],
        out_shape=[jax.ShapeDtypeStruct((bp, t, GROUP), bf16), jax.ShapeDtypeStruct((bp, DN_H, DN_D, DN_D), f32)],
        scratch_shapes=[pltpu.VMEM((bp, 8, DN_QKV), f32)],
        compiler_params=_cparams(("arbitrary",)), name="gdn_prompt")(cols.reshape(bp, t, W_B), cw, alog, dtb, ng)


def _mlstm_step_body(cols_ref, gb_ref, ng_ref, c_ref, n_ref, m_ref, h_ref, co_ref, no_ref, mo_ref,
                     q_scr, kw_scr, v_scr, f_scr, qc_scr):
    nb = DEC_BLOCK
    g = cols_ref[:, 1536:1664] + gb_ref[...]
    lf = _log_sigmoid(g)
    hs = []
    for h in range(ML_H):
        q = cols_ref[:, h * ML_DK:(h + 1) * ML_DK]
        k = cols_ref[:, 256 + h * ML_DK:256 + (h + 1) * ML_DK] * ML_DK ** -0.5
        v = cols_ref[:, 512 + h * ML_DV:512 + (h + 1) * ML_DV]
        i_pre, f_log = g[:, h:h + 1], lf[:, ML_H + h:ML_H + h + 1]
        m = m_ref[:, h:h + 1]
        inter = f_log + m
        m_t = jnp.maximum(inter, i_pre)
        w_inter = jnp.exp(inter - m_t)
        kw = k * jnp.exp(i_pre - m_t)
        q_scr[h], kw_scr[h], v_scr[h] = q, kw, v
        f_scr[h] = jnp.broadcast_to(w_inter, (nb, LANES))
        hs.append((q, k, v, i_pre, m_t, w_inter, kw))

    def row(b, carry):
        hr = range(ML_H)
        kw_col = [_col(kw_scr[h, pl.ds(b, 1), :]) for h in hr]
        c = [c_ref[b, h] for h in hr]
        qc = [_row_of(_ein(q_scr[h], c[h]), b) for h in hr]
        for h in hr:
            qc_scr[h, pl.ds(b, 1), :] = qc[h]
            co_ref[b, h] = f_scr[h, pl.ds(b, 1), :] * c[h] + kw_col[h] * v_scr[h, pl.ds(b, 1), :]
        return carry

    lax.fori_loop(0, nb, row, 0)
    for h, (q, k, v, i_pre, m_t, w_inter, kw) in enumerate(hs):
        og = cols_ref[:, 1024 + h * ML_DV:1024 + (h + 1) * ML_DV]
        n = n_ref[:, h * ML_DK:(h + 1) * ML_DK]
        s = jnp.sum(q * k, -1, keepdims=True) * jnp.exp(i_pre - m_t)
        num = w_inter * qc_scr[h] + s * v
        den = w_inter * jnp.sum(q * n, -1, keepdims=True) + s
        hh = num / jnp.maximum(jnp.abs(den), jnp.exp(-m_t))
        no_ref[:, h * ML_DK:(h + 1) * ML_DK] = w_inter * n + kw
        mo_ref[:, h:h + 1] = m_t
        hn = hh * lax.rsqrt(jnp.mean(hh * hh, -1, keepdims=True) + NORM_EPS) * ng_ref[:, h * ML_DV:(h + 1) * ML_DV]
        h_ref[:, h * ML_DV:(h + 1) * ML_DV] = (hn * jax.nn.sigmoid(og)).astype(h_ref.dtype)


def _row_spec(width, col=0):
    return pl.BlockSpec((DEC_BLOCK, width), lambda i: (i, col))


def _layer_rows(l, width):
    return _layer_spec(l, (DEC_BLOCK, width), lambda i: (i, 0))


def _layer_state(l, h, dk, dv):
    return _layer_spec(l, (DEC_BLOCK, h, dk, dv), lambda i: (i, 0, 0, 0))


def _ignore_first_ref(body):
    def with_handed_on_buffer(_stacked_out_so_far, *refs):
        body(*refs)
    return with_handed_on_buffer


def _step_call(body, l, stacked_out, in_specs, out_specs, out_shapes, state_out_index, **kw):
    if stacked_out is None:
        return lambda *args: pl.pallas_call(body, in_specs=in_specs, out_specs=out_specs, out_shape=out_shapes, **kw)(*args)
    call = pl.pallas_call(_ignore_first_ref(body), in_specs=[pl.BlockSpec(memory_space=pl.ANY)] + in_specs,
                          out_specs=out_specs, out_shape=out_shapes, input_output_aliases={0: state_out_index}, **kw)
    return lambda *args: call(stacked_out, *args)


def _mlstm_step(cols, gb, ng, c0, n0, m0, l, c_out):
    bs = cols.shape[0]
    nb = DEC_BLOCK
    call = _step_call(
        _mlstm_step_body, l, c_out,
        in_specs=[_row_spec(W_ML), _layer_spec(l, (1, LANES)), _layer_spec(l, (1, GROUP)),
                  _layer_state(l, ML_H, ML_DK, ML_DV), _layer_rows(l, ML_H * ML_DK), _layer_rows(l, ML_H)],
        out_specs=[_row_spec(GROUP), _layer_state(l, ML_H, ML_DK, ML_DV), _row_spec(ML_H * ML_DK), _row_spec(ML_H)],
        out_shapes=[jax.ShapeDtypeStruct((bs, GROUP), bf16), jax.ShapeDtypeStruct(c0.shape, f32),
                    jax.ShapeDtypeStruct((bs, ML_H * ML_DK), f32), jax.ShapeDtypeStruct((bs, ML_H), f32)],
        state_out_index=1, grid=(bs // nb,),
        scratch_shapes=[pltpu.VMEM((ML_H, nb, ML_DK), f32), pltpu.VMEM((ML_H, nb, ML_DK), f32),
                        pltpu.VMEM((ML_H, nb, ML_DV), f32), pltpu.VMEM((ML_H, nb, LANES), f32),
                        pltpu.VMEM((ML_H, nb, ML_DV), f32)],
        compiler_params=_cparams(("parallel",)), name="mlstm_step")
    return call(cols, gb, ng, c0, n0, m0)


def _gla_step_body(cols_ref, gup_ref, gb_ref, ng_ref, s_ref, h_ref, so_ref, qe_scr, ea_scr, k_scr, v_scr, qs_scr):
    nb = DEC_BLOCK
    la = _log_sigmoid(_ein(cols_ref[:, 1536:1664], gup_ref[...]) + gb_ref[...]) / GLA_TAU
    ea = jnp.exp(la)
    hs = []
    for h in range(GLA_H):
        q = cols_ref[:, h * GLA_DK:(h + 1) * GLA_DK] * GLA_DK ** -0.5
        k = cols_ref[:, 256 + h * GLA_DK:256 + (h + 1) * GLA_DK]
        v = cols_ref[:, 512 + h * GLA_DV:512 + (h + 1) * GLA_DV]
        ea_h = ea[:, h * GLA_DK:(h + 1) * GLA_DK]
        qe_scr[h], ea_scr[h], k_scr[h], v_scr[h] = q * ea_h, ea_h, k, v
        hs.append((q, k, v))

    def row(b, carry):
        hr = range(GLA_H)
        ea_col = [_col(ea_scr[h, pl.ds(b, 1), :]) for h in hr]
        k_col = [_col(k_scr[h, pl.ds(b, 1), :]) for h in hr]
        s = [s_ref[b, h] for h in hr]
        qs = [_row_of(_ein(qe_scr[h], s[h]), b) for h in hr]
        for h in hr:
            qs_scr[h, pl.ds(b, 1), :] = qs[h]
            so_ref[b, h] = ea_col[h] * s[h] + k_col[h] * v_scr[h, pl.ds(b, 1), :]
        return carry

    lax.fori_loop(0, nb, row, 0)
    for h, (q, k, v) in enumerate(hs):
        og = cols_ref[:, 1024 + h * GLA_DV:1024 + (h + 1) * GLA_DV]
        o = qs_scr[h] + jnp.sum(q * k, -1, keepdims=True) * v
        on = o * lax.rsqrt(jnp.mean(o * o, -1, keepdims=True) + NORM_EPS) * ng_ref[:, h * GLA_DV:(h + 1) * GLA_DV]
        h_ref[:, h * GLA_DV:(h + 1) * GLA_DV] = (on * _silu(og)).astype(h_ref.dtype)


def _gla_step(cols, gup, gb, ng, s0, l, s_out):
    bs = cols.shape[0]
    nb = DEC_BLOCK
    call = _step_call(
        _gla_step_body, l, s_out,
        in_specs=[_row_spec(W_GLA, col=1), _layer_spec(l, (LANES, GLA_H * GLA_DK)), _layer_spec(l, (1, GLA_H * GLA_DK)),
                  _layer_spec(l, (1, GROUP)), _layer_state(l, GLA_H, GLA_DK, GLA_DV)],
        out_specs=[_row_spec(GROUP), _layer_state(l, GLA_H, GLA_DK, GLA_DV)],
        out_shapes=[jax.ShapeDtypeStruct((bs, GROUP), bf16), jax.ShapeDtypeStruct(s0.shape, f32)],
        state_out_index=1, grid=(bs // nb,),
        scratch_shapes=[pltpu.VMEM((GLA_H, nb, GLA_DK), f32)] * 3 + [pltpu.VMEM((GLA_H, nb, GLA_DV), f32)] * 2,
        compiler_params=_cparams(("parallel",)), name="gla_step")
    return call(cols, gup, gb, ng, s0)


def _rwkv_step_body(cols_ref, shift_ref, mu_ref, wup_ref, aup_ref, gup_ref, vec_ref, s_ref, h_ref, so_ref,
                    w_scr, b_scr, k_scr, kkrw_scr, v_scr, sk_scr, o_scr):
    nb = DEC_BLOCK
    vec = vec_ref[...]
    r, k2, v, lw, a, g, kk_raw = _rwkv_pre(cols_ref[:, RW_OFF:RW_OFF + W_RW], shift_ref[...], mu_ref[...], wup_ref[...], aup_ref[...],
                                           gup_ref[...], vec)
    hr = range(RW_H)
    sl = [slice(h * RW_N, (h + 1) * RW_N) for h in hr]
    kk = [_l2norm(kk_raw[:, sl[h]]) for h in hr]
    w = jnp.exp(lw)
    bv = jnp.concatenate(kk, axis=-1) * a
    w_scr[...], b_scr[...], k_scr[...] = w, bv, k2
    for h in hr:
        kkrw_scr[h, 0:nb, :], kkrw_scr[h, nb:2 * nb, :] = kk[h], r[:, sl[h]] * w[:, sl[h]]
        v_scr[h] = v[:, sl[h]]

    def row(b, carry):
        w_col, b_col, k_col = (_cols_of(scr[pl.ds(b, 1), :], RW_N) for scr in (w_scr, b_scr, k_scr))
        s = [s_ref[b, h] for h in hr]
        mv = [_ein(kkrw_scr[h], s[h]) for h in hr]
        sk = [_row_of(mv[h][0:nb, :], b) for h in hr]
        rws = [_row_of(mv[h][nb:2 * nb, :], b) for h in hr]
        for h in hr:
            so_ref[b, h] = w_col[h] * s[h] - b_col[h] * sk[h] + k_col[h] * v_scr[h, pl.ds(b, 1), :]
            sk_scr[h, pl.ds(b, 1), :] = sk[h]
            o_scr[h, pl.ds(b, 1), :] = rws[h]
        return carry

    lax.fori_loop(0, nb, row, 0)
    rb = [jnp.sum(r[:, sl[h]] * bv[:, sl[h]], -1, keepdims=True) for h in hr]
    rk = [jnp.sum(r[:, sl[h]] * k2[:, sl[h]], -1, keepdims=True) for h in hr]
    o = [o_scr[h] - rb[h] * sk_scr[h] + rk[h] * v[:, sl[h]] for h in hr]
    h_ref[...] = _rwkv_post(o, r, k2, v, g, vec).astype(h_ref.dtype)


def _rwkv_step(cols, shift0, mu, wup, aup, gup, vec, s0, l, s_out):
    bs = cols.shape[0]
    nb = DEC_BLOCK
    call = _step_call(
        _rwkv_step_body, l, s_out,
        in_specs=[_row_spec(W_B), _layer_rows(l, W_RW), _layer_spec(l, (1, W_RW)), _layer_spec(l, (LANES, GROUP)),
                  _layer_spec(l, (LANES, GROUP)), _layer_spec(l, (LANES, GROUP)), _layer_spec(l, (8, GROUP)),
                  _layer_state(l, RW_H, RW_N, RW_N)],
        out_specs=[_row_spec(GROUP), _layer_state(l, RW_H, RW_N, RW_N)],
        out_shapes=[jax.ShapeDtypeStruct((bs, GROUP), bf16), jax.ShapeDtypeStruct(s0.shape, f32)],
        state_out_index=1, grid=(bs // nb,),
        scratch_shapes=[pltpu.VMEM((nb, GROUP), f32)] * 3 + [pltpu.VMEM((RW_H, 2 * nb, RW_N), f32)]
        + [pltpu.VMEM((RW_H, nb, RW_N), f32)] * 3,
        compiler_params=_cparams(("parallel",)), name="rwkv_step")
    return call(cols, shift0, mu, wup, aup, gup, vec, s0)


def _gdn_step_body(cols_ref, buf_ref, cw_ref, alog_ref, dtb_ref, ng_ref, s_ref, h_ref, so_ref,
                   kq_scr, v_scr, sc_scr, qs_scr, vn_scr):
    nb = DEC_BLOCK
    conv = cols_ref[:, 0:DN_QKV] * cw_ref[DN_CONV - 1:DN_CONV, :]
    for w in range(DN_CONV - 1):
        conv = conv + buf_ref[:, w * DN_QKV:(w + 1) * DN_QKV] * cw_ref[w:w + 1, :]
    act = _silu(conv)
    beta_all, g_all = _gdn_gates(cols_ref[:, 2048:2176], alog_ref[...], dtb_ref[...])
    eg_all = jnp.exp(g_all)
    hs = []
    for h in range(DN_H):
        q = _l2norm(act[:, h * DN_D:(h + 1) * DN_D]) * DN_D ** -0.5
        k = _l2norm(act[:, GROUP + h * DN_D:GROUP + (h + 1) * DN_D])
        v = act[:, 2 * GROUP + h * DN_D:2 * GROUP + (h + 1) * DN_D]
        beta, eg = beta_all[:, h:h + 1], eg_all[:, DN_H + h:DN_H + h + 1]
        kq_scr[h, 0:nb, :], kq_scr[h, nb:2 * nb, :], v_scr[h] = k, q, v
        sc_scr[h] = jnp.where(_iota((nb, LANES), 1) == 0, beta, eg)
        hs.append((q, k, eg))

    def row(b, carry):
        hr = range(DN_H)
        k_col = [_col(kq_scr[h, pl.ds(b, 1), :]) for h in hr]
        s = [s_ref[b, h] for h in hr]
        sc = [sc_scr[h, pl.ds(b, 1), :] for h in hr]
        mv = [_ein(kq_scr[h], s[h]) for h in hr]
        ks = [_row_of(mv[h][0:nb, :], b) for h in hr]
        qs = [_row_of(mv[h][nb:2 * nb, :], b) for h in hr]
        for h in hr:
            beta, eg = sc[h][:, 0:1], sc[h][:, 1:2]
            v_new = beta * (v_scr[h, pl.ds(b, 1), :] - eg * ks[h])
            qs_scr[h, pl.ds(b, 1), :] = qs[h]
            vn_scr[h, pl.ds(b, 1), :] = v_new
            so_ref[b, h] = eg * s[h] + k_col[h] * v_new
        return carry

    lax.fori_loop(0, nb, row, 0)
    for h, (q, k, eg) in enumerate(hs):
        z = cols_ref[:, DN_QKV + h * DN_D:DN_QKV + (h + 1) * DN_D]
        o = eg * qs_scr[h] + jnp.sum(q * k, -1, keepdims=True) * vn_scr[h]
        on = o * lax.rsqrt(jnp.mean(o * o, -1, keepdims=True) + NORM_EPS) * ng_ref[...]
        h_ref[:, h * DN_D:(h + 1) * DN_D] = (on * _silu(z)).astype(h_ref.dtype)


def _gdn_step(cols, buf, cw, alog, dtb, ng, s0, l, s_out):
    bs = cols.shape[0]
    nb = DEC_BLOCK
    call = _step_call(
        _gdn_step_body, l, s_out,
        in_specs=[_row_spec(W_DN), _layer_rows(l, (DN_CONV - 1) * DN_QKV), _layer_spec(l, (8, DN_QKV)),
                  _layer_spec(l, (1, LANES)), _layer_spec(l, (1, LANES)), _layer_spec(l, (1, DN_D)),
                  _layer_state(l, DN_H, DN_D, DN_D)],
        out_specs=[_row_spec(GROUP), _layer_state(l, DN_H, DN_D, DN_D)],
        out_shapes=[jax.ShapeDtypeStruct((bs, GROUP), bf16), jax.ShapeDtypeStruct(s0.shape, f32)],
        state_out_index=1, grid=(bs // nb,),
        scratch_shapes=[pltpu.VMEM((DN_H, 2 * nb, DN_D), f32), pltpu.VMEM((DN_H, nb, DN_D), f32),
                        pltpu.VMEM((DN_H, nb, LANES), f32)] + [pltpu.VMEM((DN_H, nb, DN_D), f32)] * 2,
        compiler_params=_cparams(("parallel",)), name="gdn_step")
    return call(cols, buf, cw, alog, dtb, ng, s0)


def _pad_last(w, width):
    return jnp.pad(w, [(0, 0)] * (w.ndim - 1) + [(0, width - w.shape[-1])])


def _pad_rows(w, start, total):
    return jnp.pad(w, ((0, 0), (start, total - start - w.shape[1]), (0, 0)))


def _lane_rows(v, start):
    return jnp.pad(v, ((0, 0), (start, LANES - start - v.shape[1])))[:, None, :]


def kernel(x_prompt, x_sample, state_mlstm_c, state_mlstm_n, state_mlstm_m, state_gla, state_rwkv, state_rwkv_shift, state_dn, state_dn_conv, w_in, w_out, w_up, w_down, ln1_g, ln1_b, ln2_g, ln2_b, mlstm_gate_b, mlstm_norm_g, gla_gate_up, gla_gate_b, gla_norm_g, rwkv_mu, rwkv_w_up, rwkv_w0, rwkv_a_up, rwkv_a0, rwkv_g_up, rwkv_k_k, rwkv_k_a, rwkv_r_k, rwkv_norm_g, rwkv_norm_b, dn_conv_w, dn_a_log, dn_dt_bias, dn_norm_g):
    bp, t, _ = x_prompt.shape
    bs = x_sample.shape[0]
    xp = x_prompt.reshape(bp * t, D_MODEL)
    xs = x_sample.reshape(bs, D_MODEL)
    xpb, xsb = xp, xs
    tm_p, tm_s, tf = 512, bs, 1024

    o1, o2, o3 = N_ML, N_ML + N_GLA, N_ML + N_GLA + N_RW
    w_ab = [jnp.concatenate([_pad_last(w_in[:, :, 0:o1].astype(bf16), W_ML),
                             _pad_last(w_in[:, :, o1:o2].astype(bf16), W_GLA)], axis=-1),
            jnp.concatenate([_pad_last(w_in[:, :, o3:].astype(bf16), W_DN), w_in[:, :, o2:o3].astype(bf16)], axis=-1)]
    wo, wu, wd = w_out.astype(bf16), w_up.astype(bf16), w_down.astype(bf16)
    g1, b1, g2, b2 = ln1_g[:, None], ln1_b[:, None], ln2_g[:, None], ln2_b[:, None]
    ml_gb, ml_ng = _lane_rows(mlstm_gate_b, 0), mlstm_norm_g[:, None]
    gla_gup, gla_gb, gla_ng = _pad_rows(gla_gate_up, 0, LANES), gla_gate_b[:, None], gla_norm_g[:, None]
    rw_mu = rwkv_mu[:, None]
    rw_wup, rw_aup, rw_gup = _pad_rows(rwkv_w_up, 0, LANES), _pad_rows(rwkv_a_up, 32, LANES), _pad_rows(rwkv_g_up, 64, LANES)
    rw_vec = jnp.stack([rwkv_w0, rwkv_a0, rwkv_k_k, rwkv_k_a, rwkv_r_k, rwkv_norm_g, rwkv_norm_b,
                        jnp.zeros_like(rwkv_w0)], axis=1)
    dn_cw = _pad_rows(dn_conv_w, 0, 8)
    dn_alog, dn_dtb, dn_ng = _lane_rows(dn_a_log, DN_H), _lane_rows(dn_dt_bias, DN_H), dn_norm_g[:, None]
    st_ml_n = state_mlstm_n.reshape(DEPTH, bs, ML_H * ML_DK)
    st_dn_conv = state_dn_conv.reshape(DEPTH, bs, (DN_CONV - 1) * DN_QKV)

    outs_p, outs_s = [], []
    ml_c_s = gla_s = rw_s = dn_s = None
    for l in range(DEPTH):
        c_a, c_b = [_proj(xpb, w, l, tm_p) for w in w_ab]
        h_ml, mc, mn, mm = _mlstm_prompt(c_a, ml_gb, ml_ng, l, bp, t)
        h_gla, gs = _gla_prompt(c_a, gla_gup, gla_gb, gla_ng, l, bp, t)
        h_rw, rs = _rwkv_prompt(c_b, rw_mu, rw_wup, rw_aup, rw_gup, rw_vec, l, bp, t)
        h_dn, ds = _gdn_prompt(c_b, dn_cw, dn_alog, dn_dtb, dn_ng, l, bp, t)
        hs_p = [h.reshape(bp * t, GROUP) for h in (h_ml, h_gla, h_rw, h_dn)]
        x1, x1b = _outproj_ln(hs_p, wo, xp, g1, b1, l, tm_p)
        xp, xpb = _ffn_ln(x1, x1b, wu, wd, g2, b2, l, tm_p, tf)
        last_rows = c_b.reshape(bp, t, W_B)[:, t - (DN_CONV - 1):]
        outs_p.append((mc, mn, mm[:, :, 0], gs, rs, last_rows[:, -1, RW_OFF:], ds, last_rows[:, :, :DN_QKV]))

        c_a, c_b = [_proj(xsb, w, l, tm_s) for w in w_ab]
        h_ml, ml_c_s, mn, mm = _mlstm_step(c_a, ml_gb, ml_ng, state_mlstm_c, st_ml_n, state_mlstm_m, l, ml_c_s)
        h_gla, gla_s = _gla_step(c_a, gla_gup, gla_gb, gla_ng, state_gla, l, gla_s)
        h_rw, rw_s = _rwkv_step(c_b, state_rwkv_shift, rw_mu, rw_wup, rw_aup, rw_gup, rw_vec, state_rwkv, l, rw_s)
        h_dn, dn_s = _gdn_step(c_b, st_dn_conv, dn_cw, dn_alog, dn_dtb, dn_ng, state_dn, l, dn_s)
        x1, x1b = _outproj_ln((h_ml, h_gla, h_rw, h_dn), wo, xs, g1, b1, l, tm_s)
        xs, xsb = _ffn_ln(x1, x1b, wu, wd, g2, b2, l, tm_s, tf)
        outs_s.append((mn.reshape(bs, ML_H, ML_DK), mm, c_b[:, RW_OFF:],
                       jnp.concatenate([state_dn_conv[l][:, 1:], c_b[:, None, :DN_QKV]], axis=1)))

    (mlstm_c_p, mlstm_n_p, mlstm_m_p, gla_p, rwkv_p, rwkv_shift_p, dn_p, dn_conv_p) = [jnp.stack(z) for z in zip(*outs_p)]
    (mlstm_n_s, mlstm_m_s, rwkv_shift_s, dn_conv_s) = [jnp.stack(z) for z in zip(*outs_s)]
    return (xp.reshape(bp, t, D_MODEL), xs.reshape(bs, 1, D_MODEL), mlstm_c_p, ml_c_s, mlstm_n_p, mlstm_n_s,
            mlstm_m_p, mlstm_m_s, gla_p, gla_s, rwkv_p, rw_s, rwkv_shift_p, rwkv_shift_s, dn_p, dn_s,
            dn_conv_p, dn_conv_s)
```

```python
import functools

import jax
import jax.numpy as jnp
from jax import lax
from jax.experimental import pallas as pl
from jax.experimental.pallas import tpu as pltpu

f32 = jnp.float32
bf16 = jnp.bfloat16

D_MODEL = 2048
DEPTH = 4
GROUP = 512
D_FF = 4 * D_MODEL
CHUNK = 64
WIDE_CHUNK = 128
ML_H, ML_DK, ML_DV = 4, 64, 128
GLA_H, GLA_DK, GLA_DV, GLA_RANK, GLA_TAU = 4, 64, 128, 16, 16.0
RW_H, RW_N = 8, 64
RW_DECAY_SCALE = 0.606531
RW_GN_EPS = 64e-5
DN_H, DN_D, DN_CONV, DN_QKV = 4, 128, 4, 1536
N_ML, N_GLA, N_RW, N_DN = 1544, 1552, 1664, 2056
W_ML, W_GLA, W_RW, W_DN = 1664, 1664, 1664, 2176
W_A, W_B = W_ML + W_GLA, W_DN + W_RW
RW_OFF = W_DN
ALPHA = (2 * DEPTH) ** 0.25
LN_EPS = 1e-5
NORM_EPS = 1e-6
LANES = 128
DEC_BLOCK = 16
VMEM_LIMIT = 56 * 1024 * 1024
LN_SUBTILE = 256


def _ein(a, b):
    return jnp.dot(a.astype(bf16), b.astype(bf16), preferred_element_type=f32)


def _ein_nt(a, b):
    return lax.dot_general(a.astype(bf16), b.astype(bf16), (((1,), (1,)), ((), ())), preferred_element_type=f32)


def _ein_tn(a, b):
    return lax.dot_general(a.astype(bf16), b.astype(bf16), (((0,), (0,)), ((), ())), preferred_element_type=f32)


def _split3(x):
    hi = x.astype(bf16)
    r1 = x - hi.astype(f32)
    mid = r1.astype(bf16)
    return hi, mid, (r1 - mid.astype(f32)).astype(bf16)


def _select_dot(sel, x):
    s = sel.astype(bf16)
    hi, mid, lo = _split3(x)
    d = functools.partial(jnp.dot, preferred_element_type=f32)
    return d(s, hi) + d(s, mid) + d(s, lo)


def _select_dot_right(x, sel):
    s = sel.astype(bf16)
    hi, mid, lo = _split3(x)
    d = functools.partial(jnp.dot, preferred_element_type=f32)
    return d(hi, s) + d(mid, s) + d(lo, s)


def _select_dot_nt(sel, x):
    s = sel.astype(bf16)
    hi, mid, lo = _split3(x)
    d = functools.partial(lax.dot_general, dimension_numbers=(((1,), (1,)), ((), ())), preferred_element_type=f32)
    return d(s, hi) + d(s, mid) + d(s, lo)


def _log_sigmoid(x):
    return jnp.minimum(x, 0.0) - jnp.log1p(jnp.exp(-jnp.abs(x)))


def _softplus(x):
    return jnp.maximum(x, 0.0) + jnp.log1p(jnp.exp(-jnp.abs(x)))


def _silu(x):
    return x * jax.nn.sigmoid(x)


def _l2norm(x):
    return x * lax.rsqrt(jnp.sum(x * x, -1, keepdims=True) + NORM_EPS)


def _iota(shape, axis):
    return lax.broadcasted_iota(jnp.int32, shape, axis)


def _col(row):
    n = row.shape[1]
    eye = _iota((n, n), 0) == _iota((n, n), 1)
    return jnp.sum(jnp.where(eye, row, 0.0), axis=1, keepdims=True)


def _row_of(x, b):
    return jnp.sum(jnp.where(_iota(x.shape, 0) == b, x, 0.0), axis=0, keepdims=True)


def _cols_of(row, n, width=None):
    width = n if width is None else width
    tiles = []
    for g in range(row.shape[1] // LANES):
        t = jnp.broadcast_to(row[:, g * LANES:(g + 1) * LANES], (LANES, LANES)).T
        tiles += [t[j * n:(j + 1) * n, 0:width] for j in range(LANES // n)]
    return tiles


def _tri_masks(n):
    r, c = _iota((n, n), 0), _iota((n, n), 1)
    return c <= r, c < r


def _split2(a):
    hi = a.astype(bf16)
    return hi, (a - hi.astype(f32)).astype(bf16)


def _dot3(a, b):
    ah, al = _split2(a)
    bh, bl = _split2(b)
    d = functools.partial(jnp.dot, preferred_element_type=f32)
    return d(ah, bh) + d(ah, bl) + d(al, bh)


def _unit_lower_inverses(a_list):
    n = a_list[0].shape[0]
    eye = jnp.where(_iota((n, n), 0) == _iota((n, n), 1), 1.0, 0.0)
    xs = [-a for a in a_list]
    ps = [eye + x for x in xs]
    for stage in range(max(n.bit_length() - 2, 0)):
        mm = _dot3 if stage == 0 else _ein
        xs = [mm(x, x) for x in xs]
        ps = [p + mm(p, x) for p, x in zip(ps, xs)]
    return ps


def _rows_of(y):
    e8 = jnp.where(_iota((8, LANES), 0) == _iota((8, LANES), 1), 1.0, 0.0)
    return _select_dot_nt(e8, y)


def _cparams(sem):
    return pltpu.CompilerParams(dimension_semantics=sem, vmem_limit_bytes=VMEM_LIMIT)


def _layer_spec(l, shape, index=None):
    nd = len(shape)
    if index is None:
        return pl.BlockSpec((None,) + tuple(shape), lambda *g: (l,) + (0,) * nd)
    return pl.BlockSpec((None,) + tuple(shape), lambda *g: (l,) + tuple(index(*g)))


def _proj_body(x_ref, w_ref, o_ref):
    o_ref[...] = jnp.dot(x_ref[...].astype(bf16), w_ref[...], preferred_element_type=f32)


def _proj(xb, w, l, tm):
    m, k = xb.shape
    n = w.shape[2]
    return pl.pallas_call(
        _proj_body, grid=(m // tm,),
        in_specs=[pl.BlockSpec((tm, k), lambda i: (i, 0)),
                  pl.BlockSpec((None, k, n), lambda i: (l, 0, 0), pipeline_mode=pl.Buffered(1))],
        out_specs=pl.BlockSpec((tm, n), lambda i: (i, 0)),
        out_shape=jax.ShapeDtypeStruct((m, n), f32),
        compiler_params=_cparams(("parallel",)), name="proj")(xb, w)


def _layernorm(y, g, b):
    mu = jnp.mean(y, -1, keepdims=True)
    yc = y - mu
    var = jnp.mean(yc * yc, -1, keepdims=True)
    return yc * lax.rsqrt(var + LN_EPS) * g + b


def _outproj_ln_body(h0, h1, h2, h3, w_ref, x_ref, g_ref, b_ref, o_ref, ob_ref):
    sub = min(LN_SUBTILE, x_ref.shape[0])
    for r in range(x_ref.shape[0] // sub):
        rows = slice(r * sub, (r + 1) * sub)
        mix = jnp.concatenate([h[rows, :] for h in (h0, h1, h2, h3)], axis=1)
        acc = ALPHA * x_ref[rows, :] + jnp.dot(mix, w_ref[...], preferred_element_type=f32)
        y = _layernorm(acc, g_ref[...], b_ref[...])
        o_ref[rows, :] = y
        ob_ref[rows, :] = y.astype(bf16)


def _outproj_ln(hs, w, x, g, b, l, tm):
    m = x.shape[0]
    hspec = pl.BlockSpec((tm, GROUP), lambda i: (i, 0))
    xspec = pl.BlockSpec((tm, D_MODEL), lambda i: (i, 0))
    vspec = _layer_spec(l, (1, D_MODEL))
    return pl.pallas_call(
        _outproj_ln_body, grid=(m // tm,),
        in_specs=[hspec] * 4 + [_layer_spec(l, (D_MODEL, D_MODEL)), xspec, vspec, vspec],
        out_specs=[xspec, xspec],
        out_shape=[jax.ShapeDtypeStruct((m, D_MODEL), f32), jax.ShapeDtypeStruct((m, D_MODEL), bf16)],
        compiler_params=_cparams(("parallel",)), name="outproj_ln")(*hs, w, x, g, b)


def _ffn_ln_body(x_ref, xb_ref, wu_ref, wd_ref, g_ref, b_ref, o_ref, ob_ref, acc_ref):
    f = pl.program_id(1)

    @pl.when(f == 0)
    def _():
        acc_ref[...] = ALPHA * x_ref[...]

    h = jnp.maximum(jnp.dot(xb_ref[...], wu_ref[...], preferred_element_type=f32), 0.0)
    acc_ref[...] += jnp.dot((h * h).astype(bf16), wd_ref[...], preferred_element_type=f32)

    @pl.when(f == pl.num_programs(1) - 1)
    def _():
        y = _layernorm(acc_ref[...], g_ref[...], b_ref[...])
        o_ref[...] = y
        ob_ref[...] = y.astype(bf16)


def _ffn_ln(x, xb, wu, wd, g, b, l, tm, tf):
    m = x.shape[0]
    xspec = pl.BlockSpec((tm, D_MODEL), lambda i, f: (i, 0))
    vspec = _layer_spec(l, (1, D_MODEL))
    return pl.pallas_call(
        _ffn_ln_body, grid=(m // tm, D_FF // tf),
        in_specs=[xspec, xspec, _layer_spec(l, (D_MODEL, tf), lambda i, f: (0, f)),
                  _layer_spec(l, (tf, D_MODEL), lambda i, f: (f, 0)), vspec, vspec],
        out_specs=[xspec, xspec],
        out_shape=[jax.ShapeDtypeStruct((m, D_MODEL), f32), jax.ShapeDtypeStruct((m, D_MODEL), bf16)],
        scratch_shapes=[pltpu.VMEM((tm, D_MODEL), f32)],
        compiler_params=_cparams(("parallel", "arbitrary")), name="ffn_ln")(x, xb, wu, wd, g, b)


def _tril_blocks(nb, L):
    r, c = _iota((nb * L, nb * L), 0), _iota((nb * L, nb * L), 1)
    return jnp.where((c <= r) & (c >= r - (r & (L - 1))), 1.0, 0.0)


def _units(nb, heads):
    return [(b, h) for b in range(nb) for h in range(heads)]


def _mlstm_prompt_body(cols_ref, gb_ref, ng_ref, h_ref, c_ref, n_ref, m_ref):
    @pl.when(pl.program_id(0) == 0)
    def _():
        c_ref[...] = jnp.zeros_like(c_ref)
        n_ref[...] = jnp.zeros_like(n_ref)
        m_ref[...] = jnp.zeros_like(m_ref)

    nb, L = cols_ref.shape[0], cols_ref.shape[1]
    causal, _ = _tri_masks(L)
    g = cols_ref[:, :, 1536:1664].reshape(nb * L, LANES) + gb_ref[...]
    bc = _select_dot(_tril_blocks(nb, L), _log_sigmoid(g))
    y_all = jnp.where(_iota((nb * L, LANES), 1) < ML_H, g, bc)
    y = [y_all[b * L:(b + 1) * L, :] for b in range(nb)]
    yt = [_rows_of(y[b]) for b in range(nb)]
    us = _units(nb, ML_H)
    ix = range(len(us))
    q = [cols_ref[b, :, h * ML_DK:(h + 1) * ML_DK] for b, h in us]
    k = [cols_ref[b, :, 256 + h * ML_DK:256 + (h + 1) * ML_DK] * ML_DK ** -0.5 for b, h in us]
    v = [cols_ref[b, :, 512 + h * ML_DV:512 + (h + 1) * ML_DV] for b, h in us]
    c = [c_ref[b, h] for b, h in us]
    n = [n_ref[b, h:h + 1, :] for b, h in us]
    m = [m_ref[b, h:h + 1, :] for b, h in us]
    qk = [_ein_nt(q[i], k[i]) for i in ix]
    qc = [_ein(q[i], c[i]) for i in ix]
    qn = [_ein_nt(q[i], jnp.broadcast_to(n[i], (LANES, ML_DK))) for i in ix]
    spread_shape = (LANES, 2 * ML_H * LANES)
    spread = jnp.where(_iota(spread_shape, 0) == jnp.right_shift(_iota(spread_shape, 1), LANES.bit_length() - 1), 1.0, 0.0)
    rep = [_select_dot_right(y[b], spread) for b in range(nb)]
    i_rep = [rep[b][:, h * LANES:(h + 1) * LANES] for b, h in us]
    b_rep = [rep[b][:, (ML_H + h) * LANES:(ML_H + h + 1) * LANES] for b, h in us]
    d = [jnp.where(causal, b_rep[i][:, :L] - yt[b][ML_H + h:ML_H + h + 1, :] + yt[b][h:h + 1, :], -jnp.inf)
         for i, (b, h) in enumerate(us)]
    inter = [b_rep[i] + m[i] for i in ix]
    m_t = [jnp.maximum(inter[i], jnp.max(d[i], axis=-1, keepdims=True)) for i in ix]
    w_inter = [jnp.exp(inter[i] - m_t[i]) for i in ix]
    s = [qk[i] * jnp.exp(d[i] - m_t[i][:, :L]) for i in ix]
    m_new = [m_t[i][L - 1:L, :] for i in ix]
    b_last = [b_rep[i][L - 1:L, :] for i in ix]
    kw = [k[i] * jnp.exp(b_last[i] - b_rep[i] + i_rep[i] - m_new[i])[:, :ML_DK] for i in ix]
    sv = [_ein(s[i], v[i]) for i in ix]
    kv = [_ein_tn(kw[i], v[i]) for i in ix]
    s_sum = [jnp.sum(s[i], -1, keepdims=True) for i in ix]
    f_state = [jnp.exp(b_last[i] + m[i] - m_new[i]) for i in ix]
    c_new = [f_state[i] * c[i] + kv[i] for i in ix]
    n_new = [f_state[i][:, :ML_DK] * n[i] + jnp.sum(kw[i], axis=0, keepdims=True) for i in ix]
    for i, (b, h) in enumerate(us):
        c_ref[b, h] = c_new[i]
        n_ref[b, h:h + 1, :] = n_new[i]
        m_ref[b, h:h + 1, :] = m_new[i]
    hh = [(w_inter[i] * qc[i] + sv[i]) / jnp.maximum(jnp.abs(w_inter[i] * qn[i] + s_sum[i]), jnp.exp(-m_t[i])) for i in ix]
    ms = [jnp.mean(hh[i] * hh[i], -1, keepdims=True) for i in ix]
    for i, (b, h) in enumerate(us):
        og = cols_ref[b, :, 1024 + h * ML_DV:1024 + (h + 1) * ML_DV]
        hn = hh[i] * lax.rsqrt(ms[i] + NORM_EPS) * ng_ref[:, h * ML_DV:(h + 1) * ML_DV]
        h_ref[b, :, h * ML_DV:(h + 1) * ML_DV] = (hn * jax.nn.sigmoid(og)).astype(h_ref.dtype)


def _chunk_grid_specs(bp, t, width, col=0, chunk=CHUNK):
    cols = pl.BlockSpec((bp, chunk, width), lambda c: (0, c, col))
    out = pl.BlockSpec((bp, chunk, GROUP), lambda c: (0, c, 0))
    return (t // chunk,), cols, out


def _whole(shape):
    return pl.BlockSpec(shape, lambda c: (0,) * len(shape))


def _mlstm_prompt(cols, gb, ng, l, bp, t):
    grid, cspec, ospec = _chunk_grid_specs(bp, t, W_ML, chunk=WIDE_CHUNK)
    return pl.pallas_call(
        _mlstm_prompt_body, grid=grid,
        in_specs=[cspec, _layer_spec(l, (1, LANES)), _layer_spec(l, (1, GROUP))],
        out_specs=[ospec, _whole((bp, ML_H, ML_DK, ML_DV)), _whole((bp, ML_H, ML_DK)), _whole((bp, ML_H, LANES))],
        out_shape=[jax.ShapeDtypeStruct((bp, t, GROUP), bf16), jax.ShapeDtypeStruct((bp, ML_H, ML_DK, ML_DV), f32),
                   jax.ShapeDtypeStruct((bp, ML_H, ML_DK), f32), jax.ShapeDtypeStruct((bp, ML_H, LANES), f32)],
        compiler_params=_cparams(("arbitrary",)), name="mlstm_prompt")(cols.reshape(bp, t, W_A), gb, ng)


def _gla_prompt_body(cols_ref, gup_ref, gb_ref, ng_ref, h_ref, s_ref):
    @pl.when(pl.program_id(0) == 0)
    def _():
        s_ref[...] = jnp.zeros_like(s_ref)

    nb, L = cols_ref.shape[0], cols_ref.shape[1]
    causal, _ = _tri_masks(L)
    lr = cols_ref[:, :, 1536:1664].reshape(nb * L, LANES)
    la = _log_sigmoid(_ein(lr, gup_ref[...]) + gb_ref[...]) / GLA_TAU
    bc_all = _select_dot(_tril_blocks(nb, L), la)
    bc = [bc_all[b * L:(b + 1) * L, :] for b in range(nb)]
    mid = [bc[b][L // 2:L // 2 + 1, :] for b in range(nb)]
    last = [bc[b][L - 1:L, :] for b in range(nb)]
    e_q_mid = [jnp.exp(bc[b] - mid[b]) for b in range(nb)]
    e_k_mid = [jnp.exp(mid[b] - bc[b]) for b in range(nb)]
    e_q = [jnp.exp(bc[b]) for b in range(nb)]
    e_k_last = [jnp.exp(last[b] - bc[b]) for b in range(nb)]
    e_last = [jnp.exp(last[b]) for b in range(nb)]
    us = _units(nb, GLA_H)
    ix = range(len(us))
    sl = [slice(h * GLA_DK, (h + 1) * GLA_DK) for _, h in us]
    q = [cols_ref[b, :, h * GLA_DK:(h + 1) * GLA_DK] * GLA_DK ** -0.5 for b, h in us]
    k = [cols_ref[b, :, 256 + h * GLA_DK:256 + (h + 1) * GLA_DK] for b, h in us]
    v = [cols_ref[b, :, 512 + h * GLA_DV:512 + (h + 1) * GLA_DV] for b, h in us]
    s = [s_ref[b, h] for b, h in us]
    a = [jnp.where(causal, _ein_nt(q[i] * e_q_mid[b][:, sl[i]], k[i] * e_k_mid[b][:, sl[i]]), 0.0)
         for i, (b, _) in enumerate(us)]
    o0 = [_ein(q[i] * e_q[b][:, sl[i]], s[i]) for i, (b, _) in enumerate(us)]
    kv = [_ein_tn(k[i] * e_k_last[b][:, sl[i]], v[i]) for i, (b, _) in enumerate(us)]
    o = [o0[i] + _ein(a[i], v[i]) for i in ix]
    for i, (b, h) in enumerate(us):
        s_ref[b, h] = _col(e_last[b][:, sl[i]]) * s[i] + kv[i]
    ms = [jnp.mean(o[i] * o[i], -1, keepdims=True) for i in ix]
    for i, (b, h) in enumerate(us):
        og = cols_ref[b, :, 1024 + h * GLA_DV:1024 + (h + 1) * GLA_DV]
        on = o[i] * lax.rsqrt(ms[i] + NORM_EPS) * ng_ref[:, h * GLA_DV:(h + 1) * GLA_DV]
        h_ref[b, :, h * GLA_DV:(h + 1) * GLA_DV] = (on * _silu(og)).astype(h_ref.dtype)


def _gla_prompt(cols, gup, gb, ng, l, bp, t):
    grid, cspec, ospec = _chunk_grid_specs(bp, t, W_GLA, col=1, chunk=WIDE_CHUNK)
    return pl.pallas_call(
        _gla_prompt_body, grid=grid,
        in_specs=[cspec, _layer_spec(l, (LANES, GLA_H * GLA_DK)), _layer_spec(l, (1, GLA_H * GLA_DK)),
                  _layer_spec(l, (1, GROUP))],
        out_specs=[ospec, _whole((bp, GLA_H, GLA_DK, GLA_DV))],
        out_shape=[jax.ShapeDtypeStruct((bp, t, GROUP), bf16), jax.ShapeDtypeStruct((bp, GLA_H, GLA_DK, GLA_DV), f32)],
        compiler_params=_cparams(("arbitrary",)), name="gla_prompt")(cols.reshape(bp, t, W_A), gup, gb, ng)


def _rwkv_pre(x, prev, mu, wup, aup, gup, vec):
    xs = x + (prev - x) * mu
    r, k, v, lb = xs[:, 0:512], xs[:, 512:1024], xs[:, 1024:1536], xs[:, 1536:1664]
    w0, a0, k_k, k_a = vec[0:1, :], vec[1:2, :], vec[2:3, :], vec[3:4, :]
    lw = -RW_DECAY_SCALE * jax.nn.sigmoid(w0 + _ein(jnp.tanh(lb), wup))
    a = jax.nn.sigmoid(a0 + _ein(lb, aup))
    g = _ein(jax.nn.sigmoid(lb), gup)
    kk_raw = k * k_k
    k2 = k * (1.0 + (a - 1.0) * k_a)
    return r, k2, v, lw, a, g, kk_raw


def _rwkv_post(o, r, k2, v, g, vec):
    hs = range(RW_H)
    sl = [slice(h * RW_N, (h + 1) * RW_N) for h in hs]
    mu_o = [jnp.mean(o[h], -1, keepdims=True) for h in hs]
    bonus_w = [jnp.sum(r[:, sl[h]] * k2[:, sl[h]] * vec[4:5, sl[h]], -1, keepdims=True) for h in hs]
    oc = [o[h] - mu_o[h] for h in hs]
    var_o = [jnp.mean(oc[h] * oc[h], -1, keepdims=True) for h in hs]
    outs = [(oc[h] * lax.rsqrt(var_o[h] + RW_GN_EPS) * vec[5:6, sl[h]] + vec[6:7, sl[h]]
             + bonus_w[h] * v[:, sl[h]]) * g[:, sl[h]] for h in hs]
    return jnp.concatenate(outs, axis=-1)


def _rwkv_prompt_body(cols_ref, mu_ref, wup_ref, aup_ref, gup_ref, vec_ref, h_ref, s_ref, prev_scr):
    @pl.when(pl.program_id(0) == 0)
    def _():
        s_ref[...] = jnp.zeros_like(s_ref)
        prev_scr[...] = jnp.zeros_like(prev_scr)

    nb, L, N = cols_ref.shape[0], CHUNK, RW_N
    first_row = _iota((L, W_RW), 0) == 0
    xb = [cols_ref[b, :, RW_OFF:RW_OFF + W_RW] for b in range(nb)]
    prevs = [jnp.where(first_row, prev_scr[b, 0:1, :], pltpu.roll(xb[b], 1, 0)) for b in range(nb)]
    for b in range(nb):
        prev_scr[b, 0:1, :] = xb[b][L - 1:L, :]
    vec = vec_ref[...]
    r, k2, v, lw, a, g, kk_raw = _rwkv_pre(jnp.concatenate(xb, axis=0), jnp.concatenate(prevs, axis=0), mu_ref[...],
                                           wup_ref[...], aup_ref[...], gup_ref[...], vec)
    cum = _select_dot(_tril_blocks(nb, L), lw)
    cum_prev = cum - lw
    rep = lambda row_of: jnp.concatenate(
        [jnp.broadcast_to(cum[b * L + row_of:b * L + row_of + 1, :], (L, RW_H * N)) for b in range(nb)], axis=0)
    mid, last = rep(L // 2), rep(L - 1)
    e_prev_mid, e_mid_cum, e_cum_mid = jnp.exp(cum_prev - mid), jnp.exp(mid - cum), jnp.exp(cum - mid)
    e_prev, e_cum, e_last_cum, e_last = jnp.exp(cum_prev), jnp.exp(cum), jnp.exp(last - cum), jnp.exp(last)
    row, col = _iota((2 * L, 2 * L), 0), _iota((2 * L, 2 * L), 1)
    tq, sq = row % L, col % L
    quad = sq < tq + jnp.where(row < L, 0, 1)
    left = col[:, :] < L
    hs = range(RW_H)
    sl = [slice(h * N, (h + 1) * N) for h in hs]
    kk_h = [_l2norm(kk_raw[:, sl[h]]) for h in hs]
    b_h = [kk_h[h] * a[:, sl[h]] for h in hs]
    f_lhs = [(kk_h[h] * e_prev_mid[:, sl[h]], r[:, sl[h]] * e_cum_mid[:, sl[h]]) for h in hs]
    f_rhs = [(b_h[h] * e_mid_cum[:, sl[h]], k2[:, sl[h]] * e_mid_cum[:, sl[h]]) for h in hs]
    f_x0 = [(kk_h[h] * e_prev[:, sl[h]], r[:, sl[h]] * e_cum[:, sl[h]]) for h in hs]
    f_dec = [(b_h[h] * e_last_cum[:, sl[h]], k2[:, sl[h]] * e_last_cum[:, sl[h]]) for h in hs]
    us = _units(nb, RW_H)
    ix = range(len(us))
    rows = lambda z, b: z[b * L:(b + 1) * L, :]
    pair = lambda f, b, h: jnp.concatenate([rows(f[h][0], b), rows(f[h][1], b)], axis=0)
    vh = [rows(v[:, sl[h]], b) for b, h in us]
    s = [s_ref[b, h] for b, h in us]
    a_all = [jnp.where(quad, _ein_nt(pair(f_lhs, b, h), pair(f_rhs, b, h)), 0.0) for b, h in us]
    t_inv = _unit_lower_inverses([a_all[i][0:L, 0:L] for i in ix])
    x0 = [pair(f_x0, b, h) for b, h in us]
    m2 = [jnp.where(left, jnp.concatenate([x0[i], x0[i]], axis=1), a_all[i]) for i in ix]
    y = [_ein(m2[i], jnp.concatenate([s[i], vh[i]], axis=0)) for i in ix]
    u = [-_dot3(t_inv[i], y[i][0:L, :]) for i in ix]
    o = [y[i][L:2 * L, :] + _ein(a_all[i][L:2 * L, 0:L], u[i]) for i in ix]
    for i, (b, h) in enumerate(us):
        s_ref[b, h] = (_col(e_last[b * L:b * L + 1, sl[h]]) * s[i]
                       + _ein_tn(pair(f_dec, b, h), jnp.concatenate([u[i], vh[i]], axis=0)))
    o_heads = [jnp.concatenate([o[b * RW_H + h] for b in range(nb)], axis=0) for h in hs]
    h_ref[...] = _rwkv_post(o_heads, r, k2, v, g, vec).reshape(nb, L, GROUP).astype(h_ref.dtype)


def _rwkv_prompt(cols, mu, wup, aup, gup, vec, l, bp, t):
    grid, cspec, ospec = _chunk_grid_specs(bp, t, W_B)
    return pl.pallas_call(
        _rwkv_prompt_body, grid=grid,
        in_specs=[cspec, _layer_spec(l, (1, W_RW)), _layer_spec(l, (LANES, GROUP)), _layer_spec(l, (LANES, GROUP)),
                  _layer_spec(l, (LANES, GROUP)), _layer_spec(l, (8, GROUP))],
        out_specs=[ospec, _whole((bp, RW_H, RW_N, RW_N))],
        out_shape=[jax.ShapeDtypeStruct((bp, t, GROUP), bf16), jax.ShapeDtypeStruct((bp, RW_H, RW_N, RW_N), f32)],
        scratch_shapes=[pltpu.VMEM((bp, 8, W_RW), f32)],
        compiler_params=_cparams(("arbitrary",)), name="rwkv_prompt")(cols.reshape(bp, t, W_B), mu, wup, aup, gup, vec)


def _gdn_gates(gb, alog, dtb):
    return jax.nn.sigmoid(gb), -jnp.exp(alog) * _softplus(gb + dtb)


def _gdn_prompt_body(cols_ref, cw_ref, alog_ref, dtb_ref, ng_ref, h_ref, s_ref, xp_scr):
    @pl.when(pl.program_id(0) == 0)
    def _():
        s_ref[...] = jnp.zeros_like(s_ref)
        xp_scr[...] = jnp.zeros_like(xp_scr)

    nb, L = cols_ref.shape[0], CHUNK
    causal, strict = _tri_masks(L)
    act = []
    for b in range(nb):
        raw = cols_ref[b, :, 0:DN_QKV]
        ext = jnp.concatenate([xp_scr[b], raw], axis=0)
        conv = raw * cw_ref[DN_CONV - 1:DN_CONV, :]
        for j in range(1, DN_CONV):
            conv = conv + pltpu.roll(ext, j, 0)[8:8 + L, :] * cw_ref[DN_CONV - 1 - j:DN_CONV - j, :]
        xp_scr[b] = raw[L - 8:L, :]
        act.append(_silu(conv))
    beta_all, g_all = _gdn_gates(cols_ref[:, :, 2048:2176].reshape(nb * L, LANES), alog_ref[...], dtb_ref[...])
    gc_all = _select_dot(_tril_blocks(nb, L), g_all)
    gc = [gc_all[b * L:(b + 1) * L, :] for b in range(nb)]
    gt = [_rows_of(gc[b]) for b in range(nb)]
    us = _units(nb, DN_H)
    ix = range(len(us))
    q = [_l2norm(act[b][:, h * DN_D:(h + 1) * DN_D]) * DN_D ** -0.5 for b, h in us]
    k = [_l2norm(act[b][:, GROUP + h * DN_D:GROUP + (h + 1) * DN_D]) for b, h in us]
    v = [act[b][:, 2 * GROUP + h * DN_D:2 * GROUP + (h + 1) * DN_D] for b, h in us]
    beta = [beta_all[b * L:(b + 1) * L, h:h + 1] for b, h in us]
    g_col = [gc[b][:, DN_H + h:DN_H + h + 1] for b, h in us]
    decay = [jnp.exp(jnp.where(causal, g_col[i] - gt[b][DN_H + h:DN_H + h + 1, :], -jnp.inf))
             for i, (b, h) in enumerate(us)]
    kb = [k[i] * beta[i] for i in ix]
    eg = [jnp.exp(g_col[i]) for i in ix]
    s = [s_ref[b, h] for b, h in us]
    kq = [_ein_nt(jnp.concatenate([kb[i], q[i]], axis=0), k[i]) for i in ix]
    t_inv = _unit_lower_inverses([jnp.where(strict, kq[i][0:L, :] * decay[i], 0.0) for i in ix])
    qk = [kq[i][L:2 * L, :] * decay[i] for i in ix]
    o0 = [_ein(q[i] * eg[i], s[i]) for i in ix]
    sol = [_dot3(t_inv[i], jnp.concatenate([v[i] * beta[i], kb[i] * eg[i]], axis=-1)) for i in ix]
    v_new = [sol[i][:, :DN_D] - _ein(sol[i][:, DN_D:], s[i]) for i in ix]
    o = [o0[i] + _ein(qk[i], v_new[i]) for i in ix]
    for i, (b, h) in enumerate(us):
        g_last = g_col[i][L - 1:L, :]
        s_ref[b, h] = jnp.exp(g_last) * s[i] + _ein_tn(k[i] * jnp.exp(g_last - g_col[i]), v_new[i])
    ms = [jnp.mean(o[i] * o[i], -1, keepdims=True) for i in ix]
    for i, (b, h) in enumerate(us):
        z = cols_ref[b, :, DN_QKV + h * DN_D:DN_QKV + (h + 1) * DN_D]
        on = o[i] * lax.rsqrt(ms[i] + NORM_EPS) * ng_ref[...]
        h_ref[b, :, h * DN_D:(h + 1) * DN_D] = (on * _silu(z)).astype(h_ref.dtype)


def _gdn_prompt(cols, cw, alog, dtb, ng, l, bp, t):
    grid, cspec, ospec = _chunk_grid_specs(bp, t, W_DN)
    return pl.pallas_call(
        _gdn_prompt_body, grid=grid,
        in_specs=[cspec, _layer_spec(l, (8, DN_QKV)), _layer_spec(l, (1, LANES)), _layer_spec(l, (1, LANES)),
                  _layer_spec(l, (1, DN_D))],
        out_specs=[ospec, _whole((bp, DN_H, DN_D, DN_D))],
        out_shape=[jax.ShapeDtypeStruct((bp, t, GROUP), bf16), jax.ShapeDtypeStruct((bp, DN_H, DN_D, DN_D), f32)],
        scratch_shapes=[pltpu.VMEM((bp, 8, DN_QKV), f32)],
        compiler_params=_cparams(("arbitrary",)), name="gdn_prompt")(cols.reshape(bp, t, W_B), cw, alog, dtb, ng)


def _mlstm_step_body(cols_ref, gb_ref, ng_ref, c_ref, n_ref, m_ref, h_ref, co_ref, no_ref, mo_ref,
                     q_scr, kw_scr, v_scr, f_scr, qc_scr):
    nb = DEC_BLOCK
    g = cols_ref[:, 1536:1664] + gb_ref[...]
    lf = _log_sigmoid(g)
    hs = []
    for h in range(ML_H):
        q = cols_ref[:, h * ML_DK:(h + 1) * ML_DK]
        k = cols_ref[:, 256 + h * ML_DK:256 + (h + 1) * ML_DK] * ML_DK ** -0.5
        v = cols_ref[:, 512 + h * ML_DV:512 + (h + 1) * ML_DV]
        i_pre, f_log = g[:, h:h + 1], lf[:, ML_H + h:ML_H + h + 1]
        m = m_ref[:, h:h + 1]
        inter = f_log + m
        m_t = jnp.maximum(inter, i_pre)
        w_inter = jnp.exp(inter - m_t)
        kw = k * jnp.exp(i_pre - m_t)
        q_scr[h], kw_scr[h], v_scr[h] = q, kw, v
        f_scr[h] = jnp.broadcast_to(w_inter, (nb, LANES))
        hs.append((q, k, v, i_pre, m_t, w_inter, kw))

    def row(b, carry):
        hr = range(ML_H)
        kw_col = [_col(kw_scr[h, pl.ds(b, 1), :]) for h in hr]
        c = [c_ref[b, h] for h in hr]
        qc = [_row_of(_ein(q_scr[h], c[h]), b) for h in hr]
        for h in hr:
            qc_scr[h, pl.ds(b, 1), :] = qc[h]
            co_ref[b, h] = f_scr[h, pl.ds(b, 1), :] * c[h] + kw_col[h] * v_scr[h, pl.ds(b, 1), :]
        return carry

    lax.fori_loop(0, nb, row, 0)
    for h, (q, k, v, i_pre, m_t, w_inter, kw) in enumerate(hs):
        og = cols_ref[:, 1024 + h * ML_DV:1024 + (h + 1) * ML_DV]
        n = n_ref[:, h * ML_DK:(h + 1) * ML_DK]
        s = jnp.sum(q * k, -1, keepdims=True) * jnp.exp(i_pre - m_t)
        num = w_inter * qc_scr[h] + s * v
        den = w_inter * jnp.sum(q * n, -1, keepdims=True) + s
        hh = num / jnp.maximum(jnp.abs(den), jnp.exp(-m_t))
        no_ref[:, h * ML_DK:(h + 1) * ML_DK] = w_inter * n + kw
        mo_ref[:, h:h + 1] = m_t
        hn = hh * lax.rsqrt(jnp.mean(hh * hh, -1, keepdims=True) + NORM_EPS) * ng_ref[:, h * ML_DV:(h + 1) * ML_DV]
        h_ref[:, h * ML_DV:(h + 1) * ML_DV] = (hn * jax.nn.sigmoid(og)).astype(h_ref.dtype)


def _row_spec(width, col=0):
    return pl.BlockSpec((DEC_BLOCK, width), lambda i: (i, col))


def _layer_rows(l, width):
    return _layer_spec(l, (DEC_BLOCK, width), lambda i: (i, 0))


def _layer_state(l, h, dk, dv):
    return _layer_spec(l, (DEC_BLOCK, h, dk, dv), lambda i: (i, 0, 0, 0))


def _ignore_first_ref(body):
    def with_handed_on_buffer(_stacked_out_so_far, *refs):
        body(*refs)
    return with_handed_on_buffer


def _step_call(body, l, stacked_out, in_specs, out_specs, out_shapes, state_out_index, **kw):
    if stacked_out is None:
        return lambda *args: pl.pallas_call(body, in_specs=in_specs, out_specs=out_specs, out_shape=out_shapes, **kw)(*args)
    call = pl.pallas_call(_ignore_first_ref(body), in_specs=[pl.BlockSpec(memory_space=pl.ANY)] + in_specs,
                          out_specs=out_specs, out_shape=out_shapes, input_output_aliases={0: state_out_index}, **kw)
    return lambda *args: call(stacked_out, *args)


def _mlstm_step(cols, gb, ng, c0, n0, m0, l, c_out):
    bs = cols.shape[0]
    nb = DEC_BLOCK
    call = _step_call(
        _mlstm_step_body, l, c_out,
        in_specs=[_row_spec(W_ML), _layer_spec(l, (1, LANES)), _layer_spec(l, (1, GROUP)),
                  _layer_state(l, ML_H, ML_DK, ML_DV), _layer_rows(l, ML_H * ML_DK), _layer_rows(l, ML_H)],
        out_specs=[_row_spec(GROUP), _layer_state(l, ML_H, ML_DK, ML_DV), _row_spec(ML_H * ML_DK), _row_spec(ML_H)],
        out_shapes=[jax.ShapeDtypeStruct((bs, GROUP), bf16), jax.ShapeDtypeStruct(c0.shape, f32),
                    jax.ShapeDtypeStruct((bs, ML_H * ML_DK), f32), jax.ShapeDtypeStruct((bs, ML_H), f32)],
        state_out_index=1, grid=(bs // nb,),
        scratch_shapes=[pltpu.VMEM((ML_H, nb, ML_DK), f32), pltpu.VMEM((ML_H, nb, ML_DK), f32),
                        pltpu.VMEM((ML_H, nb, ML_DV), f32), pltpu.VMEM((ML_H, nb, LANES), f32),
                        pltpu.VMEM((ML_H, nb, ML_DV), f32)],
        compiler_params=_cparams(("parallel",)), name="mlstm_step")
    return call(cols, gb, ng, c0, n0, m0)


def _gla_step_body(cols_ref, gup_ref, gb_ref, ng_ref, s_ref, h_ref, so_ref, qe_scr, ea_scr, k_scr, v_scr, qs_scr):
    nb = DEC_BLOCK
    la = _log_sigmoid(_ein(cols_ref[:, 1536:1664], gup_ref[...]) + gb_ref[...]) / GLA_TAU
    ea = jnp.exp(la)
    hs = []
    for h in range(GLA_H):
        q = cols_ref[:, h * GLA_DK:(h + 1) * GLA_DK] * GLA_DK ** -0.5
        k = cols_ref[:, 256 + h * GLA_DK:256 + (h + 1) * GLA_DK]
        v = cols_ref[:, 512 + h * GLA_DV:512 + (h + 1) * GLA_DV]
        ea_h = ea[:, h * GLA_DK:(h + 1) * GLA_DK]
        qe_scr[h], ea_scr[h], k_scr[h], v_scr[h] = q * ea_h, ea_h, k, v
        hs.append((q, k, v))

    def row(b, carry):
        hr = range(GLA_H)
        ea_col = [_col(ea_scr[h, pl.ds(b, 1), :]) for h in hr]
        k_col = [_col(k_scr[h, pl.ds(b, 1), :]) for h in hr]
        s = [s_ref[b, h] for h in hr]
        qs = [_row_of(_ein(qe_scr[h], s[h]), b) for h in hr]
        for h in hr:
            qs_scr[h, pl.ds(b, 1), :] = qs[h]
            so_ref[b, h] = ea_col[h] * s[h] + k_col[h] * v_scr[h, pl.ds(b, 1), :]
        return carry

    lax.fori_loop(0, nb, row, 0)
    for h, (q, k, v) in enumerate(hs):
        og = cols_ref[:, 1024 + h * GLA_DV:1024 + (h + 1) * GLA_DV]
        o = qs_scr[h] + jnp.sum(q * k, -1, keepdims=True) * v
        on = o * lax.rsqrt(jnp.mean(o * o, -1, keepdims=True) + NORM_EPS) * ng_ref[:, h * GLA_DV:(h + 1) * GLA_DV]
        h_ref[:, h * GLA_DV:(h + 1) * GLA_DV] = (on * _silu(og)).astype(h_ref.dtype)


def _gla_step(cols, gup, gb, ng, s0, l, s_out):
    bs = cols.shape[0]
    nb = DEC_BLOCK
    call = _step_call(
        _gla_step_body, l, s_out,
        in_specs=[_row_spec(W_GLA, col=1), _layer_spec(l, (LANES, GLA_H * GLA_DK)), _layer_spec(l, (1, GLA_H * GLA_DK)),
                  _layer_spec(l, (1, GROUP)), _layer_state(l, GLA_H, GLA_DK, GLA_DV)],
        out_specs=[_row_spec(GROUP), _layer_state(l, GLA_H, GLA_DK, GLA_DV)],
        out_shapes=[jax.ShapeDtypeStruct((bs, GROUP), bf16), jax.ShapeDtypeStruct(s0.shape, f32)],
        state_out_index=1, grid=(bs // nb,),
        scratch_shapes=[pltpu.VMEM((GLA_H, nb, GLA_DK), f32)] * 3 + [pltpu.VMEM((GLA_H, nb, GLA_DV), f32)] * 2,
        compiler_params=_cparams(("parallel",)), name="gla_step")
    return call(cols, gup, gb, ng, s0)


def _rwkv_step_body(cols_ref, shift_ref, mu_ref, wup_ref, aup_ref, gup_ref, vec_ref, s_ref, h_ref, so_ref,
                    w_scr, b_scr, k_scr, kkrw_scr, v_scr, sk_scr, o_scr):
    nb = DEC_BLOCK
    vec = vec_ref[...]
    r, k2, v, lw, a, g, kk_raw = _rwkv_pre(cols_ref[:, RW_OFF:RW_OFF + W_RW], shift_ref[...], mu_ref[...], wup_ref[...], aup_ref[...],
                                           gup_ref[...], vec)
    hr = range(RW_H)
    sl = [slice(h * RW_N, (h + 1) * RW_N) for h in hr]
    kk = [_l2norm(kk_raw[:, sl[h]]) for h in hr]
    w = jnp.exp(lw)
    bv = jnp.concatenate(kk, axis=-1) * a
    w_scr[...], b_scr[...], k_scr[...] = w, bv, k2
    for h in hr:
        kkrw_scr[h, 0:nb, :], kkrw_scr[h, nb:2 * nb, :] = kk[h], r[:, sl[h]] * w[:, sl[h]]
        v_scr[h] = v[:, sl[h]]

    def row(b, carry):
        w_col, b_col, k_col = (_cols_of(scr[pl.ds(b, 1), :], RW_N) for scr in (w_scr, b_scr, k_scr))
        s = [s_ref[b, h] for h in hr]
        mv = [_ein(kkrw_scr[h], s[h]) for h in hr]
        sk = [_row_of(mv[h][0:nb, :], b) for h in hr]
        rws = [_row_of(mv[h][nb:2 * nb, :], b) for h in hr]
        for h in hr:
            so_ref[b, h] = w_col[h] * s[h] - b_col[h] * sk[h] + k_col[h] * v_scr[h, pl.ds(b, 1), :]
            sk_scr[h, pl.ds(b, 1), :] = sk[h]
            o_scr[h, pl.ds(b, 1), :] = rws[h]
        return carry

    lax.fori_loop(0, nb, row, 0)
    rb = [jnp.sum(r[:, sl[h]] * bv[:, sl[h]], -1, keepdims=True) for h in hr]
    rk = [jnp.sum(r[:, sl[h]] * k2[:, sl[h]], -1, keepdims=True) for h in hr]
    o = [o_scr[h] - rb[h] * sk_scr[h] + rk[h] * v[:, sl[h]] for h in hr]
    h_ref[...] = _rwkv_post(o, r, k2, v, g, vec).astype(h_ref.dtype)


def _rwkv_step(cols, shift0, mu, wup, aup, gup, vec, s0, l, s_out):
    bs = cols.shape[0]
    nb = DEC_BLOCK
    call = _step_call(
        _rwkv_step_body, l, s_out,
        in_specs=[_row_spec(W_B), _layer_rows(l, W_RW), _layer_spec(l, (1, W_RW)), _layer_spec(l, (LANES, GROUP)),
                  _layer_spec(l, (LANES, GROUP)), _layer_spec(l, (LANES, GROUP)), _layer_spec(l, (8, GROUP)),
                  _layer_state(l, RW_H, RW_N, RW_N)],
        out_specs=[_row_spec(GROUP), _layer_state(l, RW_H, RW_N, RW_N)],
        out_shapes=[jax.ShapeDtypeStruct((bs, GROUP), bf16), jax.ShapeDtypeStruct(s0.shape, f32)],
        state_out_index=1, grid=(bs // nb,),
        scratch_shapes=[pltpu.VMEM((nb, GROUP), f32)] * 3 + [pltpu.VMEM((RW_H, 2 * nb, RW_N), f32)]
        + [pltpu.VMEM((RW_H, nb, RW_N), f32)] * 3,
        compiler_params=_cparams(("parallel",)), name="rwkv_step")
    return call(cols, shift0, mu, wup, aup, gup, vec, s0)


def _gdn_step_body(cols_ref, buf_ref, cw_ref, alog_ref, dtb_ref, ng_ref, s_ref, h_ref, so_ref,
                   kq_scr, v_scr, sc_scr, qs_scr, vn_scr):
    nb = DEC_BLOCK
    conv = cols_ref[:, 0:DN_QKV] * cw_ref[DN_CONV - 1:DN_CONV, :]
    for w in range(DN_CONV - 1):
        conv = conv + buf_ref[:, w * DN_QKV:(w + 1) * DN_QKV] * cw_ref[w:w + 1, :]
    act = _silu(conv)
    beta_all, g_all = _gdn_gates(cols_ref[:, 2048:2176], alog_ref[...], dtb_ref[...])
    eg_all = jnp.exp(g_all)
    hs = []
    for h in range(DN_H):
        q = _l2norm(act[:, h * DN_D:(h + 1) * DN_D]) * DN_D ** -0.5
        k = _l2norm(act[:, GROUP + h * DN_D:GROUP + (h + 1) * DN_D])
        v = act[:, 2 * GROUP + h * DN_D:2 * GROUP + (h + 1) * DN_D]
        beta, eg = beta_all[:, h:h + 1], eg_all[:, DN_H + h:DN_H + h + 1]
        kq_scr[h, 0:nb, :], kq_scr[h, nb:2 * nb, :], v_scr[h] = k, q, v
        sc_scr[h] = jnp.where(_iota((nb, LANES), 1) == 0, beta, eg)
        hs.append((q, k, eg))

    def row(b, carry):
        hr = range(DN_H)
        k_col = [_col(kq_scr[h, pl.ds(b, 1), :]) for h in hr]
        s = [s_ref[b, h] for h in hr]
        sc = [sc_scr[h, pl.ds(b, 1), :] for h in hr]
        mv = [_ein(kq_scr[h], s[h]) for h in hr]
        ks = [_row_of(mv[h][0:nb, :], b) for h in hr]
        qs = [_row_of(mv[h][nb:2 * nb, :], b) for h in hr]
        for h in hr:
            beta, eg = sc[h][:, 0:1], sc[h][:, 1:2]
            v_new = beta * (v_scr[h, pl.ds(b, 1), :] - eg * ks[h])
            qs_scr[h, pl.ds(b, 1), :] = qs[h]
            vn_scr[h, pl.ds(b, 1), :] = v_new
            so_ref[b, h] = eg * s[h] + k_col[h] * v_new
        return carry

    lax.fori_loop(0, nb, row, 0)
    for h, (q, k, eg) in enumerate(hs):
        z = cols_ref[:, DN_QKV + h * DN_D:DN_QKV + (h + 1) * DN_D]
        o = eg * qs_scr[h] + jnp.sum(q * k, -1, keepdims=True) * vn_scr[h]
        on = o * lax.rsqrt(jnp.mean(o * o, -1, keepdims=True) + NORM_EPS) * ng_ref[...]
        h_ref[:, h * DN_D:(h + 1) * DN_D] = (on * _silu(z)).astype(h_ref.dtype)


def _gdn_step(cols, buf, cw, alog, dtb, ng, s0, l, s_out):
    bs = cols.shape[0]
    nb = DEC_BLOCK
    call = _step_call(
        _gdn_step_body, l, s_out,
        in_specs=[_row_spec(W_DN), _layer_rows(l, (DN_CONV - 1) * DN_QKV), _layer_spec(l, (8, DN_QKV)),
                  _layer_spec(l, (1, LANES)), _layer_spec(l, (1, LANES)), _layer_spec(l, (1, DN_D)),
                  _layer_state(l, DN_H, DN_D, DN_D)],
        out_specs=[_row_spec(GROUP), _layer_state(l, DN_H, DN_D, DN_D)],
        out_shapes=[jax.ShapeDtypeStruct((bs, GROUP), bf16), jax.ShapeDtypeStruct(s0.shape, f32)],
        state_out_index=1, grid=(bs // nb,),
        scratch_shapes=[pltpu.VMEM((DN_H, 2 * nb, DN_D), f32), pltpu.VMEM((DN_H, nb, DN_D), f32),
                        pltpu.VMEM((DN_H, nb, LANES), f32)] + [pltpu.VMEM((DN_H, nb, DN_D), f32)] * 2,
        compiler_params=_cparams(("parallel",)), name="gdn_step")
    return call(cols, buf, cw, alog, dtb, ng, s0)


def _pad_last(w, width):
    return jnp.pad(w, [(0, 0)] * (w.ndim - 1) + [(0, width - w.shape[-1])])


def _pad_rows(w, start, total):
    return jnp.pad(w, ((0, 0), (start, total - start - w.shape[1]), (0, 0)))


def _lane_rows(v, start):
    return jnp.pad(v, ((0, 0), (start, LANES - start - v.shape[1])))[:, None, :]


def kernel(x_prompt, x_sample, state_mlstm_c, state_mlstm_n, state_mlstm_m, state_gla, state_rwkv, state_rwkv_shift, state_dn, state_dn_conv, w_in, w_out, w_up, w_down, ln1_g, ln1_b, ln2_g, ln2_b, mlstm_gate_b, mlstm_norm_g, gla_gate_up, gla_gate_b, gla_norm_g, rwkv_mu, rwkv_w_up, rwkv_w0, rwkv_a_up, rwkv_a0, rwkv_g_up, rwkv_k_k, rwkv_k_a, rwkv_r_k, rwkv_norm_g, rwkv_norm_b, dn_conv_w, dn_a_log, dn_dt_bias, dn_norm_g):
    bp, t, _ = x_prompt.shape
    bs = x_sample.shape[0]
    xp = x_prompt.reshape(bp * t, D_MODEL)
    xs = x_sample.reshape(bs, D_MODEL)
    xpb, xsb = xp, xs
    tm_p, tm_s, tf = 512, bs, 1024

    o1, o2, o3 = N_ML, N_ML + N_GLA, N_ML + N_GLA + N_RW
    wb = w_in.astype(bf16)
    gap = lambda n: jnp.zeros(wb.shape[:2] + (n,), bf16)
    w_ab = [jnp.concatenate([wb[:, :, 0:o1], gap(W_ML - N_ML), wb[:, :, o1:o2], gap(W_GLA - N_GLA)], axis=-1),
            jnp.concatenate([wb[:, :, o3:], gap(W_DN - N_DN), wb[:, :, o2:o3]], axis=-1)]
    wo, wu, wd = w_out.astype(bf16), w_up.astype(bf16), w_down.astype(bf16)
    g1, b1, g2, b2 = ln1_g[:, None], ln1_b[:, None], ln2_g[:, None], ln2_b[:, None]
    ml_gb, ml_ng = _lane_rows(mlstm_gate_b, 0), mlstm_norm_g[:, None]
    gla_gup, gla_gb, gla_ng = _pad_rows(gla_gate_up, 0, LANES), gla_gate_b[:, None], gla_norm_g[:, None]
    rw_mu = rwkv_mu[:, None]
    rw_wup, rw_aup, rw_gup = _pad_rows(rwkv_w_up, 0, LANES), _pad_rows(rwkv_a_up, 32, LANES), _pad_rows(rwkv_g_up, 64, LANES)
    rw_vec = jnp.stack([rwkv_w0, rwkv_a0, rwkv_k_k, rwkv_k_a, rwkv_r_k, rwkv_norm_g, rwkv_norm_b,
                        jnp.zeros_like(rwkv_w0)], axis=1)
    dn_cw = _pad_rows(dn_conv_w, 0, 8)
    dn_alog, dn_dtb, dn_ng = _lane_rows(dn_a_log, DN_H), _lane_rows(dn_dt_bias, DN_H), dn_norm_g[:, None]
    st_ml_n = state_mlstm_n.reshape(DEPTH, bs, ML_H * ML_DK)
    st_dn_conv = state_dn_conv.reshape(DEPTH, bs, (DN_CONV - 1) * DN_QKV)

    outs_p, outs_s = [], []
    ml_c_s = gla_s = rw_s = dn_s = None
    for l in range(DEPTH):
        c_a, c_b = [_proj(xpb, w, l, tm_p) for w in w_ab]
        h_ml, mc, mn, mm = _mlstm_prompt(c_a, ml_gb, ml_ng, l, bp, t)
        h_gla, gs = _gla_prompt(c_a, gla_gup, gla_gb, gla_ng, l, bp, t)
        h_rw, rs = _rwkv_prompt(c_b, rw_mu, rw_wup, rw_aup, rw_gup, rw_vec, l, bp, t)
        h_dn, ds = _gdn_prompt(c_b, dn_cw, dn_alog, dn_dtb, dn_ng, l, bp, t)
        hs_p = [h.reshape(bp * t, GROUP) for h in (h_ml, h_gla, h_rw, h_dn)]
        x1, x1b = _outproj_ln(hs_p, wo, xp, g1, b1, l, tm_p)
        xp, xpb = _ffn_ln(x1, x1b, wu, wd, g2, b2, l, tm_p, tf)
        last_rows = c_b.reshape(bp, t, W_B)[:, t - (DN_CONV - 1):]
        outs_p.append((mc, mn, mm[:, :, 0], gs, rs, last_rows[:, -1, RW_OFF:], ds, last_rows[:, :, :DN_QKV]))

        c_a, c_b = [_proj(xsb, w, l, tm_s) for w in w_ab]
        h_ml, ml_c_s, mn, mm = _mlstm_step(c_a, ml_gb, ml_ng, state_mlstm_c, st_ml_n, state_mlstm_m, l, ml_c_s)
        h_gla, gla_s = _gla_step(c_a, gla_gup, gla_gb, gla_ng, state_gla, l, gla_s)
        h_rw, rw_s = _rwkv_step(c_b, state_rwkv_shift, rw_mu, rw_wup, rw_aup, rw_gup, rw_vec, state_rwkv, l, rw_s)
        h_dn, dn_s = _gdn_step(c_b, st_dn_conv, dn_cw, dn_alog, dn_dtb, dn_ng, state_dn, l, dn_s)
        x1, x1b = _outproj_ln((h_ml, h_gla, h_rw, h_dn), wo, xs, g1, b1, l, tm_s)
        xs, xsb = _ffn_ln(x1, x1b, wu, wd, g2, b2, l, tm_s, tf)
        outs_s.append((mn.reshape(bs, ML_H, ML_DK), mm, c_b[:, RW_OFF:],
                       jnp.concatenate([state_dn_conv[l][:, 1:], c_b[:, None, :DN_QKV]], axis=1)))

    (mlstm_c_p, mlstm_n_p, mlstm_m_p, gla_p, rwkv_p, rwkv_shift_p, dn_p, dn_conv_p) = [jnp.stack(z) for z in zip(*outs_p)]
    (mlstm_n_s, mlstm_m_s, rwkv_shift_s, dn_conv_s) = [jnp.stack(z) for z in zip(*outs_s)]
    return (xp.reshape(bp, t, D_MODEL), xs.reshape(bs, 1, D_MODEL), mlstm_c_p, ml_c_s, mlstm_n_p, mlstm_n_s,
            mlstm_m_p, mlstm_m_s, gla_p, gla_s, rwkv_p, rw_s, rwkv_shift_p, rwkv_shift_s, dn_p, dn_s,
            dn_conv_p, dn_conv_s)
```

```python
import functools

import jax
import jax.numpy as jnp
from jax import lax
from jax.experimental import pallas as pl
from jax.experimental.pallas import tpu as pltpu

f32 = jnp.float32
bf16 = jnp.bfloat16

D_MODEL = 2048
DEPTH = 4
GROUP = 512
D_FF = 4 * D_MODEL
CHUNK = 64
WIDE_CHUNK = 128
ML_H, ML_DK, ML_DV = 4, 64, 128
GLA_H, GLA_DK, GLA_DV, GLA_RANK, GLA_TAU = 4, 64, 128, 16, 16.0
RW_H, RW_N = 8, 64
RW_DECAY_SCALE = 0.606531
RW_GN_EPS = 64e-5
DN_H, DN_D, DN_CONV, DN_QKV = 4, 128, 4, 1536
N_ML, N_GLA, N_RW, N_DN = 1544, 1552, 1664, 2056
W_ML, W_GLA, W_RW, W_DN = 1664, 1664, 1664, 2176
W_A, W_B = W_ML + W_GLA, W_DN + W_RW
RW_OFF = W_DN
ALPHA = (2 * DEPTH) ** 0.25
LN_EPS = 1e-5
NORM_EPS = 1e-6
LANES = 128
DEC_BLOCK = 16
VMEM_LIMIT = 56 * 1024 * 1024
LN_SUBTILE = 256


def _ein(a, b):
    return jnp.dot(a.astype(bf16), b.astype(bf16), preferred_element_type=f32)


def _ein_nt(a, b):
    return lax.dot_general(a.astype(bf16), b.astype(bf16), (((1,), (1,)), ((), ())), preferred_element_type=f32)


def _ein_tn(a, b):
    return lax.dot_general(a.astype(bf16), b.astype(bf16), (((0,), (0,)), ((), ())), preferred_element_type=f32)


def _split3(x):
    hi = x.astype(bf16)
    r1 = x - hi.astype(f32)
    mid = r1.astype(bf16)
    return hi, mid, (r1 - mid.astype(f32)).astype(bf16)


def _select_dot(sel, x):
    s = sel.astype(bf16)
    hi, mid, lo = _split3(x)
    d = functools.partial(jnp.dot, preferred_element_type=f32)
    return d(s, hi) + d(s, mid) + d(s, lo)


def _select_dot_right(x, sel):
    s = sel.astype(bf16)
    hi, mid, lo = _split3(x)
    d = functools.partial(jnp.dot, preferred_element_type=f32)
    return d(hi, s) + d(mid, s) + d(lo, s)


def _select_dot_nt(sel, x):
    s = sel.astype(bf16)
    hi, mid, lo = _split3(x)
    d = functools.partial(lax.dot_general, dimension_numbers=(((1,), (1,)), ((), ())), preferred_element_type=f32)
    return d(s, hi) + d(s, mid) + d(s, lo)


def _log_sigmoid(x):
    return jnp.minimum(x, 0.0) - jnp.log1p(jnp.exp(-jnp.abs(x)))


def _softplus(x):
    return jnp.maximum(x, 0.0) + jnp.log1p(jnp.exp(-jnp.abs(x)))


def _silu(x):
    return x * jax.nn.sigmoid(x)


def _l2norm(x):
    return x * lax.rsqrt(jnp.sum(x * x, -1, keepdims=True) + NORM_EPS)


def _iota(shape, axis):
    return lax.broadcasted_iota(jnp.int32, shape, axis)


def _col(row):
    n = row.shape[1]
    eye = _iota((n, n), 0) == _iota((n, n), 1)
    return jnp.sum(jnp.where(eye, row, 0.0), axis=1, keepdims=True)


def _row_of(x, b):
    return jnp.sum(jnp.where(_iota(x.shape, 0) == b, x, 0.0), axis=0, keepdims=True)


def _cols_of(row, n, width=None):
    width = n if width is None else width
    tiles = []
    for g in range(row.shape[1] // LANES):
        t = jnp.broadcast_to(row[:, g * LANES:(g + 1) * LANES], (LANES, LANES)).T
        tiles += [t[j * n:(j + 1) * n, 0:width] for j in range(LANES // n)]
    return tiles


def _tri_masks(n):
    r, c = _iota((n, n), 0), _iota((n, n), 1)
    return c <= r, c < r


def _split2(a):
    hi = a.astype(bf16)
    return hi, (a - hi.astype(f32)).astype(bf16)


def _dot3(a, b):
    ah, al = _split2(a)
    bh, bl = _split2(b)
    d = functools.partial(jnp.dot, preferred_element_type=f32)
    return d(ah, bh) + d(ah, bl) + d(al, bh)


def _unit_lower_inverses(a_list):
    n = a_list[0].shape[0]
    eye = jnp.where(_iota((n, n), 0) == _iota((n, n), 1), 1.0, 0.0)
    xs = [-a for a in a_list]
    ps = [eye + x for x in xs]
    for stage in range(max(n.bit_length() - 2, 0)):
        mm = _dot3 if stage == 0 else _ein
        xs = [mm(x, x) for x in xs]
        ps = [p + mm(p, x) for p, x in zip(ps, xs)]
    return ps


def _rows_of(y):
    e8 = jnp.where(_iota((8, LANES), 0) == _iota((8, LANES), 1), 1.0, 0.0)
    return _select_dot_nt(e8, y)


def _cparams(sem):
    return pltpu.CompilerParams(dimension_semantics=sem, vmem_limit_bytes=VMEM_LIMIT)


def _layer_spec(l, shape, index=None):
    nd = len(shape)
    if index is None:
        return pl.BlockSpec((None,) + tuple(shape), lambda *g: (l,) + (0,) * nd)
    return pl.BlockSpec((None,) + tuple(shape), lambda *g: (l,) + tuple(index(*g)))


def _proj_body(x_ref, w_ref, o_ref):
    o_ref[...] = jnp.dot(x_ref[...].astype(bf16), w_ref[...], preferred_element_type=f32)


def _proj(xb, w, l, tm):
    m, k = xb.shape
    n = w.shape[2]
    return pl.pallas_call(
        _proj_body, grid=(m // tm,),
        in_specs=[pl.BlockSpec((tm, k), lambda i: (i, 0)),
                  pl.BlockSpec((None, k, n), lambda i: (l, 0, 0), pipeline_mode=pl.Buffered(1))],
        out_specs=pl.BlockSpec((tm, n), lambda i: (i, 0)),
        out_shape=jax.ShapeDtypeStruct((m, n), f32),
        compiler_params=_cparams(("parallel",)), name="proj")(xb, w)


def _layernorm(y, g, b):
    mu = jnp.mean(y, -1, keepdims=True)
    yc = y - mu
    var = jnp.mean(yc * yc, -1, keepdims=True)
    return yc * lax.rsqrt(var + LN_EPS) * g + b


def _outproj_ln_body(h0, h1, h2, h3, w_ref, x_ref, g_ref, b_ref, o_ref, ob_ref):
    sub = min(LN_SUBTILE, x_ref.shape[0])
    for r in range(x_ref.shape[0] // sub):
        rows = slice(r * sub, (r + 1) * sub)
        mix = jnp.concatenate([h[rows, :] for h in (h0, h1, h2, h3)], axis=1)
        acc = ALPHA * x_ref[rows, :] + jnp.dot(mix, w_ref[...], preferred_element_type=f32)
        y = _layernorm(acc, g_ref[...], b_ref[...])
        o_ref[rows, :] = y
        ob_ref[rows, :] = y.astype(bf16)


def _outproj_ln(hs, w, x, g, b, l, tm):
    m = x.shape[0]
    hspec = pl.BlockSpec((tm, GROUP), lambda i: (i, 0))
    xspec = pl.BlockSpec((tm, D_MODEL), lambda i: (i, 0))
    vspec = _layer_spec(l, (1, D_MODEL))
    return pl.pallas_call(
        _outproj_ln_body, grid=(m // tm,),
        in_specs=[hspec] * 4 + [_layer_spec(l, (D_MODEL, D_MODEL)), xspec, vspec, vspec],
        out_specs=[xspec, xspec],
        out_shape=[jax.ShapeDtypeStruct((m, D_MODEL), f32), jax.ShapeDtypeStruct((m, D_MODEL), bf16)],
        compiler_params=_cparams(("parallel",)), name="outproj_ln")(*hs, w, x, g, b)


def _ffn_ln_body(x_ref, xb_ref, wu_ref, wd_ref, g_ref, b_ref, o_ref, ob_ref, acc_ref):
    f = pl.program_id(1)

    @pl.when(f == 0)
    def _():
        acc_ref[...] = ALPHA * x_ref[...]

    h = jnp.maximum(jnp.dot(xb_ref[...], wu_ref[...], preferred_element_type=f32), 0.0)
    acc_ref[...] += jnp.dot((h * h).astype(bf16), wd_ref[...], preferred_element_type=f32)

    @pl.when(f == pl.num_programs(1) - 1)
    def _():
        y = _layernorm(acc_ref[...], g_ref[...], b_ref[...])
        o_ref[...] = y
        ob_ref[...] = y.astype(bf16)


def _ffn_ln(x, xb, wu, wd, g, b, l, tm, tf):
    m = x.shape[0]
    xspec = pl.BlockSpec((tm, D_MODEL), lambda i, f: (i, 0))
    vspec = _layer_spec(l, (1, D_MODEL))
    return pl.pallas_call(
        _ffn_ln_body, grid=(m // tm, D_FF // tf),
        in_specs=[xspec, xspec, _layer_spec(l, (D_MODEL, tf), lambda i, f: (0, f)),
                  _layer_spec(l, (tf, D_MODEL), lambda i, f: (f, 0)), vspec, vspec],
        out_specs=[xspec, xspec],
        out_shape=[jax.ShapeDtypeStruct((m, D_MODEL), f32), jax.ShapeDtypeStruct((m, D_MODEL), bf16)],
        scratch_shapes=[pltpu.VMEM((tm, D_MODEL), f32)],
        compiler_params=_cparams(("parallel", "arbitrary")), name="ffn_ln")(x, xb, wu, wd, g, b)


def _tril_blocks(nb, L):
    r, c = _iota((nb * L, nb * L), 0), _iota((nb * L, nb * L), 1)
    return jnp.where((c <= r) & (c >= r - (r & (L - 1))), 1.0, 0.0)


def _units(nb, heads):
    return [(b, h) for b in range(nb) for h in range(heads)]


def _mlstm_prompt_body(cols_ref, gb_ref, ng_ref, h_ref, c_ref, n_ref, m_ref):
    @pl.when(pl.program_id(0) == 0)
    def _():
        c_ref[...] = jnp.zeros_like(c_ref)
        n_ref[...] = jnp.zeros_like(n_ref)
        m_ref[...] = jnp.zeros_like(m_ref)

    nb, L = cols_ref.shape[0], cols_ref.shape[1]
    causal, _ = _tri_masks(L)
    g = cols_ref[:, :, 1536:1664].reshape(nb * L, LANES) + gb_ref[...]
    bc = _select_dot(_tril_blocks(nb, L), _log_sigmoid(g))
    y_all = jnp.where(_iota((nb * L, LANES), 1) < ML_H, g, bc)
    y = [y_all[b * L:(b + 1) * L, :] for b in range(nb)]
    yt = [_rows_of(y[b]) for b in range(nb)]
    us = _units(nb, ML_H)
    ix = range(len(us))
    q = [cols_ref[b, :, h * ML_DK:(h + 1) * ML_DK] for b, h in us]
    k = [cols_ref[b, :, 256 + h * ML_DK:256 + (h + 1) * ML_DK] * ML_DK ** -0.5 for b, h in us]
    v = [cols_ref[b, :, 512 + h * ML_DV:512 + (h + 1) * ML_DV] for b, h in us]
    c = [c_ref[b, h] for b, h in us]
    n = [n_ref[b, h:h + 1, :] for b, h in us]
    m = [m_ref[b, h:h + 1, :] for b, h in us]
    qk = [_ein_nt(q[i], k[i]) for i in ix]
    qc = [_ein(q[i], c[i]) for i in ix]
    qn = [_ein_nt(q[i], jnp.broadcast_to(n[i], (LANES, ML_DK))) for i in ix]
    spread_shape = (LANES, 2 * ML_H * LANES)
    spread = jnp.where(_iota(spread_shape, 0) == jnp.right_shift(_iota(spread_shape, 1), LANES.bit_length() - 1), 1.0, 0.0)
    rep = [_select_dot_right(y[b], spread) for b in range(nb)]
    i_rep = [rep[b][:, h * LANES:(h + 1) * LANES] for b, h in us]
    b_rep = [rep[b][:, (ML_H + h) * LANES:(ML_H + h + 1) * LANES] for b, h in us]
    d = [jnp.where(causal, b_rep[i][:, :L] - yt[b][ML_H + h:ML_H + h + 1, :] + yt[b][h:h + 1, :], -jnp.inf)
         for i, (b, h) in enumerate(us)]
    inter = [b_rep[i] + m[i] for i in ix]
    m_t = [jnp.maximum(inter[i], jnp.max(d[i], axis=-1, keepdims=True)) for i in ix]
    w_inter = [jnp.exp(inter[i] - m_t[i]) for i in ix]
    s = [qk[i] * jnp.exp(d[i] - m_t[i][:, :L]) for i in ix]
    m_new = [m_t[i][L - 1:L, :] for i in ix]
    b_last = [b_rep[i][L - 1:L, :] for i in ix]
    kw = [k[i] * jnp.exp(b_last[i] - b_rep[i] + i_rep[i] - m_new[i])[:, :ML_DK] for i in ix]
    sv = [_ein(s[i], v[i]) for i in ix]
    kv = [_ein_tn(kw[i], v[i]) for i in ix]
    s_sum = [jnp.sum(s[i], -1, keepdims=True) for i in ix]
    f_state = [jnp.exp(b_last[i] + m[i] - m_new[i]) for i in ix]
    c_new = [f_state[i] * c[i] + kv[i] for i in ix]
    n_new = [f_state[i][:, :ML_DK] * n[i] + jnp.sum(kw[i], axis=0, keepdims=True) for i in ix]
    for i, (b, h) in enumerate(us):
        c_ref[b, h] = c_new[i]
        n_ref[b, h:h + 1, :] = n_new[i]
        m_ref[b, h:h + 1, :] = m_new[i]
    hh = [(w_inter[i] * qc[i] + sv[i]) / jnp.maximum(jnp.abs(w_inter[i] * qn[i] + s_sum[i]), jnp.exp(-m_t[i])) for i in ix]
    ms = [jnp.mean(hh[i] * hh[i], -1, keepdims=True) for i in ix]
    for i, (b, h) in enumerate(us):
        og = cols_ref[b, :, 1024 + h * ML_DV:1024 + (h + 1) * ML_DV]
        hn = hh[i] * lax.rsqrt(ms[i] + NORM_EPS) * ng_ref[:, h * ML_DV:(h + 1) * ML_DV]
        h_ref[b, :, h * ML_DV:(h + 1) * ML_DV] = (hn * jax.nn.sigmoid(og)).astype(h_ref.dtype)


def _chunk_grid_specs(bp, t, width, col=0, chunk=CHUNK):
    cols = pl.BlockSpec((bp, chunk, width), lambda c: (0, c, col))
    out = pl.BlockSpec((bp, chunk, GROUP), lambda c: (0, c, 0))
    return (t // chunk,), cols, out


def _whole(shape):
    return pl.BlockSpec(shape, lambda c: (0,) * len(shape))


def _mlstm_prompt(cols, gb, ng, l, bp, t):
    grid, cspec, ospec = _chunk_grid_specs(bp, t, W_ML, chunk=WIDE_CHUNK)
    return pl.pallas_call(
        _mlstm_prompt_body, grid=grid,
        in_specs=[cspec, _layer_spec(l, (1, LANES)), _layer_spec(l, (1, GROUP))],
        out_specs=[ospec, _whole((bp, ML_H, ML_DK, ML_DV)), _whole((bp, ML_H, ML_DK)), _whole((bp, ML_H, LANES))],
        out_shape=[jax.ShapeDtypeStruct((bp, t, GROUP), bf16), jax.ShapeDtypeStruct((bp, ML_H, ML_DK, ML_DV), f32),
                   jax.ShapeDtypeStruct((bp, ML_H, ML_DK), f32), jax.ShapeDtypeStruct((bp, ML_H, LANES), f32)],
        compiler_params=_cparams(("arbitrary",)), name="mlstm_prompt")(cols.reshape(bp, t, W_A), gb, ng)


def _gla_prompt_body(cols_ref, gup_ref, gb_ref, ng_ref, h_ref, s_ref):
    @pl.when(pl.program_id(0) == 0)
    def _():
        s_ref[...] = jnp.zeros_like(s_ref)

    nb, L = cols_ref.shape[0], cols_ref.shape[1]
    causal, _ = _tri_masks(L)
    lr = cols_ref[:, :, 1536:1664].reshape(nb * L, LANES)
    la = _log_sigmoid(_ein(lr, gup_ref[...]) + gb_ref[...]) / GLA_TAU
    bc_all = _select_dot(_tril_blocks(nb, L), la)
    bc = [bc_all[b * L:(b + 1) * L, :] for b in range(nb)]
    mid = [bc[b][L // 2:L // 2 + 1, :] for b in range(nb)]
    last = [bc[b][L - 1:L, :] for b in range(nb)]
    e_q_mid = [jnp.exp(bc[b] - mid[b]) for b in range(nb)]
    e_k_mid = [jnp.exp(mid[b] - bc[b]) for b in range(nb)]
    e_q = [jnp.exp(bc[b]) for b in range(nb)]
    e_k_last = [jnp.exp(last[b] - bc[b]) for b in range(nb)]
    e_last = [jnp.exp(last[b]) for b in range(nb)]
    us = _units(nb, GLA_H)
    ix = range(len(us))
    sl = [slice(h * GLA_DK, (h + 1) * GLA_DK) for _, h in us]
    q = [cols_ref[b, :, h * GLA_DK:(h + 1) * GLA_DK] * GLA_DK ** -0.5 for b, h in us]
    k = [cols_ref[b, :, 256 + h * GLA_DK:256 + (h + 1) * GLA_DK] for b, h in us]
    v = [cols_ref[b, :, 512 + h * GLA_DV:512 + (h + 1) * GLA_DV] for b, h in us]
    s = [s_ref[b, h] for b, h in us]
    a = [jnp.where(causal, _ein_nt(q[i] * e_q_mid[b][:, sl[i]], k[i] * e_k_mid[b][:, sl[i]]), 0.0)
         for i, (b, _) in enumerate(us)]
    o0 = [_ein(q[i] * e_q[b][:, sl[i]], s[i]) for i, (b, _) in enumerate(us)]
    kv = [_ein_tn(k[i] * e_k_last[b][:, sl[i]], v[i]) for i, (b, _) in enumerate(us)]
    o = [o0[i] + _ein(a[i], v[i]) for i in ix]
    for i, (b, h) in enumerate(us):
        s_ref[b, h] = _col(e_last[b][:, sl[i]]) * s[i] + kv[i]
    ms = [jnp.mean(o[i] * o[i], -1, keepdims=True) for i in ix]
    for i, (b, h) in enumerate(us):
        og = cols_ref[b, :, 1024 + h * GLA_DV:1024 + (h + 1) * GLA_DV]
        on = o[i] * lax.rsqrt(ms[i] + NORM_EPS) * ng_ref[:, h * GLA_DV:(h + 1) * GLA_DV]
        h_ref[b, :, h * GLA_DV:(h + 1) * GLA_DV] = (on * _silu(og)).astype(h_ref.dtype)


def _gla_prompt(cols, gup, gb, ng, l, bp, t):
    grid, cspec, ospec = _chunk_grid_specs(bp, t, W_GLA, col=1, chunk=WIDE_CHUNK)
    return pl.pallas_call(
        _gla_prompt_body, grid=grid,
        in_specs=[cspec, _layer_spec(l, (LANES, GLA_H * GLA_DK)), _layer_spec(l, (1, GLA_H * GLA_DK)),
                  _layer_spec(l, (1, GROUP))],
        out_specs=[ospec, _whole((bp, GLA_H, GLA_DK, GLA_DV))],
        out_shape=[jax.ShapeDtypeStruct((bp, t, GROUP), bf16), jax.ShapeDtypeStruct((bp, GLA_H, GLA_DK, GLA_DV), f32)],
        compiler_params=_cparams(("arbitrary",)), name="gla_prompt")(cols.reshape(bp, t, W_A), gup, gb, ng)


def _rwkv_pre(x, prev, mu, wup, aup, gup, vec):
    xs = x + (prev - x) * mu
    r, k, v, lb = xs[:, 0:512], xs[:, 512:1024], xs[:, 1024:1536], xs[:, 1536:1664]
    w0, a0, k_k, k_a = vec[0:1, :], vec[1:2, :], vec[2:3, :], vec[3:4, :]
    lw = -RW_DECAY_SCALE * jax.nn.sigmoid(w0 + _ein(jnp.tanh(lb), wup))
    a = jax.nn.sigmoid(a0 + _ein(lb, aup))
    g = _ein(jax.nn.sigmoid(lb), gup)
    kk_raw = k * k_k
    k2 = k * (1.0 + (a - 1.0) * k_a)
    return r, k2, v, lw, a, g, kk_raw


def _rwkv_post(o, r, k2, v, g, vec):
    hs = range(RW_H)
    sl = [slice(h * RW_N, (h + 1) * RW_N) for h in hs]
    mu_o = [jnp.mean(o[h], -1, keepdims=True) for h in hs]
    bonus_w = [jnp.sum(r[:, sl[h]] * k2[:, sl[h]] * vec[4:5, sl[h]], -1, keepdims=True) for h in hs]
    oc = [o[h] - mu_o[h] for h in hs]
    var_o = [jnp.mean(oc[h] * oc[h], -1, keepdims=True) for h in hs]
    outs = [(oc[h] * lax.rsqrt(var_o[h] + RW_GN_EPS) * vec[5:6, sl[h]] + vec[6:7, sl[h]]
             + bonus_w[h] * v[:, sl[h]]) * g[:, sl[h]] for h in hs]
    return jnp.concatenate(outs, axis=-1)


def _rwkv_prompt_body(cols_ref, mu_ref, wup_ref, aup_ref, gup_ref, vec_ref, h_ref, s_ref, prev_scr):
    @pl.when(pl.program_id(0) == 0)
    def _():
        s_ref[...] = jnp.zeros_like(s_ref)
        prev_scr[...] = jnp.zeros_like(prev_scr)

    nb, L, N = cols_ref.shape[0], CHUNK, RW_N
    first_row = _iota((L, W_RW), 0) == 0
    xb = [cols_ref[b, :, RW_OFF:RW_OFF + W_RW] for b in range(nb)]
    prevs = [jnp.where(first_row, prev_scr[b, 0:1, :], pltpu.roll(xb[b], 1, 0)) for b in range(nb)]
    for b in range(nb):
        prev_scr[b, 0:1, :] = xb[b][L - 1:L, :]
    vec = vec_ref[...]
    r, k2, v, lw, a, g, kk_raw = _rwkv_pre(jnp.concatenate(xb, axis=0), jnp.concatenate(prevs, axis=0), mu_ref[...],
                                           wup_ref[...], aup_ref[...], gup_ref[...], vec)
    cum = _select_dot(_tril_blocks(nb, L), lw)
    cum_prev = cum - lw
    rep = lambda row_of: jnp.concatenate(
        [jnp.broadcast_to(cum[b * L + row_of:b * L + row_of + 1, :], (L, RW_H * N)) for b in range(nb)], axis=0)
    mid, last = rep(L // 2), rep(L - 1)
    e_prev_mid, e_mid_cum, e_cum_mid = jnp.exp(cum_prev - mid), jnp.exp(mid - cum), jnp.exp(cum - mid)
    e_prev, e_cum, e_last_cum, e_last = jnp.exp(cum_prev), jnp.exp(cum), jnp.exp(last - cum), jnp.exp(last)
    row, col = _iota((2 * L, 2 * L), 0), _iota((2 * L, 2 * L), 1)
    tq, sq = row % L, col % L
    quad = sq < tq + jnp.where(row < L, 0, 1)
    left = col[:, :] < L
    hs = range(RW_H)
    sl = [slice(h * N, (h + 1) * N) for h in hs]
    kk_h = [_l2norm(kk_raw[:, sl[h]]) for h in hs]
    b_h = [kk_h[h] * a[:, sl[h]] for h in hs]
    f_lhs = [(kk_h[h] * e_prev_mid[:, sl[h]], r[:, sl[h]] * e_cum_mid[:, sl[h]]) for h in hs]
    f_rhs = [(b_h[h] * e_mid_cum[:, sl[h]], k2[:, sl[h]] * e_mid_cum[:, sl[h]]) for h in hs]
    f_x0 = [(kk_h[h] * e_prev[:, sl[h]], r[:, sl[h]] * e_cum[:, sl[h]]) for h in hs]
    f_dec = [(b_h[h] * e_last_cum[:, sl[h]], k2[:, sl[h]] * e_last_cum[:, sl[h]]) for h in hs]
    us = _units(nb, RW_H)
    ix = range(len(us))
    rows = lambda z, b: z[b * L:(b + 1) * L, :]
    pair = lambda f, b, h: jnp.concatenate([rows(f[h][0], b), rows(f[h][1], b)], axis=0)
    vh = [rows(v[:, sl[h]], b) for b, h in us]
    s = [s_ref[b, h] for b, h in us]
    a_all = [jnp.where(quad, _ein_nt(pair(f_lhs, b, h), pair(f_rhs, b, h)), 0.0) for b, h in us]
    t_inv = _unit_lower_inverses([a_all[i][0:L, 0:L] for i in ix])
    x0 = [pair(f_x0, b, h) for b, h in us]
    m2 = [jnp.where(left, jnp.concatenate([x0[i], x0[i]], axis=1), a_all[i]) for i in ix]
    y = [_ein(m2[i], jnp.concatenate([s[i], vh[i]], axis=0)) for i in ix]
    u = [-_dot3(t_inv[i], y[i][0:L, :]) for i in ix]
    o = [y[i][L:2 * L, :] + _ein(a_all[i][L:2 * L, 0:L], u[i]) for i in ix]
    for i, (b, h) in enumerate(us):
        s_ref[b, h] = (_col(e_last[b * L:b * L + 1, sl[h]]) * s[i]
                       + _ein_tn(pair(f_dec, b, h), jnp.concatenate([u[i], vh[i]], axis=0)))
    o_heads = [jnp.concatenate([o[b * RW_H + h] for b in range(nb)], axis=0) for h in hs]
    h_ref[...] = _rwkv_post(o_heads, r, k2, v, g, vec).reshape(nb, L, GROUP).astype(h_ref.dtype)


def _rwkv_prompt(cols, mu, wup, aup, gup, vec, l, bp, t):
    grid, cspec, ospec = _chunk_grid_specs(bp, t, W_B)
    return pl.pallas_call(
        _rwkv_prompt_body, grid=grid,
        in_specs=[cspec, _layer_spec(l, (1, W_RW)), _layer_spec(l, (LANES, GROUP)), _layer_spec(l, (LANES, GROUP)),
                  _layer_spec(l, (LANES, GROUP)), _layer_spec(l, (8, GROUP))],
        out_specs=[ospec, _whole((bp, RW_H, RW_N, RW_N))],
        out_shape=[jax.ShapeDtypeStruct((bp, t, GROUP), bf16), jax.ShapeDtypeStruct((bp, RW_H, RW_N, RW_N), f32)],
        scratch_shapes=[pltpu.VMEM((bp, 8, W_RW), f32)],
        compiler_params=_cparams(("arbitrary",)), name="rwkv_prompt")(cols.reshape(bp, t, W_B), mu, wup, aup, gup, vec)


def _gdn_gates(gb, alog, dtb):
    return jax.nn.sigmoid(gb), -jnp.exp(alog) * _softplus(gb + dtb)


def _gdn_prompt_body(cols_ref, cw_ref, alog_ref, dtb_ref, ng_ref, h_ref, s_ref, xp_scr):
    @pl.when(pl.program_id(0) == 0)
    def _():
        s_ref[...] = jnp.zeros_like(s_ref)
        xp_scr[...] = jnp.zeros_like(xp_scr)

    nb, L = cols_ref.shape[0], CHUNK
    causal, strict = _tri_masks(L)
    act = []
    for b in range(nb):
        raw = cols_ref[b, :, 0:DN_QKV]
        ext = jnp.concatenate([xp_scr[b], raw], axis=0)
        conv = raw * cw_ref[DN_CONV - 1:DN_CONV, :]
        for j in range(1, DN_CONV):
            conv = conv + pltpu.roll(ext, j, 0)[8:8 + L, :] * cw_ref[DN_CONV - 1 - j:DN_CONV - j, :]
        xp_scr[b] = raw[L - 8:L, :]
        act.append(_silu(conv))
    beta_all, g_all = _gdn_gates(cols_ref[:, :, 2048:2176].reshape(nb * L, LANES), alog_ref[...], dtb_ref[...])
    gc_all = _select_dot(_tril_blocks(nb, L), g_all)
    gc = [gc_all[b * L:(b + 1) * L, :] for b in range(nb)]
    gt = [_rows_of(gc[b]) for b in range(nb)]
    us = _units(nb, DN_H)
    ix = range(len(us))
    q = [_l2norm(act[b][:, h * DN_D:(h + 1) * DN_D]) * DN_D ** -0.5 for b, h in us]
    k = [_l2norm(act[b][:, GROUP + h * DN_D:GROUP + (h + 1) * DN_D]) for b, h in us]
    v = [act[b][:, 2 * GROUP + h * DN_D:2 * GROUP + (h + 1) * DN_D] for b, h in us]
    beta = [beta_all[b * L:(b + 1) * L, h:h + 1] for b, h in us]
    g_col = [gc[b][:, DN_H + h:DN_H + h + 1] for b, h in us]
    decay = [jnp.exp(jnp.where(causal, g_col[i] - gt[b][DN_H + h:DN_H + h + 1, :], -jnp.inf))
             for i, (b, h) in enumerate(us)]
    kb = [k[i] * beta[i] for i in ix]
    eg = [jnp.exp(g_col[i]) for i in ix]
    s = [s_ref[b, h] for b, h in us]
    kq = [_ein_nt(jnp.concatenate([kb[i], q[i]], axis=0), k[i]) for i in ix]
    t_inv = _unit_lower_inverses([jnp.where(strict, kq[i][0:L, :] * decay[i], 0.0) for i in ix])
    qk = [kq[i][L:2 * L, :] * decay[i] for i in ix]
    o0 = [_ein(q[i] * eg[i], s[i]) for i in ix]
    sol = [_dot3(t_inv[i], jnp.concatenate([v[i] * beta[i], kb[i] * eg[i]], axis=-1)) for i in ix]
    v_new = [sol[i][:, :DN_D] - _ein(sol[i][:, DN_D:], s[i]) for i in ix]
    o = [o0[i] + _ein(qk[i], v_new[i]) for i in ix]
    for i, (b, h) in enumerate(us):
        g_last = g_col[i][L - 1:L, :]
        s_ref[b, h] = jnp.exp(g_last) * s[i] + _ein_tn(k[i] * jnp.exp(g_last - g_col[i]), v_new[i])
    ms = [jnp.mean(o[i] * o[i], -1, keepdims=True) for i in ix]
    for i, (b, h) in enumerate(us):
        z = cols_ref[b, :, DN_QKV + h * DN_D:DN_QKV + (h + 1) * DN_D]
        on = o[i] * lax.rsqrt(ms[i] + NORM_EPS) * ng_ref[...]
        h_ref[b, :, h * DN_D:(h + 1) * DN_D] = (on * _silu(z)).astype(h_ref.dtype)


def _gdn_prompt(cols, cw, alog, dtb, ng, l, bp, t):
    grid, cspec, ospec = _chunk_grid_specs(bp, t, W_DN)
    return pl.pallas_call(
        _gdn_prompt_body, grid=grid,
        in_specs=[cspec, _layer_spec(l, (8, DN_QKV)), _layer_spec(l, (1, LANES)), _layer_spec(l, (1, LANES)),
                  _layer_spec(l, (1, DN_D))],
        out_specs=[ospec, _whole((bp, DN_H, DN_D, DN_D))],
        out_shape=[jax.ShapeDtypeStruct((bp, t, GROUP), bf16), jax.ShapeDtypeStruct((bp, DN_H, DN_D, DN_D), f32)],
        scratch_shapes=[pltpu.VMEM((bp, 8, DN_QKV), f32)],
        compiler_params=_cparams(("arbitrary",)), name="gdn_prompt")(cols.reshape(bp, t, W_B), cw, alog, dtb, ng)


def _mlstm_step_body(cols_ref, gb_ref, ng_ref, c_ref, n_ref, m_ref, h_ref, co_ref, no_ref, mo_ref,
                     q_scr, kw_scr, v_scr, f_scr, qc_scr):
    nb = DEC_BLOCK
    g = cols_ref[:, 1536:1664] + gb_ref[...]
    lf = _log_sigmoid(g)
    hs = []
    for h in range(ML_H):
        q = cols_ref[:, h * ML_DK:(h + 1) * ML_DK]
        k = cols_ref[:, 256 + h * ML_DK:256 + (h + 1) * ML_DK] * ML_DK ** -0.5
        v = cols_ref[:, 512 + h * ML_DV:512 + (h + 1) * ML_DV]
        i_pre, f_log = g[:, h:h + 1], lf[:, ML_H + h:ML_H + h + 1]
        m = m_ref[:, h:h + 1]
        inter = f_log + m
        m_t = jnp.maximum(inter, i_pre)
        w_inter = jnp.exp(inter - m_t)
        kw = k * jnp.exp(i_pre - m_t)
        q_scr[h], kw_scr[h], v_scr[h] = q, kw, v
        f_scr[h] = jnp.broadcast_to(w_inter, (nb, LANES))
        hs.append((q, k, v, i_pre, m_t, w_inter, kw))

    def row(b, carry):
        hr = range(ML_H)
        kw_col = [_col(kw_scr[h, pl.ds(b, 1), :]) for h in hr]
        c = [c_ref[b, h] for h in hr]
        qc = [_row_of(_ein(q_scr[h], c[h]), b) for h in hr]
        for h in hr:
            qc_scr[h, pl.ds(b, 1), :] = qc[h]
            co_ref[b, h] = f_scr[h, pl.ds(b, 1), :] * c[h] + kw_col[h] * v_scr[h, pl.ds(b, 1), :]
        return carry

    lax.fori_loop(0, nb, row, 0)
    for h, (q, k, v, i_pre, m_t, w_inter, kw) in enumerate(hs):
        og = cols_ref[:, 1024 + h * ML_DV:1024 + (h + 1) * ML_DV]
        n = n_ref[:, h * ML_DK:(h + 1) * ML_DK]
        s = jnp.sum(q * k, -1, keepdims=True) * jnp.exp(i_pre - m_t)
        num = w_inter * qc_scr[h] + s * v
        den = w_inter * jnp.sum(q * n, -1, keepdims=True) + s
        hh = num / jnp.maximum(jnp.abs(den), jnp.exp(-m_t))
        no_ref[:, h * ML_DK:(h + 1) * ML_DK] = w_inter * n + kw
        mo_ref[:, h:h + 1] = m_t
        hn = hh * lax.rsqrt(jnp.mean(hh * hh, -1, keepdims=True) + NORM_EPS) * ng_ref[:, h * ML_DV:(h + 1) * ML_DV]
        h_ref[:, h * ML_DV:(h + 1) * ML_DV] = (hn * jax.nn.sigmoid(og)).astype(h_ref.dtype)


def _row_spec(width, col=0):
    return pl.BlockSpec((DEC_BLOCK, width), lambda i: (i, col))


def _layer_rows(l, width):
    return _layer_spec(l, (DEC_BLOCK, width), lambda i: (i, 0))


def _layer_state(l, h, dk, dv):
    return _layer_spec(l, (DEC_BLOCK, h, dk, dv), lambda i: (i, 0, 0, 0))


def _ignore_first_ref(body):
    def with_handed_on_buffer(_stacked_out_so_far, *refs):
        body(*refs)
    return with_handed_on_buffer


def _step_call(body, l, stacked_out, in_specs, out_specs, out_shapes, state_out_index, **kw):
    if stacked_out is None:
        return lambda *args: pl.pallas_call(body, in_specs=in_specs, out_specs=out_specs, out_shape=out_shapes, **kw)(*args)
    call = pl.pallas_call(_ignore_first_ref(body), in_specs=[pl.BlockSpec(memory_space=pl.ANY)] + in_specs,
                          out_specs=out_specs, out_shape=out_shapes, input_output_aliases={0: state_out_index}, **kw)
    return lambda *args: call(stacked_out, *args)


def _mlstm_step(cols, gb, ng, c0, n0, m0, l, c_out):
    bs = cols.shape[0]
    nb = DEC_BLOCK
    call = _step_call(
        _mlstm_step_body, l, c_out,
        in_specs=[_row_spec(W_ML), _layer_spec(l, (1, LANES)), _layer_spec(l, (1, GROUP)),
                  _layer_state(l, ML_H, ML_DK, ML_DV), _layer_rows(l, ML_H * ML_DK), _layer_rows(l, ML_H)],
        out_specs=[_row_spec(GROUP), _layer_state(l, ML_H, ML_DK, ML_DV), _row_spec(ML_H * ML_DK), _row_spec(ML_H)],
        out_shapes=[jax.ShapeDtypeStruct((bs, GROUP), bf16), jax.ShapeDtypeStruct(c0.shape, f32),
                    jax.ShapeDtypeStruct((bs, ML_H * ML_DK), f32), jax.ShapeDtypeStruct((bs, ML_H), f32)],
        state_out_index=1, grid=(bs // nb,),
        scratch_shapes=[pltpu.VMEM((ML_H, nb, ML_DK), f32), pltpu.VMEM((ML_H, nb, ML_DK), f32),
                        pltpu.VMEM((ML_H, nb, ML_DV), f32), pltpu.VMEM((ML_H, nb, LANES), f32),
                        pltpu.VMEM((ML_H, nb, ML_DV), f32)],
        compiler_params=_cparams(("parallel",)), name="mlstm_step")
    return call(cols, gb, ng, c0, n0, m0)


def _gla_step_body(cols_ref, gup_ref, gb_ref, ng_ref, s_ref, h_ref, so_ref, qe_scr, ea_scr, k_scr, v_scr, qs_scr):
    nb = DEC_BLOCK
    la = _log_sigmoid(_ein(cols_ref[:, 1536:1664], gup_ref[...]) + gb_ref[...]) / GLA_TAU
    ea = jnp.exp(la)
    hs = []
    for h in range(GLA_H):
        q = cols_ref[:, h * GLA_DK:(h + 1) * GLA_DK] * GLA_DK ** -0.5
        k = cols_ref[:, 256 + h * GLA_DK:256 + (h + 1) * GLA_DK]
        v = cols_ref[:, 512 + h * GLA_DV:512 + (h + 1) * GLA_DV]
        ea_h = ea[:, h * GLA_DK:(h + 1) * GLA_DK]
        qe_scr[h], ea_scr[h], k_scr[h], v_scr[h] = q * ea_h, ea_h, k, v
        hs.append((q, k, v))

    def row(b, carry):
        hr = range(GLA_H)
        ea_col = [_col(ea_scr[h, pl.ds(b, 1), :]) for h in hr]
        k_col = [_col(k_scr[h, pl.ds(b, 1), :]) for h in hr]
        s = [s_ref[b, h] for h in hr]
        qs = [_row_of(_ein(qe_scr[h], s[h]), b) for h in hr]
        for h in hr:
            qs_scr[h, pl.ds(b, 1), :] = qs[h]
            so_ref[b, h] = ea_col[h] * s[h] + k_col[h] * v_scr[h, pl.ds(b, 1), :]
        return carry

    lax.fori_loop(0, nb, row, 0)
    for h, (q, k, v) in enumerate(hs):
        og = cols_ref[:, 1024 + h * GLA_DV:1024 + (h + 1) * GLA_DV]
        o = qs_scr[h] + jnp.sum(q * k, -1, keepdims=True) * v
        on = o * lax.rsqrt(jnp.mean(o * o, -1, keepdims=True) + NORM_EPS) * ng_ref[:, h * GLA_DV:(h + 1) * GLA_DV]
        h_ref[:, h * GLA_DV:(h + 1) * GLA_DV] = (on * _silu(og)).astype(h_ref.dtype)


def _gla_step(cols, gup, gb, ng, s0, l, s_out):
    bs = cols.shape[0]
    nb = DEC_BLOCK
    call = _step_call(
        _gla_step_body, l, s_out,
        in_specs=[_row_spec(W_GLA, col=1), _layer_spec(l, (LANES, GLA_H * GLA_DK)), _layer_spec(l, (1, GLA_H * GLA_DK)),
                  _layer_spec(l, (1, GROUP)), _layer_state(l, GLA_H, GLA_DK, GLA_DV)],
        out_specs=[_row_spec(GROUP), _layer_state(l, GLA_H, GLA_DK, GLA_DV)],
        out_shapes=[jax.ShapeDtypeStruct((bs, GROUP), bf16), jax.ShapeDtypeStruct(s0.shape, f32)],
        state_out_index=1, grid=(bs // nb,),
        scratch_shapes=[pltpu.VMEM((GLA_H, nb, GLA_DK), f32)] * 3 + [pltpu.VMEM((GLA_H, nb, GLA_DV), f32)] * 2,
        compiler_params=_cparams(("parallel",)), name="gla_step")
    return call(cols, gup, gb, ng, s0)


def _rwkv_step_body(cols_ref, shift_ref, mu_ref, wup_ref, aup_ref, gup_ref, vec_ref, s_ref, h_ref, so_ref,
                    w_scr, b_scr, k_scr, kkrw_scr, v_scr, sk_scr, o_scr):
    nb = DEC_BLOCK
    vec = vec_ref[...]
    r, k2, v, lw, a, g, kk_raw = _rwkv_pre(cols_ref[:, RW_OFF:RW_OFF + W_RW], shift_ref[...], mu_ref[...], wup_ref[...], aup_ref[...],
                                           gup_ref[...], vec)
    hr = range(RW_H)
    sl = [slice(h * RW_N, (h + 1) * RW_N) for h in hr]
    kk = [_l2norm(kk_raw[:, sl[h]]) for h in hr]
    w = jnp.exp(lw)
    bv = jnp.concatenate(kk, axis=-1) * a
    w_scr[...], b_scr[...], k_scr[...] = w, bv, k2
    for h in hr:
        kkrw_scr[h, 0:nb, :], kkrw_scr[h, nb:2 * nb, :] = kk[h], r[:, sl[h]] * w[:, sl[h]]
        v_scr[h] = v[:, sl[h]]

    def row(b, carry):
        w_col, b_col, k_col = (_cols_of(scr[pl.ds(b, 1), :], RW_N) for scr in (w_scr, b_scr, k_scr))
        s = [s_ref[b, h] for h in hr]
        mv = [_ein(kkrw_scr[h], s[h]) for h in hr]
        sk = [_row_of(mv[h][0:nb, :], b) for h in hr]
        rws = [_row_of(mv[h][nb:2 * nb, :], b) for h in hr]
        for h in hr:
            so_ref[b, h] = w_col[h] * s[h] - b_col[h] * sk[h] + k_col[h] * v_scr[h, pl.ds(b, 1), :]
            sk_scr[h, pl.ds(b, 1), :] = sk[h]
            o_scr[h, pl.ds(b, 1), :] = rws[h]
        return carry

    lax.fori_loop(0, nb, row, 0)
    rb = [jnp.sum(r[:, sl[h]] * bv[:, sl[h]], -1, keepdims=True) for h in hr]
    rk = [jnp.sum(r[:, sl[h]] * k2[:, sl[h]], -1, keepdims=True) for h in hr]
    o = [o_scr[h] - rb[h] * sk_scr[h] + rk[h] * v[:, sl[h]] for h in hr]
    h_ref[...] = _rwkv_post(o, r, k2, v, g, vec).astype(h_ref.dtype)


def _rwkv_step(cols, shift0, mu, wup, aup, gup, vec, s0, l, s_out):
    bs = cols.shape[0]
    nb = DEC_BLOCK
    call = _step_call(
        _rwkv_step_body, l, s_out,
        in_specs=[_row_spec(W_B), _layer_rows(l, W_RW), _layer_spec(l, (1, W_RW)), _layer_spec(l, (LANES, GROUP)),
                  _layer_spec(l, (LANES, GROUP)), _layer_spec(l, (LANES, GROUP)), _layer_spec(l, (8, GROUP)),
                  _layer_state(l, RW_H, RW_N, RW_N)],
        out_specs=[_row_spec(GROUP), _layer_state(l, RW_H, RW_N, RW_N)],
        out_shapes=[jax.ShapeDtypeStruct((bs, GROUP), bf16), jax.ShapeDtypeStruct(s0.shape, f32)],
        state_out_index=1, grid=(bs // nb,),
        scratch_shapes=[pltpu.VMEM((nb, GROUP), f32)] * 3 + [pltpu.VMEM((RW_H, 2 * nb, RW_N), f32)]
        + [pltpu.VMEM((RW_H, nb, RW_N), f32)] * 3,
        compiler_params=_cparams(("parallel",)), name="rwkv_step")
    return call(cols, shift0, mu, wup, aup, gup, vec, s0)


def _gdn_step_body(cols_ref, buf_ref, cw_ref, alog_ref, dtb_ref, ng_ref, s_ref, h_ref, so_ref,
                   kq_scr, v_scr, sc_scr, qs_scr, vn_scr):
    nb = DEC_BLOCK
    conv = cols_ref[:, 0:DN_QKV] * cw_ref[DN_CONV - 1:DN_CONV, :]
    for w in range(DN_CONV - 1):
        conv = conv + buf_ref[:, w * DN_QKV:(w + 1) * DN_QKV] * cw_ref[w:w + 1, :]
    act = _silu(conv)
    beta_all, g_all = _gdn_gates(cols_ref[:, 2048:2176], alog_ref[...], dtb_ref[...])
    eg_all = jnp.exp(g_all)
    hs = []
    for h in range(DN_H):
        q = _l2norm(act[:, h * DN_D:(h + 1) * DN_D]) * DN_D ** -0.5
        k = _l2norm(act[:, GROUP + h * DN_D:GROUP + (h + 1) * DN_D])
        v = act[:, 2 * GROUP + h * DN_D:2 * GROUP + (h + 1) * DN_D]
        beta, eg = beta_all[:, h:h + 1], eg_all[:, DN_H + h:DN_H + h + 1]
        kq_scr[h, 0:nb, :], kq_scr[h, nb:2 * nb, :], v_scr[h] = k, q, v
        sc_scr[h] = jnp.where(_iota((nb, LANES), 1) == 0, beta, eg)
        hs.append((q, k, eg))

    def row(b, carry):
        hr = range(DN_H)
        k_col = [_col(kq_scr[h, pl.ds(b, 1), :]) for h in hr]
        s = [s_ref[b, h] for h in hr]
        sc = [sc_scr[h, pl.ds(b, 1), :] for h in hr]
        mv = [_ein(kq_scr[h], s[h]) for h in hr]
        ks = [_row_of(mv[h][0:nb, :], b) for h in hr]
        qs = [_row_of(mv[h][nb:2 * nb, :], b) for h in hr]
        for h in hr:
            beta, eg = sc[h][:, 0:1], sc[h][:, 1:2]
            v_new = beta * (v_scr[h, pl.ds(b, 1), :] - eg * ks[h])
            qs_scr[h, pl.ds(b, 1), :] = qs[h]
            vn_scr[h, pl.ds(b, 1), :] = v_new
            so_ref[b, h] = eg * s[h] + k_col[h] * v_new
        return carry

    lax.fori_loop(0, nb, row, 0)
    for h, (q, k, eg) in enumerate(hs):
        z = cols_ref[:, DN_QKV + h * DN_D:DN_QKV + (h + 1) * DN_D]
        o = eg * qs_scr[h] + jnp.sum(q * k, -1, keepdims=True) * vn_scr[h]
        on = o * lax.rsqrt(jnp.mean(o * o, -1, keepdims=True) + NORM_EPS) * ng_ref[...]
        h_ref[:, h * DN_D:(h + 1) * DN_D] = (on * _silu(z)).astype(h_ref.dtype)


def _gdn_step(cols, buf, cw, alog, dtb, ng, s0, l, s_out):
    bs = cols.shape[0]
    nb = DEC_BLOCK
    call = _step_call(
        _gdn_step_body, l, s_out,
        in_specs=[_row_spec(W_DN), _layer_rows(l, (DN_CONV - 1) * DN_QKV), _layer_spec(l, (8, DN_QKV)),
                  _layer_spec(l, (1, LANES)), _layer_spec(l, (1, LANES)), _layer_spec(l, (1, DN_D)),
                  _layer_state(l, DN_H, DN_D, DN_D)],
        out_specs=[_row_spec(GROUP), _layer_state(l, DN_H, DN_D, DN_D)],
        out_shapes=[jax.ShapeDtypeStruct((bs, GROUP), bf16), jax.ShapeDtypeStruct(s0.shape, f32)],
        state_out_index=1, grid=(bs // nb,),
        scratch_shapes=[pltpu.VMEM((DN_H, 2 * nb, DN_D), f32), pltpu.VMEM((DN_H, nb, DN_D), f32),
                        pltpu.VMEM((DN_H, nb, LANES), f32)] + [pltpu.VMEM((DN_H, nb, DN_D), f32)] * 2,
        compiler_params=_cparams(("parallel",)), name="gdn_step")
    return call(cols, buf, cw, alog, dtb, ng, s0)


def _pad_last(w, width):
    return jnp.pad(w, [(0, 0)] * (w.ndim - 1) + [(0, width - w.shape[-1])])


def _pad_rows(w, start, total):
    return jnp.pad(w, ((0, 0), (start, total - start - w.shape[1]), (0, 0)))


def _lane_rows(v, start):
    return jnp.pad(v, ((0, 0), (start, LANES - start - v.shape[1])))[:, None, :]


def kernel(x_prompt, x_sample, state_mlstm_c, state_mlstm_n, state_mlstm_m, state_gla, state_rwkv, state_rwkv_shift, state_dn, state_dn_conv, w_in, w_out, w_up, w_down, ln1_g, ln1_b, ln2_g, ln2_b, mlstm_gate_b, mlstm_norm_g, gla_gate_up, gla_gate_b, gla_norm_g, rwkv_mu, rwkv_w_up, rwkv_w0, rwkv_a_up, rwkv_a0, rwkv_g_up, rwkv_k_k, rwkv_k_a, rwkv_r_k, rwkv_norm_g, rwkv_norm_b, dn_conv_w, dn_a_log, dn_dt_bias, dn_norm_g):
    bp, t, _ = x_prompt.shape
    bs = x_sample.shape[0]
    xp = x_prompt.reshape(bp * t, D_MODEL)
    xs = x_sample.reshape(bs, D_MODEL)
    xpb, xsb = xp, xs
    tm_p, tm_s, tf = 512, bs, 1024

    o1, o2, o3 = N_ML, N_ML + N_GLA, N_ML + N_GLA + N_RW
    w_ab = [jnp.concatenate([_pad_last(w_in[:, :, 0:o1].astype(bf16), W_ML),
                             _pad_last(w_in[:, :, o1:o2].astype(bf16), W_GLA)], axis=-1),
            jnp.concatenate([_pad_last(w_in[:, :, o3:].astype(bf16), W_DN), w_in[:, :, o2:o3].astype(bf16)], axis=-1)]
    wo, wu, wd = w_out.astype(bf16), w_up.astype(bf16), w_down.astype(bf16)
    g1, b1, g2, b2 = ln1_g[:, None], ln1_b[:, None], ln2_g[:, None], ln2_b[:, None]
    ml_gb, ml_ng = _lane_rows(mlstm_gate_b, 0), mlstm_norm_g[:, None]
    gla_gup, gla_gb, gla_ng = _pad_rows(gla_gate_up, 0, LANES), gla_gate_b[:, None], gla_norm_g[:, None]
    rw_mu = rwkv_mu[:, None]
    rw_wup, rw_aup, rw_gup = _pad_rows(rwkv_w_up, 0, LANES), _pad_rows(rwkv_a_up, 32, LANES), _pad_rows(rwkv_g_up, 64, LANES)
    rw_vec = jnp.stack([rwkv_w0, rwkv_a0, rwkv_k_k, rwkv_k_a, rwkv_r_k, rwkv_norm_g, rwkv_norm_b,
                        jnp.zeros_like(rwkv_w0)], axis=1)
    dn_cw = _pad_rows(dn_conv_w, 0, 8)
    dn_alog, dn_dtb, dn_ng = _lane_rows(dn_a_log, DN_H), _lane_rows(dn_dt_bias, DN_H), dn_norm_g[:, None]
    st_ml_n = state_mlstm_n.reshape(DEPTH, bs, ML_H * ML_DK)
    st_dn_conv = state_dn_conv.reshape(DEPTH, bs, (DN_CONV - 1) * DN_QKV)

    outs_p, outs_s = [], []
    ml_c_s = gla_s = rw_s = dn_s = None
    for l in range(DEPTH):
        c_a, c_b = [_proj(xpb, w, l, tm_p) for w in w_ab]
        h_ml, mc, mn, mm = _mlstm_prompt(c_a, ml_gb, ml_ng, l, bp, t)
        h_gla, gs = _gla_prompt(c_a, gla_gup, gla_gb, gla_ng, l, bp, t)
        h_rw, rs = _rwkv_prompt(c_b, rw_mu, rw_wup, rw_aup, rw_gup, rw_vec, l, bp, t)
        h_dn, ds = _gdn_prompt(c_b, dn_cw, dn_alog, dn_dtb, dn_ng, l, bp, t)
        hs_p = [h.reshape(bp * t, GROUP) for h in (h_ml, h_gla, h_rw, h_dn)]
        x1, x1b = _outproj_ln(hs_p, wo, xp, g1, b1, l, tm_p)
        xp, xpb = _ffn_ln(x1, x1b, wu, wd, g2, b2, l, tm_p, tf)
        last_rows = c_b.reshape(bp, t, W_B)[:, t - (DN_CONV - 1):]
        outs_p.append((mc, mn, mm[:, :, 0], gs, rs, last_rows[:, -1, RW_OFF:], ds, last_rows[:, :, :DN_QKV]))

        c_a, c_b = [_proj(xsb, w, l, tm_s) for w in w_ab]
        h_ml, ml_c_s, mn, mm = _mlstm_step(c_a, ml_gb, ml_ng, state_mlstm_c, st_ml_n, state_mlstm_m, l, ml_c_s)
        h_gla, gla_s = _gla_step(c_a, gla_gup, gla_gb, gla_ng, state_gla, l, gla_s)
        h_rw, rw_s = _rwkv_step(c_b, state_rwkv_shift, rw_mu, rw_wup, rw_aup, rw_gup, rw_vec, state_rwkv, l, rw_s)
        h_dn, dn_s = _gdn_step(c_b, st_dn_conv, dn_cw, dn_alog, dn_dtb, dn_ng, state_dn, l, dn_s)
        x1, x1b = _outproj_ln((h_ml, h_gla, h_rw, h_dn), wo, xs, g1, b1, l, tm_s)
        xs, xsb = _ffn_ln(x1, x1b, wu, wd, g2, b2, l, tm_s, tf)
        outs_s.append((mn.reshape(bs, ML_H, ML_DK), mm, c_b[:, RW_OFF:],
                       jnp.concatenate([state_dn_conv[l][:, 1:], c_b[:, None, :DN_QKV]], axis=1)))

    (mlstm_c_p, mlstm_n_p, mlstm_m_p, gla_p, rwkv_p, rwkv_shift_p, dn_p, dn_conv_p) = [jnp.stack(z) for z in zip(*outs_p)]
    (mlstm_n_s, mlstm_m_s, rwkv_shift_s, dn_conv_s) = [jnp.stack(z) for z in zip(*outs_s)]
    return (xp.reshape(bp, t, D_MODEL), xs.reshape(bs, 1, D_MODEL), mlstm_c_p, ml_c_s, mlstm_n_p, mlstm_n_s,
            mlstm_m_p, mlstm_m_s, gla_p, gla_s, rwkv_p, rw_s, rwkv_shift_p, rwkv_shift_s, dn_p, dn_s,
            dn_conv_p, dn_conv_s)
```
